```python
import jax
import jax.numpy as jnp
from jax import lax
import numpy as np

D_MODEL = 1024
BATCH = 1
SEQ = 16384
DEPTH = 1

HEAD_DIM = 64
N_ATT_HEADS = 8
N_RWKV_HEADS = 8
D_ATT = N_ATT_HEADS * HEAD_DIM
D_RWKV = N_RWKV_HEADS * HEAD_DIM
D_MIX = D_ATT + D_RWKV
ROPE_THETA = 500000.0
N_IDX_HEADS = 8
IDX_DIM = 32
TOPK_MAX = 256
Q_BLOCK = 128
D_DECAY_LORA = 32
D_AAA_LORA = 32
D_GATE_LORA = 64
RWKV_GN_EPS = 64e-5
N_EXPERTS = 32
TOP_K_EXPERTS = 4
D_FF_EXPERT = 1024
SWIGLU_LIMIT = 7.0
SWIGLU_ALPHA = 1.702
MOE_BLOCK = 128
RMS_EPS = 1e-6

ATT_COL_SIZES = (D_ATT, D_ATT, D_ATT, N_IDX_HEADS * IDX_DIM, IDX_DIM, N_IDX_HEADS)
RWKV_COL_SIZES = (D_RWKV, D_RWKV, D_RWKV, D_DECAY_LORA, D_AAA_LORA, D_GATE_LORA)
D_ATT_COLS = 3 * D_ATT + N_IDX_HEADS * IDX_DIM + IDX_DIM + N_IDX_HEADS
D_RWKV_COLS = 3 * D_RWKV + D_DECAY_LORA + D_AAA_LORA + D_GATE_LORA
D_IN = D_ATT_COLS + D_RWKV_COLS

kernel_name = "hybrid_dsa_rwkv7_moe_layer"


def _split_cols(z, sizes):
    return jnp.split(z, np.cumsum(sizes)[:-1].tolist(), axis=-1)


def rms_norm(x, w):
    xf = x.astype(jnp.float32)
    y = xf * lax.rsqrt(jnp.mean(xf * xf, axis=-1, keepdims=True) + RMS_EPS)
    return (y * w.astype(jnp.float32)).astype(x.dtype)


def partial_rotary(x, positions):
    d = x.shape[-1]
    rot = d // 4
    half = rot // 2
    inv_freq = ROPE_THETA ** (-jnp.arange(half, dtype=jnp.float32) / half)
    ang = positions.astype(jnp.float32)[..., None] * inv_freq
    ang = ang.reshape(ang.shape[:2] + (1,) * (x.ndim - 3) + (half,))
    cos, sin = jnp.cos(ang), jnp.sin(ang)
    xf = x.astype(jnp.float32)
    x1, x2, rest = xf[..., :half], xf[..., half:rot], xf[..., rot:]
    out = jnp.concatenate([x1 * cos - x2 * sin, x2 * cos + x1 * sin, rest], axis=-1)
    return out.astype(x.dtype)


def dsa_sparse_attention(q, k, v, iq, ik, iw):
    B, T, H, dh = q.shape
    n_sel = min(TOPK_MAX, T // 4)
    nb = T // Q_BLOCK
    idx_scale = (IDX_DIM * N_IDX_HEADS) ** -0.5
    att_scale = HEAD_DIM ** -0.5
    batch_ix = jnp.arange(B)[:, None, None]
    s_pos = jnp.arange(T)

    def to_blocks(a):
        return a.reshape((B, nb, Q_BLOCK) + a.shape[2:]).swapaxes(0, 1)

    def one_block(args):
        qb, iqb, iwb, start = args
        t_pos = start + jnp.arange(Q_BLOCK)
        dots = jnp.einsum("bqhd,bsd->bqhs", iqb, ik, preferred_element_type=jnp.float32)
        score = jnp.einsum("bqh,bqhs->bqs", iwb.astype(jnp.float32), jax.nn.relu(dots)) * idx_scale
        causal = s_pos[None, :] <= t_pos[:, None]
        score = jnp.where(causal[None], score, -jnp.inf)
        _, sel = lax.top_k(score, n_sel)
        valid = sel <= t_pos[None, :, None]
        k_sel = k[batch_ix, sel]
        v_sel = v[batch_ix, sel]
        logits = jnp.einsum("bqhd,bqkhd->bhqk", qb, k_sel, preferred_element_type=jnp.float32) * att_scale
        logits = jnp.where(valid[:, None], logits, -jnp.inf)
        p = jax.nn.softmax(logits, axis=-1)
        return jnp.einsum("bhqk,bqkhd->bqhd", p.astype(v.dtype), v_sel)

    starts = jnp.arange(nb) * Q_BLOCK
    out = lax.map(one_block, (to_blocks(q), to_blocks(iq), to_blocks(iw), starts))
    return out.swapaxes(0, 1).reshape(B, T, H, dh)


def token_shift(z, mu):
    prev = jnp.pad(z, ((0, 0), (1, 0), (0, 0)))[:, :-1]
    return z + (prev - z) * mu


def rwkv7_time_mix(r, k, v, wd, ad, gd, w0, w2, a0, a2, g2, k_k, k_a, r_k, ln_w, ln_b):
    B, T, _ = r.shape
    H, N = N_RWKV_HEADS, HEAD_DIM
    f32 = jnp.float32
    r, k, v, wd, ad, gd = (z.astype(f32) for z in (r, k, v, wd, ad, gd))
    log_w = -jax.nn.softplus(-(w0 + jnp.tanh(wd) @ w2)) - 0.5
    decay = jnp.exp(-jnp.exp(log_w))
    a = jax.nn.sigmoid(a0 + ad @ a2)
    g = jax.nn.sigmoid(gd) @ g2
    kk = (k * k_k).reshape(B, T, H, N)
    kk = kk * lax.rsqrt(jnp.maximum(jnp.sum(kk * kk, axis=-1, keepdims=True), 1e-24))
    k = k * (1.0 + (a - 1.0) * k_a)
    heads = lambda z: z.reshape(B, T, H, N)
    r, decay, k, v, a = heads(r), heads(decay), heads(k), heads(v), heads(a)

    def step(S, inp):
        r_t, w_t, k_t, v_t, kk_t, a_t = inp
        sa = jnp.einsum("bhij,bhj->bhi", S, kk_t)
        S = (S * w_t[:, :, None, :]
             - sa[..., None] * (kk_t * a_t)[:, :, None, :]
             + v_t[..., None] * k_t[:, :, None, :])
        return S, jnp.einsum("bhij,bhj->bhi", S, r_t)

    xs = tuple(z.swapaxes(0, 1) for z in (r, decay, k, v, kk, a))
    _, y = lax.scan(step, jnp.zeros((B, H, N, N), f32), xs)
    y = y.swapaxes(0, 1)
    mean = jnp.mean(y, axis=-1, keepdims=True)
    var = jnp.mean(jnp.square(y - mean), axis=-1, keepdims=True)
    y = ((y - mean) * lax.rsqrt(var + RWKV_GN_EPS)).reshape(B, T, H * N) * ln_w + ln_b
    bonus = jnp.sum(r * k * r_k, axis=-1, keepdims=True) * v
    return (y + bonus.reshape(B, T, H * N)) * g


def moe_ffn(xn, router_w, router_b, w_gu, b_gu, w_dn, b_dn):
    B, T, D = xn.shape
    n_tok = B * T
    xt = xn.reshape(n_tok, D)
    logits = (xt @ router_w + router_b).astype(jnp.float32)
    top_val, top_e = lax.top_k(logits, TOP_K_EXPERTS)
    gates = jax.nn.softmax(top_val, axis=-1)
    n_asg = n_tok * TOP_K_EXPERTS
    flat_e = top_e.reshape(n_asg)
    flat_tok = jnp.arange(n_asg, dtype=jnp.int32) // TOP_K_EXPERTS
    order = jnp.argsort(flat_e)
    se, stok, sg = flat_e[order], flat_tok[order], gates.reshape(n_asg)[order]
    counts = jnp.zeros((N_EXPERTS,), jnp.int32).at[flat_e].add(1)
    padded = (counts + MOE_BLOCK - 1) // MOE_BLOCK * MOE_BLOCK
    start = jnp.cumsum(counts) - counts
    pend = jnp.cumsum(padded)
    pstart = pend - padded
    dest = pstart[se] + (jnp.arange(n_asg, dtype=jnp.int32) - start[se])
    n_rows = (-(-n_asg // MOE_BLOCK) + N_EXPERTS) * MOE_BLOCK
    n_blk = n_rows // MOE_BLOCK
    row_tok = jnp.full((n_rows,), n_tok, jnp.int32).at[dest].set(stok)
    row_g = jnp.zeros((n_rows,), jnp.float32).at[dest].set(sg)
    blk_e = jnp.clip(jnp.searchsorted(pend, jnp.arange(n_blk) * MOE_BLOCK, side="right"), 0, N_EXPERTS - 1)
    x_pad = jnp.concatenate([xt, jnp.zeros((1, D), xt.dtype)], axis=0)
    xb = x_pad[row_tok].reshape(n_blk, MOE_BLOCK, D)

    def expert_block(args):
        xblk, e = args
        hgu = xblk @ w_gu[e] + b_gu[e]
        glu = jnp.minimum(hgu[:, :D_FF_EXPERT], SWIGLU_LIMIT)
        lin = jnp.clip(hgu[:, D_FF_EXPERT:], -SWIGLU_LIMIT, SWIGLU_LIMIT)
        act = (lin + 1.0) * glu * jax.nn.sigmoid(SWIGLU_ALPHA * glu)
        return act @ w_dn[e] + b_dn[e]

    yb = lax.map(expert_block, (xb, blk_e)).reshape(n_rows, D)
    y = jax.ops.segment_sum(yb * row_g[:, None].astype(yb.dtype), row_tok, num_segments=n_tok + 1)[:n_tok]
    return y.reshape(B, T, D)


def setup_inputs(seed: int = 0) -> dict:
    key = jax.random.key(seed)
    ks = jax.random.split(key, 24)
    f32 = jnp.float32
    L = DEPTH

    def nrm(kk, shape, scale):
        return jax.random.normal(kk, shape, f32) * scale

    return {
        "x": nrm(ks[0], (BATCH, SEQ, D_MODEL), 1.0),
        "positions": jnp.tile(jnp.arange(SEQ, dtype=jnp.int32)[None], (BATCH, 1)),
        "norm1_w": 1.0 + nrm(ks[1], (L, D_MODEL), 0.02),
        "w_in": nrm(ks[2], (L, D_MODEL, D_IN), D_MODEL ** -0.5),
        "q_norm_w": 1.0 + nrm(ks[3], (L, HEAD_DIM), 0.02),
        "k_norm_w": 1.0 + nrm(ks[4], (L, HEAD_DIM), 0.02),
        "rwkv_mu": jax.random.uniform(ks[5], (L, D_RWKV_COLS), f32),
        "rwkv_w0": jax.random.uniform(ks[6], (L, D_RWKV), f32, -5.0, 1.0),
        "rwkv_w2": nrm(ks[7], (L, D_DECAY_LORA, D_RWKV), 0.1),
        "rwkv_a0": nrm(ks[8], (L, D_RWKV), 0.1),
        "rwkv_a2": nrm(ks[9], (L, D_AAA_LORA, D_RWKV), D_AAA_LORA ** -0.5),
        "rwkv_g2": nrm(ks[10], (L, D_GATE_LORA, D_RWKV), D_GATE_LORA ** -0.5),
        "rwkv_k_k": 0.85 + nrm(ks[11], (L, D_RWKV), 0.05),
        "rwkv_k_a": 1.0 + nrm(ks[12], (L, D_RWKV), 0.05),
        "rwkv_r_k": nrm(ks[13], (L, N_RWKV_HEADS, HEAD_DIM), 0.1),
        "rwkv_ln_w": 1.0 + nrm(ks[14], (L, D_RWKV), 0.02),
        "rwkv_ln_b": nrm(ks[15], (L, D_RWKV), 0.01),
        "w_out": nrm(ks[16], (L, D_MIX, D_MODEL), D_MIX ** -0.5),
        "norm2_w": 1.0 + nrm(ks[17], (L, D_MODEL), 0.02),
        "router_w": nrm(ks[18], (L, D_MODEL, N_EXPERTS), D_MODEL ** -0.5),
        "router_b": nrm(ks[19], (L, N_EXPERTS), 0.01),
        "exp_w_gu": nrm(ks[20], (L, N_EXPERTS, D_MODEL, 2 * D_FF_EXPERT), D_MODEL ** -0.5),
        "exp_b_gu": nrm(ks[21], (L, N_EXPERTS, 2 * D_FF_EXPERT), 0.01),
        "exp_w_down": nrm(ks[22], (L, N_EXPERTS, D_FF_EXPERT, D_MODEL), D_FF_EXPERT ** -0.5),
        "exp_b_down": nrm(ks[23], (L, N_EXPERTS, D_MODEL), 0.01),
    }


def reference(x, positions, norm1_w, w_in, q_norm_w, k_norm_w, rwkv_mu, rwkv_w0, rwkv_w2,
              rwkv_a0, rwkv_a2, rwkv_g2, rwkv_k_k, rwkv_k_a, rwkv_r_k, rwkv_ln_w, rwkv_ln_b,
              w_out, norm2_w, router_w, router_b, exp_w_gu, exp_b_gu, exp_w_down, exp_b_down):
    B, T, _ = x.shape
    h = x
    for l in range(DEPTH):
        xn = rms_norm(h, norm1_w[l])
        z = xn @ w_in[l]
        att_cols, rw_cols = z[..., :D_ATT_COLS], z[..., D_ATT_COLS:]

        q, k, v, iq, ik, iw = _split_cols(att_cols, ATT_COL_SIZES)
        q = partial_rotary(rms_norm(q.reshape(B, T, N_ATT_HEADS, HEAD_DIM), q_norm_w[l]), positions)
        k = partial_rotary(rms_norm(k.reshape(B, T, N_ATT_HEADS, HEAD_DIM), k_norm_w[l]), positions)
        v = v.reshape(B, T, N_ATT_HEADS, HEAD_DIM)
        iq = partial_rotary(iq.reshape(B, T, N_IDX_HEADS, IDX_DIM), positions)
        ik = partial_rotary(ik, positions)
        att = dsa_sparse_attention(q, k, v, iq, ik, iw).reshape(B, T, D_ATT)

        rw_cols = token_shift(rw_cols, rwkv_mu[l])
        r, kr, vr, wd, ad, gd = _split_cols(rw_cols, RWKV_COL_SIZES)
        rw = rwkv7_time_mix(r, kr, vr, wd, ad, gd, rwkv_w0[l], rwkv_w2[l], rwkv_a0[l], rwkv_a2[l],
                            rwkv_g2[l], rwkv_k_k[l], rwkv_k_a[l], rwkv_r_k[l], rwkv_ln_w[l], rwkv_ln_b[l])

        h = h + jnp.concatenate([att, rw.astype(h.dtype)], axis=-1) @ w_out[l]

        h = h + moe_ffn(rms_norm(h, norm2_w[l]), router_w[l], router_b[l], exp_w_gu[l], exp_b_gu[l],
                        exp_w_down[l], exp_b_down[l]).astype(h.dtype)
    return h
```

```python
import functools

import jax
import jax.numpy as jnp
import numpy as np
from jax import lax
from jax.experimental import pallas as pl
from jax.experimental.pallas import tpu as pltpu

F32 = jnp.float32
BF16 = jnp.bfloat16
HIGHEST = lax.Precision.HIGHEST

D_MODEL = 1024
HEAD_DIM = 64
N_HEADS = 8
D_GROUP = N_HEADS * HEAD_DIM
ROPE_THETA = 500000.0
N_IDX_HEADS = 8
IDX_DIM = 32
D_IDX = N_IDX_HEADS * IDX_DIM
TOPK_MAX = 256
D_DECAY_LORA = 32
D_AAA_LORA = 32
D_GATE_LORA = 64
RWKV_GN_EPS = 64e-5
N_EXPERTS = 32
TOP_K_EXPERTS = 4
D_FF = 1024
SWIGLU_LIMIT = 7.0
SWIGLU_ALPHA = 1.702
RMS_EPS = 1e-6

LANES = 128
VMEM_LIMIT = 56 * 1024 * 1024

NEG_BIG = -1e30
INT_MIN = -(2 ** 31)


def _dot(a, b):
    return jnp.dot(a.astype(BF16), b.astype(BF16), preferred_element_type=F32)


def _dotf(a, b):
    return jnp.dot(a, b, preferred_element_type=F32, precision=HIGHEST)


def _dotf_nt(a, b):
    return lax.dot_general(a, b, (((1,), (1,)), ((), ())), precision=HIGHEST,
                           preferred_element_type=F32)


def _dotf_tn(a, b):
    return lax.dot_general(a, b, (((0,), (0,)), ((), ())), precision=HIGHEST,
                           preferred_element_type=F32)


def _group_sum(z, g_ref):
    return _dotf(z, g_ref[...])


_C_Q, _C_K, _C_V, _C_RR, _C_RK, _C_RV = (i * D_GROUP for i in range(6))
_C_IQ = 6 * D_GROUP
_C_IKREP = _C_IQ + D_IDX
_C_SM = _C_IKREP + D_IDX
_C_LORA = _C_SM + LANES
D_IN_PACKED = _C_LORA + LANES
D_SHIFT = 3 * D_GROUP + LANES


def _in_proj_kernel(x_ref, pos_ref, n1w_ref, w_ref, mu_ref, qnw_ref, knw_ref, g_ref, rope_ref,
                    q_ref, k_ref, v_ref, rr_ref, rk_ref, rv_ref, iq_ref, ikrep_ref, sm_ref,
                    lora_ref, carry_ref):
    tm = x_ref.shape[0]

    @pl.when(pl.program_id(0) == 0)
    def _():
        carry_ref[...] = jnp.zeros_like(carry_ref)

    x = x_ref[...]
    xn = x * lax.rsqrt(jnp.mean(x * x, axis=-1, keepdims=True) + RMS_EPS) * n1w_ref[...]
    xb = xn.astype(BF16)
    pos = pos_ref[...]
    rope = rope_ref[...]

    def tables(frow, srow, reps):
        ang = pos * rope[frow:frow + 1, :]
        c = jnp.cos(ang)
        s = jnp.sin(ang) * rope[srow:srow + 1, :]
        if reps > 1:
            c = jnp.concatenate([c] * reps, axis=1)
            s = jnp.concatenate([s] * reps, axis=1)
        return c, s

    def rotary(z, c, s, first_row, half):
        w = z.shape[1]
        first = jnp.concatenate([rope[first_row:first_row + 1, :]] * (w // LANES), axis=1) > 0.5
        partner = jnp.where(first, pltpu.roll(z, w - half, 1), pltpu.roll(z, half, 1))
        return z * c + partner * s

    def head_norm(z, w_row):
        ms = _group_sum(z * z, g_ref) * (1.0 / HEAD_DIM)
        return z * lax.rsqrt(ms + RMS_EPS) * w_row

    def proj(c0, width):
        return jnp.dot(xb, w_ref[:, c0:c0 + width], preferred_element_type=F32)

    def shift(z, c0):
        width = z.shape[1]
        row = lax.broadcasted_iota(jnp.int32, z.shape, 0)
        prev = jnp.where(row == 0, carry_ref[0:1, c0:c0 + width], pltpu.roll(z, 1, 0))
        carry_ref[0:1, c0:c0 + width] = z[tm - 1:tm, :]
        return z + (prev - z) * mu_ref[:, c0:c0 + width]

    cq, sq = tables(0, 1, D_GROUP // LANES)
    q = rotary(head_norm(proj(_C_Q, D_GROUP), qnw_ref[...]), cq, sq, 2, HEAD_DIM // 8)
    q_ref[...] = q.astype(BF16)
    k = rotary(head_norm(proj(_C_K, D_GROUP), knw_ref[...]), cq, sq, 2, HEAD_DIM // 8)
    k_ref[...] = k.astype(BF16)
    v_ref[...] = proj(_C_V, D_GROUP).astype(BF16)

    rr_ref[...] = shift(proj(_C_RR, D_GROUP), 0)
    rk_ref[...] = shift(proj(_C_RK, D_GROUP), D_GROUP)
    rv_ref[...] = shift(proj(_C_RV, D_GROUP), 2 * D_GROUP)
    lora_ref[...] = shift(proj(_C_LORA, LANES), 3 * D_GROUP)

    ci, si = tables(3, 4, D_IDX // LANES)
    iq_ref[...] = rotary(proj(_C_IQ, D_IDX), ci, si, 5, IDX_DIM // 8)
    ikrep_ref[...] = rotary(proj(_C_IKREP, D_IDX), ci, si, 5, IDX_DIM // 8).astype(BF16)
    sm_ref[...] = proj(_C_SM, LANES)


def _in_proj(x2, pos_f, n1w, w_packed, mu_packed, qnw, knw, gmat, rope, tm):
    t = x2.shape[0]
    full = lambda shape: pl.BlockSpec(shape, lambda i: (0,) * len(shape))
    row = lambda width: pl.BlockSpec((tm, width), lambda i: (i, 0))
    out_shapes = (
        jax.ShapeDtypeStruct((t, D_GROUP), BF16),
        jax.ShapeDtypeStruct((t, D_GROUP), BF16),
        jax.ShapeDtypeStruct((t, D_GROUP), BF16),
        jax.ShapeDtypeStruct((t, D_GROUP), F32),
        jax.ShapeDtypeStruct((t, D_GROUP), F32),
        jax.ShapeDtypeStruct((t, D_GROUP), F32),
        jax.ShapeDtypeStruct((t, D_IDX), F32),
        jax.ShapeDtypeStruct((t, D_IDX), BF16),
        jax.ShapeDtypeStruct((t, LANES), F32),
        jax.ShapeDtypeStruct((t, LANES), F32),
    )
    return pl.pallas_call(
        _in_proj_kernel,
        grid=(t // tm,),
        in_specs=[row(D_MODEL), row(1), full((1, D_MODEL)), full((D_MODEL, D_IN_PACKED)),
                  full((1, D_SHIFT)), full((1, D_GROUP)), full((1, D_GROUP)),
                  full((D_GROUP, D_GROUP)), full((8, LANES))],
        out_specs=[row(D_GROUP)] * 6 + [row(D_IDX), row(D_IDX), row(LANES), row(LANES)],
        out_shape=out_shapes,
        scratch_shapes=[pltpu.VMEM((8, D_SHIFT), F32)],
        compiler_params=pltpu.CompilerParams(dimension_semantics=("arbitrary",),
                                             vmem_limit_bytes=VMEM_LIMIT),
        name="in_proj",
    )(x2, pos_f, n1w, w_packed, mu_packed, qnw, knw, gmat, rope)


RWKV_CHUNK = 64
PAIR = 2 * HEAD_DIM


def _rwkv_kernel(r_ref, k_ref, v_ref, lora_ref, w0_ref, w2_ref, a0_ref, a2_ref, g2_ref,
                 kk_ref, ka_ref, rk_ref, lnw_ref, lnb_ref, g_ref,
                 o_ref,
                 s_ref, ld_s, r_s, k2_s, b_s, kk_s, y_s):
    tm = r_ref.shape[0]
    n_chunks = tm // RWKV_CHUNK
    n_pairs = D_GROUP // PAIR
    c = RWKV_CHUNK

    @pl.when(pl.program_id(0) == 0)
    def _():
        s_ref[...] = jnp.zeros_like(s_ref)

    lora = lora_ref[...]
    r = r_ref[...]
    k = k_ref[...]
    v = v_ref[...]
    zarg = w0_ref[...] + _dotf(jnp.tanh(lora), w2_ref[...])
    sp = jnp.maximum(-zarg, 0.0) + jnp.log1p(jnp.exp(-jnp.abs(zarg)))
    ld_s[...] = -jnp.exp(-sp - 0.5)
    a = jax.nn.sigmoid(a0_ref[...] + _dotf(lora, a2_ref[...]))
    g = _dotf(jax.nn.sigmoid(lora), g2_ref[...])
    kk = k * kk_ref[...]
    kk = kk * lax.rsqrt(jnp.maximum(_group_sum(kk * kk, g_ref), 1e-24))
    k2 = k * (1.0 + (a - 1.0) * ka_ref[...])
    bonus = _group_sum(r * k2 * rk_ref[...], g_ref) * v
    r_s[...] = r
    k2_s[...] = k2
    kk_s[...] = kk
    b_s[...] = kk * a

    row = lax.broadcasted_iota(jnp.int32, (2 * c, 2 * c), 0)
    col = lax.broadcasted_iota(jnp.int32, (2 * c, 2 * c), 1)
    same_head = (row >= c) == (col >= c)
    strict = same_head & (col < row)
    incl = same_head & (col <= row)
    eye = (row == col).astype(F32)
    tri = (lax.broadcasted_iota(jnp.int32, (c, c), 1)
           <= lax.broadcasted_iota(jnp.int32, (c, c), 0)).astype(F32)
    lane = lax.broadcasted_iota(jnp.int32, (c, PAIR), 1)
    head0 = lane < HEAD_DIM

    def stack(z):
        return jnp.concatenate([jnp.where(head0, z, 0.0), jnp.where(head0, 0.0, z)], axis=0)

    def chunk_body(ci, carry):
        r0 = pl.multiple_of(ci * c, c)
        for p in range(n_pairs):
            cols = slice(p * PAIR, (p + 1) * PAIR)
            ld = ld_s[pl.ds(r0, c), cols]
            cum = _dotf(tri, ld)
            gam = jnp.exp(cum)
            inv = jnp.exp(-cum)
            gam_prev = jnp.exp(cum - ld)
            gam_end = gam[c - 1:c, :]
            rt = stack(r_s[pl.ds(r0, c), cols] * gam)
            kt_raw = k2_s[pl.ds(r0, c), cols] * inv
            bt_raw = b_s[pl.ds(r0, c), cols] * inv
            kt = stack(kt_raw)
            bt = stack(bt_raw)
            kp = stack(kk_s[pl.ds(r0, c), cols] * gam_prev)
            vs = stack(v_ref[pl.ds(r0, c), cols])
            b2 = jnp.concatenate([bt_raw, bt_raw], axis=0)
            k2u = jnp.concatenate([kt_raw, kt_raw], axis=0)
            a_bk = jnp.where(strict, _dotf_nt(kp, b2), 0.0)
            a_kk = jnp.where(strict, _dotf_nt(kp, k2u), 0.0)
            a_rb = jnp.where(incl, _dotf_nt(rt, b2), 0.0)
            a_rk = jnp.where(incl, _dotf_nt(rt, k2u), 0.0)
            n = -a_bk
            tinv = eye + n
            steps = int(np.log2(c)) - 1
            for _ in range(steps):
                n = _dotf(n, n)
                tinv = tinv + _dotf(tinv, n)
            w1 = _dotf(tinv, kp)
            u2 = _dotf(tinv, _dotf(a_kk, vs))
            r2 = rt - _dotf(a_rb, w1)
            y2 = _dotf(a_rk, vs) - _dotf(a_rb, u2)
            s0 = s_ref[p]
            ys = _dotf_nt(r2, s0) + y2
            y_s[pl.ds(r0, c), cols] = ys[:c] + ys[c:]
            m = (eye - _dotf_tn(w1, bt)) * gam_end
            z = (_dotf_tn(vs, kt) - _dotf_tn(u2, bt)) * gam_end
            s_ref[p] = _dotf(s0, m) + z
        return carry

    lax.fori_loop(0, n_chunks, chunk_body, 0)

    y = y_s[...]
    mean = _group_sum(y, g_ref) * (1.0 / HEAD_DIM)
    yc = y - mean
    var = _group_sum(yc * yc, g_ref) * (1.0 / HEAD_DIM)
    yn = yc * lax.rsqrt(var + RWKV_GN_EPS) * lnw_ref[...] + lnb_ref[...]
    o_ref[...] = (yn + bonus) * g


def _rwkv(rr, rk, rv, lora, w0, w2p, a0, a2p, g2p, k_k, k_a, r_k, ln_w, ln_b, gsum, tm):
    t = rr.shape[0]
    full = lambda shape: pl.BlockSpec(shape, lambda i: (0,) * len(shape))
    row = lambda width: pl.BlockSpec((tm, width), lambda i: (i, 0))
    vec = full((1, D_GROUP))
    big = pltpu.VMEM((tm, D_GROUP), F32)
    return pl.pallas_call(
        _rwkv_kernel,
        grid=(t // tm,),
        in_specs=[row(D_GROUP), row(D_GROUP), row(D_GROUP), row(LANES),
                  vec, full((LANES, D_GROUP)), vec, full((LANES, D_GROUP)),
                  full((LANES, D_GROUP)), vec, vec, vec, vec, vec, full((D_GROUP, D_GROUP))],
        out_specs=row(D_GROUP),
        out_shape=jax.ShapeDtypeStruct((t, D_GROUP), F32),
        scratch_shapes=[pltpu.VMEM((D_GROUP // PAIR, PAIR, PAIR), F32),
                        big, big, big, big, big, big],
        compiler_params=pltpu.CompilerParams(dimension_semantics=("arbitrary",),
                                             vmem_limit_bytes=VMEM_LIMIT),
        name="rwkv",
    )(rr, rk, rv, lora, w0, w2p, a0, a2p, g2p, k_k, k_a, r_k, ln_w, ln_b, gsum)


DSA_QB = 128
DSA_KB = 512


def _dsa_kernel(q_ref, k_ref, v_ref, iq_ref, sm_ref, ikrep_ref, o_ref,
                key_s, thr_s, wrep_s, m_s, l_s, acc_s, *, n_sel):
    qb = q_ref.shape[1]
    kb = k_ref.shape[1]
    i = pl.program_id(0)
    j = pl.program_id(1)
    q0 = i * qb
    j_last = (q0 + qb - 1) // kb
    n_kc = j_last + 1
    idx_scale = float((IDX_DIM * N_IDX_HEADS) ** -0.5)

    def causal(jblk):
        t_pos = q0 + lax.broadcasted_iota(jnp.int32, (qb, kb), 0)
        s_pos = jblk * kb + lax.broadcasted_iota(jnp.int32, (qb, kb), 1)
        return s_pos <= t_pos

    @pl.when(j == 0)
    def _():
        sm = sm_ref[...]
        src = lax.broadcasted_iota(jnp.int32, (LANES, LANES), 0)
        for h in range(N_IDX_HEADS):
            pick = (src == IDX_DIM + h).astype(F32)
            wrep_s[h] = _dotf(sm, pick)
        iq = iq_ref[...]
        lane = lax.broadcasted_iota(jnp.int32, iq.shape, 1)
        iqh = [jnp.where((lane >= h * IDX_DIM) & (lane < (h + 1) * IDX_DIM), iq, 0.0).astype(BF16)
               for h in range(N_IDX_HEADS)]

        def score_chunk(kc, carry):
            ik = ikrep_ref[pl.ds(pl.multiple_of(kc * kb, kb), kb), :]
            score = jnp.zeros((qb, kb), F32)
            for h in range(N_IDX_HEADS):
                d = lax.dot_general(iqh[h], ik, (((1,), (1,)), ((), ())),
                                    preferred_element_type=F32)
                w = jnp.concatenate([wrep_s[h]] * (kb // LANES), axis=1)
                score = score + jnp.maximum(d, 0.0) * w
            score = jnp.where(causal(kc), score * idx_scale, -jnp.inf)
            bits = pltpu.bitcast(score, jnp.int32)
            key_s[kc] = jnp.where(bits < 0, bits ^ jnp.int32(0x7FFFFFFF), bits)
            return carry

        lax.fori_loop(0, n_kc, score_chunk, 0)

        def bit_step(b, cur):
            bit = lax.shift_left(jnp.int32(1), jnp.int32(31) - b)
            cand = (cur | bit) ^ jnp.int32(INT_MIN)

            def count_chunk(kc, acc):
                keys = key_s[kc]
                for cb in range(kb // LANES):
                    acc = acc + jnp.where(keys[:, cb * LANES:(cb + 1) * LANES] >= cand, 1, 0)
                return acc

            acc = lax.fori_loop(0, n_kc, count_chunk, jnp.zeros((qb, LANES), jnp.int32))
            cnt = jnp.sum(acc, axis=1, keepdims=True)
            return jnp.where(cnt >= n_sel, cur | bit, cur)

        cur = lax.fori_loop(0, 32, bit_step, jnp.zeros((qb, LANES), jnp.int32))
        thr_s[...] = cur ^ jnp.int32(INT_MIN)
        m_s[...] = jnp.full_like(m_s, NEG_BIG)
        l_s[...] = jnp.zeros_like(l_s)
        acc_s[...] = jnp.zeros_like(acc_s)

    @pl.when(j <= j_last)
    def _():
        thr = jnp.concatenate([thr_s[...]] * (kb // LANES), axis=1)
        mask = (key_s[j] >= thr) & causal(j)
        for h in range(N_HEADS):
            qh = q_ref[h] * jnp.asarray(HEAD_DIM ** -0.5, BF16)
            s = lax.dot_general(qh, k_ref[h], (((1,), (1,)), ((), ())),
                                preferred_element_type=F32)
            s = jnp.where(mask, s, NEG_BIG)
            m_old = m_s[h]
            m_new = jnp.maximum(m_old, jnp.max(s, axis=1, keepdims=True))
            alpha = jnp.exp(m_old - m_new)
            p = jnp.where(mask, jnp.exp(s - jnp.concatenate([m_new] * (kb // LANES), axis=1)), 0.0)
            l_s[h] = alpha * l_s[h] + jnp.sum(p, axis=1, keepdims=True)
            acc_s[h] = alpha[:, :HEAD_DIM] * acc_s[h] + jnp.dot(
                p.astype(BF16), v_ref[h], preferred_element_type=F32)
            m_s[h] = m_new

    @pl.when(j == j_last)
    def _():
        for h in range(N_HEADS):
            o_ref[h] = acc_s[h] / l_s[h][:, :HEAD_DIM]


def _dsa(qh, kh, vh, iq, sm, ikrep, n_sel):
    t = iq.shape[0]
    qb, kb = DSA_QB, min(DSA_KB, t)
    nq, nk = t // qb, t // kb
    last = lambda i: (i * qb + qb - 1) // kb
    kv_spec = pl.BlockSpec((N_HEADS, kb, HEAD_DIM), lambda i, j: (0, jnp.minimum(j, last(i)), 0))
    return pl.pallas_call(
        functools.partial(_dsa_kernel, n_sel=n_sel),
        grid=(nq, nk),
        in_specs=[pl.BlockSpec((N_HEADS, qb, HEAD_DIM), lambda i, j: (0, i, 0)),
                  kv_spec, kv_spec,
                  pl.BlockSpec((qb, D_IDX), lambda i, j: (i, 0)),
                  pl.BlockSpec((qb, LANES), lambda i, j: (i, 0)),
                  pl.BlockSpec((t, D_IDX), lambda i, j: (0, 0))],
        out_specs=pl.BlockSpec((N_HEADS, qb, HEAD_DIM), lambda i, j: (0, i, 0)),
        out_shape=jax.ShapeDtypeStruct((N_HEADS, t, HEAD_DIM), F32),
        scratch_shapes=[pltpu.VMEM((nk, qb, kb), jnp.int32),
                        pltpu.VMEM((qb, LANES), jnp.int32),
                        pltpu.VMEM((N_IDX_HEADS, qb, LANES), F32),
                        pltpu.VMEM((N_HEADS, qb, LANES), F32),
                        pltpu.VMEM((N_HEADS, qb, LANES), F32),
                        pltpu.VMEM((N_HEADS, qb, HEAD_DIM), F32)],
        compiler_params=pltpu.CompilerParams(dimension_semantics=("arbitrary", "arbitrary"),
                                             vmem_limit_bytes=VMEM_LIMIT),
        name="dsa",
    )(qh, kh, vh, iq, sm, ikrep)


def _out_proj_kernel(x_ref, att_ref, rw_ref, wo_ref, n2w_ref, rwt_ref, rb_ref,
                     h_ref, xn_ref, gate_ref):
    acc = x_ref[...] + _dot(rw_ref[...], wo_ref[D_GROUP:, :])
    for h in range(N_HEADS):
        acc = acc + _dot(att_ref[h], wo_ref[h * HEAD_DIM:(h + 1) * HEAD_DIM, :])
    h_ref[...] = acc
    xn = acc * lax.rsqrt(jnp.mean(acc * acc, axis=-1, keepdims=True) + RMS_EPS) * n2w_ref[...]
    xn_ref[...] = xn.astype(BF16)
    logits = _dotf(xn, rwt_ref[...]) + rb_ref[...]
    lane = lax.broadcasted_iota(jnp.int32, logits.shape, 1)
    work = logits
    vals, hots = [], []
    for _ in range(TOP_K_EXPERTS):
        m = jnp.max(work, axis=1, keepdims=True)
        idx = jnp.min(jnp.where(work == m, lane, LANES), axis=1, keepdims=True)
        hot = lane == idx
        vals.append(m)
        hots.append(hot)
        work = jnp.where(hot, -jnp.inf, work)
    es = [jnp.exp(vv - vals[0]) for vv in vals]
    denom = es[0] + es[1] + es[2] + es[3]
    gates = jnp.zeros_like(logits)
    for e, hot in zip(es, hots):
        gates = gates + jnp.where(hot, e / denom, 0.0)
    gate_ref[...] = gates


def _out_proj(x2, att_h, rw, wo, n2w, rwt, rb, tm):
    t = x2.shape[0]
    full = lambda shape: pl.BlockSpec(shape, lambda i: (0,) * len(shape))
    row = lambda width: pl.BlockSpec((tm, width), lambda i: (i, 0))
    return pl.pallas_call(
        _out_proj_kernel,
        grid=(t // tm,),
        in_specs=[row(D_MODEL), pl.BlockSpec((N_HEADS, tm, HEAD_DIM), lambda i: (0, i, 0)),
                  row(D_GROUP), full((2 * D_GROUP, D_MODEL)), full((1, D_MODEL)),
                  full((D_MODEL, LANES)), full((1, LANES))],
        out_specs=[row(D_MODEL), row(D_MODEL), row(LANES)],
        out_shape=(jax.ShapeDtypeStruct((t, D_MODEL), F32),
                   jax.ShapeDtypeStruct((t, D_MODEL), BF16),
                   jax.ShapeDtypeStruct((t, LANES), F32)),
        compiler_params=pltpu.CompilerParams(dimension_semantics=("arbitrary",),
                                             vmem_limit_bytes=VMEM_LIMIT),
        name="out_proj",
    )(x2, att_h, rw, wo, n2w, rwt, rb)


MOE_ROWS = 256
MOE_FC = 512


def _moe_kernel(h_ref, xn_ref, gate_ref, wgu_ref, bgu_ref, wdn_ref, bdn_ref, o_ref):
    e = pl.program_id(1)
    tm = h_ref.shape[0]

    @pl.when(e == 0)
    def _():
        o_ref[...] = h_ref[...]

    def row_block(rbi, carry):
        r0 = pl.multiple_of(rbi * MOE_ROWS, MOE_ROWS)
        xb = xn_ref[pl.ds(r0, MOE_ROWS), :]
        y = jnp.zeros((MOE_ROWS, D_MODEL), F32)
        for fc in range(D_FF // MOE_FC):
            c0 = fc * MOE_FC
            hg = jnp.dot(xb, wgu_ref[0, :, c0:c0 + MOE_FC], preferred_element_type=F32) \
                + bgu_ref[0, :, c0:c0 + MOE_FC]
            hl = jnp.dot(xb, wgu_ref[0, :, D_FF + c0:D_FF + c0 + MOE_FC],
                         preferred_element_type=F32) + bgu_ref[0, :, D_FF + c0:D_FF + c0 + MOE_FC]
            glu = jnp.minimum(hg, SWIGLU_LIMIT)
            lin = jnp.clip(hl, -SWIGLU_LIMIT, SWIGLU_LIMIT)
            act = (lin + 1.0) * glu * jax.nn.sigmoid(SWIGLU_ALPHA * glu)
            y = y + jnp.dot(act.astype(BF16), wdn_ref[0, c0:c0 + MOE_FC, :],
                            preferred_element_type=F32)
        gates = gate_ref[pl.ds(r0, MOE_ROWS), :]
        lane = lax.broadcasted_iota(jnp.int32, gates.shape, 1)
        ge = jnp.sum(jnp.where(lane == e, gates, 0.0), axis=1, keepdims=True)
        o_ref[pl.ds(r0, MOE_ROWS), :] += (y + bdn_ref[0]) * ge
        return carry

    lax.fori_loop(0, tm // MOE_ROWS, row_block, 0)


def _moe(h1, xn2, gates, wgu, bgu, wdn, bdn, tm):
    t = h1.shape[0]
    row = lambda width: pl.BlockSpec((tm, width), lambda i, e: (i, 0))
    return pl.pallas_call(
        _moe_kernel,
        grid=(t // tm, N_EXPERTS),
        in_specs=[row(D_MODEL), row(D_MODEL), row(LANES),
                  pl.BlockSpec((1, D_MODEL, 2 * D_FF), lambda i, e: (e, 0, 0)),
                  pl.BlockSpec((1, 1, 2 * D_FF), lambda i, e: (e, 0, 0)),
                  pl.BlockSpec((1, D_FF, D_MODEL), lambda i, e: (e, 0, 0)),
                  pl.BlockSpec((1, 1, D_MODEL), lambda i, e: (e, 0, 0))],
        out_specs=row(D_MODEL),
        out_shape=jax.ShapeDtypeStruct((t, D_MODEL), F32),
        compiler_params=pltpu.CompilerParams(dimension_semantics=("arbitrary", "arbitrary"),
                                             vmem_limit_bytes=VMEM_LIMIT),
        name="moe",
    )(h1, xn2, gates, wgu, bgu, wdn, bdn)


def _rope_tables():
    lane = jnp.arange(LANES)

    def rows(period, rot):
        half = rot // 2
        inv_freq = ROPE_THETA ** (-jnp.arange(half, dtype=F32) / half)
        jm = lane % period
        freq = jnp.where(jm < rot, inv_freq[jm % half], 0.0)
        sign = jnp.where(jm < half, -1.0, jnp.where(jm < rot, 1.0, 0.0))
        first = (jm < half).astype(F32)
        return [freq, sign, first]

    z = jnp.zeros((LANES,), F32)
    return jnp.stack(rows(HEAD_DIM, HEAD_DIM // 4) + rows(IDX_DIM, IDX_DIM // 4) + [z, z]).astype(F32)


def _group_matrix():
    g = np.arange(D_GROUP) // HEAD_DIM
    return jnp.asarray((g[:, None] == g[None, :]).astype(np.float32))


def _pad_rows(w, r0, rows):
    return jnp.zeros((rows, w.shape[1]), w.dtype).at[r0:r0 + w.shape[0]].set(w)


def kernel(x, positions, norm1_w, w_in, q_norm_w, k_norm_w, rwkv_mu, rwkv_w0, rwkv_w2, rwkv_a0,
           rwkv_a2, rwkv_g2, rwkv_k_k, rwkv_k_a, rwkv_r_k, rwkv_ln_w, rwkv_ln_b, w_out, norm2_w,
           router_w, router_b, exp_w_gu, exp_b_gu, exp_w_down, exp_b_down):
    b, t, _ = x.shape
    assert b == 1 and w_in.shape[0] == 1, "single sequence, single layer"
    assert t % DSA_QB == 0 and t % 256 == 0
    x2 = x[0]
    pos_f = positions[0].astype(F32)[:, None]
    n_sel = min(TOPK_MAX, t // 4)

    w = w_in[0]
    a0 = 3 * D_GROUP
    att_cols = a0 + D_IDX + IDX_DIM + N_IDX_HEADS
    w_att, w_rw = w[:, :att_cols], w[:, att_cols:]
    w_ik = w_att[:, a0 + D_IDX:a0 + D_IDX + IDX_DIM]
    w_sm = jnp.zeros((D_MODEL, LANES), F32).at[:, :IDX_DIM + N_IDX_HEADS].set(w_att[:, a0 + D_IDX:])
    w_packed = jnp.concatenate(
        [w_att[:, :a0], w_rw[:, :a0], w_att[:, a0:a0 + D_IDX], jnp.tile(w_ik, (1, N_IDX_HEADS)),
         w_sm, w_rw[:, a0:]], axis=1).astype(BF16)
    mu = rwkv_mu[0][None, :]
    tile8 = lambda z: jnp.tile(z, N_HEADS)[None, :]
    gsum = _group_matrix()

    tm = 256
    q, k, v, rr, rk, rv, iq, ikrep, sm, lora = _in_proj(
        x2, pos_f, norm1_w, w_packed, mu, tile8(q_norm_w[0]), tile8(k_norm_w[0]), gsum,
        _rope_tables(), tm)

    vec = lambda z: z.reshape(1, D_GROUP)
    rw = _rwkv(rr, rk, rv, lora, vec(rwkv_w0[0]),
               _pad_rows(rwkv_w2[0], 0, LANES), vec(rwkv_a0[0]),
               _pad_rows(rwkv_a2[0], D_DECAY_LORA, LANES),
               _pad_rows(rwkv_g2[0], D_DECAY_LORA + D_AAA_LORA, LANES),
               vec(rwkv_k_k[0]), vec(rwkv_k_a[0]), vec(rwkv_r_k[0]), vec(rwkv_ln_w[0]),
               vec(rwkv_ln_b[0]), gsum, tm)

    heads = lambda z: z.reshape(t, N_HEADS, HEAD_DIM).transpose(1, 0, 2)
    att_h = _dsa(heads(q), heads(k), heads(v), iq, sm, ikrep, n_sel)

    rwt = jnp.zeros((D_MODEL, LANES), F32).at[:, :N_EXPERTS].set(router_w[0])
    rb = jnp.full((1, LANES), NEG_BIG, F32).at[0, :N_EXPERTS].set(router_b[0])
    h1, xn2, gates = _out_proj(x2, att_h, rw, w_out[0].astype(BF16), norm2_w, rwt, rb, tm)

    out = _moe(h1, xn2, gates, exp_w_gu[0].astype(BF16), exp_b_gu[0][:, None, :],
               exp_w_down[0].astype(BF16), exp_b_down[0][:, None, :], min(1024, t))
    return out[None]
```

```python
import functools

import jax
import jax.numpy as jnp
import numpy as np
from jax import lax
from jax.experimental import pallas as pl
from jax.experimental.pallas import tpu as pltpu

F32 = jnp.float32
BF16 = jnp.bfloat16
HIGHEST = lax.Precision.HIGHEST

D_MODEL = 1024
HEAD_DIM = 64
N_HEADS = 8
D_GROUP = N_HEADS * HEAD_DIM
ROPE_THETA = 500000.0
N_IDX_HEADS = 8
IDX_DIM = 32
D_IDX = N_IDX_HEADS * IDX_DIM
TOPK_MAX = 256
D_DECAY_LORA = 32
D_AAA_LORA = 32
D_GATE_LORA = 64
RWKV_GN_EPS = 64e-5
N_EXPERTS = 32
TOP_K_EXPERTS = 4
D_FF = 1024
SWIGLU_LIMIT = 7.0
SWIGLU_ALPHA = 1.702
RMS_EPS = 1e-6

LANES = 128
VMEM_LIMIT = 56 * 1024 * 1024

NEG_BIG = -1e30
INT_MIN = -(2 ** 31)


def _dot(a, b):
    return jnp.dot(a.astype(BF16), b.astype(BF16), preferred_element_type=F32)


def _dotf(a, b):
    return jnp.dot(a, b, preferred_element_type=F32, precision=HIGHEST)


def _dot_nt(a, b):
    return lax.dot_general(a.astype(BF16), b.astype(BF16), (((1,), (1,)), ((), ())),
                           preferred_element_type=F32)


def _dot_tn(a, b):
    return lax.dot_general(a.astype(BF16), b.astype(BF16), (((0,), (0,)), ((), ())),
                           preferred_element_type=F32)


def _group_sum(z, g_ref):
    hi = z.astype(BF16)
    lo = (z - hi.astype(F32)).astype(BF16)
    g = g_ref[...]
    return (jnp.dot(hi, g, preferred_element_type=F32)
            + jnp.dot(lo, g, preferred_element_type=F32))


_C_Q, _C_K, _C_V, _C_RR, _C_RK, _C_RV = (i * D_GROUP for i in range(6))
_C_IQ = 6 * D_GROUP
_C_SM = _C_IQ + D_IDX
_C_LORA = _C_SM + LANES
D_IN_PACKED = _C_LORA + LANES
D_SHIFT = 3 * D_GROUP + LANES


def _in_proj_kernel(x_ref, pos_ref, n1w_ref, w_ref, mu_ref, qnw_ref, knw_ref, g_ref, rope_ref,
                    q_ref, k_ref, v_ref, rr_ref, rk_ref, rv_ref, iq_ref, sm_ref, lora_ref,
                    carry_ref):
    tm = x_ref.shape[0]

    @pl.when(pl.program_id(0) == 0)
    def _():
        carry_ref[...] = jnp.zeros_like(carry_ref)

    x = x_ref[...]
    xn = x * lax.rsqrt(jnp.mean(x * x, axis=-1, keepdims=True) + RMS_EPS) * n1w_ref[...]
    xb = xn.astype(BF16)
    pos = pos_ref[...]
    rope = rope_ref[...]

    def tables(frow, srow, reps):
        ang = pos * rope[frow:frow + 1, :]
        c = jnp.cos(ang)
        s = jnp.sin(ang) * rope[srow:srow + 1, :]
        if reps > 1:
            c = jnp.concatenate([c] * reps, axis=1)
            s = jnp.concatenate([s] * reps, axis=1)
        return c, s

    def rotary(z, c, s, first_row, half):
        w = z.shape[1]
        first = jnp.concatenate([rope[first_row:first_row + 1, :]] * (w // LANES), axis=1) > 0.5
        partner = jnp.where(first, pltpu.roll(z, w - half, 1), pltpu.roll(z, half, 1))
        return z * c + partner * s

    def head_norm(z, w_row):
        ms = _group_sum(z * z, g_ref) * (1.0 / HEAD_DIM)
        return z * lax.rsqrt(ms + RMS_EPS) * w_row

    def proj(c0, width):
        return jnp.dot(xb, w_ref[:, c0:c0 + width], preferred_element_type=F32)

    def shift(z, c0):
        width = z.shape[1]
        row = lax.broadcasted_iota(jnp.int32, z.shape, 0)
        prev = jnp.where(row == 0, carry_ref[0:1, c0:c0 + width], pltpu.roll(z, 1, 0))
        carry_ref[0:1, c0:c0 + width] = z[tm - 1:tm, :]
        return z + (prev - z) * mu_ref[:, c0:c0 + width]

    cq, sq = tables(0, 1, D_GROUP // LANES)
    q = rotary(head_norm(proj(_C_Q, D_GROUP), qnw_ref[...]), cq, sq, 2, HEAD_DIM // 8)
    q_ref[...] = q.astype(BF16)
    k = rotary(head_norm(proj(_C_K, D_GROUP), knw_ref[...]), cq, sq, 2, HEAD_DIM // 8)
    k_ref[...] = k.astype(BF16)
    v_ref[...] = proj(_C_V, D_GROUP).astype(BF16)

    rr_ref[...] = shift(proj(_C_RR, D_GROUP), 0)
    rk_ref[...] = shift(proj(_C_RK, D_GROUP), D_GROUP)
    rv_ref[...] = shift(proj(_C_RV, D_GROUP), 2 * D_GROUP)
    lora_ref[...] = shift(proj(_C_LORA, LANES), 3 * D_GROUP)

    ci, si = tables(3, 4, D_IDX // LANES)
    iq_ref[...] = rotary(proj(_C_IQ, D_IDX), ci, si, 5, IDX_DIM // 8)
    ck, sk = tables(6, 7, 1)
    sm_ref[...] = rotary(proj(_C_SM, LANES), ck, sk, 5, IDX_DIM // 8)


def _in_proj(x2, pos_f, n1w, w_packed, mu_packed, qnw, knw, gmat, rope, tm):
    t = x2.shape[0]
    full = lambda shape: pl.BlockSpec(shape, lambda i: (0,) * len(shape))
    row = lambda width: pl.BlockSpec((tm, width), lambda i: (i, 0))
    out_shapes = (
        jax.ShapeDtypeStruct((t, D_GROUP), BF16),
        jax.ShapeDtypeStruct((t, D_GROUP), BF16),
        jax.ShapeDtypeStruct((t, D_GROUP), BF16),
        jax.ShapeDtypeStruct((t, D_GROUP), F32),
        jax.ShapeDtypeStruct((t, D_GROUP), F32),
        jax.ShapeDtypeStruct((t, D_GROUP), F32),
        jax.ShapeDtypeStruct((t, D_IDX), F32),
        jax.ShapeDtypeStruct((t, LANES), F32),
        jax.ShapeDtypeStruct((t, LANES), F32),
    )
    return pl.pallas_call(
        _in_proj_kernel,
        grid=(t // tm,),
        in_specs=[row(D_MODEL), row(1), full((1, D_MODEL)), full((D_MODEL, D_IN_PACKED)),
                  full((1, D_SHIFT)), full((1, D_GROUP)), full((1, D_GROUP)),
                  full((D_GROUP, D_GROUP)), full((8, LANES))],
        out_specs=[row(D_GROUP)] * 6 + [row(D_IDX), row(LANES), row(LANES)],
        out_shape=out_shapes,
        scratch_shapes=[pltpu.VMEM((8, D_SHIFT), F32)],
        compiler_params=pltpu.CompilerParams(dimension_semantics=("arbitrary",),
                                             vmem_limit_bytes=VMEM_LIMIT),
        name="in_proj",
    )(x2, pos_f, n1w, w_packed, mu_packed, qnw, knw, gmat, rope)


RWKV_CHUNK = 64
PAIR = 2 * HEAD_DIM


def _rwkv_kernel(r_ref, k_ref, v_ref, lora_ref, w0_ref, w2_ref, a0_ref, a2_ref, g2_ref,
                 kk_ref, ka_ref, rk_ref, lnw_ref, lnb_ref, g_ref,
                 o_ref,
                 s_ref, ld_s, r_s, k2_s, b_s, kk_s, y_s):
    tm = r_ref.shape[0]
    n_chunks = tm // RWKV_CHUNK
    n_pairs = D_GROUP // PAIR
    c = RWKV_CHUNK

    @pl.when(pl.program_id(0) == 0)
    def _():
        s_ref[...] = jnp.zeros_like(s_ref)

    lora = lora_ref[...]
    r = r_ref[...]
    k = k_ref[...]
    v = v_ref[...]
    zarg = w0_ref[...] + _dotf(jnp.tanh(lora), w2_ref[...])
    sp = jnp.maximum(-zarg, 0.0) + jnp.log1p(jnp.exp(-jnp.abs(zarg)))
    ld_s[...] = -jnp.exp(-sp - 0.5)
    a = jax.nn.sigmoid(a0_ref[...] + _dotf(lora, a2_ref[...]))
    g = _dotf(jax.nn.sigmoid(lora), g2_ref[...])
    kk = k * kk_ref[...]
    kk = kk * lax.rsqrt(jnp.maximum(_group_sum(kk * kk, g_ref), 1e-24))
    k2 = k * (1.0 + (a - 1.0) * ka_ref[...])
    bonus = _group_sum(r * k2 * rk_ref[...], g_ref) * v
    r_s[...] = r
    k2_s[...] = k2
    kk_s[...] = kk
    b_s[...] = kk * a

    row = lax.broadcasted_iota(jnp.int32, (2 * c, 2 * c), 0)
    col = lax.broadcasted_iota(jnp.int32, (2 * c, 2 * c), 1)
    same_head = (row >= c) == (col >= c)
    strict = same_head & (col < row)
    incl = same_head & (col <= row)
    eye = (row == col).astype(F32)
    tri = (lax.broadcasted_iota(jnp.int32, (c, c), 1)
           <= lax.broadcasted_iota(jnp.int32, (c, c), 0)).astype(F32)
    lane = lax.broadcasted_iota(jnp.int32, (c, PAIR), 1)
    head0 = lane < HEAD_DIM

    def stack(z):
        return jnp.concatenate([jnp.where(head0, z, 0.0), jnp.where(head0, 0.0, z)], axis=0)

    def chunk_body(ci, carry):
        r0 = pl.multiple_of(ci * c, c)
        for p in range(n_pairs):
            cols = slice(p * PAIR, (p + 1) * PAIR)
            ld = ld_s[pl.ds(r0, c), cols]
            cum = _dotf(tri, ld)
            gam = jnp.exp(cum)
            inv = jnp.exp(-cum)
            gam_prev = jnp.exp(cum - ld)
            gam_end = gam[c - 1:c, :]
            rt = stack(r_s[pl.ds(r0, c), cols] * gam)
            kt_raw = k2_s[pl.ds(r0, c), cols] * inv
            bt_raw = b_s[pl.ds(r0, c), cols] * inv
            kt = stack(kt_raw)
            bt = stack(bt_raw)
            kp = stack(kk_s[pl.ds(r0, c), cols] * gam_prev)
            vs = stack(v_ref[pl.ds(r0, c), cols])
            b2 = jnp.concatenate([bt_raw, bt_raw], axis=0)
            k2u = jnp.concatenate([kt_raw, kt_raw], axis=0)
            a_bk = jnp.where(strict, _dot_nt(kp, b2), 0.0)
            a_kk = jnp.where(strict, _dot_nt(kp, k2u), 0.0)
            a_rb = jnp.where(incl, _dot_nt(rt, b2), 0.0)
            a_rk = jnp.where(incl, _dot_nt(rt, k2u), 0.0)
            n = -a_bk
            tinv = eye + n
            steps = int(np.log2(c)) - 1
            for _ in range(steps):
                n = _dot(n, n)
                tinv = tinv + _dot(tinv, n)
            w1 = _dot(tinv, kp)
            u2 = _dot(tinv, _dot(a_kk, vs))
            r2 = rt - _dot(a_rb, w1)
            y2 = _dot(a_rk, vs) - _dot(a_rb, u2)
            s0 = s_ref[p]
            ys = _dot_nt(r2, s0) + y2
            y_s[pl.ds(r0, c), cols] = ys[:c] + ys[c:]
            m = (eye - _dot_tn(w1, bt)) * gam_end
            z = (_dot_tn(vs, kt) - _dot_tn(u2, bt)) * gam_end
            s_ref[p] = _dot(s0, m) + z
        return carry

    lax.fori_loop(0, n_chunks, chunk_body, 0)

    y = y_s[...]
    mean = _group_sum(y, g_ref) * (1.0 / HEAD_DIM)
    yc = y - mean
    var = _group_sum(yc * yc, g_ref) * (1.0 / HEAD_DIM)
    yn = yc * lax.rsqrt(var + RWKV_GN_EPS) * lnw_ref[...] + lnb_ref[...]
    o_ref[...] = (yn + bonus) * g


def _rwkv(rr, rk, rv, lora, w0, w2p, a0, a2p, g2p, k_k, k_a, r_k, ln_w, ln_b, gsum, tm):
    t = rr.shape[0]
    full = lambda shape: pl.BlockSpec(shape, lambda i: (0,) * len(shape))
    row = lambda width: pl.BlockSpec((tm, width), lambda i: (i, 0))
    vec = full((1, D_GROUP))
    big = pltpu.VMEM((tm, D_GROUP), F32)
    return pl.pallas_call(
        _rwkv_kernel,
        grid=(t // tm,),
        in_specs=[row(D_GROUP), row(D_GROUP), row(D_GROUP), row(LANES),
                  vec, full((LANES, D_GROUP)), vec, full((LANES, D_GROUP)),
                  full((LANES, D_GROUP)), vec, vec, vec, vec, vec, full((D_GROUP, D_GROUP))],
        out_specs=row(D_GROUP),
        out_shape=jax.ShapeDtypeStruct((t, D_GROUP), F32),
        scratch_shapes=[pltpu.VMEM((D_GROUP // PAIR, PAIR, PAIR), F32),
                        big, big, big, big, big, big],
        compiler_params=pltpu.CompilerParams(dimension_semantics=("arbitrary",),
                                             vmem_limit_bytes=VMEM_LIMIT),
        name="rwkv",
    )(rr, rk, rv, lora, w0, w2p, a0, a2p, g2p, k_k, k_a, r_k, ln_w, ln_b, gsum)


DSA_QB = 256
DSA_KB = 512
M_INIT = -5e29


def _dsa_kernel(qi_ref, kj_ref, qt_ref, k_ref, vt_ref, iqt_ref, ik_ref, iwt_ref, o_ref,
                key_s, thr_s, m_s, l_s, acc_s, *, n_sel):
    qb = qt_ref.shape[2]
    kb = k_ref.shape[1]
    step = pl.program_id(0)
    qi = qi_ref[step]
    kj = kj_ref[step]
    q0 = qi * qb
    j_last = (q0 + qb - 1) // kb
    n_kc = j_last + 1
    idx_scale = float((IDX_DIM * N_IDX_HEADS) ** -0.5)

    def causal(jblk):
        s_pos = jblk * kb + lax.broadcasted_iota(jnp.int32, (kb, qb), 0)
        t_pos = q0 + lax.broadcasted_iota(jnp.int32, (kb, qb), 1)
        return s_pos <= t_pos

    @pl.when(kj == 0)
    def _():
        iqt = iqt_ref[...].astype(BF16)
        iwt = iwt_ref[...]

        def score_chunk(kc, carry):
            ik = ik_ref[pl.ds(pl.multiple_of(kc * kb, kb), kb), :]
            score = jnp.zeros((kb, qb), F32)
            for h in range(N_IDX_HEADS):
                d = jnp.dot(ik, iqt[h * IDX_DIM:(h + 1) * IDX_DIM, :],
                            preferred_element_type=F32)
                score = score + jnp.maximum(d, 0.0) * iwt[h:h + 1, :]
            score = jnp.where(causal(kc), score * idx_scale, -jnp.inf)
            bits = pltpu.bitcast(score, jnp.int32)
            key_s[kc] = jnp.where(bits < 0, bits ^ jnp.int32(0x7FFFFFFF), bits)
            return carry

        lax.fori_loop(0, n_kc, score_chunk, 0)

        def bit_step(b, cur):
            bit = lax.shift_left(jnp.int32(1), jnp.int32(31) - b)
            cand = (cur | bit) ^ jnp.int32(INT_MIN)

            def count_chunk(kc, acc):
                ind = jnp.where(key_s[kc] >= cand, 1.0, 0.0)
                return acc + jnp.sum(ind.reshape(kb // 8, 8, qb), axis=0)

            acc = lax.fori_loop(0, n_kc, count_chunk, jnp.zeros((8, qb), F32))
            cnt = jnp.sum(acc, axis=0, keepdims=True)
            return jnp.where(cnt >= float(n_sel), cur | bit, cur)

        cur = lax.fori_loop(0, 32, bit_step, jnp.zeros((1, qb), jnp.int32))
        thr_s[...] = cur ^ jnp.int32(INT_MIN)
        m_s[...] = jnp.full_like(m_s, M_INIT)
        l_s[...] = jnp.zeros_like(l_s)
        acc_s[...] = jnp.zeros_like(acc_s)

    mask = (key_s[kj] >= thr_s[...]) & causal(kj)
    for h in range(N_HEADS):
        qh = qt_ref[h] * jnp.asarray(HEAD_DIM ** -0.5, BF16)
        st = jnp.dot(k_ref[h], qh, preferred_element_type=F32)
        st = jnp.where(mask, st, NEG_BIG)
        m_old = m_s[h:h + 1, :]
        m_new = jnp.maximum(m_old, jnp.max(st, axis=0, keepdims=True))
        alpha = jnp.exp(m_old - m_new)
        p = jnp.exp(st - m_new)
        l_s[h:h + 1, :] = alpha * l_s[h:h + 1, :] + jnp.sum(p, axis=0, keepdims=True)
        acc_s[h] = alpha * acc_s[h] + jnp.dot(vt_ref[h], p.astype(BF16),
                                              preferred_element_type=F32)
        m_s[h:h + 1, :] = m_new

    @pl.when(kj == j_last)
    def _():
        for h in range(N_HEADS):
            o_ref[h] = acc_s[h] / l_s[h:h + 1, :]


def _dsa(qt, kh, vt, iqt, ik, iwt, n_sel):
    t = ik.shape[0]
    qb, kb = DSA_QB, DSA_KB
    nq, nk = t // qb, t // kb
    pairs = [(i, j) for i in range(nq) for j in range((i * qb + qb - 1) // kb + 1)]
    qi = jnp.asarray(np.array([p[0] for p in pairs], np.int32))
    kj = jnp.asarray(np.array([p[1] for p in pairs], np.int32))
    grid_spec = pltpu.PrefetchScalarGridSpec(
        num_scalar_prefetch=2,
        grid=(len(pairs),),
        in_specs=[pl.BlockSpec((N_HEADS, HEAD_DIM, qb), lambda s, qi, kj: (0, 0, qi[s])),
                  pl.BlockSpec((N_HEADS, kb, HEAD_DIM), lambda s, qi, kj: (0, kj[s], 0)),
                  pl.BlockSpec((N_HEADS, HEAD_DIM, kb), lambda s, qi, kj: (0, 0, kj[s])),
                  pl.BlockSpec((D_IDX, qb), lambda s, qi, kj: (0, qi[s])),
                  pl.BlockSpec((t, IDX_DIM), lambda s, qi, kj: (0, 0)),
                  pl.BlockSpec((N_IDX_HEADS, qb), lambda s, qi, kj: (0, qi[s]))],
        out_specs=pl.BlockSpec((N_HEADS, HEAD_DIM, qb), lambda s, qi, kj: (0, 0, qi[s])),
        scratch_shapes=[pltpu.VMEM((nk, kb, qb), jnp.int32),
                        pltpu.VMEM((1, qb), jnp.int32),
                        pltpu.VMEM((N_HEADS, qb), F32),
                        pltpu.VMEM((N_HEADS, qb), F32),
                        pltpu.VMEM((N_HEADS, HEAD_DIM, qb), F32)])
    return pl.pallas_call(
        functools.partial(_dsa_kernel, n_sel=n_sel),
        grid_spec=grid_spec,
        out_shape=jax.ShapeDtypeStruct((N_HEADS, HEAD_DIM, t), F32),
        compiler_params=pltpu.CompilerParams(dimension_semantics=("arbitrary",),
                                             vmem_limit_bytes=VMEM_LIMIT),
        name="dsa",
    )(qi, kj, qt, kh, vt, iqt, ik, iwt)


def _out_proj_kernel(x_ref, att_ref, rw_ref, wo_ref, n2w_ref, rwt_ref, rb_ref,
                     h_ref, xn_ref, gate_ref):
    acc = x_ref[...] + _dot(rw_ref[...], wo_ref[D_GROUP:, :])
    for h in range(N_HEADS):
        acc = acc + _dot(att_ref[h].T, wo_ref[h * HEAD_DIM:(h + 1) * HEAD_DIM, :])
    h_ref[...] = acc
    xn = acc * lax.rsqrt(jnp.mean(acc * acc, axis=-1, keepdims=True) + RMS_EPS) * n2w_ref[...]
    xn_ref[...] = xn.astype(BF16)
    logits = _dotf(xn, rwt_ref[...]) + rb_ref[...]
    lane = lax.broadcasted_iota(jnp.int32, logits.shape, 1)
    work = logits
    vals, hots = [], []
    for _ in range(TOP_K_EXPERTS):
        m = jnp.max(work, axis=1, keepdims=True)
        idx = jnp.min(jnp.where(work == m, lane, LANES), axis=1, keepdims=True)
        hot = lane == idx
        vals.append(m)
        hots.append(hot)
        work = jnp.where(hot, -jnp.inf, work)
    es = [jnp.exp(vv - vals[0]) for vv in vals]
    denom = es[0] + es[1] + es[2] + es[3]
    gates = jnp.zeros_like(logits)
    for e, hot in zip(es, hots):
        gates = gates + jnp.where(hot, e / denom, 0.0)
    gate_ref[...] = gates


def _out_proj(x2, att_t, rw, wo, n2w, rwt, rb, tm):
    t = x2.shape[0]
    full = lambda shape: pl.BlockSpec(shape, lambda i: (0,) * len(shape))
    row = lambda width: pl.BlockSpec((tm, width), lambda i: (i, 0))
    return pl.pallas_call(
        _out_proj_kernel,
        grid=(t // tm,),
        in_specs=[row(D_MODEL), pl.BlockSpec((N_HEADS, HEAD_DIM, tm), lambda i: (0, 0, i)),
                  row(D_GROUP), full((2 * D_GROUP, D_MODEL)), full((1, D_MODEL)),
                  full((D_MODEL, LANES)), full((1, LANES))],
        out_specs=[row(D_MODEL), row(D_MODEL), row(LANES)],
        out_shape=(jax.ShapeDtypeStruct((t, D_MODEL), F32),
                   jax.ShapeDtypeStruct((t, D_MODEL), BF16),
                   jax.ShapeDtypeStruct((t, LANES), F32)),
        compiler_params=pltpu.CompilerParams(dimension_semantics=("arbitrary",),
                                             vmem_limit_bytes=VMEM_LIMIT),
        name="out_proj",
    )(x2, att_t, rw, wo, n2w, rwt, rb)


MOE_ROWS = 256
MOE_FC = 512


def _moe_kernel(h_ref, xn_ref, gate_ref, wgu_ref, bgu_ref, wdn_ref, bdn_ref, o_ref):
    e = pl.program_id(1)
    tm = h_ref.shape[0]

    @pl.when(e == 0)
    def _():
        o_ref[...] = h_ref[...]

    def row_block(rbi, carry):
        r0 = pl.multiple_of(rbi * MOE_ROWS, MOE_ROWS)
        xb = xn_ref[pl.ds(r0, MOE_ROWS), :]
        y = jnp.zeros((MOE_ROWS, D_MODEL), F32)
        for fc in range(D_FF // MOE_FC):
            c0 = fc * MOE_FC
            hg = jnp.dot(xb, wgu_ref[0, :, c0:c0 + MOE_FC], preferred_element_type=F32) \
                + bgu_ref[0, :, c0:c0 + MOE_FC]
            hl = jnp.dot(xb, wgu_ref[0, :, D_FF + c0:D_FF + c0 + MOE_FC],
                         preferred_element_type=F32) + bgu_ref[0, :, D_FF + c0:D_FF + c0 + MOE_FC]
            glu = jnp.minimum(hg, SWIGLU_LIMIT)
            lin = jnp.clip(hl, -SWIGLU_LIMIT, SWIGLU_LIMIT)
            act = (lin + 1.0) * glu * jax.nn.sigmoid(SWIGLU_ALPHA * glu)
            y = y + jnp.dot(act.astype(BF16), wdn_ref[0, c0:c0 + MOE_FC, :],
                            preferred_element_type=F32)
        gates = gate_ref[pl.ds(r0, MOE_ROWS), :]
        lane = lax.broadcasted_iota(jnp.int32, gates.shape, 1)
        ge = jnp.sum(jnp.where(lane == e, gates, 0.0), axis=1, keepdims=True)
        o_ref[pl.ds(r0, MOE_ROWS), :] += (y + bdn_ref[0]) * ge
        return carry

    lax.fori_loop(0, tm // MOE_ROWS, row_block, 0)


def _moe(h1, xn2, gates, wgu, bgu, wdn, bdn, tm):
    t = h1.shape[0]
    row = lambda width: pl.BlockSpec((tm, width), lambda i, e: (i, 0))
    return pl.pallas_call(
        _moe_kernel,
        grid=(t // tm, N_EXPERTS),
        in_specs=[row(D_MODEL), row(D_MODEL), row(LANES),
                  pl.BlockSpec((1, D_MODEL, 2 * D_FF), lambda i, e: (e, 0, 0)),
                  pl.BlockSpec((1, 1, 2 * D_FF), lambda i, e: (e, 0, 0)),
                  pl.BlockSpec((1, D_FF, D_MODEL), lambda i, e: (e, 0, 0)),
                  pl.BlockSpec((1, 1, D_MODEL), lambda i, e: (e, 0, 0))],
        out_specs=row(D_MODEL),
        out_shape=jax.ShapeDtypeStruct((t, D_MODEL), F32),
        compiler_params=pltpu.CompilerParams(dimension_semantics=("arbitrary", "arbitrary"),
                                             vmem_limit_bytes=VMEM_LIMIT),
        name="moe",
    )(h1, xn2, gates, wgu, bgu, wdn, bdn)


def _rope_tables():
    lane = jnp.arange(LANES)

    def rows(period, rot):
        half = rot // 2
        inv_freq = ROPE_THETA ** (-jnp.arange(half, dtype=F32) / half)
        jm = lane % period
        freq = jnp.where(jm < rot, inv_freq[jm % half], 0.0)
        sign = jnp.where(jm < half, -1.0, jnp.where(jm < rot, 1.0, 0.0))
        first = (jm < half).astype(F32)
        return [freq, sign, first]

    idx_rows = rows(IDX_DIM, IDX_DIM // 4)
    only_key = (lane < IDX_DIM).astype(F32)
    return jnp.stack(rows(HEAD_DIM, HEAD_DIM // 4) + idx_rows
                     + [idx_rows[0] * only_key, idx_rows[1] * only_key]).astype(F32)


def _group_matrix():
    g = np.arange(D_GROUP) // HEAD_DIM
    return jnp.asarray((g[:, None] == g[None, :]).astype(np.float32), dtype=BF16)


def _pad_rows(w, r0, rows):
    return jnp.zeros((rows, w.shape[1]), w.dtype).at[r0:r0 + w.shape[0]].set(w)


def kernel(x, positions, norm1_w, w_in, q_norm_w, k_norm_w, rwkv_mu, rwkv_w0, rwkv_w2, rwkv_a0,
           rwkv_a2, rwkv_g2, rwkv_k_k, rwkv_k_a, rwkv_r_k, rwkv_ln_w, rwkv_ln_b, w_out, norm2_w,
           router_w, router_b, exp_w_gu, exp_b_gu, exp_w_down, exp_b_down):
    b, t, _ = x.shape
    assert b == 1 and w_in.shape[0] == 1, "single sequence, single layer"
    assert t % DSA_KB == 0 and t % DSA_QB == 0
    x2 = x[0]
    pos_f = positions[0].astype(F32)[:, None]
    n_sel = min(TOPK_MAX, t // 4)

    w = w_in[0]
    a0 = 3 * D_GROUP
    att_cols = a0 + D_IDX + IDX_DIM + N_IDX_HEADS
    w_att, w_rw = w[:, :att_cols], w[:, att_cols:]
    w_sm = jnp.zeros((D_MODEL, LANES), F32).at[:, :IDX_DIM + N_IDX_HEADS].set(w_att[:, a0 + D_IDX:])
    w_packed = jnp.concatenate(
        [w_att[:, :a0], w_rw[:, :a0], w_att[:, a0:a0 + D_IDX], w_sm, w_rw[:, a0:]],
        axis=1).astype(BF16)
    mu = rwkv_mu[0][None, :]
    tile8 = lambda z: jnp.tile(z, N_HEADS)[None, :]
    gsum = _group_matrix()

    tm = 256
    q, k, v, rr, rk, rv, iq, sm, lora = _in_proj(
        x2, pos_f, norm1_w, w_packed, mu, tile8(q_norm_w[0]), tile8(k_norm_w[0]), gsum,
        _rope_tables(), tm)

    vec = lambda z: z.reshape(1, D_GROUP)
    rw = _rwkv(rr, rk, rv, lora, vec(rwkv_w0[0]),
               _pad_rows(rwkv_w2[0], 0, LANES), vec(rwkv_a0[0]),
               _pad_rows(rwkv_a2[0], D_DECAY_LORA, LANES),
               _pad_rows(rwkv_g2[0], D_DECAY_LORA + D_AAA_LORA, LANES),
               vec(rwkv_k_k[0]), vec(rwkv_k_a[0]), vec(rwkv_r_k[0]), vec(rwkv_ln_w[0]),
               vec(rwkv_ln_b[0]), gsum, tm)

    heads = lambda z: z.reshape(t, N_HEADS, HEAD_DIM)
    att_t = _dsa(heads(q).transpose(1, 2, 0), heads(k).transpose(1, 0, 2),
                 heads(v).transpose(1, 2, 0), iq.T, sm[:, :IDX_DIM].astype(BF16),
                 sm[:, IDX_DIM:IDX_DIM + N_IDX_HEADS].T, n_sel)

    rwt = jnp.zeros((D_MODEL, LANES), F32).at[:, :N_EXPERTS].set(router_w[0])
    rb = jnp.full((1, LANES), NEG_BIG, F32).at[0, :N_EXPERTS].set(router_b[0])
    h1, xn2, gates = _out_proj(x2, att_t, rw, w_out[0].astype(BF16), norm2_w, rwt, rb, tm)

    out = _moe(h1, xn2, gates, exp_w_gu[0].astype(BF16), exp_b_gu[0][:, None, :],
               exp_w_down[0].astype(BF16), exp_b_down[0][:, None, :], min(1024, t))
    return out[None]
```

```python
import functools

import jax
import jax.numpy as jnp
import numpy as np
from jax import lax
from jax.experimental import pallas as pl
from jax.experimental.pallas import tpu as pltpu

F32 = jnp.float32
BF16 = jnp.bfloat16
HIGHEST = lax.Precision.HIGHEST

D_MODEL = 1024
HEAD_DIM = 64
N_HEADS = 8
D_GROUP = N_HEADS * HEAD_DIM
ROPE_THETA = 500000.0
N_IDX_HEADS = 8
IDX_DIM = 32
D_IDX = N_IDX_HEADS * IDX_DIM
TOPK_MAX = 256
D_DECAY_LORA = 32
D_AAA_LORA = 32
D_GATE_LORA = 64
RWKV_GN_EPS = 64e-5
N_EXPERTS = 32
TOP_K_EXPERTS = 4
D_FF = 1024
SWIGLU_LIMIT = 7.0
SWIGLU_ALPHA = 1.702
RMS_EPS = 1e-6

LANES = 128
VMEM_LIMIT = 56 * 1024 * 1024

NEG_BIG = -1e30
INT_MIN = -(2 ** 31)


def _dot(a, b):
    return jnp.dot(a.astype(BF16), b.astype(BF16), preferred_element_type=F32)


def _dotf(a, b):
    return jnp.dot(a, b, preferred_element_type=F32, precision=HIGHEST)


def _dot_nt(a, b):
    return lax.dot_general(a.astype(BF16), b.astype(BF16), (((1,), (1,)), ((), ())),
                           preferred_element_type=F32)


def _dot_tn(a, b):
    return lax.dot_general(a.astype(BF16), b.astype(BF16), (((0,), (0,)), ((), ())),
                           preferred_element_type=F32)


def _group_sum(z, g_ref):
    hi = z.astype(BF16)
    lo = (z - hi.astype(F32)).astype(BF16)
    g = g_ref[...]
    return (jnp.dot(hi, g, preferred_element_type=F32)
            + jnp.dot(lo, g, preferred_element_type=F32))


_C_Q, _C_K, _C_V, _C_RR, _C_RK, _C_RV = (i * D_GROUP for i in range(6))
_C_IQ = 6 * D_GROUP
_C_SM = _C_IQ + D_IDX
_C_LORA = _C_SM + LANES
D_IN_PACKED = _C_LORA + LANES
D_SHIFT = 3 * D_GROUP + LANES


def _in_proj_kernel(x_ref, pos_ref, n1w_ref, w_ref, mu_ref, qnw_ref, knw_ref, g_ref, rope_ref,
                    q_ref, k_ref, v_ref, rr_ref, rk_ref, rv_ref, iq_ref, sm_ref, lora_ref,
                    carry_ref):
    tm = x_ref.shape[0]

    @pl.when(pl.program_id(0) == 0)
    def _():
        carry_ref[...] = jnp.zeros_like(carry_ref)

    x = x_ref[...]
    xn = x * lax.rsqrt(jnp.mean(x * x, axis=-1, keepdims=True) + RMS_EPS) * n1w_ref[...]
    xb = xn.astype(BF16)
    pos = pos_ref[...]
    rope = rope_ref[...]

    def tables(frow, srow, reps):
        ang = pos * rope[frow:frow + 1, :]
        c = jnp.cos(ang)
        s = jnp.sin(ang) * rope[srow:srow + 1, :]
        if reps > 1:
            c = jnp.concatenate([c] * reps, axis=1)
            s = jnp.concatenate([s] * reps, axis=1)
        return c, s

    def rotary(z, c, s, first_row, half):
        w = z.shape[1]
        first = jnp.concatenate([rope[first_row:first_row + 1, :]] * (w // LANES), axis=1) > 0.5
        partner = jnp.where(first, pltpu.roll(z, w - half, 1), pltpu.roll(z, half, 1))
        return z * c + partner * s

    def head_norm(z, w_row):
        ms = _group_sum(z * z, g_ref) * (1.0 / HEAD_DIM)
        return z * lax.rsqrt(ms + RMS_EPS) * w_row

    def proj(c0, width):
        return jnp.dot(xb, w_ref[:, c0:c0 + width], preferred_element_type=F32)

    def shift(z, c0):
        width = z.shape[1]
        row = lax.broadcasted_iota(jnp.int32, z.shape, 0)
        prev = jnp.where(row == 0, carry_ref[0:1, c0:c0 + width], pltpu.roll(z, 1, 0))
        carry_ref[0:1, c0:c0 + width] = z[tm - 1:tm, :]
        return z + (prev - z) * mu_ref[:, c0:c0 + width]

    cq, sq = tables(0, 1, D_GROUP // LANES)
    q = rotary(head_norm(proj(_C_Q, D_GROUP), qnw_ref[...]), cq, sq, 2, HEAD_DIM // 8)
    q_ref[...] = q.astype(BF16)
    k = rotary(head_norm(proj(_C_K, D_GROUP), knw_ref[...]), cq, sq, 2, HEAD_DIM // 8)
    k_ref[...] = k.astype(BF16)
    v_ref[...] = proj(_C_V, D_GROUP).astype(BF16)

    rr_ref[...] = shift(proj(_C_RR, D_GROUP), 0)
    rk_ref[...] = shift(proj(_C_RK, D_GROUP), D_GROUP)
    rv_ref[...] = shift(proj(_C_RV, D_GROUP), 2 * D_GROUP)
    lora_ref[...] = shift(proj(_C_LORA, LANES), 3 * D_GROUP)

    ci, si = tables(3, 4, D_IDX // LANES)
    iq_ref[...] = rotary(proj(_C_IQ, D_IDX), ci, si, 5, IDX_DIM // 8)
    ck, sk = tables(6, 7, 1)
    sm_ref[...] = rotary(proj(_C_SM, LANES), ck, sk, 5, IDX_DIM // 8)


def _in_proj(x2, pos_f, n1w, w_packed, mu_packed, qnw, knw, gmat, rope, tm):
    t = x2.shape[0]
    full = lambda shape: pl.BlockSpec(shape, lambda i: (0,) * len(shape))
    row = lambda width: pl.BlockSpec((tm, width), lambda i: (i, 0))
    out_shapes = (
        jax.ShapeDtypeStruct((t, D_GROUP), BF16),
        jax.ShapeDtypeStruct((t, D_GROUP), BF16),
        jax.ShapeDtypeStruct((t, D_GROUP), BF16),
        jax.ShapeDtypeStruct((t, D_GROUP), F32),
        jax.ShapeDtypeStruct((t, D_GROUP), F32),
        jax.ShapeDtypeStruct((t, D_GROUP), F32),
        jax.ShapeDtypeStruct((t, D_IDX), F32),
        jax.ShapeDtypeStruct((t, LANES), F32),
        jax.ShapeDtypeStruct((t, LANES), F32),
    )
    return pl.pallas_call(
        _in_proj_kernel,
        grid=(t // tm,),
        in_specs=[row(D_MODEL), row(1), full((1, D_MODEL)), full((D_MODEL, D_IN_PACKED)),
                  full((1, D_SHIFT)), full((1, D_GROUP)), full((1, D_GROUP)),
                  full((D_GROUP, D_GROUP)), full((8, LANES))],
        out_specs=[row(D_GROUP)] * 6 + [row(D_IDX), row(LANES), row(LANES)],
        out_shape=out_shapes,
        scratch_shapes=[pltpu.VMEM((8, D_SHIFT), F32)],
        compiler_params=pltpu.CompilerParams(dimension_semantics=("arbitrary",),
                                             vmem_limit_bytes=VMEM_LIMIT),
        name="in_proj",
    )(x2, pos_f, n1w, w_packed, mu_packed, qnw, knw, gmat, rope)


RWKV_CHUNK = 64
PAIR = 2 * HEAD_DIM


def _rwkv_kernel(r_ref, k_ref, v_ref, lora_ref, w0_ref, w2_ref, a0_ref, a2_ref, g2_ref,
                 kk_ref, ka_ref, rk_ref, lnw_ref, lnb_ref, g_ref,
                 o_ref,
                 s_ref, ld_s, r_s, k2_s, b_s, kk_s, y_s):
    tm = r_ref.shape[0]
    n_chunks = tm // RWKV_CHUNK
    n_pairs = D_GROUP // PAIR
    c = RWKV_CHUNK

    @pl.when(pl.program_id(0) == 0)
    def _():
        s_ref[...] = jnp.zeros_like(s_ref)

    lora = lora_ref[...]
    r = r_ref[...]
    k = k_ref[...]
    v = v_ref[...]
    zarg = w0_ref[...] + _dotf(jnp.tanh(lora), w2_ref[...])
    sp = jnp.maximum(-zarg, 0.0) + jnp.log1p(jnp.exp(-jnp.abs(zarg)))
    ld_s[...] = -jnp.exp(-sp - 0.5)
    a = jax.nn.sigmoid(a0_ref[...] + _dotf(lora, a2_ref[...]))
    g = _dotf(jax.nn.sigmoid(lora), g2_ref[...])
    kk = k * kk_ref[...]
    kk = kk * lax.rsqrt(jnp.maximum(_group_sum(kk * kk, g_ref), 1e-24))
    k2 = k * (1.0 + (a - 1.0) * ka_ref[...])
    bonus = _group_sum(r * k2 * rk_ref[...], g_ref) * v
    r_s[...] = r
    k2_s[...] = k2
    kk_s[...] = kk
    b_s[...] = kk * a

    row = lax.broadcasted_iota(jnp.int32, (2 * c, 2 * c), 0)
    col = lax.broadcasted_iota(jnp.int32, (2 * c, 2 * c), 1)
    same_head = (row >= c) == (col >= c)
    strict = same_head & (col < row)
    incl = same_head & (col <= row)
    eye = (row == col).astype(F32)
    tri = (lax.broadcasted_iota(jnp.int32, (c, c), 1)
           <= lax.broadcasted_iota(jnp.int32, (c, c), 0)).astype(F32)
    lane = lax.broadcasted_iota(jnp.int32, (c, PAIR), 1)
    head0 = lane < HEAD_DIM

    def stack(z):
        return jnp.concatenate([jnp.where(head0, z, 0.0), jnp.where(head0, 0.0, z)], axis=0)

    def chunk_body(ci, carry):
        r0 = pl.multiple_of(ci * c, c)
        for p in range(n_pairs):
            cols = slice(p * PAIR, (p + 1) * PAIR)
            ld = ld_s[pl.ds(r0, c), cols]
            cum = _dotf(tri, ld)
            gam = jnp.exp(cum)
            inv = jnp.exp(-cum)
            gam_prev = jnp.exp(cum - ld)
            gam_end = gam[c - 1:c, :]
            rt = stack(r_s[pl.ds(r0, c), cols] * gam)
            kt_raw = k2_s[pl.ds(r0, c), cols] * inv
            bt_raw = b_s[pl.ds(r0, c), cols] * inv
            kt = stack(kt_raw)
            bt = stack(bt_raw)
            kp = stack(kk_s[pl.ds(r0, c), cols] * gam_prev)
            vs = stack(v_ref[pl.ds(r0, c), cols])
            b2 = jnp.concatenate([bt_raw, bt_raw], axis=0)
            k2u = jnp.concatenate([kt_raw, kt_raw], axis=0)
            a_bk = jnp.where(strict, _dot_nt(kp, b2), 0.0)
            a_kk = jnp.where(strict, _dot_nt(kp, k2u), 0.0)
            a_rb = jnp.where(incl, _dot_nt(rt, b2), 0.0)
            a_rk = jnp.where(incl, _dot_nt(rt, k2u), 0.0)
            n = -a_bk
            tinv = eye + n
            steps = int(np.log2(c)) - 1
            for _ in range(steps):
                n = _dot(n, n)
                tinv = tinv + _dot(tinv, n)
            w1 = _dot(tinv, kp)
            u2 = _dot(tinv, _dot(a_kk, vs))
            r2 = rt - _dot(a_rb, w1)
            y2 = _dot(a_rk, vs) - _dot(a_rb, u2)
            s0 = s_ref[p]
            ys = _dot_nt(r2, s0) + y2
            y_s[pl.ds(r0, c), cols] = ys[:c] + ys[c:]
            m = (eye - _dot_tn(w1, bt)) * gam_end
            z = (_dot_tn(vs, kt) - _dot_tn(u2, bt)) * gam_end
            s_ref[p] = _dot(s0, m) + z
        return carry

    lax.fori_loop(0, n_chunks, chunk_body, 0)

    y = y_s[...]
    mean = _group_sum(y, g_ref) * (1.0 / HEAD_DIM)
    yc = y - mean
    var = _group_sum(yc * yc, g_ref) * (1.0 / HEAD_DIM)
    yn = yc * lax.rsqrt(var + RWKV_GN_EPS) * lnw_ref[...] + lnb_ref[...]
    o_ref[...] = (yn + bonus) * g


def _rwkv(rr, rk, rv, lora, w0, w2p, a0, a2p, g2p, k_k, k_a, r_k, ln_w, ln_b, gsum, tm):
    t = rr.shape[0]
    full = lambda shape: pl.BlockSpec(shape, lambda i: (0,) * len(shape))
    row = lambda width: pl.BlockSpec((tm, width), lambda i: (i, 0))
    vec = full((1, D_GROUP))
    big = pltpu.VMEM((tm, D_GROUP), F32)
    return pl.pallas_call(
        _rwkv_kernel,
        grid=(t // tm,),
        in_specs=[row(D_GROUP), row(D_GROUP), row(D_GROUP), row(LANES),
                  vec, full((LANES, D_GROUP)), vec, full((LANES, D_GROUP)),
                  full((LANES, D_GROUP)), vec, vec, vec, vec, vec, full((D_GROUP, D_GROUP))],
        out_specs=row(D_GROUP),
        out_shape=jax.ShapeDtypeStruct((t, D_GROUP), F32),
        scratch_shapes=[pltpu.VMEM((D_GROUP // PAIR, PAIR, PAIR), F32),
                        big, big, big, big, big, big],
        compiler_params=pltpu.CompilerParams(dimension_semantics=("arbitrary",),
                                             vmem_limit_bytes=VMEM_LIMIT),
        name="rwkv",
    )(rr, rk, rv, lora, w0, w2p, a0, a2p, g2p, k_k, k_a, r_k, ln_w, ln_b, gsum)


DSA_QB = 256
DSA_KB = 512
M_INIT = -5e29
DSA_HEAD_GROUP = 4
DSA_ACC_ROWS = 32
DSA_V_ROWS = HEAD_DIM + 16


def _dsa_kernel(qi_ref, kj_ref, qt_ref, k_ref, vt_ref, iqt_ref, ik_ref, iwt_ref, o_ref,
                key_s, thr_s, m_s, acc_s, bdq_s, *, n_sel):
    qb = qt_ref.shape[2]
    kb = k_ref.shape[0]
    step = pl.program_id(0)
    qi = qi_ref[step]
    kj = kj_ref[step]
    q0 = qi * qb
    j_last = (q0 + qb - 1) // kb
    n_kc = j_last + 1
    idx_scale = float((IDX_DIM * N_IDX_HEADS) ** -0.5)

    def causal(jblk):
        s_pos = jblk * kb + lax.broadcasted_iota(jnp.int32, (kb, qb), 0)
        t_pos = q0 + lax.broadcasted_iota(jnp.int32, (kb, qb), 1)
        return s_pos <= t_pos

    @pl.when(kj == 0)
    def _():
        iqt = iqt_ref[...].astype(BF16)
        iwt = iwt_ref[...]

        def score_chunk(kc, carry):
            ik = ik_ref[pl.ds(pl.multiple_of(kc * kb, kb), kb), :]
            score = jnp.zeros((kb, qb), F32)
            for h in range(N_IDX_HEADS):
                d = jnp.dot(ik, iqt[h * IDX_DIM:(h + 1) * IDX_DIM, :],
                            preferred_element_type=F32)
                score = score + jnp.maximum(d, 0.0) * iwt[h:h + 1, :]
            score = jnp.where(causal(kc), score * idx_scale, -jnp.inf)
            bits = pltpu.bitcast(score, jnp.int32)
            key_s[kc] = jnp.where(bits < 0, bits ^ jnp.int32(0x7FFFFFFF), bits)
            return carry

        lax.fori_loop(0, n_kc, score_chunk, 0)

        def bit_step(b, cur):
            bit = lax.shift_left(jnp.int32(1), jnp.int32(31) - b)
            cand = (cur | bit) ^ jnp.int32(INT_MIN)

            def count_chunk(kc, acc):
                ind = jnp.where(key_s[kc] >= cand, 1.0, 0.0)
                return acc + jnp.sum(ind.reshape(kb // DSA_ACC_ROWS, DSA_ACC_ROWS, qb), axis=0)

            acc = lax.fori_loop(0, n_kc, count_chunk, jnp.zeros((DSA_ACC_ROWS, qb), F32))
            cnt = jnp.sum(acc, axis=0, keepdims=True)
            return jnp.where(cnt >= float(n_sel), cur | bit, cur)

        cur = lax.fori_loop(0, 32, bit_step, jnp.zeros((1, qb), jnp.int32))
        thr_s[...] = cur ^ jnp.int32(INT_MIN)
        m_s[...] = jnp.full_like(m_s, M_INIT)
        acc_s[...] = jnp.zeros_like(acc_s)
        bdq_s[...] = jnp.zeros_like(bdq_s)
        for h in range(N_HEADS):
            g, hh = divmod(h, DSA_HEAD_GROUP)
            bdq_s[g, hh * HEAD_DIM:(hh + 1) * HEAD_DIM, hh * qb:(hh + 1) * qb] = (
                qt_ref[h] * jnp.asarray(HEAD_DIM ** -0.5, BF16))

    mask = (key_s[kj] >= thr_s[...]) & causal(kj)
    hg = DSA_HEAD_GROUP
    for g in range(N_HEADS // hg):
        st_g = jnp.dot(k_ref[:, g * hg * HEAD_DIM:(g + 1) * hg * HEAD_DIM], bdq_s[g],
                       preferred_element_type=F32)
        for hh in range(hg):
            h = g * hg + hh
            st = jnp.where(mask, st_g[:, hh * qb:(hh + 1) * qb], NEG_BIG)
            m_old = m_s[h]
            part = jnp.max(st.reshape(kb // DSA_ACC_ROWS, DSA_ACC_ROWS, qb), axis=0)
            m_new = jnp.maximum(m_old, jnp.max(part, axis=0, keepdims=True))
            p = jnp.exp((st - m_new).astype(BF16))
            acc_s[h] = jnp.exp(m_old - m_new) * acc_s[h] + jnp.dot(
                vt_ref[h], p, preferred_element_type=F32)
            m_s[h] = m_new

    @pl.when(kj == j_last)
    def _():
        for h in range(N_HEADS):
            acc = acc_s[h]
            o_ref[h] = acc[:HEAD_DIM] / acc[HEAD_DIM:HEAD_DIM + 1]


def _dsa(qt, k, vt, iqt, ik, iwt, n_sel):
    t = ik.shape[0]
    qb, kb = DSA_QB, DSA_KB
    nq, nk = t // qb, t // kb
    pairs = [(i, j) for i in range(nq) for j in range((i * qb + qb - 1) // kb + 1)]
    qi = jnp.asarray(np.array([p[0] for p in pairs], np.int32))
    kj = jnp.asarray(np.array([p[1] for p in pairs], np.int32))
    grid_spec = pltpu.PrefetchScalarGridSpec(
        num_scalar_prefetch=2,
        grid=(len(pairs),),
        in_specs=[pl.BlockSpec((N_HEADS, HEAD_DIM, qb), lambda s, qi, kj: (0, 0, qi[s])),
                  pl.BlockSpec((kb, D_GROUP), lambda s, qi, kj: (kj[s], 0)),
                  pl.BlockSpec((N_HEADS, DSA_V_ROWS, kb), lambda s, qi, kj: (0, 0, kj[s])),
                  pl.BlockSpec((D_IDX, qb), lambda s, qi, kj: (0, qi[s])),
                  pl.BlockSpec((t, IDX_DIM), lambda s, qi, kj: (0, 0)),
                  pl.BlockSpec((N_IDX_HEADS, qb), lambda s, qi, kj: (0, qi[s]))],
        out_specs=pl.BlockSpec((N_HEADS, HEAD_DIM, qb), lambda s, qi, kj: (0, 0, qi[s])),
        scratch_shapes=[pltpu.VMEM((nk, kb, qb), jnp.int32),
                        pltpu.VMEM((1, qb), jnp.int32),
                        pltpu.VMEM((N_HEADS, 1, qb), F32),
                        pltpu.VMEM((N_HEADS, DSA_V_ROWS, qb), F32),
                        pltpu.VMEM((N_HEADS // DSA_HEAD_GROUP, DSA_HEAD_GROUP * HEAD_DIM,
                                    DSA_HEAD_GROUP * qb), BF16)])
    return pl.pallas_call(
        functools.partial(_dsa_kernel, n_sel=n_sel),
        grid_spec=grid_spec,
        out_shape=jax.ShapeDtypeStruct((N_HEADS, HEAD_DIM, t), F32),
        compiler_params=pltpu.CompilerParams(dimension_semantics=("arbitrary",),
                                             vmem_limit_bytes=VMEM_LIMIT),
        name="dsa",
    )(qi, kj, qt, k, vt, iqt, ik, iwt)


def _out_proj_kernel(x_ref, att_ref, rw_ref, wo_ref, n2w_ref, rwt_ref, rb_ref,
                     h_ref, xn_ref, gate_ref):
    acc = x_ref[...] + _dot(rw_ref[...], wo_ref[D_GROUP:, :])
    for h in range(N_HEADS):
        acc = acc + _dot(att_ref[h].T, wo_ref[h * HEAD_DIM:(h + 1) * HEAD_DIM, :])
    h_ref[...] = acc
    xn = acc * lax.rsqrt(jnp.mean(acc * acc, axis=-1, keepdims=True) + RMS_EPS) * n2w_ref[...]
    xn_ref[...] = xn.astype(BF16)
    logits = _dotf(xn, rwt_ref[...]) + rb_ref[...]
    lane = lax.broadcasted_iota(jnp.int32, logits.shape, 1)
    work = logits
    vals, hots = [], []
    for _ in range(TOP_K_EXPERTS):
        m = jnp.max(work, axis=1, keepdims=True)
        idx = jnp.min(jnp.where(work == m, lane, LANES), axis=1, keepdims=True)
        hot = lane == idx
        vals.append(m)
        hots.append(hot)
        work = jnp.where(hot, -jnp.inf, work)
    es = [jnp.exp(vv - vals[0]) for vv in vals]
    denom = es[0] + es[1] + es[2] + es[3]
    gates = jnp.zeros_like(logits)
    for e, hot in zip(es, hots):
        gates = gates + jnp.where(hot, e / denom, 0.0)
    gate_ref[...] = gates


def _out_proj(x2, att_t, rw, wo, n2w, rwt, rb, tm):
    t = x2.shape[0]
    full = lambda shape: pl.BlockSpec(shape, lambda i: (0,) * len(shape))
    row = lambda width: pl.BlockSpec((tm, width), lambda i: (i, 0))
    return pl.pallas_call(
        _out_proj_kernel,
        grid=(t // tm,),
        in_specs=[row(D_MODEL), pl.BlockSpec((N_HEADS, HEAD_DIM, tm), lambda i: (0, 0, i)),
                  row(D_GROUP), full((2 * D_GROUP, D_MODEL)), full((1, D_MODEL)),
                  full((D_MODEL, LANES)), full((1, LANES))],
        out_specs=[row(D_MODEL), row(D_MODEL), row(LANES)],
        out_shape=(jax.ShapeDtypeStruct((t, D_MODEL), F32),
                   jax.ShapeDtypeStruct((t, D_MODEL), BF16),
                   jax.ShapeDtypeStruct((t, LANES), F32)),
        compiler_params=pltpu.CompilerParams(dimension_semantics=("arbitrary",),
                                             vmem_limit_bytes=VMEM_LIMIT),
        name="out_proj",
    )(x2, att_t, rw, wo, n2w, rwt, rb)


MOE_ROWS = 256
MOE_FC = 512


def _moe_kernel(h_ref, xn_ref, gate_ref, wgu_ref, bgu_ref, wdn_ref, bdn_ref, o_ref):
    e = pl.program_id(1)
    tm = h_ref.shape[0]

    @pl.when(e == 0)
    def _():
        o_ref[...] = h_ref[...]

    def row_block(rbi, carry):
        r0 = pl.multiple_of(rbi * MOE_ROWS, MOE_ROWS)
        xb = xn_ref[pl.ds(r0, MOE_ROWS), :]
        y = jnp.zeros((MOE_ROWS, D_MODEL), F32)
        for fc in range(D_FF // MOE_FC):
            c0 = fc * MOE_FC
            hg = jnp.dot(xb, wgu_ref[0, :, c0:c0 + MOE_FC], preferred_element_type=F32) \
                + bgu_ref[0, :, c0:c0 + MOE_FC]
            hl = jnp.dot(xb, wgu_ref[0, :, D_FF + c0:D_FF + c0 + MOE_FC],
                         preferred_element_type=F32) + bgu_ref[0, :, D_FF + c0:D_FF + c0 + MOE_FC]
            glu = jnp.minimum(hg, SWIGLU_LIMIT)
            lin = jnp.clip(hl, -SWIGLU_LIMIT, SWIGLU_LIMIT)
            act = (lin + 1.0) * glu * jax.nn.sigmoid(SWIGLU_ALPHA * glu)
            y = y + jnp.dot(act.astype(BF16), wdn_ref[0, c0:c0 + MOE_FC, :],
                            preferred_element_type=F32)
        gates = gate_ref[pl.ds(r0, MOE_ROWS), :]
        lane = lax.broadcasted_iota(jnp.int32, gates.shape, 1)
        ge = jnp.sum(jnp.where(lane == e, gates, 0.0), axis=1, keepdims=True)
        o_ref[pl.ds(r0, MOE_ROWS), :] += (y + bdn_ref[0]) * ge
        return carry

    lax.fori_loop(0, tm // MOE_ROWS, row_block, 0)


def _moe(h1, xn2, gates, wgu, bgu, wdn, bdn, tm):
    t = h1.shape[0]
    row = lambda width: pl.BlockSpec((tm, width), lambda i, e: (i, 0))
    return pl.pallas_call(
        _moe_kernel,
        grid=(t // tm, N_EXPERTS),
        in_specs=[row(D_MODEL), row(D_MODEL), row(LANES),
                  pl.BlockSpec((1, D_MODEL, 2 * D_FF), lambda i, e: (e, 0, 0)),
                  pl.BlockSpec((1, 1, 2 * D_FF), lambda i, e: (e, 0, 0)),
                  pl.BlockSpec((1, D_FF, D_MODEL), lambda i, e: (e, 0, 0)),
                  pl.BlockSpec((1, 1, D_MODEL), lambda i, e: (e, 0, 0))],
        out_specs=row(D_MODEL),
        out_shape=jax.ShapeDtypeStruct((t, D_MODEL), F32),
        compiler_params=pltpu.CompilerParams(dimension_semantics=("arbitrary", "arbitrary"),
                                             vmem_limit_bytes=VMEM_LIMIT),
        name="moe",
    )(h1, xn2, gates, wgu, bgu, wdn, bdn)


def _rope_tables():
    lane = jnp.arange(LANES)

    def rows(period, rot):
        half = rot // 2
        inv_freq = ROPE_THETA ** (-jnp.arange(half, dtype=F32) / half)
        jm = lane % period
        freq = jnp.where(jm < rot, inv_freq[jm % half], 0.0)
        sign = jnp.where(jm < half, -1.0, jnp.where(jm < rot, 1.0, 0.0))
        first = (jm < half).astype(F32)
        return [freq, sign, first]

    idx_rows = rows(IDX_DIM, IDX_DIM // 4)
    only_key = (lane < IDX_DIM).astype(F32)
    return jnp.stack(rows(HEAD_DIM, HEAD_DIM // 4) + idx_rows
                     + [idx_rows[0] * only_key, idx_rows[1] * only_key]).astype(F32)


def _group_matrix():
    g = np.arange(D_GROUP) // HEAD_DIM
    return jnp.asarray((g[:, None] == g[None, :]).astype(np.float32), dtype=BF16)


def _pad_rows(w, r0, rows):
    return jnp.zeros((rows, w.shape[1]), w.dtype).at[r0:r0 + w.shape[0]].set(w)


def kernel(x, positions, norm1_w, w_in, q_norm_w, k_norm_w, rwkv_mu, rwkv_w0, rwkv_w2, rwkv_a0,
           rwkv_a2, rwkv_g2, rwkv_k_k, rwkv_k_a, rwkv_r_k, rwkv_ln_w, rwkv_ln_b, w_out, norm2_w,
           router_w, router_b, exp_w_gu, exp_b_gu, exp_w_down, exp_b_down):
    b, t, _ = x.shape
    assert b == 1 and w_in.shape[0] == 1, "single sequence, single layer"
    assert t % DSA_KB == 0 and t % DSA_QB == 0
    x2 = x[0]
    pos_f = positions[0].astype(F32)[:, None]
    n_sel = min(TOPK_MAX, t // 4)

    w = w_in[0]
    a0 = 3 * D_GROUP
    att_cols = a0 + D_IDX + IDX_DIM + N_IDX_HEADS
    w_att, w_rw = w[:, :att_cols], w[:, att_cols:]
    w_sm = jnp.zeros((D_MODEL, LANES), F32).at[:, :IDX_DIM + N_IDX_HEADS].set(w_att[:, a0 + D_IDX:])
    w_packed = jnp.concatenate(
        [w_att[:, :a0], w_rw[:, :a0], w_att[:, a0:a0 + D_IDX], w_sm, w_rw[:, a0:]],
        axis=1).astype(BF16)
    mu = rwkv_mu[0][None, :]
    tile8 = lambda z: jnp.tile(z, N_HEADS)[None, :]
    gsum = _group_matrix()

    tm = 256
    q, k, v, rr, rk, rv, iq, sm, lora = _in_proj(
        x2, pos_f, norm1_w, w_packed, mu, tile8(q_norm_w[0]), tile8(k_norm_w[0]), gsum,
        _rope_tables(), tm)

    vec = lambda z: z.reshape(1, D_GROUP)
    rw = _rwkv(rr, rk, rv, lora, vec(rwkv_w0[0]),
               _pad_rows(rwkv_w2[0], 0, LANES), vec(rwkv_a0[0]),
               _pad_rows(rwkv_a2[0], D_DECAY_LORA, LANES),
               _pad_rows(rwkv_g2[0], D_DECAY_LORA + D_AAA_LORA, LANES),
               vec(rwkv_k_k[0]), vec(rwkv_k_a[0]), vec(rwkv_r_k[0]), vec(rwkv_ln_w[0]),
               vec(rwkv_ln_b[0]), gsum, tm)

    heads = lambda z: z.reshape(t, N_HEADS, HEAD_DIM)
    vt = jnp.concatenate([heads(v).transpose(1, 2, 0),
                          jnp.ones((N_HEADS, DSA_V_ROWS - HEAD_DIM, t), BF16)], axis=1)
    att_t = _dsa(heads(q).transpose(1, 2, 0), k, vt,
                 iq.T, sm[:, :IDX_DIM].astype(BF16),
                 sm[:, IDX_DIM:IDX_DIM + N_IDX_HEADS].T, n_sel)

    rwt = jnp.zeros((D_MODEL, LANES), F32).at[:, :N_EXPERTS].set(router_w[0])
    rb = jnp.full((1, LANES), NEG_BIG, F32).at[0, :N_EXPERTS].set(router_b[0])
    h1, xn2, gates = _out_proj(x2, att_t, rw, w_out[0].astype(BF16), norm2_w, rwt, rb, tm)

    out = _moe(h1, xn2, gates, exp_w_gu[0].astype(BF16), exp_b_gu[0][:, None, :],
               exp_w_down[0].astype(BF16), exp_b_down[0][:, None, :], min(1024, t))
    return out[None]
```

```python
import functools

import jax
import jax.numpy as jnp
import numpy as np
from jax import lax
from jax.experimental import pallas as pl
from jax.experimental.pallas import tpu as pltpu

F32 = jnp.float32
BF16 = jnp.bfloat16
HIGHEST = lax.Precision.HIGHEST

D_MODEL = 1024
HEAD_DIM = 64
N_HEADS = 8
D_GROUP = N_HEADS * HEAD_DIM
ROPE_THETA = 500000.0
N_IDX_HEADS = 8
IDX_DIM = 32
D_IDX = N_IDX_HEADS * IDX_DIM
TOPK_MAX = 256
D_DECAY_LORA = 32
D_AAA_LORA = 32
D_GATE_LORA = 64
RWKV_GN_EPS = 64e-5
N_EXPERTS = 32
TOP_K_EXPERTS = 4
D_FF = 1024
SWIGLU_LIMIT = 7.0
SWIGLU_ALPHA = 1.702
RMS_EPS = 1e-6

LANES = 128
VMEM_LIMIT = 56 * 1024 * 1024

NEG_BIG = -1e30
INT_MIN = -(2 ** 31)


def _dot(a, b):
    return jnp.dot(a.astype(BF16), b.astype(BF16), preferred_element_type=F32)


def _dotf(a, b):
    return jnp.dot(a, b, preferred_element_type=F32, precision=HIGHEST)


def _dot_nt(a, b):
    return lax.dot_general(a.astype(BF16), b.astype(BF16), (((1,), (1,)), ((), ())),
                           preferred_element_type=F32)


def _dot_tn(a, b):
    return lax.dot_general(a.astype(BF16), b.astype(BF16), (((0,), (0,)), ((), ())),
                           preferred_element_type=F32)


def _group_sum(z, g_ref):
    hi = z.astype(BF16)
    lo = (z - hi.astype(F32)).astype(BF16)
    g = g_ref[...]
    return (jnp.dot(hi, g, preferred_element_type=F32)
            + jnp.dot(lo, g, preferred_element_type=F32))


_C_Q, _C_K, _C_V, _C_RR, _C_RK, _C_RV = (i * D_GROUP for i in range(6))
_C_IQ = 6 * D_GROUP
_C_SM = _C_IQ + D_IDX
_C_LORA = _C_SM + LANES
D_IN_PACKED = _C_LORA + LANES
D_SHIFT = 3 * D_GROUP + LANES


def _in_proj_kernel(x_ref, pos_ref, n1w_ref, w_ref, mu_ref, qnw_ref, knw_ref, g_ref, rope_ref,
                    q_ref, k_ref, v_ref, rr_ref, rk_ref, rv_ref, iq_ref, sm_ref, lora_ref,
                    carry_ref):
    tm = x_ref.shape[0]

    @pl.when(pl.program_id(0) == 0)
    def _():
        carry_ref[...] = jnp.zeros_like(carry_ref)

    x = x_ref[...]
    xn = x * lax.rsqrt(jnp.mean(x * x, axis=-1, keepdims=True) + RMS_EPS) * n1w_ref[...]
    xb = xn.astype(BF16)
    pos = pos_ref[...]
    rope = rope_ref[...]

    def tables(frow, srow, reps):
        ang = pos * rope[frow:frow + 1, :]
        c = jnp.cos(ang)
        s = jnp.sin(ang) * rope[srow:srow + 1, :]
        if reps > 1:
            c = jnp.concatenate([c] * reps, axis=1)
            s = jnp.concatenate([s] * reps, axis=1)
        return c, s

    def rotary(z, c, s, first_row, half):
        w = z.shape[1]
        first = jnp.concatenate([rope[first_row:first_row + 1, :]] * (w // LANES), axis=1) > 0.5
        partner = jnp.where(first, pltpu.roll(z, w - half, 1), pltpu.roll(z, half, 1))
        return z * c + partner * s

    def head_norm(z, w_row):
        ms = _group_sum(z * z, g_ref) * (1.0 / HEAD_DIM)
        return z * lax.rsqrt(ms + RMS_EPS) * w_row

    def proj(c0, width):
        return jnp.dot(xb, w_ref[:, c0:c0 + width], preferred_element_type=F32)

    def shift(z, c0):
        width = z.shape[1]
        row = lax.broadcasted_iota(jnp.int32, z.shape, 0)
        prev = jnp.where(row == 0, carry_ref[0:1, c0:c0 + width], pltpu.roll(z, 1, 0))
        carry_ref[0:1, c0:c0 + width] = z[tm - 1:tm, :]
        return z + (prev - z) * mu_ref[:, c0:c0 + width]

    cq, sq = tables(0, 1, D_GROUP // LANES)
    q = rotary(head_norm(proj(_C_Q, D_GROUP), qnw_ref[...]), cq, sq, 2, HEAD_DIM // 8)
    q_ref[...] = q.astype(BF16)
    k = rotary(head_norm(proj(_C_K, D_GROUP), knw_ref[...]), cq, sq, 2, HEAD_DIM // 8)
    k_ref[...] = k.astype(BF16)
    v_ref[...] = proj(_C_V, D_GROUP).astype(BF16)

    rr_ref[...] = shift(proj(_C_RR, D_GROUP), 0)
    rk_ref[...] = shift(proj(_C_RK, D_GROUP), D_GROUP)
    rv_ref[...] = shift(proj(_C_RV, D_GROUP), 2 * D_GROUP)
    lora_ref[...] = shift(proj(_C_LORA, LANES), 3 * D_GROUP)

    ci, si = tables(3, 4, D_IDX // LANES)
    iq_ref[...] = rotary(proj(_C_IQ, D_IDX), ci, si, 5, IDX_DIM // 8)
    ck, sk = tables(6, 7, 1)
    sm_ref[...] = rotary(proj(_C_SM, LANES), ck, sk, 5, IDX_DIM // 8)


def _in_proj(x2, pos_f, n1w, w_packed, mu_packed, qnw, knw, gmat, rope, tm):
    t = x2.shape[0]
    full = lambda shape: pl.BlockSpec(shape, lambda i: (0,) * len(shape))
    row = lambda width: pl.BlockSpec((tm, width), lambda i: (i, 0))
    out_shapes = (
        jax.ShapeDtypeStruct((t, D_GROUP), BF16),
        jax.ShapeDtypeStruct((t, D_GROUP), BF16),
        jax.ShapeDtypeStruct((t, D_GROUP), BF16),
        jax.ShapeDtypeStruct((t, D_GROUP), F32),
        jax.ShapeDtypeStruct((t, D_GROUP), F32),
        jax.ShapeDtypeStruct((t, D_GROUP), F32),
        jax.ShapeDtypeStruct((t, D_IDX), F32),
        jax.ShapeDtypeStruct((t, LANES), F32),
        jax.ShapeDtypeStruct((t, LANES), F32),
    )
    return pl.pallas_call(
        _in_proj_kernel,
        grid=(t // tm,),
        in_specs=[row(D_MODEL), row(1), full((1, D_MODEL)), full((D_MODEL, D_IN_PACKED)),
                  full((1, D_SHIFT)), full((1, D_GROUP)), full((1, D_GROUP)),
                  full((D_GROUP, D_GROUP)), full((8, LANES))],
        out_specs=[row(D_GROUP)] * 6 + [row(D_IDX), row(LANES), row(LANES)],
        out_shape=out_shapes,
        scratch_shapes=[pltpu.VMEM((8, D_SHIFT), F32)],
        compiler_params=pltpu.CompilerParams(dimension_semantics=("arbitrary",),
                                             vmem_limit_bytes=VMEM_LIMIT),
        name="in_proj",
    )(x2, pos_f, n1w, w_packed, mu_packed, qnw, knw, gmat, rope)


RWKV_CHUNK = 64
PAIR = 2 * HEAD_DIM


def _rwkv_kernel(r_ref, k_ref, v_ref, lora_ref, w0_ref, w2_ref, a0_ref, a2_ref, g2_ref,
                 kk_ref, ka_ref, rk_ref, lnw_ref, lnb_ref, g_ref,
                 o_ref,
                 s_ref, ld_s, r_s, k2_s, b_s, kk_s, y_s):
    tm = r_ref.shape[0]
    n_chunks = tm // RWKV_CHUNK
    n_pairs = D_GROUP // PAIR
    c = RWKV_CHUNK

    @pl.when(pl.program_id(0) == 0)
    def _():
        s_ref[...] = jnp.zeros_like(s_ref)

    lora = lora_ref[...]
    r = r_ref[...]
    k = k_ref[...]
    v = v_ref[...]
    zarg = w0_ref[...] + _dotf(jnp.tanh(lora), w2_ref[...])
    sp = jnp.maximum(-zarg, 0.0) + jnp.log1p(jnp.exp(-jnp.abs(zarg)))
    ld_s[...] = -jnp.exp(-sp - 0.5)
    a = jax.nn.sigmoid(a0_ref[...] + _dotf(lora, a2_ref[...]))
    g = _dotf(jax.nn.sigmoid(lora), g2_ref[...])
    kk = k * kk_ref[...]
    kk = kk * lax.rsqrt(jnp.maximum(_group_sum(kk * kk, g_ref), 1e-24))
    k2 = k * (1.0 + (a - 1.0) * ka_ref[...])
    bonus = _group_sum(r * k2 * rk_ref[...], g_ref) * v
    r_s[...] = r
    k2_s[...] = k2
    kk_s[...] = kk
    b_s[...] = kk * a

    row = lax.broadcasted_iota(jnp.int32, (2 * c, 2 * c), 0)
    col = lax.broadcasted_iota(jnp.int32, (2 * c, 2 * c), 1)
    same_head = (row >= c) == (col >= c)
    strict = same_head & (col < row)
    incl = same_head & (col <= row)
    eye = (row == col).astype(F32)
    tri = (lax.broadcasted_iota(jnp.int32, (c, c), 1)
           <= lax.broadcasted_iota(jnp.int32, (c, c), 0)).astype(F32)
    lane = lax.broadcasted_iota(jnp.int32, (c, PAIR), 1)
    head0 = lane < HEAD_DIM

    def stack(z):
        return jnp.concatenate([jnp.where(head0, z, 0.0), jnp.where(head0, 0.0, z)], axis=0)

    def chunk_body(ci, carry):
        r0 = pl.multiple_of(ci * c, c)
        for p in range(n_pairs):
            cols = slice(p * PAIR, (p + 1) * PAIR)
            ld = ld_s[pl.ds(r0, c), cols]
            cum = _dotf(tri, ld)
            gam = jnp.exp(cum)
            inv = jnp.exp(-cum)
            gam_prev = jnp.exp(cum - ld)
            gam_end = gam[c - 1:c, :]
            rt = stack(r_s[pl.ds(r0, c), cols] * gam)
            kt_raw = k2_s[pl.ds(r0, c), cols] * inv
            bt_raw = b_s[pl.ds(r0, c), cols] * inv
            kt = stack(kt_raw)
            bt = stack(bt_raw)
            kp = stack(kk_s[pl.ds(r0, c), cols] * gam_prev)
            vs = stack(v_ref[pl.ds(r0, c), cols])
            b2 = jnp.concatenate([bt_raw, bt_raw], axis=0)
            k2u = jnp.concatenate([kt_raw, kt_raw], axis=0)
            a_bk = jnp.where(strict, _dot_nt(kp, b2), 0.0)
            a_kk = jnp.where(strict, _dot_nt(kp, k2u), 0.0)
            a_rb = jnp.where(incl, _dot_nt(rt, b2), 0.0)
            a_rk = jnp.where(incl, _dot_nt(rt, k2u), 0.0)
            n = -a_bk
            tinv = eye + n
            steps = int(np.log2(c)) - 1
            for _ in range(steps):
                n = _dot(n, n)
                tinv = tinv + _dot(tinv, n)
            w1 = _dot(tinv, kp)
            u2 = _dot(tinv, _dot(a_kk, vs))
            r2 = rt - _dot(a_rb, w1)
            y2 = _dot(a_rk, vs) - _dot(a_rb, u2)
            s0 = s_ref[p]
            ys = _dot_nt(r2, s0) + y2
            y_s[pl.ds(r0, c), cols] = ys[:c] + ys[c:]
            m = (eye - _dot_tn(w1, bt)) * gam_end
            z = (_dot_tn(vs, kt) - _dot_tn(u2, bt)) * gam_end
            s_ref[p] = _dot(s0, m) + z
        return carry

    lax.fori_loop(0, n_chunks, chunk_body, 0)

    y = y_s[...]
    mean = _group_sum(y, g_ref) * (1.0 / HEAD_DIM)
    yc = y - mean
    var = _group_sum(yc * yc, g_ref) * (1.0 / HEAD_DIM)
    yn = yc * lax.rsqrt(var + RWKV_GN_EPS) * lnw_ref[...] + lnb_ref[...]
    o_ref[...] = (yn + bonus) * g


def _rwkv(rr, rk, rv, lora, w0, w2p, a0, a2p, g2p, k_k, k_a, r_k, ln_w, ln_b, gsum, tm):
    t = rr.shape[0]
    full = lambda shape: pl.BlockSpec(shape, lambda i: (0,) * len(shape))
    row = lambda width: pl.BlockSpec((tm, width), lambda i: (i, 0))
    vec = full((1, D_GROUP))
    big = pltpu.VMEM((tm, D_GROUP), F32)
    return pl.pallas_call(
        _rwkv_kernel,
        grid=(t // tm,),
        in_specs=[row(D_GROUP), row(D_GROUP), row(D_GROUP), row(LANES),
                  vec, full((LANES, D_GROUP)), vec, full((LANES, D_GROUP)),
                  full((LANES, D_GROUP)), vec, vec, vec, vec, vec, full((D_GROUP, D_GROUP))],
        out_specs=row(D_GROUP),
        out_shape=jax.ShapeDtypeStruct((t, D_GROUP), F32),
        scratch_shapes=[pltpu.VMEM((D_GROUP // PAIR, PAIR, PAIR), F32),
                        big, big, big, big, big, big],
        compiler_params=pltpu.CompilerParams(dimension_semantics=("arbitrary",),
                                             vmem_limit_bytes=VMEM_LIMIT),
        name="rwkv",
    )(rr, rk, rv, lora, w0, w2p, a0, a2p, g2p, k_k, k_a, r_k, ln_w, ln_b, gsum)


DSA_QB = 256
DSA_KB = 512
M_INIT = -5e29
DSA_HEAD_GROUP = 4
DSA_ACC_ROWS = 32
DSA_V_ROWS = HEAD_DIM + 16


def _dsa_kernel(qi_ref, kj_ref, qt_ref, k_ref, vt_ref, iqt_ref, ik_ref, iwt_ref, o_ref,
                key_s, thr_s, m_s, acc_s, bdq_s, *, n_sel):
    qb = qt_ref.shape[2]
    kb = k_ref.shape[0]
    step = pl.program_id(0)
    qi = qi_ref[step]
    kj = kj_ref[step]
    q0 = qi * qb
    j_last = (q0 + qb - 1) // kb
    n_kc = j_last + 1
    idx_scale = float((IDX_DIM * N_IDX_HEADS) ** -0.5)

    def causal(jblk):
        s_pos = jblk * kb + lax.broadcasted_iota(jnp.int32, (kb, qb), 0)
        t_pos = q0 + lax.broadcasted_iota(jnp.int32, (kb, qb), 1)
        return s_pos <= t_pos

    @pl.when(kj == 0)
    def _():
        iqt = iqt_ref[...].astype(BF16)
        iwt = iwt_ref[...]

        def score_chunk(kc, carry):
            ik = ik_ref[pl.ds(pl.multiple_of(kc * kb, kb), kb), :]
            score = jnp.zeros((kb, qb), F32)
            for h in range(N_IDX_HEADS):
                d = jnp.dot(ik, iqt[h * IDX_DIM:(h + 1) * IDX_DIM, :],
                            preferred_element_type=F32)
                score = score + jnp.maximum(d, 0.0) * iwt[h:h + 1, :]
            score = jnp.where(causal(kc), score * idx_scale, -jnp.inf)
            bits = pltpu.bitcast(score, jnp.int32)
            key_s[kc] = jnp.where(bits < 0, bits ^ jnp.int32(0x7FFFFFFF), bits)
            return carry

        lax.fori_loop(0, n_kc, score_chunk, 0)

        def bit_step(b, cur):
            bit = lax.shift_left(jnp.int32(1), jnp.int32(31) - b)
            cand = (cur | bit) ^ jnp.int32(INT_MIN)

            def count_chunk(kc, acc):
                ind = jnp.where(key_s[kc] >= cand, 1.0, 0.0)
                return acc + jnp.sum(ind.reshape(kb // DSA_ACC_ROWS, DSA_ACC_ROWS, qb), axis=0)

            acc = lax.fori_loop(0, n_kc, count_chunk, jnp.zeros((DSA_ACC_ROWS, qb), F32))
            cnt = jnp.sum(acc, axis=0, keepdims=True)
            return jnp.where(cnt >= float(n_sel), cur | bit, cur)

        cur = lax.fori_loop(0, 32, bit_step, jnp.zeros((1, qb), jnp.int32))
        thr_s[...] = cur ^ jnp.int32(INT_MIN)
        m_s[...] = jnp.full_like(m_s, M_INIT)
        acc_s[...] = jnp.zeros_like(acc_s)
        bdq_s[...] = jnp.zeros_like(bdq_s)
        for h in range(N_HEADS):
            g, hh = divmod(h, DSA_HEAD_GROUP)
            bdq_s[g, hh * HEAD_DIM:(hh + 1) * HEAD_DIM, hh * qb:(hh + 1) * qb] = (
                qt_ref[h] * jnp.asarray(HEAD_DIM ** -0.5, BF16))

    mask = (key_s[kj] >= thr_s[...]) & causal(kj)
    hg = DSA_HEAD_GROUP
    for g in range(N_HEADS // hg):
        st_g = jnp.dot(k_ref[:, g * hg * HEAD_DIM:(g + 1) * hg * HEAD_DIM], bdq_s[g],
                       preferred_element_type=F32)
        for hh in range(hg):
            h = g * hg + hh
            st = jnp.where(mask, st_g[:, hh * qb:(hh + 1) * qb], NEG_BIG)
            m_old = m_s[h]
            part = jnp.max(st.reshape(kb // DSA_ACC_ROWS, DSA_ACC_ROWS, qb), axis=0)
            m_new = jnp.maximum(m_old, jnp.max(part, axis=0, keepdims=True))
            p = jnp.exp((st - m_new).astype(BF16))
            acc_s[h] = jnp.exp(m_old - m_new) * acc_s[h] + jnp.dot(
                vt_ref[h], p, preferred_element_type=F32)
            m_s[h] = m_new

    @pl.when(kj == j_last)
    def _():
        for h in range(N_HEADS):
            acc = acc_s[h]
            o_ref[h] = acc[:HEAD_DIM] / acc[HEAD_DIM:HEAD_DIM + 1]


def _dsa(qt, k, vt, iqt, ik, iwt, n_sel):
    t = ik.shape[0]
    qb, kb = DSA_QB, DSA_KB
    nq, nk = t // qb, t // kb
    pairs = [(i, j) for i in range(nq) for j in range((i * qb + qb - 1) // kb + 1)]
    qi = jnp.asarray(np.array([p[0] for p in pairs], np.int32))
    kj = jnp.asarray(np.array([p[1] for p in pairs], np.int32))
    grid_spec = pltpu.PrefetchScalarGridSpec(
        num_scalar_prefetch=2,
        grid=(len(pairs),),
        in_specs=[pl.BlockSpec((N_HEADS, HEAD_DIM, qb), lambda s, qi, kj: (0, 0, qi[s])),
                  pl.BlockSpec((kb, D_GROUP), lambda s, qi, kj: (kj[s], 0)),
                  pl.BlockSpec((N_HEADS, DSA_V_ROWS, kb), lambda s, qi, kj: (0, 0, kj[s])),
                  pl.BlockSpec((D_IDX, qb), lambda s, qi, kj: (0, qi[s])),
                  pl.BlockSpec((t, IDX_DIM), lambda s, qi, kj: (0, 0)),
                  pl.BlockSpec((N_IDX_HEADS, qb), lambda s, qi, kj: (0, qi[s]))],
        out_specs=pl.BlockSpec((N_HEADS, HEAD_DIM, qb), lambda s, qi, kj: (0, 0, qi[s])),
        scratch_shapes=[pltpu.VMEM((nk, kb, qb), jnp.int32),
                        pltpu.VMEM((1, qb), jnp.int32),
                        pltpu.VMEM((N_HEADS, 1, qb), F32),
                        pltpu.VMEM((N_HEADS, DSA_V_ROWS, qb), F32),
                        pltpu.VMEM((N_HEADS // DSA_HEAD_GROUP, DSA_HEAD_GROUP * HEAD_DIM,
                                    DSA_HEAD_GROUP * qb), BF16)])
    return pl.pallas_call(
        functools.partial(_dsa_kernel, n_sel=n_sel),
        grid_spec=grid_spec,
        out_shape=jax.ShapeDtypeStruct((N_HEADS, HEAD_DIM, t), F32),
        compiler_params=pltpu.CompilerParams(dimension_semantics=("arbitrary",),
                                             vmem_limit_bytes=VMEM_LIMIT),
        name="dsa",
    )(qi, kj, qt, k, vt, iqt, ik, iwt)


def _out_proj_kernel(x_ref, att_ref, rw_ref, wo_ref, n2w_ref, rwt_ref, rb_ref,
                     h_ref, xn_ref, gate_ref):
    acc = x_ref[...] + _dot(rw_ref[...], wo_ref[D_GROUP:, :])
    for h in range(N_HEADS):
        acc = acc + _dot(att_ref[h].T, wo_ref[h * HEAD_DIM:(h + 1) * HEAD_DIM, :])
    h_ref[...] = acc
    xn = acc * lax.rsqrt(jnp.mean(acc * acc, axis=-1, keepdims=True) + RMS_EPS) * n2w_ref[...]
    xn_ref[...] = xn.astype(BF16)
    logits = _dotf(xn, rwt_ref[...]) + rb_ref[...]
    lane = lax.broadcasted_iota(jnp.int32, logits.shape, 1)
    work = logits
    vals, hots = [], []
    for _ in range(TOP_K_EXPERTS):
        m = jnp.max(work, axis=1, keepdims=True)
        idx = jnp.min(jnp.where(work == m, lane, LANES), axis=1, keepdims=True)
        hot = lane == idx
        vals.append(m)
        hots.append(hot)
        work = jnp.where(hot, -jnp.inf, work)
    es = [jnp.exp(vv - vals[0]) for vv in vals]
    denom = es[0] + es[1] + es[2] + es[3]
    gates = jnp.zeros_like(logits)
    for e, hot in zip(es, hots):
        gates = gates + jnp.where(hot, e / denom, 0.0)
    gate_ref[...] = gates


def _out_proj(x2, att_t, rw, wo, n2w, rwt, rb, tm):
    t = x2.shape[0]
    full = lambda shape: pl.BlockSpec(shape, lambda i: (0,) * len(shape))
    row = lambda width: pl.BlockSpec((tm, width), lambda i: (i, 0))
    return pl.pallas_call(
        _out_proj_kernel,
        grid=(t // tm,),
        in_specs=[row(D_MODEL), pl.BlockSpec((N_HEADS, HEAD_DIM, tm), lambda i: (0, 0, i)),
                  row(D_GROUP), full((2 * D_GROUP, D_MODEL)), full((1, D_MODEL)),
                  full((D_MODEL, LANES)), full((1, LANES))],
        out_specs=[row(D_MODEL), row(D_MODEL), row(LANES)],
        out_shape=(jax.ShapeDtypeStruct((t, D_MODEL), F32),
                   jax.ShapeDtypeStruct((t, D_MODEL), BF16),
                   jax.ShapeDtypeStruct((t, LANES), F32)),
        compiler_params=pltpu.CompilerParams(dimension_semantics=("arbitrary",),
                                             vmem_limit_bytes=VMEM_LIMIT),
        name="out_proj",
    )(x2, att_t, rw, wo, n2w, rwt, rb)


MOE_TM = 1024
MOE_RB = 144
MOE_KPAD = 256
MOE_FC = 512


def _route_kernel(gate_ref, rank_ref, rankt_ref, cnt_ref):
    tm = gate_ref.shape[0]
    hot = gate_ref[...] > 0.0
    ind = jnp.where(hot, 1.0, 0.0)
    before = (lax.broadcasted_iota(jnp.int32, (tm, tm), 1)
              < lax.broadcasted_iota(jnp.int32, (tm, tm), 0))
    rank = jnp.dot(jnp.where(before, 1.0, 0.0).astype(BF16), ind.astype(BF16),
                   preferred_element_type=F32)
    rank = jnp.where(hot, rank, -1.0)
    rank_ref[...] = rank
    rankt_ref[...] = rank.T
    cnt = jnp.sum(ind, axis=0, keepdims=True)
    cnt_ref[...] = jnp.broadcast_to(cnt[None], cnt_ref.shape)


def _route(gates, tm):
    t = gates.shape[0]
    return pl.pallas_call(
        _route_kernel,
        grid=(t // tm,),
        in_specs=[pl.BlockSpec((tm, LANES), lambda i: (i, 0))],
        out_specs=[pl.BlockSpec((tm, LANES), lambda i: (i, 0)),
                   pl.BlockSpec((LANES, tm), lambda i: (0, i)),
                   pl.BlockSpec((1, 8, LANES), lambda i: (i, 0, 0))],
        out_shape=(jax.ShapeDtypeStruct((t, LANES), F32),
                   jax.ShapeDtypeStruct((LANES, t), F32),
                   jax.ShapeDtypeStruct((t // tm, 8, LANES), F32)),
        compiler_params=pltpu.CompilerParams(dimension_semantics=("arbitrary",),
                                             vmem_limit_bytes=VMEM_LIMIT),
        name="route",
    )(gates)


def _moe_kernel(cnt_ref, h_ref, xn_ref, gate_ref, rank_ref, rankt_ref, wgu_ref, bgu_ref,
                wdn_ref, bdn_ref, o_ref):
    i = pl.program_id(0)
    e = pl.program_id(1)
    tm = h_ref.shape[0]

    @pl.when(e == 0)
    def _():
        o_ref[...] = h_ref[...]

    n_rows = cnt_ref[i * N_EXPERTS + e]
    n_blocks = (n_rows + MOE_RB - 1) // MOE_RB
    pick = lax.broadcasted_iota(jnp.int32, (tm, LANES), 1) == e
    rank_col = jnp.sum(jnp.where(pick, rank_ref[...], 0.0), axis=1, keepdims=True)
    gate_col = jnp.sum(jnp.where(pick, gate_ref[...], 0.0), axis=1, keepdims=True)
    rank_row = rankt_ref[pl.ds(e, 1), :]

    def row_block(bi, carry):
        r0 = (bi * MOE_RB).astype(F32)
        rowid = lax.broadcasted_iota(jnp.int32, (MOE_RB, tm), 0).astype(F32) + r0
        sel = jnp.where(rank_row == rowid, 1.0, 0.0).astype(BF16)
        xb = jnp.dot(sel, xn_ref[...], preferred_element_type=F32).astype(BF16)
        y = jnp.zeros((MOE_RB, D_MODEL), F32)
        for fc in range(D_FF // MOE_FC):
            c0 = fc * MOE_FC
            hg = jnp.dot(xb, wgu_ref[0, :, c0:c0 + MOE_FC], preferred_element_type=F32) \
                + bgu_ref[0, :, c0:c0 + MOE_FC]
            hl = jnp.dot(xb, wgu_ref[0, :, D_FF + c0:D_FF + c0 + MOE_FC],
                         preferred_element_type=F32) + bgu_ref[0, :, D_FF + c0:D_FF + c0 + MOE_FC]
            glu = jnp.minimum(hg, SWIGLU_LIMIT)
            lin = jnp.clip(hl, -SWIGLU_LIMIT, SWIGLU_LIMIT)
            act = (lin + 1.0) * glu * jax.nn.sigmoid(SWIGLU_ALPHA * glu)
            y = y + jnp.dot(act.astype(BF16), wdn_ref[0, c0:c0 + MOE_FC, :],
                            preferred_element_type=F32)
        y = (y + bdn_ref[0]).astype(BF16)
        ypad = jnp.concatenate([y, jnp.zeros((MOE_KPAD - MOE_RB, D_MODEL), BF16)], axis=0)
        colid = lax.broadcasted_iota(jnp.int32, (tm, MOE_KPAD), 1).astype(F32) + r0
        put = jnp.where(rank_col == colid, 1.0, 0.0).astype(BF16)
        o_ref[...] += jnp.dot(put, ypad, preferred_element_type=F32) * gate_col
        return carry

    lax.fori_loop(0, n_blocks, row_block, 0)


def _moe(counts, h1, xn2, gates, rank, rank_t, wgu, bgu, wdn, bdn, tm):
    t = h1.shape[0]
    row = lambda width: pl.BlockSpec((tm, width), lambda i, e, c: (i, 0))
    grid_spec = pltpu.PrefetchScalarGridSpec(
        num_scalar_prefetch=1,
        grid=(t // tm, N_EXPERTS),
        in_specs=[row(D_MODEL), row(D_MODEL), row(LANES), row(LANES),
                  pl.BlockSpec((LANES, tm), lambda i, e, c: (0, i)),
                  pl.BlockSpec((1, D_MODEL, 2 * D_FF), lambda i, e, c: (e, 0, 0)),
                  pl.BlockSpec((1, 1, 2 * D_FF), lambda i, e, c: (e, 0, 0)),
                  pl.BlockSpec((1, D_FF, D_MODEL), lambda i, e, c: (e, 0, 0)),
                  pl.BlockSpec((1, 1, D_MODEL), lambda i, e, c: (e, 0, 0))],
        out_specs=row(D_MODEL))
    return pl.pallas_call(
        _moe_kernel,
        grid_spec=grid_spec,
        out_shape=jax.ShapeDtypeStruct((t, D_MODEL), F32),
        compiler_params=pltpu.CompilerParams(dimension_semantics=("arbitrary", "arbitrary"),
                                             vmem_limit_bytes=VMEM_LIMIT),
        name="moe",
    )(counts, h1, xn2, gates, rank, rank_t, wgu, bgu, wdn, bdn)


def _rope_tables():
    lane = jnp.arange(LANES)

    def rows(period, rot):
        half = rot // 2
        inv_freq = ROPE_THETA ** (-jnp.arange(half, dtype=F32) / half)
        jm = lane % period
        freq = jnp.where(jm < rot, inv_freq[jm % half], 0.0)
        sign = jnp.where(jm < half, -1.0, jnp.where(jm < rot, 1.0, 0.0))
        first = (jm < half).astype(F32)
        return [freq, sign, first]

    idx_rows = rows(IDX_DIM, IDX_DIM // 4)
    only_key = (lane < IDX_DIM).astype(F32)
    return jnp.stack(rows(HEAD_DIM, HEAD_DIM // 4) + idx_rows
                     + [idx_rows[0] * only_key, idx_rows[1] * only_key]).astype(F32)


def _group_matrix():
    g = np.arange(D_GROUP) // HEAD_DIM
    return jnp.asarray((g[:, None] == g[None, :]).astype(np.float32), dtype=BF16)


def _pad_rows(w, r0, rows):
    return jnp.zeros((rows, w.shape[1]), w.dtype).at[r0:r0 + w.shape[0]].set(w)


def kernel(x, positions, norm1_w, w_in, q_norm_w, k_norm_w, rwkv_mu, rwkv_w0, rwkv_w2, rwkv_a0,
           rwkv_a2, rwkv_g2, rwkv_k_k, rwkv_k_a, rwkv_r_k, rwkv_ln_w, rwkv_ln_b, w_out, norm2_w,
           router_w, router_b, exp_w_gu, exp_b_gu, exp_w_down, exp_b_down):
    b, t, _ = x.shape
    assert b == 1 and w_in.shape[0] == 1, "single sequence, single layer"
    assert t % DSA_KB == 0 and t % DSA_QB == 0 and t % MOE_TM == 0
    x2 = x[0]
    pos_f = positions[0].astype(F32)[:, None]
    n_sel = min(TOPK_MAX, t // 4)

    w = w_in[0]
    a0 = 3 * D_GROUP
    att_cols = a0 + D_IDX + IDX_DIM + N_IDX_HEADS
    w_att, w_rw = w[:, :att_cols], w[:, att_cols:]
    w_sm = jnp.zeros((D_MODEL, LANES), F32).at[:, :IDX_DIM + N_IDX_HEADS].set(w_att[:, a0 + D_IDX:])
    w_packed = jnp.concatenate(
        [w_att[:, :a0], w_rw[:, :a0], w_att[:, a0:a0 + D_IDX], w_sm, w_rw[:, a0:]],
        axis=1).astype(BF16)
    mu = rwkv_mu[0][None, :]
    tile8 = lambda z: jnp.tile(z, N_HEADS)[None, :]
    gsum = _group_matrix()

    tm = 256
    q, k, v, rr, rk, rv, iq, sm, lora = _in_proj(
        x2, pos_f, norm1_w, w_packed, mu, tile8(q_norm_w[0]), tile8(k_norm_w[0]), gsum,
        _rope_tables(), tm)

    vec = lambda z: z.reshape(1, D_GROUP)
    rw = _rwkv(rr, rk, rv, lora, vec(rwkv_w0[0]),
               _pad_rows(rwkv_w2[0], 0, LANES), vec(rwkv_a0[0]),
               _pad_rows(rwkv_a2[0], D_DECAY_LORA, LANES),
               _pad_rows(rwkv_g2[0], D_DECAY_LORA + D_AAA_LORA, LANES),
               vec(rwkv_k_k[0]), vec(rwkv_k_a[0]), vec(rwkv_r_k[0]), vec(rwkv_ln_w[0]),
               vec(rwkv_ln_b[0]), gsum, tm)

    heads = lambda z: z.reshape(t, N_HEADS, HEAD_DIM)
    vt = jnp.concatenate([heads(v).transpose(1, 2, 0),
                          jnp.ones((N_HEADS, DSA_V_ROWS - HEAD_DIM, t), BF16)], axis=1)
    att_t = _dsa(heads(q).transpose(1, 2, 0), k, vt,
                 iq.T, sm[:, :IDX_DIM].astype(BF16),
                 sm[:, IDX_DIM:IDX_DIM + N_IDX_HEADS].T, n_sel)

    rwt = jnp.zeros((D_MODEL, LANES), F32).at[:, :N_EXPERTS].set(router_w[0])
    rb = jnp.full((1, LANES), NEG_BIG, F32).at[0, :N_EXPERTS].set(router_b[0])
    h1, xn2, gates = _out_proj(x2, att_t, rw, w_out[0].astype(BF16), norm2_w, rwt, rb, tm)

    rank, rank_t, cnt = _route(gates, MOE_TM)
    counts = cnt[:, 0, :N_EXPERTS].astype(jnp.int32).reshape(-1)
    out = _moe(counts, h1, xn2, gates, rank, rank_t, exp_w_gu[0].astype(BF16),
               exp_b_gu[0][:, None, :], exp_w_down[0].astype(BF16), exp_b_down[0][:, None, :],
               MOE_TM)
    return out[None]
```

```python
import functools

import jax
import jax.numpy as jnp
import numpy as np
from jax import lax
from jax.experimental import pallas as pl
from jax.experimental.pallas import tpu as pltpu

F32 = jnp.float32
BF16 = jnp.bfloat16
HIGHEST = lax.Precision.HIGHEST

D_MODEL = 1024
HEAD_DIM = 64
N_HEADS = 8
D_GROUP = N_HEADS * HEAD_DIM
ROPE_THETA = 500000.0
N_IDX_HEADS = 8
IDX_DIM = 32
D_IDX = N_IDX_HEADS * IDX_DIM
TOPK_MAX = 256
D_DECAY_LORA = 32
D_AAA_LORA = 32
D_GATE_LORA = 64
RWKV_GN_EPS = 64e-5
N_EXPERTS = 32
TOP_K_EXPERTS = 4
D_FF = 1024
SWIGLU_LIMIT = 7.0
SWIGLU_ALPHA = 1.702
RMS_EPS = 1e-6

LANES = 128
VMEM_LIMIT = 56 * 1024 * 1024

NEG_BIG = -1e30
INT_MIN = -(2 ** 31)


def _dot(a, b):
    return jnp.dot(a.astype(BF16), b.astype(BF16), preferred_element_type=F32)


def _dotf(a, b):
    return jnp.dot(a, b, preferred_element_type=F32, precision=HIGHEST)


def _dot_nt(a, b):
    return lax.dot_general(a.astype(BF16), b.astype(BF16), (((1,), (1,)), ((), ())),
                           preferred_element_type=F32)


def _dot_tn(a, b):
    return lax.dot_general(a.astype(BF16), b.astype(BF16), (((0,), (0,)), ((), ())),
                           preferred_element_type=F32)


def _group_sum(z, g_ref):
    hi = z.astype(BF16)
    lo = (z - hi.astype(F32)).astype(BF16)
    g = g_ref[...]
    return (jnp.dot(hi, g, preferred_element_type=F32)
            + jnp.dot(lo, g, preferred_element_type=F32))


_C_Q, _C_K, _C_V, _C_RR, _C_RK, _C_RV = (i * D_GROUP for i in range(6))
_C_IQ = 6 * D_GROUP
_C_SM = _C_IQ + D_IDX
_C_LORA = _C_SM + LANES
D_IN_PACKED = _C_LORA + LANES
D_SHIFT = 3 * D_GROUP + LANES


def _in_proj_kernel(x_ref, pos_ref, n1w_ref, w_ref, mu_ref, qnw_ref, knw_ref, g_ref, rope_ref,
                    q_ref, k_ref, v_ref, rr_ref, rk_ref, rv_ref, iq_ref, sm_ref, lora_ref,
                    carry_ref):
    tm = x_ref.shape[0]

    @pl.when(pl.program_id(0) == 0)
    def _():
        carry_ref[...] = jnp.zeros_like(carry_ref)

    x = x_ref[...]
    xn = x * lax.rsqrt(jnp.mean(x * x, axis=-1, keepdims=True) + RMS_EPS) * n1w_ref[...]
    xb = xn.astype(BF16)
    pos = pos_ref[...]
    rope = rope_ref[...]

    def tables(frow, srow, reps):
        ang = pos * rope[frow:frow + 1, :]
        c = jnp.cos(ang)
        s = jnp.sin(ang) * rope[srow:srow + 1, :]
        if reps > 1:
            c = jnp.concatenate([c] * reps, axis=1)
            s = jnp.concatenate([s] * reps, axis=1)
        return c, s

    def rotary(z, c, s, first_row, half):
        w = z.shape[1]
        first = jnp.concatenate([rope[first_row:first_row + 1, :]] * (w // LANES), axis=1) > 0.5
        partner = jnp.where(first, pltpu.roll(z, w - half, 1), pltpu.roll(z, half, 1))
        return z * c + partner * s

    def head_norm(z, w_row):
        ms = _group_sum(z * z, g_ref) * (1.0 / HEAD_DIM)
        return z * lax.rsqrt(ms + RMS_EPS) * w_row

    def proj(c0, width):
        return jnp.dot(xb, w_ref[:, c0:c0 + width], preferred_element_type=F32)

    def shift(z, c0):
        width = z.shape[1]
        row = lax.broadcasted_iota(jnp.int32, z.shape, 0)
        prev = jnp.where(row == 0, carry_ref[0:1, c0:c0 + width], pltpu.roll(z, 1, 0))
        carry_ref[0:1, c0:c0 + width] = z[tm - 1:tm, :]
        return z + (prev - z) * mu_ref[:, c0:c0 + width]

    cq, sq = tables(0, 1, D_GROUP // LANES)
    q = rotary(head_norm(proj(_C_Q, D_GROUP), qnw_ref[...]), cq, sq, 2, HEAD_DIM // 8)
    q_ref[...] = q.astype(BF16)
    k = rotary(head_norm(proj(_C_K, D_GROUP), knw_ref[...]), cq, sq, 2, HEAD_DIM // 8)
    k_ref[...] = k.astype(BF16)
    v_ref[...] = proj(_C_V, D_GROUP).astype(BF16)

    rr_ref[...] = shift(proj(_C_RR, D_GROUP), 0)
    rk_ref[...] = shift(proj(_C_RK, D_GROUP), D_GROUP)
    rv_ref[...] = shift(proj(_C_RV, D_GROUP), 2 * D_GROUP)
    lora_ref[...] = shift(proj(_C_LORA, LANES), 3 * D_GROUP)

    ci, si = tables(3, 4, D_IDX // LANES)
    iq_ref[...] = rotary(proj(_C_IQ, D_IDX), ci, si, 5, IDX_DIM // 8)
    ck, sk = tables(6, 7, 1)
    sm_ref[...] = rotary(proj(_C_SM, LANES), ck, sk, 5, IDX_DIM // 8)


def _in_proj(x2, pos_f, n1w, w_packed, mu_packed, qnw, knw, gmat, rope, tm):
    t = x2.shape[0]
    full = lambda shape: pl.BlockSpec(shape, lambda i: (0,) * len(shape))
    row = lambda width: pl.BlockSpec((tm, width), lambda i: (i, 0))
    out_shapes = (
        jax.ShapeDtypeStruct((t, D_GROUP), BF16),
        jax.ShapeDtypeStruct((t, D_GROUP), BF16),
        jax.ShapeDtypeStruct((t, D_GROUP), BF16),
        jax.ShapeDtypeStruct((t, D_GROUP), F32),
        jax.ShapeDtypeStruct((t, D_GROUP), F32),
        jax.ShapeDtypeStruct((t, D_GROUP), F32),
        jax.ShapeDtypeStruct((t, D_IDX), F32),
        jax.ShapeDtypeStruct((t, LANES), F32),
        jax.ShapeDtypeStruct((t, LANES), F32),
    )
    return pl.pallas_call(
        _in_proj_kernel,
        grid=(t // tm,),
        in_specs=[row(D_MODEL), row(1), full((1, D_MODEL)), full((D_MODEL, D_IN_PACKED)),
                  full((1, D_SHIFT)), full((1, D_GROUP)), full((1, D_GROUP)),
                  full((D_GROUP, D_GROUP)), full((8, LANES))],
        out_specs=[row(D_GROUP)] * 6 + [row(D_IDX), row(LANES), row(LANES)],
        out_shape=out_shapes,
        scratch_shapes=[pltpu.VMEM((8, D_SHIFT), F32)],
        compiler_params=pltpu.CompilerParams(dimension_semantics=("arbitrary",),
                                             vmem_limit_bytes=VMEM_LIMIT),
        name="in_proj",
    )(x2, pos_f, n1w, w_packed, mu_packed, qnw, knw, gmat, rope)


RWKV_CHUNK = 64
PAIR = 2 * HEAD_DIM


def _rwkv_kernel(r_ref, k_ref, v_ref, lora_ref, w0_ref, w2_ref, a0_ref, a2_ref, g2_ref,
                 kk_ref, ka_ref, rk_ref, lnw_ref, lnb_ref, g_ref,
                 o_ref,
                 s_ref, ld_s, r_s, k2_s, b_s, kk_s, y_s):
    tm = r_ref.shape[0]
    n_chunks = tm // RWKV_CHUNK
    n_pairs = D_GROUP // PAIR
    c = RWKV_CHUNK

    @pl.when(pl.program_id(0) == 0)
    def _():
        s_ref[...] = jnp.zeros_like(s_ref)

    lora = lora_ref[...]
    r = r_ref[...]
    k = k_ref[...]
    v = v_ref[...]
    zarg = w0_ref[...] + _dotf(jnp.tanh(lora), w2_ref[...])
    sp = jnp.maximum(-zarg, 0.0) + jnp.log1p(jnp.exp(-jnp.abs(zarg)))
    ld_s[...] = -jnp.exp(-sp - 0.5)
    a = jax.nn.sigmoid(a0_ref[...] + _dotf(lora, a2_ref[...]))
    g = _dotf(jax.nn.sigmoid(lora), g2_ref[...])
    kk = k * kk_ref[...]
    kk = kk * lax.rsqrt(jnp.maximum(_group_sum(kk * kk, g_ref), 1e-24))
    k2 = k * (1.0 + (a - 1.0) * ka_ref[...])
    bonus = _group_sum(r * k2 * rk_ref[...], g_ref) * v
    r_s[...] = r
    k2_s[...] = k2
    kk_s[...] = kk
    b_s[...] = kk * a

    row = lax.broadcasted_iota(jnp.int32, (2 * c, 2 * c), 0)
    col = lax.broadcasted_iota(jnp.int32, (2 * c, 2 * c), 1)
    same_head = (row >= c) == (col >= c)
    strict = same_head & (col < row)
    incl = same_head & (col <= row)
    eye = (row == col).astype(F32)
    tri = (lax.broadcasted_iota(jnp.int32, (c, c), 1)
           <= lax.broadcasted_iota(jnp.int32, (c, c), 0)).astype(F32)
    lane = lax.broadcasted_iota(jnp.int32, (c, PAIR), 1)
    head0 = lane < HEAD_DIM

    def stack(z):
        return jnp.concatenate([jnp.where(head0, z, 0.0), jnp.where(head0, 0.0, z)], axis=0)

    def chunk_body(ci, carry):
        r0 = pl.multiple_of(ci * c, c)
        ld_all = ld_s[pl.ds(r0, c), :]
        cum_all = _dotf(tri, ld_all)
        c2 = 2 * c
        pairs = range(n_pairs)
        rt, kt, bt, kp, vs, lhs, rhs, gam_end = [], [], [], [], [], [], [], []
        for p in pairs:
            cols = slice(p * PAIR, (p + 1) * PAIR)
            ld = ld_all[:, cols]
            cum = cum_all[:, cols]
            gam = jnp.exp(cum)
            inv = jnp.exp(-cum)
            gam_prev = jnp.exp(cum - ld)
            gam_end.append(gam[c - 1:c, :])
            rt.append(stack(r_s[pl.ds(r0, c), cols] * gam))
            kt_raw = k2_s[pl.ds(r0, c), cols] * inv
            bt_raw = b_s[pl.ds(r0, c), cols] * inv
            kt.append(stack(kt_raw))
            bt.append(stack(bt_raw))
            kp.append(stack(kk_s[pl.ds(r0, c), cols] * gam_prev))
            vs.append(stack(v_ref[pl.ds(r0, c), cols]))
            lhs.append(jnp.concatenate([kp[p], rt[p]], axis=0))
            rhs.append(jnp.concatenate([bt_raw, bt_raw, kt_raw, kt_raw], axis=0))
        aa = [_dot_nt(lhs[p], rhs[p]) for p in pairs]
        a_kk = [jnp.where(strict, aa[p][:c2, c2:], 0.0) for p in pairs]
        a_rb = [jnp.where(incl, aa[p][c2:, :c2], 0.0) for p in pairs]
        a_rk = [jnp.where(incl, aa[p][c2:, c2:], 0.0) for p in pairs]
        n = [-jnp.where(strict, aa[p][:c2, :c2], 0.0) for p in pairs]
        prod = [eye + n[p] for p in pairs]
        n = [_dot(n[p], n[p]) for p in pairs]
        for _ in range(int(np.log2(c)) - 2):
            both = [_dot(jnp.concatenate([n[p], prod[p]], axis=0), n[p]) for p in pairs]
            prod = [prod[p] + both[p][c2:] for p in pairs]
            n = [both[p][:c2] for p in pairs]
        av = [_dot(jnp.concatenate([a_kk[p], a_rk[p]], axis=0), vs[p]) for p in pairs]
        tinv = [prod[p] + _dot(prod[p], n[p]) for p in pairs]
        wu = [_dot(tinv[p], jnp.concatenate([kp[p], av[p][:c2]], axis=1)) for p in pairs]
        rb = [_dot(a_rb[p], wu[p]) for p in pairs]
        tn = [_dot_tn(jnp.concatenate([wu[p], vs[p]], axis=1),
                      jnp.concatenate([bt[p], kt[p]], axis=1)) for p in pairs]
        s0 = [s_ref[p] for p in pairs]
        ys = [_dot_nt(rt[p] - rb[p][:, :PAIR], s0[p]) + (av[p][c2:] - rb[p][:, PAIR:])
              for p in pairs]
        s_mix = [(eye - tn[p][:PAIR, :PAIR]) * gam_end[p] for p in pairs]
        s_new = [_dot(s0[p], s_mix[p]) for p in pairs]
        for p in pairs:
            y_s[pl.ds(r0, c), p * PAIR:(p + 1) * PAIR] = ys[p][:c] + ys[p][c:]
            s_ref[p] = s_new[p] + (tn[p][2 * PAIR:, PAIR:] - tn[p][PAIR:2 * PAIR, :PAIR]) * gam_end[p]
        return carry

    lax.fori_loop(0, n_chunks, chunk_body, 0)

    y = y_s[...]
    mean = _group_sum(y, g_ref) * (1.0 / HEAD_DIM)
    yc = y - mean
    var = _group_sum(yc * yc, g_ref) * (1.0 / HEAD_DIM)
    yn = yc * lax.rsqrt(var + RWKV_GN_EPS) * lnw_ref[...] + lnb_ref[...]
    o_ref[...] = (yn + bonus) * g


def _rwkv(rr, rk, rv, lora, w0, w2p, a0, a2p, g2p, k_k, k_a, r_k, ln_w, ln_b, gsum, tm):
    t = rr.shape[0]
    full = lambda shape: pl.BlockSpec(shape, lambda i: (0,) * len(shape))
    row = lambda width: pl.BlockSpec((tm, width), lambda i: (i, 0))
    vec = full((1, D_GROUP))
    big = pltpu.VMEM((tm, D_GROUP), F32)
    return pl.pallas_call(
        _rwkv_kernel,
        grid=(t // tm,),
        in_specs=[row(D_GROUP), row(D_GROUP), row(D_GROUP), row(LANES),
                  vec, full((LANES, D_GROUP)), vec, full((LANES, D_GROUP)),
                  full((LANES, D_GROUP)), vec, vec, vec, vec, vec, full((D_GROUP, D_GROUP))],
        out_specs=row(D_GROUP),
        out_shape=jax.ShapeDtypeStruct((t, D_GROUP), F32),
        scratch_shapes=[pltpu.VMEM((D_GROUP // PAIR, PAIR, PAIR), F32),
                        big, big, big, big, big, big],
        compiler_params=pltpu.CompilerParams(dimension_semantics=("arbitrary",),
                                             vmem_limit_bytes=VMEM_LIMIT),
        name="rwkv",
    )(rr, rk, rv, lora, w0, w2p, a0, a2p, g2p, k_k, k_a, r_k, ln_w, ln_b, gsum)


DSA_QB = 256
DSA_KB = 512
M_INIT = -5e29
DSA_HEAD_GROUP = 4
DSA_ACC_ROWS = 32
DSA_GROUPS = 256
DSA_V_ROWS = HEAD_DIM + 16


def _dsa_kernel(qi_ref, kj_ref, qt_ref, k_ref, vt_ref, iqt_ref, ik_ref, iwt_ref, o_ref,
                key_s, thr_s, m_s, acc_s, bdq_s, gmax_s, *, n_sel):
    qb = qt_ref.shape[2]
    kb = k_ref.shape[0]
    step = pl.program_id(0)
    qi = qi_ref[step]
    kj = kj_ref[step]
    q0 = qi * qb
    j_last = (q0 + qb - 1) // kb
    n_kc = j_last + 1
    idx_scale = float((IDX_DIM * N_IDX_HEADS) ** -0.5)

    def causal(jblk):
        s_pos = jblk * kb + lax.broadcasted_iota(jnp.int32, (kb, qb), 0)
        t_pos = q0 + lax.broadcasted_iota(jnp.int32, (kb, qb), 1)
        return s_pos <= t_pos

    @pl.when(kj == 0)
    def _():
        iqt = iqt_ref[...].astype(BF16)
        iwt = iwt_ref[...]

        def score_chunk(kc, carry):
            ik = ik_ref[pl.ds(pl.multiple_of(kc * kb, kb), kb), :]
            score = jnp.zeros((kb, qb), F32)
            for h in range(N_IDX_HEADS):
                d = jnp.dot(ik, iqt[h * IDX_DIM:(h + 1) * IDX_DIM, :],
                            preferred_element_type=F32)
                score = score + jnp.maximum(d, 0.0) * iwt[h:h + 1, :]
            score = jnp.where(causal(kc), score * idx_scale, -jnp.inf)
            bits = pltpu.bitcast(score, jnp.int32)
            keys = jnp.where(bits < 0, bits ^ jnp.int32(0x7FFFFFFF), bits)
            key_s[kc] = keys
            gm = gmax_s[...]
            for r in range(kb // DSA_GROUPS):
                gm = jnp.maximum(gm, keys[r * DSA_GROUPS:(r + 1) * DSA_GROUPS])
            gmax_s[...] = gm
            return carry

        gmax_s[...] = jnp.full_like(gmax_s, INT_MIN)
        lax.fori_loop(0, n_kc, score_chunk, 0)

        def rows_reduce(x, op):
            while x.shape[0] > 8:
                half = x.shape[0] // 2
                x = op(x[:half], x[half:])
            r = x[0:1]
            for s in range(1, 8):
                r = op(r, x[s:s + 1])
            return r

        gm = gmax_s[...]
        hi = rows_reduce(gm, jnp.maximum) ^ jnp.int32(INT_MIN)
        lo = rows_reduce(gm, jnp.minimum) ^ jnp.int32(INT_MIN)
        shared = lax.clz(hi ^ lo)
        b0 = jnp.min(shared.astype(F32)).astype(jnp.int32)
        keep = jnp.where(b0 > 0,
                         lax.shift_left(jnp.int32(-1), jnp.int32(32) - jnp.maximum(b0, 1)), 0)

        def pending(state):
            b, _, cnt_cur = state
            unsettled = jnp.max(jnp.where(cnt_cur == float(n_sel), 0.0, 1.0))
            return (b < 32) & (unsettled > 0.0)

        def bit_step(state):
            b, cur, cnt_cur = state
            bit = lax.shift_left(jnp.int32(1), jnp.int32(31) - b)
            cand = (cur | bit) ^ jnp.int32(INT_MIN)

            def count_chunk(kc, acc):
                ind = jnp.where(key_s[kc] >= cand, 1.0, 0.0)
                return acc + jnp.sum(ind.reshape(kb // DSA_ACC_ROWS, DSA_ACC_ROWS, qb), axis=0)

            acc = lax.fori_loop(0, n_kc, count_chunk, jnp.zeros((DSA_ACC_ROWS, qb), F32))
            cnt = jnp.sum(acc, axis=0, keepdims=True)
            take = cnt >= float(n_sel)
            return b + 1, jnp.where(take, cur | bit, cur), jnp.where(take, cnt, cnt_cur)

        _, cur, _ = lax.while_loop(pending, bit_step,
                                   (b0, hi & keep, jnp.full((1, qb), -1.0, F32)))
        thr_s[...] = cur ^ jnp.int32(INT_MIN)
        m_s[...] = jnp.full_like(m_s, M_INIT)
        acc_s[...] = jnp.zeros_like(acc_s)
        bdq_s[...] = jnp.zeros_like(bdq_s)
        for h in range(N_HEADS):
            g, hh = divmod(h, DSA_HEAD_GROUP)
            bdq_s[g, hh * HEAD_DIM:(hh + 1) * HEAD_DIM, hh * qb:(hh + 1) * qb] = (
                qt_ref[h] * jnp.asarray(HEAD_DIM ** -0.5, BF16))

    mask = (key_s[kj] >= thr_s[...]) & causal(kj)
    hg = DSA_HEAD_GROUP
    st_all = [jnp.dot(k_ref[:, g * hg * HEAD_DIM:(g + 1) * hg * HEAD_DIM], bdq_s[g],
                      preferred_element_type=F32) for g in range(N_HEADS // hg)]
    for g in range(N_HEADS // hg):
        st_g = st_all[g]
        for hh in range(hg):
            h = g * hg + hh
            st = jnp.where(mask, st_g[:, hh * qb:(hh + 1) * qb], NEG_BIG)
            m_old = m_s[h]
            part = jnp.max(st.reshape(kb // DSA_ACC_ROWS, DSA_ACC_ROWS, qb), axis=0)
            m_new = jnp.maximum(m_old, jnp.max(part, axis=0, keepdims=True))
            p = jnp.exp((st - m_new).astype(BF16))
            acc_s[h] = jnp.exp(m_old - m_new) * acc_s[h] + jnp.dot(
                vt_ref[h], p, preferred_element_type=F32)
            m_s[h] = m_new

    @pl.when(kj == j_last)
    def _():
        for h in range(N_HEADS):
            acc = acc_s[h]
            o_ref[h] = acc[:HEAD_DIM] / acc[HEAD_DIM:HEAD_DIM + 1]


def _dsa(qt, k, vt, iqt, ik, iwt, n_sel):
    t = ik.shape[0]
    qb, kb = DSA_QB, DSA_KB
    assert n_sel <= DSA_GROUPS and kb % DSA_GROUPS == 0
    nq, nk = t // qb, t // kb
    pairs = [(i, j) for i in range(nq) for j in range((i * qb + qb - 1) // kb + 1)]
    qi = jnp.asarray(np.array([p[0] for p in pairs], np.int32))
    kj = jnp.asarray(np.array([p[1] for p in pairs], np.int32))
    grid_spec = pltpu.PrefetchScalarGridSpec(
        num_scalar_prefetch=2,
        grid=(len(pairs),),
        in_specs=[pl.BlockSpec((N_HEADS, HEAD_DIM, qb), lambda s, qi, kj: (0, 0, qi[s])),
                  pl.BlockSpec((kb, D_GROUP), lambda s, qi, kj: (kj[s], 0)),
                  pl.BlockSpec((N_HEADS, DSA_V_ROWS, kb), lambda s, qi, kj: (0, 0, kj[s])),
                  pl.BlockSpec((D_IDX, qb), lambda s, qi, kj: (0, qi[s])),
                  pl.BlockSpec((t, IDX_DIM), lambda s, qi, kj: (0, 0)),
                  pl.BlockSpec((N_IDX_HEADS, qb), lambda s, qi, kj: (0, qi[s]))],
        out_specs=pl.BlockSpec((N_HEADS, HEAD_DIM, qb), lambda s, qi, kj: (0, 0, qi[s])),
        scratch_shapes=[pltpu.VMEM((nk, kb, qb), jnp.int32),
                        pltpu.VMEM((1, qb), jnp.int32),
                        pltpu.VMEM((N_HEADS, 1, qb), F32),
                        pltpu.VMEM((N_HEADS, DSA_V_ROWS, qb), F32),
                        pltpu.VMEM((N_HEADS // DSA_HEAD_GROUP, DSA_HEAD_GROUP * HEAD_DIM,
                                    DSA_HEAD_GROUP * qb), BF16),
                        pltpu.VMEM((DSA_GROUPS, qb), jnp.int32)])
    return pl.pallas_call(
        functools.partial(_dsa_kernel, n_sel=n_sel),
        grid_spec=grid_spec,
        out_shape=jax.ShapeDtypeStruct((N_HEADS, HEAD_DIM, t), F32),
        compiler_params=pltpu.CompilerParams(dimension_semantics=("arbitrary",),
                                             vmem_limit_bytes=VMEM_LIMIT),
        name="dsa",
    )(qi, kj, qt, k, vt, iqt, ik, iwt)


def _out_proj_kernel(x_ref, att_ref, rw_ref, wo_ref, n2w_ref, rwt_ref, rb_ref,
                     h_ref, xn_ref, gate_ref):
    acc = x_ref[...] + _dot(rw_ref[...], wo_ref[D_GROUP:, :])
    for h in range(N_HEADS):
        acc = acc + _dot(att_ref[h].T, wo_ref[h * HEAD_DIM:(h + 1) * HEAD_DIM, :])
    h_ref[...] = acc
    xn = acc * lax.rsqrt(jnp.mean(acc * acc, axis=-1, keepdims=True) + RMS_EPS) * n2w_ref[...]
    xn_ref[...] = xn.astype(BF16)
    logits = _dotf(xn, rwt_ref[...]) + rb_ref[...]
    lane = lax.broadcasted_iota(jnp.int32, logits.shape, 1)
    work = logits
    vals, hots = [], []
    for _ in range(TOP_K_EXPERTS):
        m = jnp.max(work, axis=1, keepdims=True)
        idx = jnp.min(jnp.where(work == m, lane, LANES), axis=1, keepdims=True)
        hot = lane == idx
        vals.append(m)
        hots.append(hot)
        work = jnp.where(hot, -jnp.inf, work)
    es = [jnp.exp(vv - vals[0]) for vv in vals]
    denom = es[0] + es[1] + es[2] + es[3]
    gates = jnp.zeros_like(logits)
    for e, hot in zip(es, hots):
        gates = gates + jnp.where(hot, e / denom, 0.0)
    gate_ref[...] = gates


def _out_proj(x2, att_t, rw, wo, n2w, rwt, rb, tm):
    t = x2.shape[0]
    full = lambda shape: pl.BlockSpec(shape, lambda i: (0,) * len(shape))
    row = lambda width: pl.BlockSpec((tm, width), lambda i: (i, 0))
    return pl.pallas_call(
        _out_proj_kernel,
        grid=(t // tm,),
        in_specs=[row(D_MODEL), pl.BlockSpec((N_HEADS, HEAD_DIM, tm), lambda i: (0, 0, i)),
                  row(D_GROUP), full((2 * D_GROUP, D_MODEL)), full((1, D_MODEL)),
                  full((D_MODEL, LANES)), full((1, LANES))],
        out_specs=[row(D_MODEL), row(D_MODEL), row(LANES)],
        out_shape=(jax.ShapeDtypeStruct((t, D_MODEL), F32),
                   jax.ShapeDtypeStruct((t, D_MODEL), BF16),
                   jax.ShapeDtypeStruct((t, LANES), F32)),
        compiler_params=pltpu.CompilerParams(dimension_semantics=("arbitrary",),
                                             vmem_limit_bytes=VMEM_LIMIT),
        name="out_proj",
    )(x2, att_t, rw, wo, n2w, rwt, rb)


MOE_TM = 1024
MOE_RB = 144
MOE_KPAD = 256
MOE_FC = 512


def _route_kernel(gate_ref, rank_ref, rankt_ref, cnt_ref):
    tm = gate_ref.shape[0]
    hot = gate_ref[...] > 0.0
    ind = jnp.where(hot, 1.0, 0.0)
    before = (lax.broadcasted_iota(jnp.int32, (tm, tm), 1)
              < lax.broadcasted_iota(jnp.int32, (tm, tm), 0))
    rank = jnp.dot(jnp.where(before, 1.0, 0.0).astype(BF16), ind.astype(BF16),
                   preferred_element_type=F32)
    rank = jnp.where(hot, rank, -1.0)
    rank_ref[...] = rank
    rankt_ref[...] = rank.T
    cnt = jnp.sum(ind, axis=0, keepdims=True)
    cnt_ref[...] = jnp.broadcast_to(cnt[None], cnt_ref.shape)


def _route(gates, tm):
    t = gates.shape[0]
    return pl.pallas_call(
        _route_kernel,
        grid=(t // tm,),
        in_specs=[pl.BlockSpec((tm, LANES), lambda i: (i, 0))],
        out_specs=[pl.BlockSpec((tm, LANES), lambda i: (i, 0)),
                   pl.BlockSpec((LANES, tm), lambda i: (0, i)),
                   pl.BlockSpec((1, 8, LANES), lambda i: (i, 0, 0))],
        out_shape=(jax.ShapeDtypeStruct((t, LANES), F32),
                   jax.ShapeDtypeStruct((LANES, t), F32),
                   jax.ShapeDtypeStruct((t // tm, 8, LANES), F32)),
        compiler_params=pltpu.CompilerParams(dimension_semantics=("arbitrary",),
                                             vmem_limit_bytes=VMEM_LIMIT),
        name="route",
    )(gates)


def _moe_kernel(cnt_ref, h_ref, xn_ref, gate_ref, rank_ref, rankt_ref, wgu_ref, bgu_ref,
                wdn_ref, bdn_ref, o_ref):
    i = pl.program_id(0)
    e = pl.program_id(1)
    tm = h_ref.shape[0]

    @pl.when(e == 0)
    def _():
        o_ref[...] = h_ref[...]

    n_rows = cnt_ref[i * N_EXPERTS + e]
    n_blocks = (n_rows + MOE_RB - 1) // MOE_RB
    pick = lax.broadcasted_iota(jnp.int32, (tm, LANES), 1) == e
    rank_col = jnp.sum(jnp.where(pick, rank_ref[...], 0.0), axis=1, keepdims=True)
    gate_col = jnp.sum(jnp.where(pick, gate_ref[...], 0.0), axis=1, keepdims=True)
    rank_row = rankt_ref[pl.ds(e, 1), :]

    def row_block(bi, carry):
        r0 = (bi * MOE_RB).astype(F32)
        rowid = lax.broadcasted_iota(jnp.int32, (MOE_RB, tm), 0).astype(F32) + r0
        sel = jnp.where(rank_row == rowid, 1.0, 0.0).astype(BF16)
        xb = jnp.dot(sel, xn_ref[...], preferred_element_type=F32).astype(BF16)
        y = jnp.zeros((MOE_RB, D_MODEL), F32)
        for fc in range(D_FF // MOE_FC):
            c0 = fc * MOE_FC
            hg = jnp.dot(xb, wgu_ref[0, :, c0:c0 + MOE_FC], preferred_element_type=F32) \
                + bgu_ref[0, :, c0:c0 + MOE_FC]
            hl = jnp.dot(xb, wgu_ref[0, :, D_FF + c0:D_FF + c0 + MOE_FC],
                         preferred_element_type=F32) + bgu_ref[0, :, D_FF + c0:D_FF + c0 + MOE_FC]
            glu = jnp.minimum(hg, SWIGLU_LIMIT)
            lin = jnp.clip(hl, -SWIGLU_LIMIT, SWIGLU_LIMIT)
            act = (lin + 1.0) * glu * jax.nn.sigmoid(SWIGLU_ALPHA * glu)
            y = y + jnp.dot(act.astype(BF16), wdn_ref[0, c0:c0 + MOE_FC, :],
                            preferred_element_type=F32)
        y = (y + bdn_ref[0]).astype(BF16)
        ypad = jnp.concatenate([y, jnp.zeros((MOE_KPAD - MOE_RB, D_MODEL), BF16)], axis=0)
        colid = lax.broadcasted_iota(jnp.int32, (tm, MOE_KPAD), 1).astype(F32) + r0
        put = jnp.where(rank_col == colid, 1.0, 0.0).astype(BF16)
        o_ref[...] += jnp.dot(put, ypad, preferred_element_type=F32) * gate_col
        return carry

    lax.fori_loop(0, n_blocks, row_block, 0)


def _moe(counts, h1, xn2, gates, rank, rank_t, wgu, bgu, wdn, bdn, tm):
    t = h1.shape[0]
    row = lambda width: pl.BlockSpec((tm, width), lambda i, e, c: (i, 0))
    grid_spec = pltpu.PrefetchScalarGridSpec(
        num_scalar_prefetch=1,
        grid=(t // tm, N_EXPERTS),
        in_specs=[row(D_MODEL), row(D_MODEL), row(LANES), row(LANES),
                  pl.BlockSpec((LANES, tm), lambda i, e, c: (0, i)),
                  pl.BlockSpec((1, D_MODEL, 2 * D_FF), lambda i, e, c: (e, 0, 0)),
                  pl.BlockSpec((1, 1, 2 * D_FF), lambda i, e, c: (e, 0, 0)),
                  pl.BlockSpec((1, D_FF, D_MODEL), lambda i, e, c: (e, 0, 0)),
                  pl.BlockSpec((1, 1, D_MODEL), lambda i, e, c: (e, 0, 0))],
        out_specs=row(D_MODEL))
    return pl.pallas_call(
        _moe_kernel,
        grid_spec=grid_spec,
        out_shape=jax.ShapeDtypeStruct((t, D_MODEL), F32),
        compiler_params=pltpu.CompilerParams(dimension_semantics=("arbitrary", "arbitrary"),
                                             vmem_limit_bytes=VMEM_LIMIT),
        name="moe",
    )(counts, h1, xn2, gates, rank, rank_t, wgu, bgu, wdn, bdn)


def _rope_tables():
    lane = jnp.arange(LANES)

    def rows(period, rot):
        half = rot // 2
        inv_freq = ROPE_THETA ** (-jnp.arange(half, dtype=F32) / half)
        jm = lane % period
        freq = jnp.where(jm < rot, inv_freq[jm % half], 0.0)
        sign = jnp.where(jm < half, -1.0, jnp.where(jm < rot, 1.0, 0.0))
        first = (jm < half).astype(F32)
        return [freq, sign, first]

    idx_rows = rows(IDX_DIM, IDX_DIM // 4)
    only_key = (lane < IDX_DIM).astype(F32)
    return jnp.stack(rows(HEAD_DIM, HEAD_DIM // 4) + idx_rows
                     + [idx_rows[0] * only_key, idx_rows[1] * only_key]).astype(F32)


def _group_matrix():
    g = np.arange(D_GROUP) // HEAD_DIM
    return jnp.asarray((g[:, None] == g[None, :]).astype(np.float32), dtype=BF16)


def _pad_rows(w, r0, rows):
    return jnp.zeros((rows, w.shape[1]), w.dtype).at[r0:r0 + w.shape[0]].set(w)


def kernel(x, positions, norm1_w, w_in, q_norm_w, k_norm_w, rwkv_mu, rwkv_w0, rwkv_w2, rwkv_a0,
           rwkv_a2, rwkv_g2, rwkv_k_k, rwkv_k_a, rwkv_r_k, rwkv_ln_w, rwkv_ln_b, w_out, norm2_w,
           router_w, router_b, exp_w_gu, exp_b_gu, exp_w_down, exp_b_down):
    b, t, _ = x.shape
    assert b == 1 and w_in.shape[0] == 1, "single sequence, single layer"
    assert t % DSA_KB == 0 and t % DSA_QB == 0 and t % MOE_TM == 0
    x2 = x[0]
    pos_f = positions[0].astype(F32)[:, None]
    n_sel = min(TOPK_MAX, t // 4)

    w = w_in[0]
    a0 = 3 * D_GROUP
    att_cols = a0 + D_IDX + IDX_DIM + N_IDX_HEADS
    w_att, w_rw = w[:, :att_cols], w[:, att_cols:]
    w_sm = jnp.zeros((D_MODEL, LANES), F32).at[:, :IDX_DIM + N_IDX_HEADS].set(w_att[:, a0 + D_IDX:])
    w_packed = jnp.concatenate(
        [w_att[:, :a0], w_rw[:, :a0], w_att[:, a0:a0 + D_IDX], w_sm, w_rw[:, a0:]],
        axis=1).astype(BF16)
    mu = rwkv_mu[0][None, :]
    tile8 = lambda z: jnp.tile(z, N_HEADS)[None, :]
    gsum = _group_matrix()

    tm = 256
    q, k, v, rr, rk, rv, iq, sm, lora = _in_proj(
        x2, pos_f, norm1_w, w_packed, mu, tile8(q_norm_w[0]), tile8(k_norm_w[0]), gsum,
        _rope_tables(), tm)

    vec = lambda z: z.reshape(1, D_GROUP)
    rw = _rwkv(rr, rk, rv, lora, vec(rwkv_w0[0]),
               _pad_rows(rwkv_w2[0], 0, LANES), vec(rwkv_a0[0]),
               _pad_rows(rwkv_a2[0], D_DECAY_LORA, LANES),
               _pad_rows(rwkv_g2[0], D_DECAY_LORA + D_AAA_LORA, LANES),
               vec(rwkv_k_k[0]), vec(rwkv_k_a[0]), vec(rwkv_r_k[0]), vec(rwkv_ln_w[0]),
               vec(rwkv_ln_b[0]), gsum, tm)

    heads = lambda z: z.reshape(t, N_HEADS, HEAD_DIM)
    vt = jnp.concatenate([heads(v).transpose(1, 2, 0),
                          jnp.ones((N_HEADS, DSA_V_ROWS - HEAD_DIM, t), BF16)], axis=1)
    att_t = _dsa(heads(q).transpose(1, 2, 0), k, vt,
                 iq.T, sm[:, :IDX_DIM].astype(BF16),
                 sm[:, IDX_DIM:IDX_DIM + N_IDX_HEADS].T, n_sel)

    rwt = jnp.zeros((D_MODEL, LANES), F32).at[:, :N_EXPERTS].set(router_w[0])
    rb = jnp.full((1, LANES), NEG_BIG, F32).at[0, :N_EXPERTS].set(router_b[0])
    h1, xn2, gates = _out_proj(x2, att_t, rw, w_out[0].astype(BF16), norm2_w, rwt, rb, tm)

    rank, rank_t, cnt = _route(gates, MOE_TM)
    counts = cnt[:, 0, :N_EXPERTS].astype(jnp.int32).reshape(-1)
    out = _moe(counts, h1, xn2, gates, rank, rank_t, exp_w_gu[0].astype(BF16),
               exp_b_gu[0][:, None, :], exp_w_down[0].astype(BF16), exp_b_down[0][:, None, :],
               MOE_TM)
    return out[None]
```

```python
import functools

import jax
import jax.numpy as jnp
import numpy as np
from jax import lax
from jax.experimental import pallas as pl
from jax.experimental.pallas import tpu as pltpu

F32 = jnp.float32
BF16 = jnp.bfloat16
HIGHEST = lax.Precision.HIGHEST

D_MODEL = 1024
HEAD_DIM = 64
N_HEADS = 8
D_GROUP = N_HEADS * HEAD_DIM
ROPE_THETA = 500000.0
N_IDX_HEADS = 8
IDX_DIM = 32
D_IDX = N_IDX_HEADS * IDX_DIM
TOPK_MAX = 256
D_DECAY_LORA = 32
D_AAA_LORA = 32
D_GATE_LORA = 64
RWKV_GN_EPS = 64e-5
N_EXPERTS = 32
TOP_K_EXPERTS = 4
D_FF = 1024
SWIGLU_LIMIT = 7.0
SWIGLU_ALPHA = 1.702
RMS_EPS = 1e-6

LANES = 128
VMEM_LIMIT = 56 * 1024 * 1024

NEG_BIG = -1e30
INT_MIN = -(2 ** 31)


def _dot(a, b):
    return jnp.dot(a.astype(BF16), b.astype(BF16), preferred_element_type=F32)


def _dotf(a, b):
    return jnp.dot(a, b, preferred_element_type=F32, precision=HIGHEST)


def _dot_nt(a, b):
    return lax.dot_general(a.astype(BF16), b.astype(BF16), (((1,), (1,)), ((), ())),
                           preferred_element_type=F32)


def _dot_tn(a, b):
    return lax.dot_general(a.astype(BF16), b.astype(BF16), (((0,), (0,)), ((), ())),
                           preferred_element_type=F32)


def _group_sum(z, g_ref):
    hi = z.astype(BF16)
    lo = (z - hi.astype(F32)).astype(BF16)
    g = g_ref[...]
    return (jnp.dot(hi, g, preferred_element_type=F32)
            + jnp.dot(lo, g, preferred_element_type=F32))


_C_Q, _C_K, _C_V, _C_RR, _C_RK, _C_RV = (i * D_GROUP for i in range(6))
_C_IQ = 6 * D_GROUP
_C_SM = _C_IQ + D_IDX
_C_LORA = _C_SM + LANES
D_IN_PACKED = _C_LORA + LANES
D_SHIFT = 3 * D_GROUP + LANES


def _in_proj_kernel(x_ref, pos_ref, n1w_ref, w_ref, mu_ref, qnw_ref, knw_ref, g_ref, rope_ref,
                    q_ref, k_ref, v_ref, rr_ref, rk_ref, rv_ref, iq_ref, sm_ref, lora_ref,
                    carry_ref):
    tm = x_ref.shape[0]

    @pl.when(pl.program_id(0) == 0)
    def _():
        carry_ref[...] = jnp.zeros_like(carry_ref)

    x = x_ref[...]
    xn = x * lax.rsqrt(jnp.mean(x * x, axis=-1, keepdims=True) + RMS_EPS) * n1w_ref[...]
    xb = xn.astype(BF16)
    pos = pos_ref[...]
    rope = rope_ref[...]

    def tables(frow, srow, reps):
        ang = pos * rope[frow:frow + 1, :]
        c = jnp.cos(ang)
        s = jnp.sin(ang) * rope[srow:srow + 1, :]
        if reps > 1:
            c = jnp.concatenate([c] * reps, axis=1)
            s = jnp.concatenate([s] * reps, axis=1)
        return c, s

    def rotary(z, c, s, first_row, half):
        w = z.shape[1]
        first = jnp.concatenate([rope[first_row:first_row + 1, :]] * (w // LANES), axis=1) > 0.5
        partner = jnp.where(first, pltpu.roll(z, w - half, 1), pltpu.roll(z, half, 1))
        return z * c + partner * s

    def head_norm(z, w_row):
        ms = _group_sum(z * z, g_ref) * (1.0 / HEAD_DIM)
        return z * lax.rsqrt(ms + RMS_EPS) * w_row

    def proj(c0, width):
        return jnp.dot(xb, w_ref[:, c0:c0 + width], preferred_element_type=F32)

    def shift(z, c0):
        width = z.shape[1]
        row = lax.broadcasted_iota(jnp.int32, z.shape, 0)
        prev = jnp.where(row == 0, carry_ref[0:1, c0:c0 + width], pltpu.roll(z, 1, 0))
        carry_ref[0:1, c0:c0 + width] = z[tm - 1:tm, :]
        return z + (prev - z) * mu_ref[:, c0:c0 + width]

    cq, sq = tables(0, 1, D_GROUP // LANES)
    q = rotary(head_norm(proj(_C_Q, D_GROUP), qnw_ref[...]), cq, sq, 2, HEAD_DIM // 8)
    q_ref[...] = q.astype(BF16)
    k = rotary(head_norm(proj(_C_K, D_GROUP), knw_ref[...]), cq, sq, 2, HEAD_DIM // 8)
    k_ref[...] = k.astype(BF16)
    v_ref[...] = proj(_C_V, D_GROUP).astype(BF16)

    rr_ref[...] = shift(proj(_C_RR, D_GROUP), 0)
    rk_ref[...] = shift(proj(_C_RK, D_GROUP), D_GROUP)
    rv_ref[...] = shift(proj(_C_RV, D_GROUP), 2 * D_GROUP)
    lora_ref[...] = shift(proj(_C_LORA, LANES), 3 * D_GROUP)

    ci, si = tables(3, 4, D_IDX // LANES)
    iq_ref[...] = rotary(proj(_C_IQ, D_IDX), ci, si, 5, IDX_DIM // 8)
    ck, sk = tables(6, 7, 1)
    sm_ref[...] = rotary(proj(_C_SM, LANES), ck, sk, 5, IDX_DIM // 8)


def _in_proj(x2, pos_f, n1w, w_packed, mu_packed, qnw, knw, gmat, rope, tm):
    t = x2.shape[0]
    full = lambda shape: pl.BlockSpec(shape, lambda i: (0,) * len(shape))
    row = lambda width: pl.BlockSpec((tm, width), lambda i: (i, 0))
    out_shapes = (
        jax.ShapeDtypeStruct((t, D_GROUP), BF16),
        jax.ShapeDtypeStruct((t, D_GROUP), BF16),
        jax.ShapeDtypeStruct((t, D_GROUP), BF16),
        jax.ShapeDtypeStruct((t, D_GROUP), F32),
        jax.ShapeDtypeStruct((t, D_GROUP), F32),
        jax.ShapeDtypeStruct((t, D_GROUP), F32),
        jax.ShapeDtypeStruct((t, D_IDX), F32),
        jax.ShapeDtypeStruct((t, LANES), F32),
        jax.ShapeDtypeStruct((t, LANES), F32),
    )
    return pl.pallas_call(
        _in_proj_kernel,
        grid=(t // tm,),
        in_specs=[row(D_MODEL), row(1), full((1, D_MODEL)), full((D_MODEL, D_IN_PACKED)),
                  full((1, D_SHIFT)), full((1, D_GROUP)), full((1, D_GROUP)),
                  full((D_GROUP, D_GROUP)), full((8, LANES))],
        out_specs=[row(D_GROUP)] * 6 + [row(D_IDX), row(LANES), row(LANES)],
        out_shape=out_shapes,
        scratch_shapes=[pltpu.VMEM((8, D_SHIFT), F32)],
        compiler_params=pltpu.CompilerParams(dimension_semantics=("arbitrary",),
                                             vmem_limit_bytes=VMEM_LIMIT),
        name="in_proj",
    )(x2, pos_f, n1w, w_packed, mu_packed, qnw, knw, gmat, rope)


RWKV_CHUNK = 64
PAIR = 2 * HEAD_DIM


def _rwkv_kernel(r_ref, k_ref, v_ref, lora_ref, w0_ref, w2_ref, a0_ref, a2_ref, g2_ref,
                 kk_ref, ka_ref, rk_ref, lnw_ref, lnb_ref, g_ref,
                 o_ref,
                 s_ref, ld_s, r_s, k2_s, b_s, kk_s, y_s):
    tm = r_ref.shape[0]
    n_chunks = tm // RWKV_CHUNK
    n_pairs = D_GROUP // PAIR
    c = RWKV_CHUNK

    @pl.when(pl.program_id(0) == 0)
    def _():
        s_ref[...] = jnp.zeros_like(s_ref)

    lora = lora_ref[...]
    r = r_ref[...]
    k = k_ref[...]
    v = v_ref[...]
    zarg = w0_ref[...] + _dotf(jnp.tanh(lora), w2_ref[...])
    sp = jnp.maximum(-zarg, 0.0) + jnp.log1p(jnp.exp(-jnp.abs(zarg)))
    ld_s[...] = -jnp.exp(-sp - 0.5)
    a = jax.nn.sigmoid(a0_ref[...] + _dotf(lora, a2_ref[...]))
    g = _dotf(jax.nn.sigmoid(lora), g2_ref[...])
    kk = k * kk_ref[...]
    kk = kk * lax.rsqrt(jnp.maximum(_group_sum(kk * kk, g_ref), 1e-24))
    k2 = k * (1.0 + (a - 1.0) * ka_ref[...])
    bonus = _group_sum(r * k2 * rk_ref[...], g_ref) * v
    r_s[...] = r
    k2_s[...] = k2
    kk_s[...] = kk
    b_s[...] = kk * a

    row = lax.broadcasted_iota(jnp.int32, (2 * c, 2 * c), 0)
    col = lax.broadcasted_iota(jnp.int32, (2 * c, 2 * c), 1)
    same_head = (row >= c) == (col >= c)
    strict = same_head & (col < row)
    incl = same_head & (col <= row)
    eye = (row == col).astype(F32)
    tri = (lax.broadcasted_iota(jnp.int32, (c, c), 1)
           <= lax.broadcasted_iota(jnp.int32, (c, c), 0)).astype(F32)
    lane = lax.broadcasted_iota(jnp.int32, (c, PAIR), 1)
    head0 = lane < HEAD_DIM

    def stack(z):
        return jnp.concatenate([jnp.where(head0, z, 0.0), jnp.where(head0, 0.0, z)], axis=0)

    def chunk_body(ci, carry):
        r0 = pl.multiple_of(ci * c, c)
        ld_all = ld_s[pl.ds(r0, c), :]
        cum_all = _dotf(tri, ld_all)
        c2 = 2 * c
        pairs = range(n_pairs)
        rt, kt, bt, kp, vs, lhs, rhs, gam_end = [], [], [], [], [], [], [], []
        for p in pairs:
            cols = slice(p * PAIR, (p + 1) * PAIR)
            ld = ld_all[:, cols]
            cum = cum_all[:, cols]
            gam = jnp.exp(cum)
            inv = jnp.exp(-cum)
            gam_prev = jnp.exp(cum - ld)
            gam_end.append(gam[c - 1:c, :])
            rt.append(stack(r_s[pl.ds(r0, c), cols] * gam))
            kt_raw = k2_s[pl.ds(r0, c), cols] * inv
            bt_raw = b_s[pl.ds(r0, c), cols] * inv
            kt.append(stack(kt_raw))
            bt.append(stack(bt_raw))
            kp.append(stack(kk_s[pl.ds(r0, c), cols] * gam_prev))
            vs.append(stack(v_ref[pl.ds(r0, c), cols]))
            lhs.append(jnp.concatenate([kp[p], rt[p]], axis=0))
            rhs.append(jnp.concatenate([bt_raw, bt_raw, kt_raw, kt_raw], axis=0))
        aa = [_dot_nt(lhs[p], rhs[p]) for p in pairs]
        a_kk = [jnp.where(strict, aa[p][:c2, c2:], 0.0) for p in pairs]
        a_rb = [jnp.where(incl, aa[p][c2:, :c2], 0.0) for p in pairs]
        a_rk = [jnp.where(incl, aa[p][c2:, c2:], 0.0) for p in pairs]
        n = [-jnp.where(strict, aa[p][:c2, :c2], 0.0) for p in pairs]
        prod = [eye + n[p] for p in pairs]
        n = [_dot(n[p], n[p]) for p in pairs]
        for _ in range(int(np.log2(c)) - 2):
            both = [_dot(jnp.concatenate([n[p], prod[p]], axis=0), n[p]) for p in pairs]
            prod = [prod[p] + both[p][c2:] for p in pairs]
            n = [both[p][:c2] for p in pairs]
        av = [_dot(jnp.concatenate([a_kk[p], a_rk[p]], axis=0), vs[p]) for p in pairs]
        tinv = [prod[p] + _dot(prod[p], n[p]) for p in pairs]
        wu = [_dot(tinv[p], jnp.concatenate([kp[p], av[p][:c2]], axis=1)) for p in pairs]
        rb = [_dot(a_rb[p], wu[p]) for p in pairs]
        tn = [_dot_tn(jnp.concatenate([wu[p], vs[p]], axis=1),
                      jnp.concatenate([bt[p], kt[p]], axis=1)) for p in pairs]
        s0 = [s_ref[p] for p in pairs]
        ys = [_dot_nt(rt[p] - rb[p][:, :PAIR], s0[p]) + (av[p][c2:] - rb[p][:, PAIR:])
              for p in pairs]
        s_mix = [(eye - tn[p][:PAIR, :PAIR]) * gam_end[p] for p in pairs]
        s_new = [_dot(s0[p], s_mix[p]) for p in pairs]
        for p in pairs:
            y_s[pl.ds(r0, c), p * PAIR:(p + 1) * PAIR] = ys[p][:c] + ys[p][c:]
            s_ref[p] = s_new[p] + (tn[p][2 * PAIR:, PAIR:] - tn[p][PAIR:2 * PAIR, :PAIR]) * gam_end[p]
        return carry

    lax.fori_loop(0, n_chunks, chunk_body, 0)

    y = y_s[...]
    mean = _group_sum(y, g_ref) * (1.0 / HEAD_DIM)
    yc = y - mean
    var = _group_sum(yc * yc, g_ref) * (1.0 / HEAD_DIM)
    yn = yc * lax.rsqrt(var + RWKV_GN_EPS) * lnw_ref[...] + lnb_ref[...]
    o_ref[...] = (yn + bonus) * g


def _rwkv(rr, rk, rv, lora, w0, w2p, a0, a2p, g2p, k_k, k_a, r_k, ln_w, ln_b, gsum, tm):
    t = rr.shape[0]
    full = lambda shape: pl.BlockSpec(shape, lambda i: (0,) * len(shape))
    row = lambda width: pl.BlockSpec((tm, width), lambda i: (i, 0))
    vec = full((1, D_GROUP))
    big = pltpu.VMEM((tm, D_GROUP), F32)
    return pl.pallas_call(
        _rwkv_kernel,
        grid=(t // tm,),
        in_specs=[row(D_GROUP), row(D_GROUP), row(D_GROUP), row(LANES),
                  vec, full((LANES, D_GROUP)), vec, full((LANES, D_GROUP)),
                  full((LANES, D_GROUP)), vec, vec, vec, vec, vec, full((D_GROUP, D_GROUP))],
        out_specs=row(D_GROUP),
        out_shape=jax.ShapeDtypeStruct((t, D_GROUP), F32),
        scratch_shapes=[pltpu.VMEM((D_GROUP // PAIR, PAIR, PAIR), F32),
                        big, big, big, big, big, big],
        compiler_params=pltpu.CompilerParams(dimension_semantics=("arbitrary",),
                                             vmem_limit_bytes=VMEM_LIMIT),
        name="rwkv",
    )(rr, rk, rv, lora, w0, w2p, a0, a2p, g2p, k_k, k_a, r_k, ln_w, ln_b, gsum)


DSA_QB = 256
DSA_KB = 512
M_INIT = -5e29
DSA_HEAD_GROUP = 4
DSA_ACC_ROWS = 32
DSA_GROUPS = 256
DSA_TOP = 10
DSA_V_ROWS = HEAD_DIM + 16


def _dsa_kernel(qi_ref, kj_ref, qt_ref, k_ref, vt_ref, iqt_ref, ik_ref, iwt_ref, o_ref,
                key_s, thr_s, m_s, acc_s, bdq_s, top_s, *, n_sel):
    qb = qt_ref.shape[2]
    kb = k_ref.shape[0]
    step = pl.program_id(0)
    qi = qi_ref[step]
    kj = kj_ref[step]
    q0 = qi * qb
    j_last = (q0 + qb - 1) // kb
    n_kc = j_last + 1
    idx_scale = float((IDX_DIM * N_IDX_HEADS) ** -0.5)

    def causal(jblk):
        s_pos = jblk * kb + lax.broadcasted_iota(jnp.int32, (kb, qb), 0)
        t_pos = q0 + lax.broadcasted_iota(jnp.int32, (kb, qb), 1)
        return s_pos <= t_pos

    @pl.when(kj == 0)
    def _():
        iqt = iqt_ref[...].astype(BF16)
        iwt = iwt_ref[...]

        def score_chunk(kc, carry):
            ik = ik_ref[pl.ds(pl.multiple_of(kc * kb, kb), kb), :]
            score = jnp.zeros((kb, qb), F32)
            for h in range(N_IDX_HEADS):
                d = jnp.dot(ik, iqt[h * IDX_DIM:(h + 1) * IDX_DIM, :],
                            preferred_element_type=F32)
                score = score + jnp.maximum(d, 0.0) * iwt[h:h + 1, :]
            score = jnp.where(causal(kc), score * idx_scale, -jnp.inf)
            bits = pltpu.bitcast(score, jnp.int32)
            keys = jnp.where(bits < 0, bits ^ jnp.int32(0x7FFFFFFF), bits)
            key_s[kc] = keys
            xs = [keys[r * DSA_GROUPS:(r + 1) * DSA_GROUPS] for r in range(kb // DSA_GROUPS)]
            for lvl in range(DSA_TOP):
                s = top_s[lvl]
                for r in range(len(xs)):
                    s, xs[r] = jnp.maximum(s, xs[r]), jnp.minimum(s, xs[r])
                top_s[lvl] = s
            return carry

        top_s[...] = jnp.full_like(top_s, INT_MIN)
        lax.fori_loop(0, n_kc, score_chunk, 0)

        def select(count_keys):
            def bit_step(b, cur):
                bit = lax.shift_left(jnp.int32(1), jnp.int32(31) - b)
                cand = (cur | bit) ^ jnp.int32(INT_MIN)
                cnt = jnp.sum(count_keys(cand), axis=0, keepdims=True)
                return jnp.where(cnt >= float(n_sel), cur | bit, cur)

            cur = lax.fori_loop(0, 32, bit_step, jnp.zeros((1, qb), jnp.int32))
            return cur ^ jnp.int32(INT_MIN)

        def part_count(keys, cand):
            ind = jnp.where(keys >= cand, 1.0, 0.0)
            return jnp.sum(ind.reshape(-1, DSA_ACC_ROWS, qb), axis=0)

        def count_top(cand):
            return lax.fori_loop(0, DSA_TOP, lambda lvl, a: a + part_count(top_s[lvl], cand),
                                 jnp.zeros((DSA_ACC_ROWS, qb), F32))

        def count_all(cand):
            return lax.fori_loop(0, n_kc, lambda kc, a: a + part_count(key_s[kc], cand),
                                 jnp.zeros((DSA_ACC_ROWS, qb), F32))

        thr = select(count_top)
        thr_s[...] = thr
        hidden = jnp.where((top_s[DSA_TOP - 1] >= thr) & (thr > jnp.int32(INT_MIN)), 1.0, 0.0)

        @pl.when(jnp.max(hidden) > 0.0)
        def _():
            thr_s[...] = select(count_all)

        m_s[...] = jnp.full_like(m_s, M_INIT)
        acc_s[...] = jnp.zeros_like(acc_s)
        bdq_s[...] = jnp.zeros_like(bdq_s)
        for h in range(N_HEADS):
            g, hh = divmod(h, DSA_HEAD_GROUP)
            bdq_s[g, hh * HEAD_DIM:(hh + 1) * HEAD_DIM, hh * qb:(hh + 1) * qb] = (
                qt_ref[h] * jnp.asarray(HEAD_DIM ** -0.5, BF16))

    mask = (key_s[kj] >= thr_s[...]) & causal(kj)
    hg = DSA_HEAD_GROUP
    st_all = [jnp.dot(k_ref[:, g * hg * HEAD_DIM:(g + 1) * hg * HEAD_DIM], bdq_s[g],
                      preferred_element_type=F32) for g in range(N_HEADS // hg)]
    for g in range(N_HEADS // hg):
        st_g = st_all[g]
        for hh in range(hg):
            h = g * hg + hh
            st = jnp.where(mask, st_g[:, hh * qb:(hh + 1) * qb], NEG_BIG)
            m_old = m_s[h]
            part = jnp.max(st.reshape(kb // DSA_ACC_ROWS, DSA_ACC_ROWS, qb), axis=0)
            m_new = jnp.maximum(m_old, jnp.max(part, axis=0, keepdims=True))
            p = jnp.exp((st - m_new).astype(BF16))
            acc_s[h] = jnp.exp(m_old - m_new) * acc_s[h] + jnp.dot(
                vt_ref[h], p, preferred_element_type=F32)
            m_s[h] = m_new

    @pl.when(kj == j_last)
    def _():
        for h in range(N_HEADS):
            acc = acc_s[h]
            o_ref[h] = acc[:HEAD_DIM] / acc[HEAD_DIM:HEAD_DIM + 1]


def _dsa(qt, k, vt, iqt, ik, iwt, n_sel):
    t = ik.shape[0]
    qb, kb = DSA_QB, DSA_KB
    assert n_sel <= DSA_GROUPS and kb % DSA_GROUPS == 0
    nq, nk = t // qb, t // kb
    pairs = [(i, j) for i in range(nq) for j in range((i * qb + qb - 1) // kb + 1)]
    qi = jnp.asarray(np.array([p[0] for p in pairs], np.int32))
    kj = jnp.asarray(np.array([p[1] for p in pairs], np.int32))
    grid_spec = pltpu.PrefetchScalarGridSpec(
        num_scalar_prefetch=2,
        grid=(len(pairs),),
        in_specs=[pl.BlockSpec((N_HEADS, HEAD_DIM, qb), lambda s, qi, kj: (0, 0, qi[s])),
                  pl.BlockSpec((kb, D_GROUP), lambda s, qi, kj: (kj[s], 0)),
                  pl.BlockSpec((N_HEADS, DSA_V_ROWS, kb), lambda s, qi, kj: (0, 0, kj[s])),
                  pl.BlockSpec((D_IDX, qb), lambda s, qi, kj: (0, qi[s])),
                  pl.BlockSpec((t, IDX_DIM), lambda s, qi, kj: (0, 0)),
                  pl.BlockSpec((N_IDX_HEADS, qb), lambda s, qi, kj: (0, qi[s]))],
        out_specs=pl.BlockSpec((N_HEADS, HEAD_DIM, qb), lambda s, qi, kj: (0, 0, qi[s])),
        scratch_shapes=[pltpu.VMEM((nk, kb, qb), jnp.int32),
                        pltpu.VMEM((1, qb), jnp.int32),
                        pltpu.VMEM((N_HEADS, 1, qb), F32),
                        pltpu.VMEM((N_HEADS, DSA_V_ROWS, qb), F32),
                        pltpu.VMEM((N_HEADS // DSA_HEAD_GROUP, DSA_HEAD_GROUP * HEAD_DIM,
                                    DSA_HEAD_GROUP * qb), BF16),
                        pltpu.VMEM((DSA_TOP, DSA_GROUPS, qb), jnp.int32)])
    return pl.pallas_call(
        functools.partial(_dsa_kernel, n_sel=n_sel),
        grid_spec=grid_spec,
        out_shape=jax.ShapeDtypeStruct((N_HEADS, HEAD_DIM, t), F32),
        compiler_params=pltpu.CompilerParams(dimension_semantics=("arbitrary",),
                                             vmem_limit_bytes=VMEM_LIMIT),
        name="dsa",
    )(qi, kj, qt, k, vt, iqt, ik, iwt)


def _out_proj_kernel(x_ref, att_ref, rw_ref, wo_ref, n2w_ref, rwt_ref, rb_ref,
                     h_ref, xn_ref, gate_ref):
    acc = x_ref[...] + _dot(rw_ref[...], wo_ref[D_GROUP:, :])
    for h in range(N_HEADS):
        acc = acc + _dot(att_ref[h].T, wo_ref[h * HEAD_DIM:(h + 1) * HEAD_DIM, :])
    h_ref[...] = acc
    xn = acc * lax.rsqrt(jnp.mean(acc * acc, axis=-1, keepdims=True) + RMS_EPS) * n2w_ref[...]
    xn_ref[...] = xn.astype(BF16)
    logits = _dotf(xn, rwt_ref[...]) + rb_ref[...]
    lane = lax.broadcasted_iota(jnp.int32, logits.shape, 1)
    work = logits
    vals, hots = [], []
    for _ in range(TOP_K_EXPERTS):
        m = jnp.max(work, axis=1, keepdims=True)
        idx = jnp.min(jnp.where(work == m, lane, LANES), axis=1, keepdims=True)
        hot = lane == idx
        vals.append(m)
        hots.append(hot)
        work = jnp.where(hot, -jnp.inf, work)
    es = [jnp.exp(vv - vals[0]) for vv in vals]
    denom = es[0] + es[1] + es[2] + es[3]
    gates = jnp.zeros_like(logits)
    for e, hot in zip(es, hots):
        gates = gates + jnp.where(hot, e / denom, 0.0)
    gate_ref[...] = gates


def _out_proj(x2, att_t, rw, wo, n2w, rwt, rb, tm):
    t = x2.shape[0]
    full = lambda shape: pl.BlockSpec(shape, lambda i: (0,) * len(shape))
    row = lambda width: pl.BlockSpec((tm, width), lambda i: (i, 0))
    return pl.pallas_call(
        _out_proj_kernel,
        grid=(t // tm,),
        in_specs=[row(D_MODEL), pl.BlockSpec((N_HEADS, HEAD_DIM, tm), lambda i: (0, 0, i)),
                  row(D_GROUP), full((2 * D_GROUP, D_MODEL)), full((1, D_MODEL)),
                  full((D_MODEL, LANES)), full((1, LANES))],
        out_specs=[row(D_MODEL), row(D_MODEL), row(LANES)],
        out_shape=(jax.ShapeDtypeStruct((t, D_MODEL), F32),
                   jax.ShapeDtypeStruct((t, D_MODEL), BF16),
                   jax.ShapeDtypeStruct((t, LANES), F32)),
        compiler_params=pltpu.CompilerParams(dimension_semantics=("arbitrary",),
                                             vmem_limit_bytes=VMEM_LIMIT),
        name="out_proj",
    )(x2, att_t, rw, wo, n2w, rwt, rb)


MOE_TM = 1024
MOE_RB = 144
MOE_KPAD = 256
MOE_FC = 512


def _route_kernel(gate_ref, rank_ref, rankt_ref, cnt_ref):
    tm = gate_ref.shape[0]
    hot = gate_ref[...] > 0.0
    ind = jnp.where(hot, 1.0, 0.0)
    before = (lax.broadcasted_iota(jnp.int32, (tm, tm), 1)
              < lax.broadcasted_iota(jnp.int32, (tm, tm), 0))
    rank = jnp.dot(jnp.where(before, 1.0, 0.0).astype(BF16), ind.astype(BF16),
                   preferred_element_type=F32)
    rank = jnp.where(hot, rank, -1.0)
    rank_ref[...] = rank
    rankt_ref[...] = rank.T
    cnt = jnp.sum(ind, axis=0, keepdims=True)
    cnt_ref[...] = jnp.broadcast_to(cnt[None], cnt_ref.shape)


def _route(gates, tm):
    t = gates.shape[0]
    return pl.pallas_call(
        _route_kernel,
        grid=(t // tm,),
        in_specs=[pl.BlockSpec((tm, LANES), lambda i: (i, 0))],
        out_specs=[pl.BlockSpec((tm, LANES), lambda i: (i, 0)),
                   pl.BlockSpec((LANES, tm), lambda i: (0, i)),
                   pl.BlockSpec((1, 8, LANES), lambda i: (i, 0, 0))],
        out_shape=(jax.ShapeDtypeStruct((t, LANES), F32),
                   jax.ShapeDtypeStruct((LANES, t), F32),
                   jax.ShapeDtypeStruct((t // tm, 8, LANES), F32)),
        compiler_params=pltpu.CompilerParams(dimension_semantics=("arbitrary",),
                                             vmem_limit_bytes=VMEM_LIMIT),
        name="route",
    )(gates)


def _moe_kernel(cnt_ref, h_ref, xn_ref, gate_ref, rank_ref, rankt_ref, wgu_ref, bgu_ref,
                wdn_ref, bdn_ref, o_ref):
    i = pl.program_id(0)
    e = pl.program_id(1)
    tm = h_ref.shape[0]

    @pl.when(e == 0)
    def _():
        o_ref[...] = h_ref[...]

    n_rows = cnt_ref[i * N_EXPERTS + e]
    n_blocks = (n_rows + MOE_RB - 1) // MOE_RB
    pick = lax.broadcasted_iota(jnp.int32, (tm, LANES), 1) == e
    rank_col = jnp.sum(jnp.where(pick, rank_ref[...], 0.0), axis=1, keepdims=True)
    gate_col = jnp.sum(jnp.where(pick, gate_ref[...], 0.0), axis=1, keepdims=True)
    rank_row = rankt_ref[pl.ds(e, 1), :]

    def row_block(bi, carry):
        r0 = (bi * MOE_RB).astype(F32)
        rowid = lax.broadcasted_iota(jnp.int32, (MOE_RB, tm), 0).astype(F32) + r0
        sel = jnp.where(rank_row == rowid, 1.0, 0.0).astype(BF16)
        xb = jnp.dot(sel, xn_ref[...], preferred_element_type=F32).astype(BF16)
        y = jnp.zeros((MOE_RB, D_MODEL), F32)
        for fc in range(D_FF // MOE_FC):
            c0 = fc * MOE_FC
            hg = jnp.dot(xb, wgu_ref[0, :, c0:c0 + MOE_FC], preferred_element_type=F32) \
                + bgu_ref[0, :, c0:c0 + MOE_FC]
            hl = jnp.dot(xb, wgu_ref[0, :, D_FF + c0:D_FF + c0 + MOE_FC],
                         preferred_element_type=F32) + bgu_ref[0, :, D_FF + c0:D_FF + c0 + MOE_FC]
            glu = jnp.minimum(hg, SWIGLU_LIMIT)
            lin = jnp.clip(hl, -SWIGLU_LIMIT, SWIGLU_LIMIT)
            act = (lin + 1.0) * glu * jax.nn.sigmoid(SWIGLU_ALPHA * glu)
            y = y + jnp.dot(act.astype(BF16), wdn_ref[0, c0:c0 + MOE_FC, :],
                            preferred_element_type=F32)
        y = (y + bdn_ref[0]).astype(BF16)
        ypad = jnp.concatenate([y, jnp.zeros((MOE_KPAD - MOE_RB, D_MODEL), BF16)], axis=0)
        colid = lax.broadcasted_iota(jnp.int32, (tm, MOE_KPAD), 1).astype(F32) + r0
        put = jnp.where(rank_col == colid, 1.0, 0.0).astype(BF16)
        o_ref[...] += jnp.dot(put, ypad, preferred_element_type=F32) * gate_col
        return carry

    lax.fori_loop(0, n_blocks, row_block, 0)


def _moe(counts, h1, xn2, gates, rank, rank_t, wgu, bgu, wdn, bdn, tm):
    t = h1.shape[0]
    row = lambda width: pl.BlockSpec((tm, width), lambda i, e, c: (i, 0))
    grid_spec = pltpu.PrefetchScalarGridSpec(
        num_scalar_prefetch=1,
        grid=(t // tm, N_EXPERTS),
        in_specs=[row(D_MODEL), row(D_MODEL), row(LANES), row(LANES),
                  pl.BlockSpec((LANES, tm), lambda i, e, c: (0, i)),
                  pl.BlockSpec((1, D_MODEL, 2 * D_FF), lambda i, e, c: (e, 0, 0)),
                  pl.BlockSpec((1, 1, 2 * D_FF), lambda i, e, c: (e, 0, 0)),
                  pl.BlockSpec((1, D_FF, D_MODEL), lambda i, e, c: (e, 0, 0)),
                  pl.BlockSpec((1, 1, D_MODEL), lambda i, e, c: (e, 0, 0))],
        out_specs=row(D_MODEL))
    return pl.pallas_call(
        _moe_kernel,
        grid_spec=grid_spec,
        out_shape=jax.ShapeDtypeStruct((t, D_MODEL), F32),
        compiler_params=pltpu.CompilerParams(dimension_semantics=("arbitrary", "arbitrary"),
                                             vmem_limit_bytes=VMEM_LIMIT),
        name="moe",
    )(counts, h1, xn2, gates, rank, rank_t, wgu, bgu, wdn, bdn)


def _rope_tables():
    lane = jnp.arange(LANES)

    def rows(period, rot):
        half = rot // 2
        inv_freq = ROPE_THETA ** (-jnp.arange(half, dtype=F32) / half)
        jm = lane % period
        freq = jnp.where(jm < rot, inv_freq[jm % half], 0.0)
        sign = jnp.where(jm < half, -1.0, jnp.where(jm < rot, 1.0, 0.0))
        first = (jm < half).astype(F32)
        return [freq, sign, first]

    idx_rows = rows(IDX_DIM, IDX_DIM // 4)
    only_key = (lane < IDX_DIM).astype(F32)
    return jnp.stack(rows(HEAD_DIM, HEAD_DIM // 4) + idx_rows
                     + [idx_rows[0] * only_key, idx_rows[1] * only_key]).astype(F32)


def _group_matrix():
    g = np.arange(D_GROUP) // HEAD_DIM
    return jnp.asarray((g[:, None] == g[None, :]).astype(np.float32), dtype=BF16)


def _pad_rows(w, r0, rows):
    return jnp.zeros((rows, w.shape[1]), w.dtype).at[r0:r0 + w.shape[0]].set(w)


def kernel(x, positions, norm1_w, w_in, q_norm_w, k_norm_w, rwkv_mu, rwkv_w0, rwkv_w2, rwkv_a0,
           rwkv_a2, rwkv_g2, rwkv_k_k, rwkv_k_a, rwkv_r_k, rwkv_ln_w, rwkv_ln_b, w_out, norm2_w,
           router_w, router_b, exp_w_gu, exp_b_gu, exp_w_down, exp_b_down):
    b, t, _ = x.shape
    assert b == 1 and w_in.shape[0] == 1, "single sequence, single layer"
    assert t % DSA_KB == 0 and t % DSA_QB == 0 and t % MOE_TM == 0
    x2 = x[0]
    pos_f = positions[0].astype(F32)[:, None]
    n_sel = min(TOPK_MAX, t // 4)

    w = w_in[0]
    a0 = 3 * D_GROUP
    att_cols = a0 + D_IDX + IDX_DIM + N_IDX_HEADS
    w_att, w_rw = w[:, :att_cols], w[:, att_cols:]
    w_sm = jnp.zeros((D_MODEL, LANES), F32).at[:, :IDX_DIM + N_IDX_HEADS].set(w_att[:, a0 + D_IDX:])
    w_packed = jnp.concatenate(
        [w_att[:, :a0], w_rw[:, :a0], w_att[:, a0:a0 + D_IDX], w_sm, w_rw[:, a0:]],
        axis=1).astype(BF16)
    mu = rwkv_mu[0][None, :]
    tile8 = lambda z: jnp.tile(z, N_HEADS)[None, :]
    gsum = _group_matrix()

    tm = 256
    q, k, v, rr, rk, rv, iq, sm, lora = _in_proj(
        x2, pos_f, norm1_w, w_packed, mu, tile8(q_norm_w[0]), tile8(k_norm_w[0]), gsum,
        _rope_tables(), tm)

    vec = lambda z: z.reshape(1, D_GROUP)
    rw = _rwkv(rr, rk, rv, lora, vec(rwkv_w0[0]),
               _pad_rows(rwkv_w2[0], 0, LANES), vec(rwkv_a0[0]),
               _pad_rows(rwkv_a2[0], D_DECAY_LORA, LANES),
               _pad_rows(rwkv_g2[0], D_DECAY_LORA + D_AAA_LORA, LANES),
               vec(rwkv_k_k[0]), vec(rwkv_k_a[0]), vec(rwkv_r_k[0]), vec(rwkv_ln_w[0]),
               vec(rwkv_ln_b[0]), gsum, tm)

    heads = lambda z: z.reshape(t, N_HEADS, HEAD_DIM)
    vt = jnp.concatenate([heads(v).transpose(1, 2, 0),
                          jnp.ones((N_HEADS, DSA_V_ROWS - HEAD_DIM, t), BF16)], axis=1)
    att_t = _dsa(heads(q).transpose(1, 2, 0), k, vt,
                 iq.T, sm[:, :IDX_DIM].astype(BF16),
                 sm[:, IDX_DIM:IDX_DIM + N_IDX_HEADS].T, n_sel)

    rwt = jnp.zeros((D_MODEL, LANES), F32).at[:, :N_EXPERTS].set(router_w[0])
    rb = jnp.full((1, LANES), NEG_BIG, F32).at[0, :N_EXPERTS].set(router_b[0])
    h1, xn2, gates = _out_proj(x2, att_t, rw, w_out[0].astype(BF16), norm2_w, rwt, rb, tm)

    rank, rank_t, cnt = _route(gates, MOE_TM)
    counts = cnt[:, 0, :N_EXPERTS].astype(jnp.int32).reshape(-1)
    out = _moe(counts, h1, xn2, gates, rank, rank_t, exp_w_gu[0].astype(BF16),
               exp_b_gu[0][:, None, :], exp_w_down[0].astype(BF16), exp_b_down[0][:, None, :],
               MOE_TM)
    return out[None]
```

```python
import functools

import jax
import jax.numpy as jnp
import numpy as np
from jax import lax
from jax.experimental import pallas as pl
from jax.experimental.pallas import tpu as pltpu

F32 = jnp.float32
BF16 = jnp.bfloat16
HIGHEST = lax.Precision.HIGHEST

D_MODEL = 1024
HEAD_DIM = 64
N_HEADS = 8
D_GROUP = N_HEADS * HEAD_DIM
ROPE_THETA = 500000.0
N_IDX_HEADS = 8
IDX_DIM = 32
D_IDX = N_IDX_HEADS * IDX_DIM
TOPK_MAX = 256
D_DECAY_LORA = 32
D_AAA_LORA = 32
D_GATE_LORA = 64
RWKV_GN_EPS = 64e-5
N_EXPERTS = 32
TOP_K_EXPERTS = 4
D_FF = 1024
SWIGLU_LIMIT = 7.0
SWIGLU_ALPHA = 1.702
RMS_EPS = 1e-6

LANES = 128
VMEM_LIMIT = 56 * 1024 * 1024

NEG_BIG = -1e30
INT_MIN = -(2 ** 31)


def _dot(a, b):
    return jnp.dot(a.astype(BF16), b.astype(BF16), preferred_element_type=F32)


def _dotf(a, b):
    return jnp.dot(a, b, preferred_element_type=F32, precision=HIGHEST)


def _dot_nt(a, b):
    return lax.dot_general(a.astype(BF16), b.astype(BF16), (((1,), (1,)), ((), ())),
                           preferred_element_type=F32)


def _dot_tn(a, b):
    return lax.dot_general(a.astype(BF16), b.astype(BF16), (((0,), (0,)), ((), ())),
                           preferred_element_type=F32)


def _group_sum(z, g_ref):
    hi = z.astype(BF16)
    lo = (z - hi.astype(F32)).astype(BF16)
    g = g_ref[...]
    return (jnp.dot(hi, g, preferred_element_type=F32)
            + jnp.dot(lo, g, preferred_element_type=F32))


_C_Q, _C_K, _C_V, _C_RR, _C_RK, _C_RV = (i * D_GROUP for i in range(6))
_C_IQ = 6 * D_GROUP
_C_SM = _C_IQ + D_IDX
_C_LORA = _C_SM + LANES
D_IN_PACKED = _C_LORA + LANES
D_SHIFT = 3 * D_GROUP + LANES


def _in_proj_kernel(x_ref, pos_ref, n1w_ref, w_ref, mu_ref, qnw_ref, knw_ref, g_ref, rope_ref,
                    q_ref, k_ref, v_ref, rr_ref, rk_ref, rv_ref, iq_ref, sm_ref, lora_ref,
                    carry_ref):
    tm = x_ref.shape[0]

    @pl.when(pl.program_id(0) == 0)
    def _():
        carry_ref[...] = jnp.zeros_like(carry_ref)

    x = x_ref[...]
    xn = x * lax.rsqrt(jnp.mean(x * x, axis=-1, keepdims=True) + RMS_EPS) * n1w_ref[...]
    xb = xn.astype(BF16)
    pos = pos_ref[...]
    rope = rope_ref[...]

    def tables(frow, srow, reps):
        ang = pos * rope[frow:frow + 1, :]
        c = jnp.cos(ang)
        s = jnp.sin(ang) * rope[srow:srow + 1, :]
        if reps > 1:
            c = jnp.concatenate([c] * reps, axis=1)
            s = jnp.concatenate([s] * reps, axis=1)
        return c, s

    def rotary(z, c, s, first_row, half):
        w = z.shape[1]
        first = jnp.concatenate([rope[first_row:first_row + 1, :]] * (w // LANES), axis=1) > 0.5
        partner = jnp.where(first, pltpu.roll(z, w - half, 1), pltpu.roll(z, half, 1))
        return z * c + partner * s

    def head_norm(z, w_row):
        ms = _group_sum(z * z, g_ref) * (1.0 / HEAD_DIM)
        return z * lax.rsqrt(ms + RMS_EPS) * w_row

    def proj(c0, width):
        return jnp.dot(xb, w_ref[:, c0:c0 + width], preferred_element_type=F32)

    def shift(z, c0):
        width = z.shape[1]
        row = lax.broadcasted_iota(jnp.int32, z.shape, 0)
        prev = jnp.where(row == 0, carry_ref[0:1, c0:c0 + width], pltpu.roll(z, 1, 0))
        carry_ref[0:1, c0:c0 + width] = z[tm - 1:tm, :]
        return z + (prev - z) * mu_ref[:, c0:c0 + width]

    cq, sq = tables(0, 1, D_GROUP // LANES)
    q = rotary(head_norm(proj(_C_Q, D_GROUP), qnw_ref[...]), cq, sq, 2, HEAD_DIM // 8)
    q_ref[...] = q.astype(BF16)
    k = rotary(head_norm(proj(_C_K, D_GROUP), knw_ref[...]), cq, sq, 2, HEAD_DIM // 8)
    k_ref[...] = k.astype(BF16)
    v_ref[...] = proj(_C_V, D_GROUP).astype(BF16)

    rr_ref[...] = shift(proj(_C_RR, D_GROUP), 0)
    rk_ref[...] = shift(proj(_C_RK, D_GROUP), D_GROUP)
    rv_ref[...] = shift(proj(_C_RV, D_GROUP), 2 * D_GROUP)
    lora_ref[...] = shift(proj(_C_LORA, LANES), 3 * D_GROUP)

    ci, si = tables(3, 4, D_IDX // LANES)
    iq_ref[...] = rotary(proj(_C_IQ, D_IDX), ci, si, 5, IDX_DIM // 8)
    ck, sk = tables(6, 7, 1)
    sm_ref[...] = rotary(proj(_C_SM, LANES), ck, sk, 5, IDX_DIM // 8)


def _in_proj(x2, pos_f, n1w, w_packed, mu_packed, qnw, knw, gmat, rope, tm):
    t = x2.shape[0]
    full = lambda shape: pl.BlockSpec(shape, lambda i: (0,) * len(shape))
    row = lambda width: pl.BlockSpec((tm, width), lambda i: (i, 0))
    out_shapes = (
        jax.ShapeDtypeStruct((t, D_GROUP), BF16),
        jax.ShapeDtypeStruct((t, D_GROUP), BF16),
        jax.ShapeDtypeStruct((t, D_GROUP), BF16),
        jax.ShapeDtypeStruct((t, D_GROUP), F32),
        jax.ShapeDtypeStruct((t, D_GROUP), F32),
        jax.ShapeDtypeStruct((t, D_GROUP), F32),
        jax.ShapeDtypeStruct((t, D_IDX), F32),
        jax.ShapeDtypeStruct((t, LANES), F32),
        jax.ShapeDtypeStruct((t, LANES), F32),
    )
    return pl.pallas_call(
        _in_proj_kernel,
        grid=(t // tm,),
        in_specs=[row(D_MODEL), row(1), full((1, D_MODEL)), full((D_MODEL, D_IN_PACKED)),
                  full((1, D_SHIFT)), full((1, D_GROUP)), full((1, D_GROUP)),
                  full((D_GROUP, D_GROUP)), full((8, LANES))],
        out_specs=[row(D_GROUP)] * 6 + [row(D_IDX), row(LANES), row(LANES)],
        out_shape=out_shapes,
        scratch_shapes=[pltpu.VMEM((8, D_SHIFT), F32)],
        compiler_params=pltpu.CompilerParams(dimension_semantics=("arbitrary",),
                                             vmem_limit_bytes=VMEM_LIMIT),
        name="in_proj",
    )(x2, pos_f, n1w, w_packed, mu_packed, qnw, knw, gmat, rope)


RWKV_CHUNK = 64
PAIR = 2 * HEAD_DIM


def _rwkv_kernel(r_ref, k_ref, v_ref, lora_ref, w0_ref, w2_ref, a0_ref, a2_ref, g2_ref,
                 kk_ref, ka_ref, rk_ref, lnw_ref, lnb_ref, g_ref,
                 o_ref,
                 s_ref, ld_s, r_s, k2_s, b_s, kk_s, y_s):
    tm = r_ref.shape[0]
    n_chunks = tm // RWKV_CHUNK
    n_pairs = D_GROUP // PAIR
    c = RWKV_CHUNK

    @pl.when(pl.program_id(0) == 0)
    def _():
        s_ref[...] = jnp.zeros_like(s_ref)

    lora = lora_ref[...]
    r = r_ref[...]
    k = k_ref[...]
    v = v_ref[...]
    zarg = w0_ref[...] + _dotf(jnp.tanh(lora), w2_ref[...])
    sp = jnp.maximum(-zarg, 0.0) + jnp.log1p(jnp.exp(-jnp.abs(zarg)))
    ld_s[...] = -jnp.exp(-sp - 0.5)
    a = jax.nn.sigmoid(a0_ref[...] + _dotf(lora, a2_ref[...]))
    g = _dotf(jax.nn.sigmoid(lora), g2_ref[...])
    kk = k * kk_ref[...]
    kk = kk * lax.rsqrt(jnp.maximum(_group_sum(kk * kk, g_ref), 1e-24))
    k2 = k * (1.0 + (a - 1.0) * ka_ref[...])
    bonus = _group_sum(r * k2 * rk_ref[...], g_ref) * v
    r_s[...] = r
    k2_s[...] = k2
    kk_s[...] = kk
    b_s[...] = kk * a

    row = lax.broadcasted_iota(jnp.int32, (2 * c, 2 * c), 0)
    col = lax.broadcasted_iota(jnp.int32, (2 * c, 2 * c), 1)
    same_head = (row >= c) == (col >= c)
    strict = same_head & (col < row)
    incl = same_head & (col <= row)
    eye = (row == col).astype(F32)
    tri = (lax.broadcasted_iota(jnp.int32, (c, c), 1)
           <= lax.broadcasted_iota(jnp.int32, (c, c), 0)).astype(F32)
    lane = lax.broadcasted_iota(jnp.int32, (c, PAIR), 1)
    head0 = lane < HEAD_DIM

    def stack(z):
        return jnp.concatenate([jnp.where(head0, z, 0.0), jnp.where(head0, 0.0, z)], axis=0)

    def chunk_body(ci, carry):
        r0 = pl.multiple_of(ci * c, c)
        ld_all = ld_s[pl.ds(r0, c), :]
        cum_all = _dotf(tri, ld_all)
        c2 = 2 * c
        pairs = range(n_pairs)
        rt, kt, bt, kp, vs, lhs, rhs, gam_end = [], [], [], [], [], [], [], []
        for p in pairs:
            cols = slice(p * PAIR, (p + 1) * PAIR)
            ld = ld_all[:, cols]
            cum = cum_all[:, cols]
            gam = jnp.exp(cum)
            inv = jnp.exp(-cum)
            gam_prev = jnp.exp(cum - ld)
            gam_end.append(gam[c - 1:c, :])
            rt.append(stack(r_s[pl.ds(r0, c), cols] * gam))
            kt_raw = k2_s[pl.ds(r0, c), cols] * inv
            bt_raw = b_s[pl.ds(r0, c), cols] * inv
            kt.append(stack(kt_raw))
            bt.append(stack(bt_raw))
            kp.append(stack(kk_s[pl.ds(r0, c), cols] * gam_prev))
            vs.append(stack(v_ref[pl.ds(r0, c), cols]))
            lhs.append(jnp.concatenate([kp[p], rt[p]], axis=0))
            rhs.append(jnp.concatenate([bt_raw, bt_raw, kt_raw, kt_raw], axis=0))
        aa = [_dot_nt(lhs[p], rhs[p]) for p in pairs]
        a_kk = [jnp.where(strict, aa[p][:c2, c2:], 0.0) for p in pairs]
        a_rb = [jnp.where(incl, aa[p][c2:, :c2], 0.0) for p in pairs]
        a_rk = [jnp.where(incl, aa[p][c2:, c2:], 0.0) for p in pairs]
        n = [-jnp.where(strict, aa[p][:c2, :c2], 0.0) for p in pairs]
        prod = [eye + n[p] for p in pairs]
        n = [_dot(n[p], n[p]) for p in pairs]
        for _ in range(int(np.log2(c)) - 2):
            both = [_dot(jnp.concatenate([n[p], prod[p]], axis=0), n[p]) for p in pairs]
            prod = [prod[p] + both[p][c2:] for p in pairs]
            n = [both[p][:c2] for p in pairs]
        av = [_dot(jnp.concatenate([a_kk[p], a_rk[p]], axis=0), vs[p]) for p in pairs]
        tinv = [prod[p] + _dot(prod[p], n[p]) for p in pairs]
        wu = [_dot(tinv[p], jnp.concatenate([kp[p], av[p][:c2]], axis=1)) for p in pairs]
        rb = [_dot(a_rb[p], wu[p]) for p in pairs]
        tn = [_dot_tn(jnp.concatenate([wu[p], vs[p]], axis=1),
                      jnp.concatenate([bt[p], kt[p]], axis=1)) for p in pairs]
        s0 = [s_ref[p] for p in pairs]
        ys = [_dot_nt(rt[p] - rb[p][:, :PAIR], s0[p]) + (av[p][c2:] - rb[p][:, PAIR:])
              for p in pairs]
        s_mix = [(eye - tn[p][:PAIR, :PAIR]) * gam_end[p] for p in pairs]
        s_new = [_dot(s0[p], s_mix[p]) for p in pairs]
        for p in pairs:
            y_s[pl.ds(r0, c), p * PAIR:(p + 1) * PAIR] = ys[p][:c] + ys[p][c:]
            s_ref[p] = s_new[p] + (tn[p][2 * PAIR:, PAIR:] - tn[p][PAIR:2 * PAIR, :PAIR]) * gam_end[p]
        return carry

    lax.fori_loop(0, n_chunks, chunk_body, 0)

    y = y_s[...]
    mean = _group_sum(y, g_ref) * (1.0 / HEAD_DIM)
    yc = y - mean
    var = _group_sum(yc * yc, g_ref) * (1.0 / HEAD_DIM)
    yn = yc * lax.rsqrt(var + RWKV_GN_EPS) * lnw_ref[...] + lnb_ref[...]
    o_ref[...] = (yn + bonus) * g


def _rwkv(rr, rk, rv, lora, w0, w2p, a0, a2p, g2p, k_k, k_a, r_k, ln_w, ln_b, gsum, tm):
    t = rr.shape[0]
    full = lambda shape: pl.BlockSpec(shape, lambda i: (0,) * len(shape))
    row = lambda width: pl.BlockSpec((tm, width), lambda i: (i, 0))
    vec = full((1, D_GROUP))
    big = pltpu.VMEM((tm, D_GROUP), F32)
    return pl.pallas_call(
        _rwkv_kernel,
        grid=(t // tm,),
        in_specs=[row(D_GROUP), row(D_GROUP), row(D_GROUP), row(LANES),
                  vec, full((LANES, D_GROUP)), vec, full((LANES, D_GROUP)),
                  full((LANES, D_GROUP)), vec, vec, vec, vec, vec, full((D_GROUP, D_GROUP))],
        out_specs=row(D_GROUP),
        out_shape=jax.ShapeDtypeStruct((t, D_GROUP), F32),
        scratch_shapes=[pltpu.VMEM((D_GROUP // PAIR, PAIR, PAIR), F32),
                        big, big, big, big, big, big],
        compiler_params=pltpu.CompilerParams(dimension_semantics=("arbitrary",),
                                             vmem_limit_bytes=VMEM_LIMIT),
        name="rwkv",
    )(rr, rk, rv, lora, w0, w2p, a0, a2p, g2p, k_k, k_a, r_k, ln_w, ln_b, gsum)


DSA_QB = 256
DSA_KB = 512
M_INIT = -5e29
DSA_HEAD_GROUP = 4
DSA_ACC_ROWS = 32
DSA_GROUPS = 256
DSA_TOP = 10
DSA_V_ROWS = HEAD_DIM + 16


def _dsa_kernel(qi_ref, kj_ref, qt_ref, k_ref, vt_ref, iqt_ref, ik_ref, iwt_ref, o_ref,
                key_s, thr_s, m_s, acc_s, bdq_s, top_s, *, n_sel):
    qb = qt_ref.shape[2]
    kb = k_ref.shape[0]
    step = pl.program_id(0)
    qi = qi_ref[step]
    kj = kj_ref[step]
    q0 = qi * qb
    j_last = (q0 + qb - 1) // kb
    n_kc = j_last + 1
    idx_scale = float((IDX_DIM * N_IDX_HEADS) ** -0.5)

    def causal(jblk):
        s_pos = jblk * kb + lax.broadcasted_iota(jnp.int32, (kb, qb), 0)
        t_pos = q0 + lax.broadcasted_iota(jnp.int32, (kb, qb), 1)
        return s_pos <= t_pos

    @pl.when(kj == 0)
    def _():
        iqt = iqt_ref[...].astype(BF16)
        iwt = iwt_ref[...] * idx_scale

        def score_chunk(kc, diagonal):
            ik = ik_ref[pl.ds(pl.multiple_of(kc * kb, kb), kb), :]
            score = jnp.zeros((kb, qb), F32)
            for h in range(N_IDX_HEADS):
                d = jnp.dot(ik, iqt[h * IDX_DIM:(h + 1) * IDX_DIM, :],
                            preferred_element_type=F32)
                score = score + jnp.maximum(d, 0.0) * iwt[h:h + 1, :]
            bits = pltpu.bitcast(score, jnp.int32)
            keys = jnp.where(bits < 0, bits ^ jnp.int32(0x7FFFFFFF), bits)
            if diagonal:
                keys = jnp.where(causal(kc), keys, jnp.int32(INT_MIN))
            key_s[kc] = keys
            xs = [keys[r * DSA_GROUPS:(r + 1) * DSA_GROUPS] for r in range(kb // DSA_GROUPS)]
            for lvl in range(DSA_TOP):
                s = top_s[lvl]
                for r in range(len(xs)):
                    s, xs[r] = jnp.maximum(s, xs[r]), jnp.minimum(s, xs[r])
                top_s[lvl] = s

        top_s[...] = jnp.full_like(top_s, INT_MIN)

        def full_chunk(kc, carry):
            score_chunk(kc, False)
            return carry

        lax.fori_loop(0, j_last, full_chunk, 0)
        score_chunk(j_last, True)

        def select(count_keys):
            def bit_step(b, cur):
                bit = lax.shift_left(jnp.int32(1), jnp.int32(31) - b)
                cand = (cur | bit) ^ jnp.int32(INT_MIN)
                cnt = jnp.sum(count_keys(cand), axis=0, keepdims=True)
                return jnp.where(cnt >= float(n_sel), cur | bit, cur)

            cur = lax.fori_loop(0, 32, bit_step, jnp.zeros((1, qb), jnp.int32))
            return jnp.maximum(cur ^ jnp.int32(INT_MIN), jnp.int32(INT_MIN + 1))

        def part_count(keys, cand):
            ind = jnp.where(keys >= cand, 1.0, 0.0)
            return jnp.sum(ind.reshape(-1, DSA_ACC_ROWS, qb), axis=0)

        def count_top(cand):
            return lax.fori_loop(0, DSA_TOP, lambda lvl, a: a + part_count(top_s[lvl], cand),
                                 jnp.zeros((DSA_ACC_ROWS, qb), F32))

        def count_all(cand):
            return lax.fori_loop(0, n_kc, lambda kc, a: a + part_count(key_s[kc], cand),
                                 jnp.zeros((DSA_ACC_ROWS, qb), F32))

        thr = select(count_top)
        thr_s[...] = thr
        hidden = jnp.where(top_s[DSA_TOP - 1] >= thr, 1.0, 0.0)

        @pl.when(jnp.max(hidden) > 0.0)
        def _():
            thr_s[...] = select(count_all)

        m_s[...] = jnp.full_like(m_s, M_INIT)
        acc_s[...] = jnp.zeros_like(acc_s)
        bdq_s[...] = jnp.zeros_like(bdq_s)
        for h in range(N_HEADS):
            g, hh = divmod(h, DSA_HEAD_GROUP)
            bdq_s[g, hh * HEAD_DIM:(hh + 1) * HEAD_DIM, hh * qb:(hh + 1) * qb] = (
                qt_ref[h] * jnp.asarray(HEAD_DIM ** -0.5, BF16))

    mask = key_s[kj] >= thr_s[...]
    hg = DSA_HEAD_GROUP
    st_all = [jnp.dot(k_ref[:, g * hg * HEAD_DIM:(g + 1) * hg * HEAD_DIM], bdq_s[g],
                      preferred_element_type=F32) for g in range(N_HEADS // hg)]
    for g in range(N_HEADS // hg):
        st_g = st_all[g]
        for hh in range(hg):
            h = g * hg + hh
            st = jnp.where(mask, st_g[:, hh * qb:(hh + 1) * qb], NEG_BIG)
            m_old = m_s[h]
            part = jnp.max(st.reshape(kb // DSA_ACC_ROWS, DSA_ACC_ROWS, qb), axis=0)
            m_new = jnp.maximum(m_old, jnp.max(part, axis=0, keepdims=True))
            p = jnp.exp((st - m_new).astype(BF16))
            acc_s[h] = jnp.exp(m_old - m_new) * acc_s[h] + jnp.dot(
                vt_ref[h], p, preferred_element_type=F32)
            m_s[h] = m_new

    @pl.when(kj == j_last)
    def _():
        for h in range(N_HEADS):
            acc = acc_s[h]
            o_ref[h] = acc[:HEAD_DIM] / acc[HEAD_DIM:HEAD_DIM + 1]


def _dsa(qt, k, vt, iqt, ik, iwt, n_sel):
    t = ik.shape[0]
    qb, kb = DSA_QB, DSA_KB
    assert n_sel <= DSA_GROUPS and kb % DSA_GROUPS == 0
    nq, nk = t // qb, t // kb
    pairs = [(i, j) for i in range(nq) for j in range((i * qb + qb - 1) // kb + 1)]
    qi = jnp.asarray(np.array([p[0] for p in pairs], np.int32))
    kj = jnp.asarray(np.array([p[1] for p in pairs], np.int32))
    grid_spec = pltpu.PrefetchScalarGridSpec(
        num_scalar_prefetch=2,
        grid=(len(pairs),),
        in_specs=[pl.BlockSpec((N_HEADS, HEAD_DIM, qb), lambda s, qi, kj: (0, 0, qi[s])),
                  pl.BlockSpec((kb, D_GROUP), lambda s, qi, kj: (kj[s], 0)),
                  pl.BlockSpec((N_HEADS, DSA_V_ROWS, kb), lambda s, qi, kj: (0, 0, kj[s])),
                  pl.BlockSpec((D_IDX, qb), lambda s, qi, kj: (0, qi[s])),
                  pl.BlockSpec((t, IDX_DIM), lambda s, qi, kj: (0, 0)),
                  pl.BlockSpec((N_IDX_HEADS, qb), lambda s, qi, kj: (0, qi[s]))],
        out_specs=pl.BlockSpec((N_HEADS, HEAD_DIM, qb), lambda s, qi, kj: (0, 0, qi[s])),
        scratch_shapes=[pltpu.VMEM((nk, kb, qb), jnp.int32),
                        pltpu.VMEM((1, qb), jnp.int32),
                        pltpu.VMEM((N_HEADS, 1, qb), F32),
                        pltpu.VMEM((N_HEADS, DSA_V_ROWS, qb), F32),
                        pltpu.VMEM((N_HEADS // DSA_HEAD_GROUP, DSA_HEAD_GROUP * HEAD_DIM,
                                    DSA_HEAD_GROUP * qb), BF16),
                        pltpu.VMEM((DSA_TOP, DSA_GROUPS, qb), jnp.int32)])
    return pl.pallas_call(
        functools.partial(_dsa_kernel, n_sel=n_sel),
        grid_spec=grid_spec,
        out_shape=jax.ShapeDtypeStruct((N_HEADS, HEAD_DIM, t), F32),
        compiler_params=pltpu.CompilerParams(dimension_semantics=("arbitrary",),
                                             vmem_limit_bytes=VMEM_LIMIT),
        name="dsa",
    )(qi, kj, qt, k, vt, iqt, ik, iwt)


def _out_proj_kernel(x_ref, att_ref, rw_ref, wo_ref, n2w_ref, rwt_ref, rb_ref,
                     h_ref, xn_ref, gate_ref):
    acc = x_ref[...] + _dot(rw_ref[...], wo_ref[D_GROUP:, :])
    for h in range(N_HEADS):
        acc = acc + _dot(att_ref[h].T, wo_ref[h * HEAD_DIM:(h + 1) * HEAD_DIM, :])
    h_ref[...] = acc
    xn = acc * lax.rsqrt(jnp.mean(acc * acc, axis=-1, keepdims=True) + RMS_EPS) * n2w_ref[...]
    xn_ref[...] = xn.astype(BF16)
    logits = _dotf(xn, rwt_ref[...]) + rb_ref[...]
    lane = lax.broadcasted_iota(jnp.int32, logits.shape, 1)
    work = logits
    vals, hots = [], []
    for _ in range(TOP_K_EXPERTS):
        m = jnp.max(work, axis=1, keepdims=True)
        idx = jnp.min(jnp.where(work == m, lane, LANES), axis=1, keepdims=True)
        hot = lane == idx
        vals.append(m)
        hots.append(hot)
        work = jnp.where(hot, -jnp.inf, work)
    es = [jnp.exp(vv - vals[0]) for vv in vals]
    denom = es[0] + es[1] + es[2] + es[3]
    gates = jnp.zeros_like(logits)
    for e, hot in zip(es, hots):
        gates = gates + jnp.where(hot, e / denom, 0.0)
    gate_ref[...] = gates


def _out_proj(x2, att_t, rw, wo, n2w, rwt, rb, tm):
    t = x2.shape[0]
    full = lambda shape: pl.BlockSpec(shape, lambda i: (0,) * len(shape))
    row = lambda width: pl.BlockSpec((tm, width), lambda i: (i, 0))
    return pl.pallas_call(
        _out_proj_kernel,
        grid=(t // tm,),
        in_specs=[row(D_MODEL), pl.BlockSpec((N_HEADS, HEAD_DIM, tm), lambda i: (0, 0, i)),
                  row(D_GROUP), full((2 * D_GROUP, D_MODEL)), full((1, D_MODEL)),
                  full((D_MODEL, LANES)), full((1, LANES))],
        out_specs=[row(D_MODEL), row(D_MODEL), row(LANES)],
        out_shape=(jax.ShapeDtypeStruct((t, D_MODEL), F32),
                   jax.ShapeDtypeStruct((t, D_MODEL), BF16),
                   jax.ShapeDtypeStruct((t, LANES), F32)),
        compiler_params=pltpu.CompilerParams(dimension_semantics=("arbitrary",),
                                             vmem_limit_bytes=VMEM_LIMIT),
        name="out_proj",
    )(x2, att_t, rw, wo, n2w, rwt, rb)


MOE_TM = 1024
MOE_RB = 144
MOE_FC = 512


def _route_kernel(gate_ref, rankt_ref, gatet_ref, cnt_ref):
    tm = gate_ref.shape[0]
    gates = gate_ref[...]
    hot = gates > 0.0
    ind = jnp.where(hot, 1.0, 0.0)
    before = (lax.broadcasted_iota(jnp.int32, (tm, tm), 1)
              < lax.broadcasted_iota(jnp.int32, (tm, tm), 0))
    rank = jnp.dot(jnp.where(before, 1.0, 0.0).astype(BF16), ind.astype(BF16),
                   preferred_element_type=F32)
    rankt_ref[...] = jnp.where(hot, rank, -1.0).T
    gatet_ref[...] = gates.T
    cnt = jnp.sum(ind, axis=0, keepdims=True)
    cnt_ref[...] = jnp.broadcast_to(cnt[None], cnt_ref.shape)


def _route(gates, tm):
    t = gates.shape[0]
    return pl.pallas_call(
        _route_kernel,
        grid=(t // tm,),
        in_specs=[pl.BlockSpec((tm, LANES), lambda i: (i, 0))],
        out_specs=[pl.BlockSpec((LANES, tm), lambda i: (0, i)),
                   pl.BlockSpec((LANES, tm), lambda i: (0, i)),
                   pl.BlockSpec((1, 8, LANES), lambda i: (i, 0, 0))],
        out_shape=(jax.ShapeDtypeStruct((LANES, t), F32),
                   jax.ShapeDtypeStruct((LANES, t), F32),
                   jax.ShapeDtypeStruct((t // tm, 8, LANES), F32)),
        compiler_params=pltpu.CompilerParams(dimension_semantics=("arbitrary",),
                                             vmem_limit_bytes=VMEM_LIMIT),
        name="route",
    )(gates)


def _moe_kernel(cnt_ref, h_ref, xn_ref, rankt_ref, gatet_ref, wgu_ref, bgu_ref,
                wdn_ref, bdn_ref, o_ref):
    i = pl.program_id(0)
    e = pl.program_id(1)
    tm = h_ref.shape[0]

    @pl.when(e == 0)
    def _():
        o_ref[...] = h_ref[...]

    n_rows = cnt_ref[i * N_EXPERTS + e]
    n_blocks = (n_rows + MOE_RB - 1) // MOE_RB
    rank_row = rankt_ref[pl.ds(e, 1), :]
    gate_row = gatet_ref[pl.ds(e, 1), :]

    def row_block(bi, carry):
        r0 = (bi * MOE_RB).astype(F32)
        rowid = lax.broadcasted_iota(jnp.int32, (MOE_RB, tm), 0).astype(F32) + r0
        hit = rank_row == rowid
        sel = jnp.where(hit, 1.0, 0.0).astype(BF16)
        gate = jnp.sum(jnp.where(hit, gate_row, 0.0), axis=1, keepdims=True)
        xb = jnp.dot(sel, xn_ref[...], preferred_element_type=F32).astype(BF16)
        y = jnp.zeros((MOE_RB, D_MODEL), F32)
        ups = []
        for fc in range(D_FF // MOE_FC):
            c0 = fc * MOE_FC
            ups.append((jnp.dot(xb, wgu_ref[0, :, c0:c0 + MOE_FC], preferred_element_type=F32),
                        jnp.dot(xb, wgu_ref[0, :, D_FF + c0:D_FF + c0 + MOE_FC],
                                preferred_element_type=F32)))
        for fc in range(D_FF // MOE_FC):
            c0 = fc * MOE_FC
            hg = ups[fc][0] + bgu_ref[0, :, c0:c0 + MOE_FC]
            hl = ups[fc][1] + bgu_ref[0, :, D_FF + c0:D_FF + c0 + MOE_FC]
            glu = jnp.minimum(hg, SWIGLU_LIMIT)
            lin = jnp.clip(hl, -SWIGLU_LIMIT, SWIGLU_LIMIT)
            act = (lin + 1.0) * glu * jax.nn.sigmoid(SWIGLU_ALPHA * glu)
            y = y + jnp.dot(act.astype(BF16), wdn_ref[0, c0:c0 + MOE_FC, :],
                            preferred_element_type=F32)
        y = ((y + bdn_ref[0]) * gate).astype(BF16)
        o_ref[...] += _dot_tn(sel, y)
        return carry

    lax.fori_loop(0, n_blocks, row_block, 0)


def _moe(counts, h1, xn2, rank_t, gate_t, wgu, bgu, wdn, bdn, tm):
    t = h1.shape[0]
    row = lambda width: pl.BlockSpec((tm, width), lambda i, e, c: (i, 0))
    grid_spec = pltpu.PrefetchScalarGridSpec(
        num_scalar_prefetch=1,
        grid=(t // tm, N_EXPERTS),
        in_specs=[row(D_MODEL), row(D_MODEL),
                  pl.BlockSpec((LANES, tm), lambda i, e, c: (0, i)),
                  pl.BlockSpec((LANES, tm), lambda i, e, c: (0, i)),
                  pl.BlockSpec((1, D_MODEL, 2 * D_FF), lambda i, e, c: (e, 0, 0)),
                  pl.BlockSpec((1, 1, 2 * D_FF), lambda i, e, c: (e, 0, 0)),
                  pl.BlockSpec((1, D_FF, D_MODEL), lambda i, e, c: (e, 0, 0)),
                  pl.BlockSpec((1, 1, D_MODEL), lambda i, e, c: (e, 0, 0))],
        out_specs=row(D_MODEL))
    return pl.pallas_call(
        _moe_kernel,
        grid_spec=grid_spec,
        out_shape=jax.ShapeDtypeStruct((t, D_MODEL), F32),
        compiler_params=pltpu.CompilerParams(dimension_semantics=("arbitrary", "arbitrary"),
                                             vmem_limit_bytes=VMEM_LIMIT),
        name="moe",
    )(counts, h1, xn2, rank_t, gate_t, wgu, bgu, wdn, bdn)


def _rope_tables():
    lane = jnp.arange(LANES)

    def rows(period, rot):
        half = rot // 2
        inv_freq = ROPE_THETA ** (-jnp.arange(half, dtype=F32) / half)
        jm = lane % period
        freq = jnp.where(jm < rot, inv_freq[jm % half], 0.0)
        sign = jnp.where(jm < half, -1.0, jnp.where(jm < rot, 1.0, 0.0))
        first = (jm < half).astype(F32)
        return [freq, sign, first]

    idx_rows = rows(IDX_DIM, IDX_DIM // 4)
    only_key = (lane < IDX_DIM).astype(F32)
    return jnp.stack(rows(HEAD_DIM, HEAD_DIM // 4) + idx_rows
                     + [idx_rows[0] * only_key, idx_rows[1] * only_key]).astype(F32)


def _group_matrix():
    g = np.arange(D_GROUP) // HEAD_DIM
    return jnp.asarray((g[:, None] == g[None, :]).astype(np.float32), dtype=BF16)


def _pad_rows(w, r0, rows):
    return jnp.zeros((rows, w.shape[1]), w.dtype).at[r0:r0 + w.shape[0]].set(w)


def kernel(x, positions, norm1_w, w_in, q_norm_w, k_norm_w, rwkv_mu, rwkv_w0, rwkv_w2, rwkv_a0,
           rwkv_a2, rwkv_g2, rwkv_k_k, rwkv_k_a, rwkv_r_k, rwkv_ln_w, rwkv_ln_b, w_out, norm2_w,
           router_w, router_b, exp_w_gu, exp_b_gu, exp_w_down, exp_b_down):
    b, t, _ = x.shape
    assert b == 1 and w_in.shape[0] == 1, "single sequence, single layer"
    assert t % DSA_KB == 0 and t % DSA_QB == 0 and t % MOE_TM == 0
    x2 = x[0]
    pos_f = positions[0].astype(F32)[:, None]
    n_sel = min(TOPK_MAX, t // 4)

    w = w_in[0]
    a0 = 3 * D_GROUP
    att_cols = a0 + D_IDX + IDX_DIM + N_IDX_HEADS
    w_att, w_rw = w[:, :att_cols], w[:, att_cols:]
    w_sm = jnp.zeros((D_MODEL, LANES), F32).at[:, :IDX_DIM + N_IDX_HEADS].set(w_att[:, a0 + D_IDX:])
    w_packed = jnp.concatenate(
        [w_att[:, :a0], w_rw[:, :a0], w_att[:, a0:a0 + D_IDX], w_sm, w_rw[:, a0:]],
        axis=1).astype(BF16)
    mu = rwkv_mu[0][None, :]
    tile8 = lambda z: jnp.tile(z, N_HEADS)[None, :]
    gsum = _group_matrix()

    tm = 256
    q, k, v, rr, rk, rv, iq, sm, lora = _in_proj(
        x2, pos_f, norm1_w, w_packed, mu, tile8(q_norm_w[0]), tile8(k_norm_w[0]), gsum,
        _rope_tables(), tm)

    vec = lambda z: z.reshape(1, D_GROUP)
    rw = _rwkv(rr, rk, rv, lora, vec(rwkv_w0[0]),
               _pad_rows(rwkv_w2[0], 0, LANES), vec(rwkv_a0[0]),
               _pad_rows(rwkv_a2[0], D_DECAY_LORA, LANES),
               _pad_rows(rwkv_g2[0], D_DECAY_LORA + D_AAA_LORA, LANES),
               vec(rwkv_k_k[0]), vec(rwkv_k_a[0]), vec(rwkv_r_k[0]), vec(rwkv_ln_w[0]),
               vec(rwkv_ln_b[0]), gsum, tm)

    heads = lambda z: z.reshape(t, N_HEADS, HEAD_DIM)
    vt = jnp.concatenate([heads(v).transpose(1, 2, 0),
                          jnp.ones((N_HEADS, DSA_V_ROWS - HEAD_DIM, t), BF16)], axis=1)
    att_t = _dsa(heads(q).transpose(1, 2, 0), k, vt,
                 iq.T, sm[:, :IDX_DIM].astype(BF16),
                 sm[:, IDX_DIM:IDX_DIM + N_IDX_HEADS].T, n_sel)

    rwt = jnp.zeros((D_MODEL, LANES), F32).at[:, :N_EXPERTS].set(router_w[0])
    rb = jnp.full((1, LANES), NEG_BIG, F32).at[0, :N_EXPERTS].set(router_b[0])
    h1, xn2, gates = _out_proj(x2, att_t, rw, w_out[0].astype(BF16), norm2_w, rwt, rb, tm)

    rank_t, gate_t, cnt = _route(gates, MOE_TM)
    counts = cnt[:, 0, :N_EXPERTS].astype(jnp.int32).reshape(-1)
    out = _moe(counts, h1, xn2, rank_t, gate_t, exp_w_gu[0].astype(BF16),
               exp_b_gu[0][:, None, :], exp_w_down[0].astype(BF16), exp_b_down[0][:, None, :],
               MOE_TM)
    return out[None]
```

```python
import functools

import jax
import jax.numpy as jnp
import numpy as np
from jax import lax
from jax.experimental import pallas as pl
from jax.experimental.pallas import tpu as pltpu

F32 = jnp.float32
BF16 = jnp.bfloat16
HIGHEST = lax.Precision.HIGHEST

D_MODEL = 1024
HEAD_DIM = 64
N_HEADS = 8
D_GROUP = N_HEADS * HEAD_DIM
ROPE_THETA = 500000.0
N_IDX_HEADS = 8
IDX_DIM = 32
D_IDX = N_IDX_HEADS * IDX_DIM
TOPK_MAX = 256
D_DECAY_LORA = 32
D_AAA_LORA = 32
D_GATE_LORA = 64
RWKV_GN_EPS = 64e-5
N_EXPERTS = 32
TOP_K_EXPERTS = 4
D_FF = 1024
SWIGLU_LIMIT = 7.0
SWIGLU_ALPHA = 1.702
RMS_EPS = 1e-6

LANES = 128
VMEM_LIMIT = 56 * 1024 * 1024

NEG_BIG = -1e30
INT_MIN = -(2 ** 31)


def _dot(a, b):
    return jnp.dot(a.astype(BF16), b.astype(BF16), preferred_element_type=F32)


def _dotf(a, b):
    return jnp.dot(a, b, preferred_element_type=F32, precision=HIGHEST)


def _dot_nt(a, b):
    return lax.dot_general(a.astype(BF16), b.astype(BF16), (((1,), (1,)), ((), ())),
                           preferred_element_type=F32)


def _dot_tn(a, b):
    return lax.dot_general(a.astype(BF16), b.astype(BF16), (((0,), (0,)), ((), ())),
                           preferred_element_type=F32)


def _group_sum(z, g_ref):
    hi = z.astype(BF16)
    lo = (z - hi.astype(F32)).astype(BF16)
    g = g_ref[...]
    return (jnp.dot(hi, g, preferred_element_type=F32)
            + jnp.dot(lo, g, preferred_element_type=F32))


_C_Q, _C_K, _C_V, _C_RR, _C_RK, _C_RV = (i * D_GROUP for i in range(6))
_C_IQ = 6 * D_GROUP
_C_SM = _C_IQ + D_IDX
_C_LORA = _C_SM + LANES
D_IN_PACKED = _C_LORA + LANES
D_SHIFT = 3 * D_GROUP + LANES


def _in_proj_kernel(x_ref, pos_ref, n1w_ref, w_ref, mu_ref, qnw_ref, knw_ref, g_ref, rope_ref,
                    q_ref, k_ref, v_ref, rr_ref, rk_ref, rv_ref, iq_ref, sm_ref, lora_ref,
                    carry_ref):
    tm = x_ref.shape[0]

    @pl.when(pl.program_id(0) == 0)
    def _():
        carry_ref[...] = jnp.zeros_like(carry_ref)

    x = x_ref[...]
    xn = x * lax.rsqrt(jnp.mean(x * x, axis=-1, keepdims=True) + RMS_EPS) * n1w_ref[...]
    xb = xn.astype(BF16)
    pos = pos_ref[...]
    rope = rope_ref[...]

    def tables(frow, srow, reps):
        ang = pos * rope[frow:frow + 1, :]
        c = jnp.cos(ang)
        s = jnp.sin(ang) * rope[srow:srow + 1, :]
        if reps > 1:
            c = jnp.concatenate([c] * reps, axis=1)
            s = jnp.concatenate([s] * reps, axis=1)
        return c, s

    def rotary(z, c, s, first_row, half):
        w = z.shape[1]
        first = jnp.concatenate([rope[first_row:first_row + 1, :]] * (w // LANES), axis=1) > 0.5
        partner = jnp.where(first, pltpu.roll(z, w - half, 1), pltpu.roll(z, half, 1))
        return z * c + partner * s

    def head_norm(z, w_row):
        ms = _group_sum(z * z, g_ref) * (1.0 / HEAD_DIM)
        return z * lax.rsqrt(ms + RMS_EPS) * w_row

    def proj(c0, width):
        return jnp.dot(xb, w_ref[:, c0:c0 + width], preferred_element_type=F32)

    def shift(z, c0):
        width = z.shape[1]
        row = lax.broadcasted_iota(jnp.int32, z.shape, 0)
        prev = jnp.where(row == 0, carry_ref[0:1, c0:c0 + width], pltpu.roll(z, 1, 0))
        carry_ref[0:1, c0:c0 + width] = z[tm - 1:tm, :]
        return z + (prev - z) * mu_ref[:, c0:c0 + width]

    cq, sq = tables(0, 1, D_GROUP // LANES)
    q = rotary(head_norm(proj(_C_Q, D_GROUP), qnw_ref[...]), cq, sq, 2, HEAD_DIM // 8)
    q_ref[...] = q.astype(BF16)
    k = rotary(head_norm(proj(_C_K, D_GROUP), knw_ref[...]), cq, sq, 2, HEAD_DIM // 8)
    k_ref[...] = k.astype(BF16)
    v_ref[...] = proj(_C_V, D_GROUP).astype(BF16)

    rr_ref[...] = shift(proj(_C_RR, D_GROUP), 0)
    rk_ref[...] = shift(proj(_C_RK, D_GROUP), D_GROUP)
    rv_ref[...] = shift(proj(_C_RV, D_GROUP), 2 * D_GROUP)
    lora_ref[...] = shift(proj(_C_LORA, LANES), 3 * D_GROUP)

    ci, si = tables(3, 4, D_IDX // LANES)
    iq_ref[...] = rotary(proj(_C_IQ, D_IDX), ci, si, 5, IDX_DIM // 8)
    ck, sk = tables(6, 7, 1)
    sm_ref[...] = rotary(proj(_C_SM, LANES), ck, sk, 5, IDX_DIM // 8)


def _in_proj(x2, pos_f, n1w, w_packed, mu_packed, qnw, knw, gmat, rope, tm):
    t = x2.shape[0]
    full = lambda shape: pl.BlockSpec(shape, lambda i: (0,) * len(shape))
    row = lambda width: pl.BlockSpec((tm, width), lambda i: (i, 0))
    out_shapes = (
        jax.ShapeDtypeStruct((t, D_GROUP), BF16),
        jax.ShapeDtypeStruct((t, D_GROUP), BF16),
        jax.ShapeDtypeStruct((t, D_GROUP), BF16),
        jax.ShapeDtypeStruct((t, D_GROUP), F32),
        jax.ShapeDtypeStruct((t, D_GROUP), F32),
        jax.ShapeDtypeStruct((t, D_GROUP), F32),
        jax.ShapeDtypeStruct((t, D_IDX), F32),
        jax.ShapeDtypeStruct((t, LANES), F32),
        jax.ShapeDtypeStruct((t, LANES), F32),
    )
    return pl.pallas_call(
        _in_proj_kernel,
        grid=(t // tm,),
        in_specs=[row(D_MODEL), row(1), full((1, D_MODEL)), full((D_MODEL, D_IN_PACKED)),
                  full((1, D_SHIFT)), full((1, D_GROUP)), full((1, D_GROUP)),
                  full((D_GROUP, D_GROUP)), full((8, LANES))],
        out_specs=[row(D_GROUP)] * 6 + [row(D_IDX), row(LANES), row(LANES)],
        out_shape=out_shapes,
        scratch_shapes=[pltpu.VMEM((8, D_SHIFT), F32)],
        compiler_params=pltpu.CompilerParams(dimension_semantics=("arbitrary",),
                                             vmem_limit_bytes=VMEM_LIMIT),
        name="in_proj",
    )(x2, pos_f, n1w, w_packed, mu_packed, qnw, knw, gmat, rope)


RWKV_CHUNK = 64
RWKV_UNROLL = 2
PAIR = 2 * HEAD_DIM


def _rwkv_kernel(r_ref, k_ref, v_ref, lora_ref, w0_ref, w2_ref, a0_ref, a2_ref, g2_ref,
                 kk_ref, ka_ref, rk_ref, lnw_ref, lnb_ref, g_ref,
                 o_ref,
                 s_ref, ld_s, r_s, k2_s, b_s, kk_s, y_s):
    tm = r_ref.shape[0]
    n_chunks = tm // RWKV_CHUNK
    n_pairs = D_GROUP // PAIR
    c = RWKV_CHUNK

    @pl.when(pl.program_id(0) == 0)
    def _():
        s_ref[...] = jnp.zeros_like(s_ref)

    lora = lora_ref[...]
    r = r_ref[...]
    k = k_ref[...]
    v = v_ref[...]
    zarg = w0_ref[...] + _dotf(jnp.tanh(lora), w2_ref[...])
    sp = jnp.maximum(-zarg, 0.0) + jnp.log1p(jnp.exp(-jnp.abs(zarg)))
    ld_s[...] = -jnp.exp(-sp - 0.5)
    a = jax.nn.sigmoid(a0_ref[...] + _dotf(lora, a2_ref[...]))
    g = _dotf(jax.nn.sigmoid(lora), g2_ref[...])
    kk = k * kk_ref[...]
    kk = kk * lax.rsqrt(jnp.maximum(_group_sum(kk * kk, g_ref), 1e-24))
    k2 = k * (1.0 + (a - 1.0) * ka_ref[...])
    bonus = _group_sum(r * k2 * rk_ref[...], g_ref) * v
    r_s[...] = r
    k2_s[...] = k2
    kk_s[...] = kk
    b_s[...] = kk * a

    row = lax.broadcasted_iota(jnp.int32, (2 * c, 2 * c), 0)
    col = lax.broadcasted_iota(jnp.int32, (2 * c, 2 * c), 1)
    same_head = (row >= c) == (col >= c)
    strict = same_head & (col < row)
    incl = same_head & (col <= row)
    eye = (row == col).astype(F32)
    tri = (lax.broadcasted_iota(jnp.int32, (c, c), 1)
           <= lax.broadcasted_iota(jnp.int32, (c, c), 0)).astype(F32)
    lane = lax.broadcasted_iota(jnp.int32, (c, PAIR), 1)
    head0 = lane < HEAD_DIM

    def stack(z):
        return jnp.concatenate([jnp.where(head0, z, 0.0), jnp.where(head0, 0.0, z)], axis=0)

    def chunk_body(ci, carry):
        c2 = 2 * c
        units = [(cc, p) for cc in range(RWKV_UNROLL) for p in range(n_pairs)]
        uid = range(len(units))
        rows = [pl.multiple_of((ci * RWKV_UNROLL + cc) * c, c) for cc in range(RWKV_UNROLL)]
        ld_all = [ld_s[pl.ds(r0, c), :] for r0 in rows]
        cum_all = [_dotf(tri, ld) for ld in ld_all]
        rt, kt, bt, kp, vs, lhs, rhs, gam_end = [], [], [], [], [], [], [], []
        for cc, p in units:
            r0 = rows[cc]
            cols = slice(p * PAIR, (p + 1) * PAIR)
            ld = ld_all[cc][:, cols]
            cum = cum_all[cc][:, cols]
            gam = jnp.exp(cum)
            inv = jnp.exp(-cum)
            gam_prev = jnp.exp(cum - ld)
            gam_end.append(gam[c - 1:c, :])
            rt.append(stack(r_s[pl.ds(r0, c), cols] * gam))
            kt_raw = k2_s[pl.ds(r0, c), cols] * inv
            bt_raw = b_s[pl.ds(r0, c), cols] * inv
            kt.append(stack(kt_raw))
            bt.append(stack(bt_raw))
            kp.append(stack(kk_s[pl.ds(r0, c), cols] * gam_prev))
            vs.append(stack(v_ref[pl.ds(r0, c), cols]))
            lhs.append(jnp.concatenate([kp[-1], rt[-1]], axis=0))
            rhs.append(jnp.concatenate([bt_raw, bt_raw, kt_raw, kt_raw], axis=0))
        aa = [_dot_nt(lhs[u], rhs[u]) for u in uid]
        a_kk = [jnp.where(strict, aa[u][:c2, c2:], 0.0) for u in uid]
        a_rb = [jnp.where(incl, aa[u][c2:, :c2], 0.0) for u in uid]
        a_rk = [jnp.where(incl, aa[u][c2:, c2:], 0.0) for u in uid]
        n = [-jnp.where(strict, aa[u][:c2, :c2], 0.0) for u in uid]
        prod = [eye + n[u] for u in uid]
        n = [_dot(n[u], n[u]) for u in uid]
        for _ in range(int(np.log2(c)) - 2):
            both = [_dot(jnp.concatenate([n[u], prod[u]], axis=0), n[u]) for u in uid]
            prod = [prod[u] + both[u][c2:] for u in uid]
            n = [both[u][:c2] for u in uid]
        av = [_dot(jnp.concatenate([a_kk[u], a_rk[u]], axis=0), vs[u]) for u in uid]
        tinv = [prod[u] + _dot(prod[u], n[u]) for u in uid]
        wu = [_dot(tinv[u], jnp.concatenate([kp[u], av[u][:c2]], axis=1)) for u in uid]
        rb = [_dot(a_rb[u], wu[u]) for u in uid]
        tn = [_dot_tn(jnp.concatenate([wu[u], vs[u]], axis=1),
                      jnp.concatenate([bt[u], kt[u]], axis=1)) for u in uid]
        r2 = [rt[u] - rb[u][:, :PAIR] for u in uid]
        y2 = [av[u][c2:] - rb[u][:, PAIR:] for u in uid]
        s_mix = [(eye - tn[u][:PAIR, :PAIR]) * gam_end[u] for u in uid]
        s_add = [(tn[u][2 * PAIR:, PAIR:] - tn[u][PAIR:2 * PAIR, :PAIR]) * gam_end[u] for u in uid]
        state = [s_ref[p] for p in range(n_pairs)]
        for cc in range(RWKV_UNROLL):
            us = [cc * n_pairs + p for p in range(n_pairs)]
            ys = [_dot_nt(r2[u], state[p]) + y2[u] for p, u in enumerate(us)]
            state = [_dot(state[p], s_mix[u]) + s_add[u] for p, u in enumerate(us)]
            for p in range(n_pairs):
                y_s[pl.ds(rows[cc], c), p * PAIR:(p + 1) * PAIR] = ys[p][:c] + ys[p][c:]
        for p in range(n_pairs):
            s_ref[p] = state[p]
        return carry

    lax.fori_loop(0, n_chunks // RWKV_UNROLL, chunk_body, 0)

    y = y_s[...]
    mean = _group_sum(y, g_ref) * (1.0 / HEAD_DIM)
    yc = y - mean
    var = _group_sum(yc * yc, g_ref) * (1.0 / HEAD_DIM)
    yn = yc * lax.rsqrt(var + RWKV_GN_EPS) * lnw_ref[...] + lnb_ref[...]
    o_ref[...] = (yn + bonus) * g


def _rwkv(rr, rk, rv, lora, w0, w2p, a0, a2p, g2p, k_k, k_a, r_k, ln_w, ln_b, gsum, tm):
    t = rr.shape[0]
    full = lambda shape: pl.BlockSpec(shape, lambda i: (0,) * len(shape))
    row = lambda width: pl.BlockSpec((tm, width), lambda i: (i, 0))
    vec = full((1, D_GROUP))
    big = pltpu.VMEM((tm, D_GROUP), F32)
    return pl.pallas_call(
        _rwkv_kernel,
        grid=(t // tm,),
        in_specs=[row(D_GROUP), row(D_GROUP), row(D_GROUP), row(LANES),
                  vec, full((LANES, D_GROUP)), vec, full((LANES, D_GROUP)),
                  full((LANES, D_GROUP)), vec, vec, vec, vec, vec, full((D_GROUP, D_GROUP))],
        out_specs=row(D_GROUP),
        out_shape=jax.ShapeDtypeStruct((t, D_GROUP), F32),
        scratch_shapes=[pltpu.VMEM((D_GROUP // PAIR, PAIR, PAIR), F32),
                        big, big, big, big, big, big],
        compiler_params=pltpu.CompilerParams(dimension_semantics=("arbitrary",),
                                             vmem_limit_bytes=VMEM_LIMIT),
        name="rwkv",
    )(rr, rk, rv, lora, w0, w2p, a0, a2p, g2p, k_k, k_a, r_k, ln_w, ln_b, gsum)


DSA_QB = 256
DSA_KB = 512
M_INIT = -5e29
DSA_HEAD_GROUP = 4
DSA_ACC_ROWS = 32
DSA_GROUPS = 256
DSA_TOP = 10
DSA_V_ROWS = HEAD_DIM + 16


def _dsa_kernel(qi_ref, kj_ref, qt_ref, k_ref, vt_ref, iqt_ref, ik_ref, iwt_ref, o_ref,
                key_s, thr_s, m_s, acc_s, bdq_s, top_s, *, n_sel):
    qb = qt_ref.shape[2]
    kb = k_ref.shape[0]
    step = pl.program_id(0)
    qi = qi_ref[step]
    kj = kj_ref[step]
    q0 = qi * qb
    j_last = (q0 + qb - 1) // kb
    n_kc = j_last + 1
    idx_scale = float((IDX_DIM * N_IDX_HEADS) ** -0.5)

    def causal(jblk):
        s_pos = jblk * kb + lax.broadcasted_iota(jnp.int32, (kb, qb), 0)
        t_pos = q0 + lax.broadcasted_iota(jnp.int32, (kb, qb), 1)
        return s_pos <= t_pos

    @pl.when(kj == 0)
    def _():
        iqt = iqt_ref[...].astype(BF16)
        iwt = iwt_ref[...] * idx_scale

        def score_chunk(kc, diagonal):
            ik = ik_ref[pl.ds(pl.multiple_of(kc * kb, kb), kb), :]
            score = jnp.zeros((kb, qb), F32)
            for h in range(N_IDX_HEADS):
                d = jnp.dot(ik, iqt[h * IDX_DIM:(h + 1) * IDX_DIM, :],
                            preferred_element_type=F32)
                score = score + jnp.maximum(d, 0.0) * iwt[h:h + 1, :]
            bits = pltpu.bitcast(score, jnp.int32)
            keys = jnp.where(bits < 0, bits ^ jnp.int32(0x7FFFFFFF), bits)
            if diagonal:
                keys = jnp.where(causal(kc), keys, jnp.int32(INT_MIN))
            key_s[kc] = keys
            xs = [keys[r * DSA_GROUPS:(r + 1) * DSA_GROUPS] for r in range(kb // DSA_GROUPS)]
            for lvl in range(DSA_TOP):
                s = top_s[lvl]
                for r in range(len(xs)):
                    s, xs[r] = jnp.maximum(s, xs[r]), jnp.minimum(s, xs[r])
                top_s[lvl] = s

        top_s[...] = jnp.full_like(top_s, INT_MIN)

        def full_chunk(kc, carry):
            score_chunk(kc, False)
            return carry

        lax.fori_loop(0, j_last, full_chunk, 0)
        score_chunk(j_last, True)

        def select(count_keys):
            def bit_step(b, cur):
                bit = lax.shift_left(jnp.int32(1), jnp.int32(31) - b)
                cand = (cur | bit) ^ jnp.int32(INT_MIN)
                cnt = jnp.sum(count_keys(cand), axis=0, keepdims=True)
                return jnp.where(cnt >= float(n_sel), cur | bit, cur)

            cur = lax.fori_loop(0, 32, bit_step, jnp.zeros((1, qb), jnp.int32))
            return jnp.maximum(cur ^ jnp.int32(INT_MIN), jnp.int32(INT_MIN + 1))

        def part_count(keys, cand):
            ind = jnp.where(keys >= cand, 1.0, 0.0)
            return jnp.sum(ind.reshape(-1, DSA_ACC_ROWS, qb), axis=0)

        def count_top(cand):
            return lax.fori_loop(0, DSA_TOP, lambda lvl, a: a + part_count(top_s[lvl], cand),
                                 jnp.zeros((DSA_ACC_ROWS, qb), F32))

        def count_all(cand):
            return lax.fori_loop(0, n_kc, lambda kc, a: a + part_count(key_s[kc], cand),
                                 jnp.zeros((DSA_ACC_ROWS, qb), F32))

        thr = select(count_top)
        thr_s[...] = thr
        hidden = jnp.where(top_s[DSA_TOP - 1] >= thr, 1.0, 0.0)

        @pl.when(jnp.max(hidden) > 0.0)
        def _():
            thr_s[...] = select(count_all)

        m_s[...] = jnp.full_like(m_s, M_INIT)
        acc_s[...] = jnp.zeros_like(acc_s)
        bdq_s[...] = jnp.zeros_like(bdq_s)
        for h in range(N_HEADS):
            g, hh = divmod(h, DSA_HEAD_GROUP)
            bdq_s[g, hh * HEAD_DIM:(hh + 1) * HEAD_DIM, hh * qb:(hh + 1) * qb] = (
                qt_ref[h] * jnp.asarray(HEAD_DIM ** -0.5, BF16))

    mask = key_s[kj] >= thr_s[...]
    hg = DSA_HEAD_GROUP
    st_all = [jnp.dot(k_ref[:, g * hg * HEAD_DIM:(g + 1) * hg * HEAD_DIM], bdq_s[g],
                      preferred_element_type=F32) for g in range(N_HEADS // hg)]
    for g in range(N_HEADS // hg):
        st_g = st_all[g]
        for hh in range(hg):
            h = g * hg + hh
            st = jnp.where(mask, st_g[:, hh * qb:(hh + 1) * qb], NEG_BIG)
            m_old = m_s[h]
            part = jnp.max(st.reshape(kb // DSA_ACC_ROWS, DSA_ACC_ROWS, qb), axis=0)
            m_new = jnp.maximum(m_old, jnp.max(part, axis=0, keepdims=True))
            p = jnp.exp((st - m_new).astype(BF16))
            acc_s[h] = jnp.exp(m_old - m_new) * acc_s[h] + jnp.dot(
                vt_ref[h], p, preferred_element_type=F32)
            m_s[h] = m_new

    @pl.when(kj == j_last)
    def _():
        for h in range(N_HEADS):
            acc = acc_s[h]
            o_ref[h] = acc[:HEAD_DIM] / acc[HEAD_DIM:HEAD_DIM + 1]


def _dsa(qt, k, vt, iqt, ik, iwt, n_sel):
    t = ik.shape[0]
    qb, kb = DSA_QB, DSA_KB
    assert n_sel <= DSA_GROUPS and kb % DSA_GROUPS == 0
    nq, nk = t // qb, t // kb
    pairs = [(i, j) for i in range(nq) for j in range((i * qb + qb - 1) // kb + 1)]
    qi = jnp.asarray(np.array([p[0] for p in pairs], np.int32))
    kj = jnp.asarray(np.array([p[1] for p in pairs], np.int32))
    grid_spec = pltpu.PrefetchScalarGridSpec(
        num_scalar_prefetch=2,
        grid=(len(pairs),),
        in_specs=[pl.BlockSpec((N_HEADS, HEAD_DIM, qb), lambda s, qi, kj: (0, 0, qi[s])),
                  pl.BlockSpec((kb, D_GROUP), lambda s, qi, kj: (kj[s], 0)),
                  pl.BlockSpec((N_HEADS, DSA_V_ROWS, kb), lambda s, qi, kj: (0, 0, kj[s])),
                  pl.BlockSpec((D_IDX, qb), lambda s, qi, kj: (0, qi[s])),
                  pl.BlockSpec((t, IDX_DIM), lambda s, qi, kj: (0, 0)),
                  pl.BlockSpec((N_IDX_HEADS, qb), lambda s, qi, kj: (0, qi[s]))],
        out_specs=pl.BlockSpec((N_HEADS, HEAD_DIM, qb), lambda s, qi, kj: (0, 0, qi[s])),
        scratch_shapes=[pltpu.VMEM((nk, kb, qb), jnp.int32),
                        pltpu.VMEM((1, qb), jnp.int32),
                        pltpu.VMEM((N_HEADS, 1, qb), F32),
                        pltpu.VMEM((N_HEADS, DSA_V_ROWS, qb), F32),
                        pltpu.VMEM((N_HEADS // DSA_HEAD_GROUP, DSA_HEAD_GROUP * HEAD_DIM,
                                    DSA_HEAD_GROUP * qb), BF16),
                        pltpu.VMEM((DSA_TOP, DSA_GROUPS, qb), jnp.int32)])
    return pl.pallas_call(
        functools.partial(_dsa_kernel, n_sel=n_sel),
        grid_spec=grid_spec,
        out_shape=jax.ShapeDtypeStruct((N_HEADS, HEAD_DIM, t), F32),
        compiler_params=pltpu.CompilerParams(dimension_semantics=("arbitrary",),
                                             vmem_limit_bytes=VMEM_LIMIT),
        name="dsa",
    )(qi, kj, qt, k, vt, iqt, ik, iwt)


def _out_proj_kernel(x_ref, att_ref, rw_ref, wo_ref, n2w_ref, rwt_ref, rb_ref,
                     h_ref, xn_ref, gate_ref):
    tm = x_ref.shape[0]
    att = att_ref[...].reshape(D_GROUP, tm).T
    mix = jnp.concatenate([att, rw_ref[...]], axis=1)
    acc = x_ref[...] + _dot(mix, wo_ref[...])
    h_ref[...] = acc
    xn = acc * lax.rsqrt(jnp.mean(acc * acc, axis=-1, keepdims=True) + RMS_EPS) * n2w_ref[...]
    xn_ref[...] = xn.astype(BF16)
    logits = _dotf(xn, rwt_ref[...]) + rb_ref[...]
    lane = lax.broadcasted_iota(jnp.int32, logits.shape, 1)
    work = logits
    vals, hots = [], []
    for _ in range(TOP_K_EXPERTS):
        m = jnp.max(work, axis=1, keepdims=True)
        idx = jnp.min(jnp.where(work == m, lane, LANES), axis=1, keepdims=True)
        hot = lane == idx
        vals.append(m)
        hots.append(hot)
        work = jnp.where(hot, -jnp.inf, work)
    es = [jnp.exp(vv - vals[0]) for vv in vals]
    denom = es[0] + es[1] + es[2] + es[3]
    gates = jnp.zeros_like(logits)
    for e, hot in zip(es, hots):
        gates = gates + jnp.where(hot, e / denom, 0.0)
    gate_ref[...] = gates


def _out_proj(x2, att_t, rw, wo, n2w, rwt, rb, tm):
    t = x2.shape[0]
    full = lambda shape: pl.BlockSpec(shape, lambda i: (0,) * len(shape))
    row = lambda width: pl.BlockSpec((tm, width), lambda i: (i, 0))
    return pl.pallas_call(
        _out_proj_kernel,
        grid=(t // tm,),
        in_specs=[row(D_MODEL), pl.BlockSpec((N_HEADS, HEAD_DIM, tm), lambda i: (0, 0, i)),
                  row(D_GROUP), full((2 * D_GROUP, D_MODEL)), full((1, D_MODEL)),
                  full((D_MODEL, LANES)), full((1, LANES))],
        out_specs=[row(D_MODEL), row(D_MODEL), row(LANES)],
        out_shape=(jax.ShapeDtypeStruct((t, D_MODEL), F32),
                   jax.ShapeDtypeStruct((t, D_MODEL), BF16),
                   jax.ShapeDtypeStruct((t, LANES), F32)),
        compiler_params=pltpu.CompilerParams(dimension_semantics=("arbitrary",),
                                             vmem_limit_bytes=VMEM_LIMIT),
        name="out_proj",
    )(x2, att_t, rw, wo, n2w, rwt, rb)


MOE_TM = 1024
MOE_RB = 144
MOE_FC = 512


def _route_kernel(gate_ref, rankt_ref, gatet_ref, cnt_ref):
    tm = gate_ref.shape[0]
    gates = gate_ref[...]
    hot = gates > 0.0
    ind = jnp.where(hot, 1.0, 0.0)
    before = (lax.broadcasted_iota(jnp.int32, (tm, tm), 1)
              < lax.broadcasted_iota(jnp.int32, (tm, tm), 0))
    rank = jnp.dot(jnp.where(before, 1.0, 0.0).astype(BF16), ind.astype(BF16),
                   preferred_element_type=F32)
    rankt_ref[...] = jnp.where(hot, rank, -1.0).T
    gatet_ref[...] = gates.T
    cnt = jnp.sum(ind, axis=0, keepdims=True)
    cnt_ref[...] = jnp.broadcast_to(cnt[None], cnt_ref.shape)


def _route(gates, tm):
    t = gates.shape[0]
    return pl.pallas_call(
        _route_kernel,
        grid=(t // tm,),
        in_specs=[pl.BlockSpec((tm, LANES), lambda i: (i, 0))],
        out_specs=[pl.BlockSpec((LANES, tm), lambda i: (0, i)),
                   pl.BlockSpec((LANES, tm), lambda i: (0, i)),
                   pl.BlockSpec((1, 8, LANES), lambda i: (i, 0, 0))],
        out_shape=(jax.ShapeDtypeStruct((LANES, t), F32),
                   jax.ShapeDtypeStruct((LANES, t), F32),
                   jax.ShapeDtypeStruct((t // tm, 8, LANES), F32)),
        compiler_params=pltpu.CompilerParams(dimension_semantics=("arbitrary",),
                                             vmem_limit_bytes=VMEM_LIMIT),
        name="route",
    )(gates)


def _moe_kernel(cnt_ref, h_ref, xn_ref, rankt_ref, gatet_ref, wgu_ref, bgu_ref,
                wdn_ref, bdn_ref, o_ref):
    i = pl.program_id(0)
    e = pl.program_id(1)
    tm = h_ref.shape[0]

    @pl.when(e == 0)
    def _():
        o_ref[...] = h_ref[...]

    n_rows = cnt_ref[i * N_EXPERTS + e]
    n_blocks = (n_rows + MOE_RB - 1) // MOE_RB
    rank_row = rankt_ref[pl.ds(e, 1), :]
    gate_row = gatet_ref[pl.ds(e, 1), :]

    def row_block(bi, carry):
        r0 = (bi * MOE_RB).astype(F32)
        rowid = lax.broadcasted_iota(jnp.int32, (MOE_RB, tm), 0).astype(F32) + r0
        hit = rank_row == rowid
        sel = jnp.where(hit, 1.0, 0.0).astype(BF16)
        gate = jnp.sum(jnp.where(hit, gate_row, 0.0), axis=1, keepdims=True)
        xb = jnp.dot(sel, xn_ref[...], preferred_element_type=F32).astype(BF16)
        y = jnp.zeros((MOE_RB, D_MODEL), F32)
        ups = []
        for fc in range(D_FF // MOE_FC):
            c0 = fc * MOE_FC
            ups.append((jnp.dot(xb, wgu_ref[0, :, c0:c0 + MOE_FC], preferred_element_type=F32),
                        jnp.dot(xb, wgu_ref[0, :, D_FF + c0:D_FF + c0 + MOE_FC],
                                preferred_element_type=F32)))
        for fc in range(D_FF // MOE_FC):
            c0 = fc * MOE_FC
            hg = ups[fc][0] + bgu_ref[0, :, c0:c0 + MOE_FC]
            hl = ups[fc][1] + bgu_ref[0, :, D_FF + c0:D_FF + c0 + MOE_FC]
            glu = jnp.minimum(hg, SWIGLU_LIMIT)
            lin = jnp.clip(hl, -SWIGLU_LIMIT, SWIGLU_LIMIT)
            act = (lin + 1.0) * glu * jax.nn.sigmoid(SWIGLU_ALPHA * glu)
            y = y + jnp.dot(act.astype(BF16), wdn_ref[0, c0:c0 + MOE_FC, :],
                            preferred_element_type=F32)
        y = ((y + bdn_ref[0]) * gate).astype(BF16)
        o_ref[...] += _dot_tn(sel, y)
        return carry

    lax.fori_loop(0, n_blocks, row_block, 0)


def _moe(counts, h1, xn2, rank_t, gate_t, wgu, bgu, wdn, bdn, tm):
    t = h1.shape[0]
    row = lambda width: pl.BlockSpec((tm, width), lambda i, e, c: (i, 0))
    grid_spec = pltpu.PrefetchScalarGridSpec(
        num_scalar_prefetch=1,
        grid=(t // tm, N_EXPERTS),
        in_specs=[row(D_MODEL), row(D_MODEL),
                  pl.BlockSpec((LANES, tm), lambda i, e, c: (0, i)),
                  pl.BlockSpec((LANES, tm), lambda i, e, c: (0, i)),
                  pl.BlockSpec((1, D_MODEL, 2 * D_FF), lambda i, e, c: (e, 0, 0)),
                  pl.BlockSpec((1, 1, 2 * D_FF), lambda i, e, c: (e, 0, 0)),
                  pl.BlockSpec((1, D_FF, D_MODEL), lambda i, e, c: (e, 0, 0)),
                  pl.BlockSpec((1, 1, D_MODEL), lambda i, e, c: (e, 0, 0))],
        out_specs=row(D_MODEL))
    return pl.pallas_call(
        _moe_kernel,
        grid_spec=grid_spec,
        out_shape=jax.ShapeDtypeStruct((t, D_MODEL), F32),
        compiler_params=pltpu.CompilerParams(dimension_semantics=("arbitrary", "arbitrary"),
                                             vmem_limit_bytes=VMEM_LIMIT),
        name="moe",
    )(counts, h1, xn2, rank_t, gate_t, wgu, bgu, wdn, bdn)


def _rope_tables():
    lane = jnp.arange(LANES)

    def rows(period, rot):
        half = rot // 2
        inv_freq = ROPE_THETA ** (-jnp.arange(half, dtype=F32) / half)
        jm = lane % period
        freq = jnp.where(jm < rot, inv_freq[jm % half], 0.0)
        sign = jnp.where(jm < half, -1.0, jnp.where(jm < rot, 1.0, 0.0))
        first = (jm < half).astype(F32)
        return [freq, sign, first]

    idx_rows = rows(IDX_DIM, IDX_DIM // 4)
    only_key = (lane < IDX_DIM).astype(F32)
    return jnp.stack(rows(HEAD_DIM, HEAD_DIM // 4) + idx_rows
                     + [idx_rows[0] * only_key, idx_rows[1] * only_key]).astype(F32)


def _group_matrix():
    g = np.arange(D_GROUP) // HEAD_DIM
    return jnp.asarray((g[:, None] == g[None, :]).astype(np.float32), dtype=BF16)


def _pad_rows(w, r0, rows):
    return jnp.zeros((rows, w.shape[1]), w.dtype).at[r0:r0 + w.shape[0]].set(w)


def kernel(x, positions, norm1_w, w_in, q_norm_w, k_norm_w, rwkv_mu, rwkv_w0, rwkv_w2, rwkv_a0,
           rwkv_a2, rwkv_g2, rwkv_k_k, rwkv_k_a, rwkv_r_k, rwkv_ln_w, rwkv_ln_b, w_out, norm2_w,
           router_w, router_b, exp_w_gu, exp_b_gu, exp_w_down, exp_b_down):
    b, t, _ = x.shape
    assert b == 1 and w_in.shape[0] == 1, "single sequence, single layer"
    assert t % DSA_KB == 0 and t % DSA_QB == 0 and t % MOE_TM == 0
    x2 = x[0]
    pos_f = positions[0].astype(F32)[:, None]
    n_sel = min(TOPK_MAX, t // 4)

    w = w_in[0]
    a0 = 3 * D_GROUP
    att_cols = a0 + D_IDX + IDX_DIM + N_IDX_HEADS
    w_att, w_rw = w[:, :att_cols], w[:, att_cols:]
    w_sm = jnp.zeros((D_MODEL, LANES), F32).at[:, :IDX_DIM + N_IDX_HEADS].set(w_att[:, a0 + D_IDX:])
    w_packed = jnp.concatenate(
        [w_att[:, :a0], w_rw[:, :a0], w_att[:, a0:a0 + D_IDX], w_sm, w_rw[:, a0:]],
        axis=1).astype(BF16)
    mu = rwkv_mu[0][None, :]
    tile8 = lambda z: jnp.tile(z, N_HEADS)[None, :]
    gsum = _group_matrix()

    tm = 256
    q, k, v, rr, rk, rv, iq, sm, lora = _in_proj(
        x2, pos_f, norm1_w, w_packed, mu, tile8(q_norm_w[0]), tile8(k_norm_w[0]), gsum,
        _rope_tables(), tm)

    vec = lambda z: z.reshape(1, D_GROUP)
    rw = _rwkv(rr, rk, rv, lora, vec(rwkv_w0[0]),
               _pad_rows(rwkv_w2[0], 0, LANES), vec(rwkv_a0[0]),
               _pad_rows(rwkv_a2[0], D_DECAY_LORA, LANES),
               _pad_rows(rwkv_g2[0], D_DECAY_LORA + D_AAA_LORA, LANES),
               vec(rwkv_k_k[0]), vec(rwkv_k_a[0]), vec(rwkv_r_k[0]), vec(rwkv_ln_w[0]),
               vec(rwkv_ln_b[0]), gsum, tm)

    heads = lambda z: z.reshape(t, N_HEADS, HEAD_DIM)
    vt = jnp.concatenate([heads(v).transpose(1, 2, 0),
                          jnp.ones((N_HEADS, DSA_V_ROWS - HEAD_DIM, t), BF16)], axis=1)
    att_t = _dsa(heads(q).transpose(1, 2, 0), k, vt,
                 iq.T, sm[:, :IDX_DIM].astype(BF16),
                 sm[:, IDX_DIM:IDX_DIM + N_IDX_HEADS].T, n_sel)

    rwt = jnp.zeros((D_MODEL, LANES), F32).at[:, :N_EXPERTS].set(router_w[0])
    rb = jnp.full((1, LANES), NEG_BIG, F32).at[0, :N_EXPERTS].set(router_b[0])
    h1, xn2, gates = _out_proj(x2, att_t, rw, w_out[0].astype(BF16), norm2_w, rwt, rb, tm)

    rank_t, gate_t, cnt = _route(gates, MOE_TM)
    counts = cnt[:, 0, :N_EXPERTS].astype(jnp.int32).reshape(-1)
    out = _moe(counts, h1, xn2, rank_t, gate_t, exp_w_gu[0].astype(BF16),
               exp_b_gu[0][:, None, :], exp_w_down[0].astype(BF16), exp_b_down[0][:, None, :],
               MOE_TM)
    return out[None]
```

```python
import functools

import jax
import jax.numpy as jnp
import numpy as np
from jax import lax
from jax.experimental import pallas as pl
from jax.experimental.pallas import tpu as pltpu

F32 = jnp.float32
BF16 = jnp.bfloat16
HIGHEST = lax.Precision.HIGHEST

D_MODEL = 1024
HEAD_DIM = 64
N_HEADS = 8
D_GROUP = N_HEADS * HEAD_DIM
ROPE_THETA = 500000.0
N_IDX_HEADS = 8
IDX_DIM = 32
D_IDX = N_IDX_HEADS * IDX_DIM
TOPK_MAX = 256
D_DECAY_LORA = 32
D_AAA_LORA = 32
D_GATE_LORA = 64
RWKV_GN_EPS = 64e-5
N_EXPERTS = 32
TOP_K_EXPERTS = 4
D_FF = 1024
SWIGLU_LIMIT = 7.0
SWIGLU_ALPHA = 1.702
RMS_EPS = 1e-6

LANES = 128
VMEM_LIMIT = 56 * 1024 * 1024

NEG_BIG = -1e30
INT_MIN = -(2 ** 31)


def _dot(a, b):
    return jnp.dot(a.astype(BF16), b.astype(BF16), preferred_element_type=F32)


def _dotf(a, b):
    return jnp.dot(a, b, preferred_element_type=F32, precision=HIGHEST)


def _dot_nt(a, b):
    return lax.dot_general(a.astype(BF16), b.astype(BF16), (((1,), (1,)), ((), ())),
                           preferred_element_type=F32)


def _dot_tn(a, b):
    return lax.dot_general(a.astype(BF16), b.astype(BF16), (((0,), (0,)), ((), ())),
                           preferred_element_type=F32)


def _group_sum(z, g_ref):
    hi = z.astype(BF16)
    lo = (z - hi.astype(F32)).astype(BF16)
    g = g_ref[...]
    return (jnp.dot(hi, g, preferred_element_type=F32)
            + jnp.dot(lo, g, preferred_element_type=F32))


_C_Q, _C_K, _C_V, _C_RR, _C_RK, _C_RV = (i * D_GROUP for i in range(6))
_C_IQ = 6 * D_GROUP
_C_SM = _C_IQ + D_IDX
_C_LORA = _C_SM + LANES
D_IN_PACKED = _C_LORA + LANES
D_SHIFT = 3 * D_GROUP + LANES


def _in_proj_kernel(x_ref, pos_ref, n1w_ref, w_ref, mu_ref, qnw_ref, knw_ref, g_ref, rope_ref,
                    q_ref, k_ref, v_ref, rr_ref, rk_ref, rv_ref, iq_ref, sm_ref, lora_ref,
                    carry_ref):
    tm = x_ref.shape[0]

    @pl.when(pl.program_id(0) == 0)
    def _():
        carry_ref[...] = jnp.zeros_like(carry_ref)

    x = x_ref[...]
    xn = x * lax.rsqrt(jnp.mean(x * x, axis=-1, keepdims=True) + RMS_EPS) * n1w_ref[...]
    xb = xn.astype(BF16)
    pos = pos_ref[...]
    rope = rope_ref[...]

    def tables(frow, srow, reps):
        ang = pos * rope[frow:frow + 1, :]
        c = jnp.cos(ang)
        s = jnp.sin(ang) * rope[srow:srow + 1, :]
        if reps > 1:
            c = jnp.concatenate([c] * reps, axis=1)
            s = jnp.concatenate([s] * reps, axis=1)
        return c, s

    def rotary(z, c, s, first_row, half):
        w = z.shape[1]
        first = jnp.concatenate([rope[first_row:first_row + 1, :]] * (w // LANES), axis=1) > 0.5
        partner = jnp.where(first, pltpu.roll(z, w - half, 1), pltpu.roll(z, half, 1))
        return z * c + partner * s

    def head_norm(z, w_row):
        ms = _group_sum(z * z, g_ref) * (1.0 / HEAD_DIM)
        return z * lax.rsqrt(ms + RMS_EPS) * w_row

    def proj(c0, width):
        return jnp.dot(xb, w_ref[:, c0:c0 + width], preferred_element_type=F32)

    def shift(z, c0):
        width = z.shape[1]
        row = lax.broadcasted_iota(jnp.int32, z.shape, 0)
        prev = jnp.where(row == 0, carry_ref[0:1, c0:c0 + width], pltpu.roll(z, 1, 0))
        carry_ref[0:1, c0:c0 + width] = z[tm - 1:tm, :]
        return z + (prev - z) * mu_ref[:, c0:c0 + width]

    cq, sq = tables(0, 1, D_GROUP // LANES)
    q = rotary(head_norm(proj(_C_Q, D_GROUP), qnw_ref[...]), cq, sq, 2, HEAD_DIM // 8)
    q_ref[...] = q.astype(BF16)
    k = rotary(head_norm(proj(_C_K, D_GROUP), knw_ref[...]), cq, sq, 2, HEAD_DIM // 8)
    k_ref[...] = k.astype(BF16)
    v_ref[...] = proj(_C_V, D_GROUP).astype(BF16)

    rr_ref[...] = shift(proj(_C_RR, D_GROUP), 0)
    rk_ref[...] = shift(proj(_C_RK, D_GROUP), D_GROUP)
    rv_ref[...] = shift(proj(_C_RV, D_GROUP), 2 * D_GROUP)
    lora_ref[...] = shift(proj(_C_LORA, LANES), 3 * D_GROUP)

    ci, si = tables(3, 4, D_IDX // LANES)
    iq_ref[...] = rotary(proj(_C_IQ, D_IDX), ci, si, 5, IDX_DIM // 8)
    ck, sk = tables(6, 7, 1)
    sm_ref[...] = rotary(proj(_C_SM, LANES), ck, sk, 5, IDX_DIM // 8)


def _in_proj(x2, pos_f, n1w, w_packed, mu_packed, qnw, knw, gmat, rope, tm):
    t = x2.shape[0]
    full = lambda shape: pl.BlockSpec(shape, lambda i: (0,) * len(shape))
    row = lambda width: pl.BlockSpec((tm, width), lambda i: (i, 0))
    out_shapes = (
        jax.ShapeDtypeStruct((t, D_GROUP), BF16),
        jax.ShapeDtypeStruct((t, D_GROUP), BF16),
        jax.ShapeDtypeStruct((t, D_GROUP), BF16),
        jax.ShapeDtypeStruct((t, D_GROUP), F32),
        jax.ShapeDtypeStruct((t, D_GROUP), F32),
        jax.ShapeDtypeStruct((t, D_GROUP), F32),
        jax.ShapeDtypeStruct((t, D_IDX), F32),
        jax.ShapeDtypeStruct((t, LANES), F32),
        jax.ShapeDtypeStruct((t, LANES), F32),
    )
    return pl.pallas_call(
        _in_proj_kernel,
        grid=(t // tm,),
        in_specs=[row(D_MODEL), row(1), full((1, D_MODEL)), full((D_MODEL, D_IN_PACKED)),
                  full((1, D_SHIFT)), full((1, D_GROUP)), full((1, D_GROUP)),
                  full((D_GROUP, D_GROUP)), full((8, LANES))],
        out_specs=[row(D_GROUP)] * 6 + [row(D_IDX), row(LANES), row(LANES)],
        out_shape=out_shapes,
        scratch_shapes=[pltpu.VMEM((8, D_SHIFT), F32)],
        compiler_params=pltpu.CompilerParams(dimension_semantics=("arbitrary",),
                                             vmem_limit_bytes=VMEM_LIMIT),
        name="in_proj",
    )(x2, pos_f, n1w, w_packed, mu_packed, qnw, knw, gmat, rope)


RWKV_CHUNK = 64
RWKV_UNROLL = 2
PAIR = 2 * HEAD_DIM


def _rwkv_kernel(r_ref, k_ref, v_ref, lora_ref, w0_ref, w2_ref, a0_ref, a2_ref, g2_ref,
                 kk_ref, ka_ref, rk_ref, lnw_ref, lnb_ref, g_ref,
                 o_ref,
                 s_ref, ld_s, r_s, k2_s, b_s, kk_s, y_s):
    tm = r_ref.shape[0]
    n_chunks = tm // RWKV_CHUNK
    n_pairs = D_GROUP // PAIR
    c = RWKV_CHUNK

    @pl.when(pl.program_id(0) == 0)
    def _():
        s_ref[...] = jnp.zeros_like(s_ref)

    lora = lora_ref[...]
    r = r_ref[...]
    k = k_ref[...]
    v = v_ref[...]
    zarg = w0_ref[...] + _dotf(jnp.tanh(lora), w2_ref[...])
    sp = jnp.maximum(-zarg, 0.0) + jnp.log1p(jnp.exp(-jnp.abs(zarg)))
    ld_s[...] = -jnp.exp(-sp - 0.5)
    a = jax.nn.sigmoid(a0_ref[...] + _dotf(lora, a2_ref[...]))
    g = _dotf(jax.nn.sigmoid(lora), g2_ref[...])
    kk = k * kk_ref[...]
    kk = kk * lax.rsqrt(jnp.maximum(_group_sum(kk * kk, g_ref), 1e-24))
    k2 = k * (1.0 + (a - 1.0) * ka_ref[...])
    bonus = _group_sum(r * k2 * rk_ref[...], g_ref) * v
    r_s[...] = r
    k2_s[...] = k2
    kk_s[...] = kk
    b_s[...] = kk * a

    row = lax.broadcasted_iota(jnp.int32, (2 * c, 2 * c), 0)
    col = lax.broadcasted_iota(jnp.int32, (2 * c, 2 * c), 1)
    same_head = (row >= c) == (col >= c)
    strict = same_head & (col < row)
    incl = same_head & (col <= row)
    eye = (row == col).astype(F32)
    tri = (lax.broadcasted_iota(jnp.int32, (c, c), 1)
           <= lax.broadcasted_iota(jnp.int32, (c, c), 0)).astype(F32)
    lane = lax.broadcasted_iota(jnp.int32, (c, PAIR), 1)
    head0 = lane < HEAD_DIM

    def stack(z):
        return jnp.concatenate([jnp.where(head0, z, 0.0), jnp.where(head0, 0.0, z)], axis=0)

    def chunk_body(ci, carry):
        c2 = 2 * c
        units = [(cc, p) for cc in range(RWKV_UNROLL) for p in range(n_pairs)]
        uid = range(len(units))
        rows = [pl.multiple_of((ci * RWKV_UNROLL + cc) * c, c) for cc in range(RWKV_UNROLL)]
        ld_all = [ld_s[pl.ds(r0, c), :] for r0 in rows]
        cum_all = [_dotf(tri, ld) for ld in ld_all]
        rt, kt, bt, kp, vs, lhs, rhs, gam_end = [], [], [], [], [], [], [], []
        for cc, p in units:
            r0 = rows[cc]
            cols = slice(p * PAIR, (p + 1) * PAIR)
            ld = ld_all[cc][:, cols]
            cum = cum_all[cc][:, cols]
            gam = jnp.exp(cum)
            inv = jnp.exp(-cum)
            gam_prev = jnp.exp(cum - ld)
            gam_end.append(gam[c - 1:c, :])
            rt.append(stack(r_s[pl.ds(r0, c), cols] * gam))
            kt_raw = k2_s[pl.ds(r0, c), cols] * inv
            bt_raw = b_s[pl.ds(r0, c), cols] * inv
            kt.append(stack(kt_raw))
            bt.append(stack(bt_raw))
            kp.append(stack(kk_s[pl.ds(r0, c), cols] * gam_prev))
            vs.append(stack(v_ref[pl.ds(r0, c), cols]))
            lhs.append(jnp.concatenate([kp[-1], rt[-1]], axis=0))
            rhs.append(jnp.concatenate([bt_raw, bt_raw, kt_raw, kt_raw], axis=0))
        aa = [_dot_nt(lhs[u], rhs[u]) for u in uid]
        a_kk = [jnp.where(strict, aa[u][:c2, c2:], 0.0) for u in uid]
        a_rb = [jnp.where(incl, aa[u][c2:, :c2], 0.0) for u in uid]
        a_rk = [jnp.where(incl, aa[u][c2:, c2:], 0.0) for u in uid]
        n = [-jnp.where(strict, aa[u][:c2, :c2], 0.0) for u in uid]
        prod = [eye + n[u] for u in uid]
        n = [_dot(n[u], n[u]) for u in uid]
        for _ in range(int(np.log2(c)) - 2):
            both = [_dot(jnp.concatenate([n[u], prod[u]], axis=0), n[u]) for u in uid]
            prod = [prod[u] + both[u][c2:] for u in uid]
            n = [both[u][:c2] for u in uid]
        av = [_dot(jnp.concatenate([a_kk[u], a_rk[u]], axis=0), vs[u]) for u in uid]
        tinv = [prod[u] + _dot(prod[u], n[u]) for u in uid]
        wu = [_dot(tinv[u], jnp.concatenate([kp[u], av[u][:c2]], axis=1)) for u in uid]
        rb = [_dot(a_rb[u], wu[u]) for u in uid]
        tn = [_dot_tn(jnp.concatenate([wu[u], vs[u]], axis=1),
                      jnp.concatenate([bt[u], kt[u]], axis=1)) for u in uid]
        r2 = [rt[u] - rb[u][:, :PAIR] for u in uid]
        y2 = [av[u][c2:] - rb[u][:, PAIR:] for u in uid]
        s_mix = [(eye - tn[u][:PAIR, :PAIR]) * gam_end[u] for u in uid]
        s_add = [(tn[u][2 * PAIR:, PAIR:] - tn[u][PAIR:2 * PAIR, :PAIR]) * gam_end[u] for u in uid]
        state = [s_ref[p] for p in range(n_pairs)]
        for cc in range(RWKV_UNROLL):
            us = [cc * n_pairs + p for p in range(n_pairs)]
            ys = [_dot_nt(r2[u], state[p]) + y2[u] for p, u in enumerate(us)]
            state = [_dot(state[p], s_mix[u]) + s_add[u] for p, u in enumerate(us)]
            for p in range(n_pairs):
                y_s[pl.ds(rows[cc], c), p * PAIR:(p + 1) * PAIR] = ys[p][:c] + ys[p][c:]
        for p in range(n_pairs):
            s_ref[p] = state[p]
        return carry

    lax.fori_loop(0, n_chunks // RWKV_UNROLL, chunk_body, 0)

    y = y_s[...]
    mean = _group_sum(y, g_ref) * (1.0 / HEAD_DIM)
    yc = y - mean
    var = _group_sum(yc * yc, g_ref) * (1.0 / HEAD_DIM)
    yn = yc * lax.rsqrt(var + RWKV_GN_EPS) * lnw_ref[...] + lnb_ref[...]
    o_ref[...] = (yn + bonus) * g


def _rwkv(rr, rk, rv, lora, w0, w2p, a0, a2p, g2p, k_k, k_a, r_k, ln_w, ln_b, gsum, tm):
    t = rr.shape[0]
    full = lambda shape: pl.BlockSpec(shape, lambda i: (0,) * len(shape))
    row = lambda width: pl.BlockSpec((tm, width), lambda i: (i, 0))
    vec = full((1, D_GROUP))
    big = pltpu.VMEM((tm, D_GROUP), F32)
    return pl.pallas_call(
        _rwkv_kernel,
        grid=(t // tm,),
        in_specs=[row(D_GROUP), row(D_GROUP), row(D_GROUP), row(LANES),
                  vec, full((LANES, D_GROUP)), vec, full((LANES, D_GROUP)),
                  full((LANES, D_GROUP)), vec, vec, vec, vec, vec, full((D_GROUP, D_GROUP))],
        out_specs=row(D_GROUP),
        out_shape=jax.ShapeDtypeStruct((t, D_GROUP), F32),
        scratch_shapes=[pltpu.VMEM((D_GROUP // PAIR, PAIR, PAIR), F32),
                        big, big, big, big, big, big],
        compiler_params=pltpu.CompilerParams(dimension_semantics=("arbitrary",),
                                             vmem_limit_bytes=VMEM_LIMIT),
        name="rwkv",
    )(rr, rk, rv, lora, w0, w2p, a0, a2p, g2p, k_k, k_a, r_k, ln_w, ln_b, gsum)


DSA_QB = 256
DSA_KB = 512
M_INIT = -5e29
DSA_HEAD_GROUP = 4
DSA_ACC_ROWS = 32
DSA_GROUPS = 256
DSA_TOP = 10
DSA_V_ROWS = HEAD_DIM + 16


def _dsa_kernel(qi_ref, kj_ref, qt_ref, k_ref, vt_ref, iqt_ref, ik_ref, iwt_ref, o_ref,
                key_s, thr_s, need_s, over_s, m_s, acc_s, bdq_s, top_s, *, n_sel):
    qb = qt_ref.shape[2]
    kb = k_ref.shape[0]
    step = pl.program_id(0)
    qi = qi_ref[step]
    kj = kj_ref[step]
    q0 = qi * qb
    j_last = (q0 + qb - 1) // kb
    n_kc = j_last + 1
    idx_scale = float((IDX_DIM * N_IDX_HEADS) ** -0.5)

    def causal(jblk):
        s_pos = jblk * kb + lax.broadcasted_iota(jnp.int32, (kb, qb), 0)
        t_pos = q0 + lax.broadcasted_iota(jnp.int32, (kb, qb), 1)
        return s_pos <= t_pos

    @pl.when(kj == 0)
    def _():
        iqt = iqt_ref[...].astype(BF16)
        iwt = iwt_ref[...] * idx_scale

        def score_chunk(kc, diagonal):
            ik = ik_ref[pl.ds(pl.multiple_of(kc * kb, kb), kb), :]
            score = jnp.zeros((kb, qb), F32)
            for h in range(N_IDX_HEADS):
                d = jnp.dot(ik, iqt[h * IDX_DIM:(h + 1) * IDX_DIM, :],
                            preferred_element_type=F32)
                score = score + jnp.maximum(d, 0.0) * iwt[h:h + 1, :]
            bits = pltpu.bitcast(score, jnp.int32)
            keys = jnp.where(bits < 0, bits ^ jnp.int32(0x7FFFFFFF), bits)
            if diagonal:
                keys = jnp.where(causal(kc), keys, jnp.int32(INT_MIN))
            key_s[kc] = keys
            xs = [keys[r * DSA_GROUPS:(r + 1) * DSA_GROUPS] for r in range(kb // DSA_GROUPS)]
            for lvl in range(DSA_TOP):
                s = top_s[lvl]
                for r in range(len(xs)):
                    s, xs[r] = jnp.maximum(s, xs[r]), jnp.minimum(s, xs[r])
                top_s[lvl] = s

        top_s[...] = jnp.full_like(top_s, INT_MIN)

        def full_chunk(kc, carry):
            score_chunk(kc, False)
            return carry

        lax.fori_loop(0, j_last, full_chunk, 0)
        score_chunk(j_last, True)

        def select(count_keys):
            def bit_step(b, cur):
                bit = lax.shift_left(jnp.int32(1), jnp.int32(31) - b)
                cand = (cur | bit) ^ jnp.int32(INT_MIN)
                cnt = jnp.sum(count_keys(cand), axis=0, keepdims=True)
                return jnp.where(cnt >= float(n_sel), cur | bit, cur)

            cur = lax.fori_loop(0, 32, bit_step, jnp.zeros((1, qb), jnp.int32))
            return jnp.maximum(cur ^ jnp.int32(INT_MIN), jnp.int32(INT_MIN + 1))

        def part_count(keys, cand):
            ind = jnp.where(keys >= cand, 1.0, 0.0)
            return jnp.sum(ind.reshape(-1, DSA_ACC_ROWS, qb), axis=0)

        def count_top(cand):
            return lax.fori_loop(0, DSA_TOP, lambda lvl, a: a + part_count(top_s[lvl], cand),
                                 jnp.zeros((DSA_ACC_ROWS, qb), F32))

        def count_all(cand):
            return lax.fori_loop(0, n_kc, lambda kc, a: a + part_count(key_s[kc], cand),
                                 jnp.zeros((DSA_ACC_ROWS, qb), F32))

        def settle(count_keys):
            thr = select(count_keys)
            thr_s[...] = thr
            above = jnp.sum(count_keys(thr + 1), axis=0, keepdims=True)
            upto = jnp.sum(count_keys(thr), axis=0, keepdims=True)
            need_s[...] = float(n_sel) - above
            over_s[...] = upto - float(n_sel)
            return thr

        thr = settle(count_top)
        hidden = jnp.where(top_s[DSA_TOP - 1] >= thr, 1.0, 0.0)

        @pl.when(jnp.max(hidden) > 0.0)
        def _():
            settle(count_all)

        @pl.when(jnp.max(over_s[...]) > 0.0)
        def _():
            thr = thr_s[...]
            need = need_s[...]
            earlier = jnp.where(lax.broadcasted_iota(jnp.int32, (kb, kb), 1)
                                < lax.broadcasted_iota(jnp.int32, (kb, kb), 0), 1.0, 0.0).astype(BF16)

            def drop_surplus(kc, seen):
                keys = key_s[kc]
                tie = keys == thr
                tie_f = jnp.where(tie, 1.0, 0.0)
                rank = seen + jnp.dot(earlier, tie_f.astype(BF16), preferred_element_type=F32)
                key_s[kc] = jnp.where(tie & (rank >= need), thr - 1, keys)
                return seen + jnp.sum(tie_f, axis=0, keepdims=True)

            lax.fori_loop(0, n_kc, drop_surplus, jnp.zeros((1, qb), F32))

        m_s[...] = jnp.full_like(m_s, M_INIT)
        acc_s[...] = jnp.zeros_like(acc_s)
        bdq_s[...] = jnp.zeros_like(bdq_s)
        for h in range(N_HEADS):
            g, hh = divmod(h, DSA_HEAD_GROUP)
            bdq_s[g, hh * HEAD_DIM:(hh + 1) * HEAD_DIM, hh * qb:(hh + 1) * qb] = (
                qt_ref[h] * jnp.asarray(HEAD_DIM ** -0.5, BF16))

    mask = key_s[kj] >= thr_s[...]
    hg = DSA_HEAD_GROUP
    st_all = [jnp.dot(k_ref[:, g * hg * HEAD_DIM:(g + 1) * hg * HEAD_DIM], bdq_s[g],
                      preferred_element_type=F32) for g in range(N_HEADS // hg)]
    for g in range(N_HEADS // hg):
        st_g = st_all[g]
        for hh in range(hg):
            h = g * hg + hh
            st = jnp.where(mask, st_g[:, hh * qb:(hh + 1) * qb], NEG_BIG)
            m_old = m_s[h]
            part = jnp.max(st.reshape(kb // DSA_ACC_ROWS, DSA_ACC_ROWS, qb), axis=0)
            m_new = jnp.maximum(m_old, jnp.max(part, axis=0, keepdims=True))
            p = jnp.exp((st - m_new).astype(BF16))
            acc_s[h] = jnp.exp(m_old - m_new) * acc_s[h] + jnp.dot(
                vt_ref[h], p, preferred_element_type=F32)
            m_s[h] = m_new

    @pl.when(kj == j_last)
    def _():
        for h in range(N_HEADS):
            acc = acc_s[h]
            o_ref[h] = acc[:HEAD_DIM] / acc[HEAD_DIM:HEAD_DIM + 1]


def _dsa(qt, k, vt, iqt, ik, iwt, n_sel):
    t = ik.shape[0]
    qb, kb = DSA_QB, DSA_KB
    assert n_sel <= DSA_GROUPS and kb % DSA_GROUPS == 0
    nq, nk = t // qb, t // kb
    pairs = [(i, j) for i in range(nq) for j in range((i * qb + qb - 1) // kb + 1)]
    qi = jnp.asarray(np.array([p[0] for p in pairs], np.int32))
    kj = jnp.asarray(np.array([p[1] for p in pairs], np.int32))
    grid_spec = pltpu.PrefetchScalarGridSpec(
        num_scalar_prefetch=2,
        grid=(len(pairs),),
        in_specs=[pl.BlockSpec((N_HEADS, HEAD_DIM, qb), lambda s, qi, kj: (0, 0, qi[s])),
                  pl.BlockSpec((kb, D_GROUP), lambda s, qi, kj: (kj[s], 0)),
                  pl.BlockSpec((N_HEADS, DSA_V_ROWS, kb), lambda s, qi, kj: (0, 0, kj[s])),
                  pl.BlockSpec((D_IDX, qb), lambda s, qi, kj: (0, qi[s])),
                  pl.BlockSpec((t, IDX_DIM), lambda s, qi, kj: (0, 0)),
                  pl.BlockSpec((N_IDX_HEADS, qb), lambda s, qi, kj: (0, qi[s]))],
        out_specs=pl.BlockSpec((N_HEADS, HEAD_DIM, qb), lambda s, qi, kj: (0, 0, qi[s])),
        scratch_shapes=[pltpu.VMEM((nk, kb, qb), jnp.int32),
                        pltpu.VMEM((1, qb), jnp.int32),
                        pltpu.VMEM((1, qb), F32),
                        pltpu.VMEM((1, qb), F32),
                        pltpu.VMEM((N_HEADS, 1, qb), F32),
                        pltpu.VMEM((N_HEADS, DSA_V_ROWS, qb), F32),
                        pltpu.VMEM((N_HEADS // DSA_HEAD_GROUP, DSA_HEAD_GROUP * HEAD_DIM,
                                    DSA_HEAD_GROUP * qb), BF16),
                        pltpu.VMEM((DSA_TOP, DSA_GROUPS, qb), jnp.int32)])
    return pl.pallas_call(
        functools.partial(_dsa_kernel, n_sel=n_sel),
        grid_spec=grid_spec,
        out_shape=jax.ShapeDtypeStruct((N_HEADS, HEAD_DIM, t), F32),
        compiler_params=pltpu.CompilerParams(dimension_semantics=("arbitrary",),
                                             vmem_limit_bytes=VMEM_LIMIT),
        name="dsa",
    )(qi, kj, qt, k, vt, iqt, ik, iwt)


def _out_proj_kernel(x_ref, att_ref, rw_ref, wo_ref, n2w_ref, rwt_ref, rb_ref,
                     h_ref, xn_ref, gate_ref):
    tm = x_ref.shape[0]
    att = att_ref[...].reshape(D_GROUP, tm).T
    mix = jnp.concatenate([att, rw_ref[...]], axis=1)
    acc = x_ref[...] + _dot(mix, wo_ref[...])
    h_ref[...] = acc
    xn = acc * lax.rsqrt(jnp.mean(acc * acc, axis=-1, keepdims=True) + RMS_EPS) * n2w_ref[...]
    xn_ref[...] = xn.astype(BF16)
    logits = _dotf(xn, rwt_ref[...]) + rb_ref[...]
    lane = lax.broadcasted_iota(jnp.int32, logits.shape, 1)
    work = logits
    vals, hots = [], []
    for _ in range(TOP_K_EXPERTS):
        m = jnp.max(work, axis=1, keepdims=True)
        idx = jnp.min(jnp.where(work == m, lane, LANES), axis=1, keepdims=True)
        hot = lane == idx
        vals.append(m)
        hots.append(hot)
        work = jnp.where(hot, -jnp.inf, work)
    es = [jnp.exp(vv - vals[0]) for vv in vals]
    denom = es[0] + es[1] + es[2] + es[3]
    gates = jnp.zeros_like(logits)
    for e, hot in zip(es, hots):
        gates = gates + jnp.where(hot, e / denom, 0.0)
    gate_ref[...] = gates


def _out_proj(x2, att_t, rw, wo, n2w, rwt, rb, tm):
    t = x2.shape[0]
    full = lambda shape: pl.BlockSpec(shape, lambda i: (0,) * len(shape))
    row = lambda width: pl.BlockSpec((tm, width), lambda i: (i, 0))
    return pl.pallas_call(
        _out_proj_kernel,
        grid=(t // tm,),
        in_specs=[row(D_MODEL), pl.BlockSpec((N_HEADS, HEAD_DIM, tm), lambda i: (0, 0, i)),
                  row(D_GROUP), full((2 * D_GROUP, D_MODEL)), full((1, D_MODEL)),
                  full((D_MODEL, LANES)), full((1, LANES))],
        out_specs=[row(D_MODEL), row(D_MODEL), row(LANES)],
        out_shape=(jax.ShapeDtypeStruct((t, D_MODEL), F32),
                   jax.ShapeDtypeStruct((t, D_MODEL), BF16),
                   jax.ShapeDtypeStruct((t, LANES), F32)),
        compiler_params=pltpu.CompilerParams(dimension_semantics=("arbitrary",),
                                             vmem_limit_bytes=VMEM_LIMIT),
        name="out_proj",
    )(x2, att_t, rw, wo, n2w, rwt, rb)


MOE_TM = 1024
MOE_RB = 144
MOE_FC = 512


def _route_kernel(gate_ref, rankt_ref, gatet_ref, cnt_ref):
    tm = gate_ref.shape[0]
    gates = gate_ref[...]
    hot = gates > 0.0
    ind = jnp.where(hot, 1.0, 0.0)
    before = (lax.broadcasted_iota(jnp.int32, (tm, tm), 1)
              < lax.broadcasted_iota(jnp.int32, (tm, tm), 0))
    rank = jnp.dot(jnp.where(before, 1.0, 0.0).astype(BF16), ind.astype(BF16),
                   preferred_element_type=F32)
    rankt_ref[...] = jnp.where(hot, rank, -1.0).T
    gatet_ref[...] = gates.T
    cnt = jnp.sum(ind, axis=0, keepdims=True)
    cnt_ref[...] = jnp.broadcast_to(cnt[None], cnt_ref.shape)


def _route(gates, tm):
    t = gates.shape[0]
    return pl.pallas_call(
        _route_kernel,
        grid=(t // tm,),
        in_specs=[pl.BlockSpec((tm, LANES), lambda i: (i, 0))],
        out_specs=[pl.BlockSpec((LANES, tm), lambda i: (0, i)),
                   pl.BlockSpec((LANES, tm), lambda i: (0, i)),
                   pl.BlockSpec((1, 8, LANES), lambda i: (i, 0, 0))],
        out_shape=(jax.ShapeDtypeStruct((LANES, t), F32),
                   jax.ShapeDtypeStruct((LANES, t), F32),
                   jax.ShapeDtypeStruct((t // tm, 8, LANES), F32)),
        compiler_params=pltpu.CompilerParams(dimension_semantics=("arbitrary",),
                                             vmem_limit_bytes=VMEM_LIMIT),
        name="route",
    )(gates)


def _moe_kernel(cnt_ref, h_ref, xn_ref, rankt_ref, gatet_ref, wgu_ref, bgu_ref,
                wdn_ref, bdn_ref, o_ref):
    i = pl.program_id(0)
    e = pl.program_id(1)
    tm = h_ref.shape[0]

    @pl.when(e == 0)
    def _():
        o_ref[...] = h_ref[...]

    n_rows = cnt_ref[i * N_EXPERTS + e]
    n_blocks = (n_rows + MOE_RB - 1) // MOE_RB
    rank_row = rankt_ref[pl.ds(e, 1), :]
    gate_row = gatet_ref[pl.ds(e, 1), :]

    def row_block(bi, carry):
        r0 = (bi * MOE_RB).astype(F32)
        rowid = lax.broadcasted_iota(jnp.int32, (MOE_RB, tm), 0).astype(F32) + r0
        hit = rank_row == rowid
        sel = jnp.where(hit, 1.0, 0.0).astype(BF16)
        gate = jnp.sum(jnp.where(hit, gate_row, 0.0), axis=1, keepdims=True)
        xb = jnp.dot(sel, xn_ref[...], preferred_element_type=F32).astype(BF16)
        y = jnp.zeros((MOE_RB, D_MODEL), F32)
        ups = []
        for fc in range(D_FF // MOE_FC):
            c0 = fc * MOE_FC
            ups.append((jnp.dot(xb, wgu_ref[0, :, c0:c0 + MOE_FC], preferred_element_type=F32),
                        jnp.dot(xb, wgu_ref[0, :, D_FF + c0:D_FF + c0 + MOE_FC],
                                preferred_element_type=F32)))
        for fc in range(D_FF // MOE_FC):
            c0 = fc * MOE_FC
            hg = ups[fc][0] + bgu_ref[0, :, c0:c0 + MOE_FC]
            hl = ups[fc][1] + bgu_ref[0, :, D_FF + c0:D_FF + c0 + MOE_FC]
            glu = jnp.minimum(hg, SWIGLU_LIMIT)
            lin = jnp.clip(hl, -SWIGLU_LIMIT, SWIGLU_LIMIT)
            act = (lin + 1.0) * glu * jax.nn.sigmoid(SWIGLU_ALPHA * glu)
            y = y + jnp.dot(act.astype(BF16), wdn_ref[0, c0:c0 + MOE_FC, :],
                            preferred_element_type=F32)
        y = ((y + bdn_ref[0]) * gate).astype(BF16)
        o_ref[...] += _dot_tn(sel, y)
        return carry

    lax.fori_loop(0, n_blocks, row_block, 0)


def _moe(counts, h1, xn2, rank_t, gate_t, wgu, bgu, wdn, bdn, tm):
    t = h1.shape[0]
    row = lambda width: pl.BlockSpec((tm, width), lambda i, e, c: (i, 0))
    grid_spec = pltpu.PrefetchScalarGridSpec(
        num_scalar_prefetch=1,
        grid=(t // tm, N_EXPERTS),
        in_specs=[row(D_MODEL), row(D_MODEL),
                  pl.BlockSpec((LANES, tm), lambda i, e, c: (0, i)),
                  pl.BlockSpec((LANES, tm), lambda i, e, c: (0, i)),
                  pl.BlockSpec((1, D_MODEL, 2 * D_FF), lambda i, e, c: (e, 0, 0)),
                  pl.BlockSpec((1, 1, 2 * D_FF), lambda i, e, c: (e, 0, 0)),
                  pl.BlockSpec((1, D_FF, D_MODEL), lambda i, e, c: (e, 0, 0)),
                  pl.BlockSpec((1, 1, D_MODEL), lambda i, e, c: (e, 0, 0))],
        out_specs=row(D_MODEL))
    return pl.pallas_call(
        _moe_kernel,
        grid_spec=grid_spec,
        out_shape=jax.ShapeDtypeStruct((t, D_MODEL), F32),
        compiler_params=pltpu.CompilerParams(dimension_semantics=("arbitrary", "arbitrary"),
                                             vmem_limit_bytes=VMEM_LIMIT),
        name="moe",
    )(counts, h1, xn2, rank_t, gate_t, wgu, bgu, wdn, bdn)


def _rope_tables():
    lane = jnp.arange(LANES)

    def rows(period, rot):
        half = rot // 2
        inv_freq = ROPE_THETA ** (-jnp.arange(half, dtype=F32) / half)
        jm = lane % period
        freq = jnp.where(jm < rot, inv_freq[jm % half], 0.0)
        sign = jnp.where(jm < half, -1.0, jnp.where(jm < rot, 1.0, 0.0))
        first = (jm < half).astype(F32)
        return [freq, sign, first]

    idx_rows = rows(IDX_DIM, IDX_DIM // 4)
    only_key = (lane < IDX_DIM).astype(F32)
    return jnp.stack(rows(HEAD_DIM, HEAD_DIM // 4) + idx_rows
                     + [idx_rows[0] * only_key, idx_rows[1] * only_key]).astype(F32)


def _group_matrix():
    g = np.arange(D_GROUP) // HEAD_DIM
    return jnp.asarray((g[:, None] == g[None, :]).astype(np.float32), dtype=BF16)


def _pad_rows(w, r0, rows):
    return jnp.zeros((rows, w.shape[1]), w.dtype).at[r0:r0 + w.shape[0]].set(w)


def kernel(x, positions, norm1_w, w_in, q_norm_w, k_norm_w, rwkv_mu, rwkv_w0, rwkv_w2, rwkv_a0,
           rwkv_a2, rwkv_g2, rwkv_k_k, rwkv_k_a, rwkv_r_k, rwkv_ln_w, rwkv_ln_b, w_out, norm2_w,
           router_w, router_b, exp_w_gu, exp_b_gu, exp_w_down, exp_b_down):
    b, t, _ = x.shape
    assert b == 1 and w_in.shape[0] == 1, "single sequence, single layer"
    assert t % DSA_KB == 0 and t % DSA_QB == 0 and t % MOE_TM == 0
    x2 = x[0]
    pos_f = positions[0].astype(F32)[:, None]
    n_sel = min(TOPK_MAX, t // 4)

    w = w_in[0]
    a0 = 3 * D_GROUP
    att_cols = a0 + D_IDX + IDX_DIM + N_IDX_HEADS
    w_att, w_rw = w[:, :att_cols], w[:, att_cols:]
    w_sm = jnp.zeros((D_MODEL, LANES), F32).at[:, :IDX_DIM + N_IDX_HEADS].set(w_att[:, a0 + D_IDX:])
    w_packed = jnp.concatenate(
        [w_att[:, :a0], w_rw[:, :a0], w_att[:, a0:a0 + D_IDX], w_sm, w_rw[:, a0:]],
        axis=1).astype(BF16)
    mu = rwkv_mu[0][None, :]
    tile8 = lambda z: jnp.tile(z, N_HEADS)[None, :]
    gsum = _group_matrix()

    tm = 256
    q, k, v, rr, rk, rv, iq, sm, lora = _in_proj(
        x2, pos_f, norm1_w, w_packed, mu, tile8(q_norm_w[0]), tile8(k_norm_w[0]), gsum,
        _rope_tables(), tm)

    vec = lambda z: z.reshape(1, D_GROUP)
    rw = _rwkv(rr, rk, rv, lora, vec(rwkv_w0[0]),
               _pad_rows(rwkv_w2[0], 0, LANES), vec(rwkv_a0[0]),
               _pad_rows(rwkv_a2[0], D_DECAY_LORA, LANES),
               _pad_rows(rwkv_g2[0], D_DECAY_LORA + D_AAA_LORA, LANES),
               vec(rwkv_k_k[0]), vec(rwkv_k_a[0]), vec(rwkv_r_k[0]), vec(rwkv_ln_w[0]),
               vec(rwkv_ln_b[0]), gsum, tm)

    heads = lambda z: z.reshape(t, N_HEADS, HEAD_DIM)
    vt = jnp.concatenate([heads(v).transpose(1, 2, 0),
                          jnp.ones((N_HEADS, DSA_V_ROWS - HEAD_DIM, t), BF16)], axis=1)
    att_t = _dsa(heads(q).transpose(1, 2, 0), k, vt,
                 iq.T, sm[:, :IDX_DIM].astype(BF16),
                 sm[:, IDX_DIM:IDX_DIM + N_IDX_HEADS].T, n_sel)

    rwt = jnp.zeros((D_MODEL, LANES), F32).at[:, :N_EXPERTS].set(router_w[0])
    rb = jnp.full((1, LANES), NEG_BIG, F32).at[0, :N_EXPERTS].set(router_b[0])
    h1, xn2, gates = _out_proj(x2, att_t, rw, w_out[0].astype(BF16), norm2_w, rwt, rb, tm)

    rank_t, gate_t, cnt = _route(gates, MOE_TM)
    counts = cnt[:, 0, :N_EXPERTS].astype(jnp.int32).reshape(-1)
    out = _moe(counts, h1, xn2, rank_t, gate_t, exp_w_gu[0].astype(BF16),
               exp_b_gu[0][:, None, :], exp_w_down[0].astype(BF16), exp_b_down[0][:, None, :],
               MOE_TM)
    return out[None]
```

```python
import functools

import jax
import jax.numpy as jnp
import numpy as np
from jax import lax
from jax.experimental import pallas as pl
from jax.experimental.pallas import tpu as pltpu

F32 = jnp.float32
BF16 = jnp.bfloat16
HIGHEST = lax.Precision.HIGHEST

D_MODEL = 1024
HEAD_DIM = 64
N_HEADS = 8
D_GROUP = N_HEADS * HEAD_DIM
ROPE_THETA = 500000.0
N_IDX_HEADS = 8
IDX_DIM = 32
D_IDX = N_IDX_HEADS * IDX_DIM
TOPK_MAX = 256
D_DECAY_LORA = 32
D_AAA_LORA = 32
D_GATE_LORA = 64
RWKV_GN_EPS = 64e-5
N_EXPERTS = 32
TOP_K_EXPERTS = 4
D_FF = 1024
SWIGLU_LIMIT = 7.0
SWIGLU_ALPHA = 1.702
RMS_EPS = 1e-6

LANES = 128
VMEM_LIMIT = 56 * 1024 * 1024

NEG_BIG = -1e30
INT_MIN = -(2 ** 31)


def _dot(a, b):
    return jnp.dot(a.astype(BF16), b.astype(BF16), preferred_element_type=F32)


def _dotf(a, b):
    return jnp.dot(a, b, preferred_element_type=F32, precision=HIGHEST)


def _dot_nt(a, b):
    return lax.dot_general(a.astype(BF16), b.astype(BF16), (((1,), (1,)), ((), ())),
                           preferred_element_type=F32)


def _dot_tn(a, b):
    return lax.dot_general(a.astype(BF16), b.astype(BF16), (((0,), (0,)), ((), ())),
                           preferred_element_type=F32)


def _group_sum(z, g_ref):
    hi = z.astype(BF16)
    lo = (z - hi.astype(F32)).astype(BF16)
    g = g_ref[...]
    return (jnp.dot(hi, g, preferred_element_type=F32)
            + jnp.dot(lo, g, preferred_element_type=F32))


_C_Q, _C_K, _C_V, _C_RR, _C_RK, _C_RV = (i * D_GROUP for i in range(6))
_C_IQ = 6 * D_GROUP
_C_SM = _C_IQ + D_IDX
_C_LORA = _C_SM + LANES
D_IN_PACKED = _C_LORA + LANES
D_SHIFT = 3 * D_GROUP + LANES


def _in_proj_kernel(x_ref, pos_ref, n1w_ref, w_ref, mu_ref, qnw_ref, knw_ref, g_ref, rope_ref,
                    q_ref, k_ref, v_ref, rr_ref, rk_ref, rv_ref, iq_ref, sm_ref, lora_ref,
                    carry_ref):
    tm = x_ref.shape[0]

    @pl.when(pl.program_id(0) == 0)
    def _():
        carry_ref[...] = jnp.zeros_like(carry_ref)

    x = x_ref[...]
    xn = x * lax.rsqrt(jnp.mean(x * x, axis=-1, keepdims=True) + RMS_EPS) * n1w_ref[...]
    xb = xn.astype(BF16)
    pos = pos_ref[...]
    rope = rope_ref[...]

    def tables(frow, srow):
        ang = pos * rope[frow:frow + 1, :]
        return jnp.cos(ang), jnp.sin(ang) * rope[srow:srow + 1, :]

    def widen(z, reps):
        return jnp.concatenate([z] * reps, axis=1)

    def rotary(z, c, s, first_row, half):
        w = z.shape[1]
        first = jnp.concatenate([rope[first_row:first_row + 1, :]] * (w // LANES), axis=1) > 0.5
        partner = jnp.where(first, pltpu.roll(z, w - half, 1), pltpu.roll(z, half, 1))
        return z * c + partner * s

    def head_norm(z, w_row):
        ms = _group_sum(z * z, g_ref) * (1.0 / HEAD_DIM)
        return z * lax.rsqrt(ms + RMS_EPS) * w_row

    def proj(c0, width):
        return jnp.dot(xb, w_ref[:, c0:c0 + width], preferred_element_type=F32)

    def shift(z, c0):
        width = z.shape[1]
        row = lax.broadcasted_iota(jnp.int32, z.shape, 0)
        prev = jnp.where(row == 0, carry_ref[0:1, c0:c0 + width], pltpu.roll(z, 1, 0))
        carry_ref[0:1, c0:c0 + width] = z[tm - 1:tm, :]
        return z + (prev - z) * mu_ref[:, c0:c0 + width]

    cq, sq = (widen(z, D_GROUP // LANES) for z in tables(0, 1))
    q = rotary(head_norm(proj(_C_Q, D_GROUP), qnw_ref[...]), cq, sq, 2, HEAD_DIM // 8)
    q_ref[...] = q.astype(BF16)
    k = rotary(head_norm(proj(_C_K, D_GROUP), knw_ref[...]), cq, sq, 2, HEAD_DIM // 8)
    k_ref[...] = k.astype(BF16)
    v_ref[...] = proj(_C_V, D_GROUP).astype(BF16)

    rr_ref[...] = shift(proj(_C_RR, D_GROUP), 0)
    rk_ref[...] = shift(proj(_C_RK, D_GROUP), D_GROUP)
    rv_ref[...] = shift(proj(_C_RV, D_GROUP), 2 * D_GROUP)
    lora_ref[...] = shift(proj(_C_LORA, LANES), 3 * D_GROUP)

    ci, si = tables(3, 4)
    iq_ref[...] = rotary(proj(_C_IQ, D_IDX), widen(ci, D_IDX // LANES), widen(si, D_IDX // LANES),
                         5, IDX_DIM // 8)
    only_key = rope[6:7, :]
    sm_ref[...] = rotary(proj(_C_SM, LANES), ci * only_key + (1.0 - only_key), si * only_key,
                         5, IDX_DIM // 8)


def _in_proj(x2, pos_f, n1w, w_packed, mu_packed, qnw, knw, gmat, rope, tm):
    t = x2.shape[0]
    full = lambda shape: pl.BlockSpec(shape, lambda i: (0,) * len(shape))
    row = lambda width: pl.BlockSpec((tm, width), lambda i: (i, 0))
    out_shapes = (
        jax.ShapeDtypeStruct((t, D_GROUP), BF16),
        jax.ShapeDtypeStruct((t, D_GROUP), BF16),
        jax.ShapeDtypeStruct((t, D_GROUP), BF16),
        jax.ShapeDtypeStruct((t, D_GROUP), F32),
        jax.ShapeDtypeStruct((t, D_GROUP), F32),
        jax.ShapeDtypeStruct((t, D_GROUP), F32),
        jax.ShapeDtypeStruct((t, D_IDX), F32),
        jax.ShapeDtypeStruct((t, LANES), F32),
        jax.ShapeDtypeStruct((t, LANES), F32),
    )
    return pl.pallas_call(
        _in_proj_kernel,
        grid=(t // tm,),
        in_specs=[row(D_MODEL), row(1), full((1, D_MODEL)), full((D_MODEL, D_IN_PACKED)),
                  full((1, D_SHIFT)), full((1, D_GROUP)), full((1, D_GROUP)),
                  full((D_GROUP, D_GROUP)), full((8, LANES))],
        out_specs=[row(D_GROUP)] * 6 + [row(D_IDX), row(LANES), row(LANES)],
        out_shape=out_shapes,
        scratch_shapes=[pltpu.VMEM((8, D_SHIFT), F32)],
        compiler_params=pltpu.CompilerParams(dimension_semantics=("arbitrary",),
                                             vmem_limit_bytes=VMEM_LIMIT),
        name="in_proj",
    )(x2, pos_f, n1w, w_packed, mu_packed, qnw, knw, gmat, rope)


RWKV_CHUNK = 64
RWKV_UNROLL = 2
PAIR = 2 * HEAD_DIM


def _rwkv_kernel(r_ref, k_ref, v_ref, lora_ref, w0_ref, w2_ref, a0_ref, a2_ref, g2_ref,
                 kk_ref, ka_ref, rk_ref, lnw_ref, lnb_ref, g_ref,
                 o_ref,
                 s_ref, ld_s, r_s, k2_s, b_s, kk_s, y_s):
    tm = r_ref.shape[0]
    n_chunks = tm // RWKV_CHUNK
    n_pairs = D_GROUP // PAIR
    c = RWKV_CHUNK

    @pl.when(pl.program_id(0) == 0)
    def _():
        s_ref[...] = jnp.zeros_like(s_ref)

    lora = lora_ref[...]
    r = r_ref[...]
    k = k_ref[...]
    v = v_ref[...]
    zarg = w0_ref[...] + _dotf(jnp.tanh(lora), w2_ref[...])
    sp = jnp.maximum(-zarg, 0.0) + jnp.log1p(jnp.exp(-jnp.abs(zarg)))
    ld_s[...] = -jnp.exp(-sp - 0.5)
    a = jax.nn.sigmoid(a0_ref[...] + _dotf(lora, a2_ref[...]))
    g = _dotf(jax.nn.sigmoid(lora), g2_ref[...])
    kk = k * kk_ref[...]
    kk = kk * lax.rsqrt(jnp.maximum(_group_sum(kk * kk, g_ref), 1e-24))
    k2 = k * (1.0 + (a - 1.0) * ka_ref[...])
    bonus = _group_sum(r * k2 * rk_ref[...], g_ref) * v
    r_s[...] = r
    k2_s[...] = k2
    kk_s[...] = kk
    b_s[...] = kk * a

    row = lax.broadcasted_iota(jnp.int32, (2 * c, 2 * c), 0)
    col = lax.broadcasted_iota(jnp.int32, (2 * c, 2 * c), 1)
    same_head = (row >= c) == (col >= c)
    strict = same_head & (col < row)
    incl = same_head & (col <= row)
    eye = (row == col).astype(F32)
    tri = (lax.broadcasted_iota(jnp.int32, (c, c), 1)
           <= lax.broadcasted_iota(jnp.int32, (c, c), 0)).astype(F32)
    lane = lax.broadcasted_iota(jnp.int32, (c, PAIR), 1)
    head0 = lane < HEAD_DIM

    def stack(z):
        return jnp.concatenate([jnp.where(head0, z, 0.0), jnp.where(head0, 0.0, z)], axis=0)

    def chunk_body(ci, carry):
        c2 = 2 * c
        units = [(cc, p) for cc in range(RWKV_UNROLL) for p in range(n_pairs)]
        uid = range(len(units))
        rows = [pl.multiple_of((ci * RWKV_UNROLL + cc) * c, c) for cc in range(RWKV_UNROLL)]
        ld_all = [ld_s[pl.ds(r0, c), :] for r0 in rows]
        cum_all = [_dotf(tri, ld) for ld in ld_all]
        rt, kt, bt, kp, vs, lhs, rhs, gam_end = [], [], [], [], [], [], [], []
        for cc, p in units:
            r0 = rows[cc]
            cols = slice(p * PAIR, (p + 1) * PAIR)
            ld = ld_all[cc][:, cols]
            cum = cum_all[cc][:, cols]
            gam = jnp.exp(cum)
            inv = jnp.exp(-cum)
            gam_prev = jnp.exp(cum - ld)
            gam_end.append(gam[c - 1:c, :])
            rt.append(stack(r_s[pl.ds(r0, c), cols] * gam))
            kt_raw = k2_s[pl.ds(r0, c), cols] * inv
            bt_raw = b_s[pl.ds(r0, c), cols] * inv
            kt.append(stack(kt_raw))
            bt.append(stack(bt_raw))
            kp.append(stack(kk_s[pl.ds(r0, c), cols] * gam_prev))
            vs.append(stack(v_ref[pl.ds(r0, c), cols]))
            lhs.append(jnp.concatenate([kp[-1], rt[-1]], axis=0))
            rhs.append(jnp.concatenate([bt_raw, bt_raw, kt_raw, kt_raw], axis=0))
        aa = [_dot_nt(lhs[u], rhs[u]) for u in uid]
        a_kk = [jnp.where(strict, aa[u][:c2, c2:], 0.0) for u in uid]
        a_rb = [jnp.where(incl, aa[u][c2:, :c2], 0.0) for u in uid]
        a_rk = [jnp.where(incl, aa[u][c2:, c2:], 0.0) for u in uid]
        n = [-jnp.where(strict, aa[u][:c2, :c2], 0.0) for u in uid]
        prod = [eye + n[u] for u in uid]
        n = [_dot(n[u], n[u]) for u in uid]
        for _ in range(int(np.log2(c)) - 2):
            both = [_dot(jnp.concatenate([n[u], prod[u]], axis=0), n[u]) for u in uid]
            prod = [prod[u] + both[u][c2:] for u in uid]
            n = [both[u][:c2] for u in uid]
        av = [_dot(jnp.concatenate([a_kk[u], a_rk[u]], axis=0), vs[u]) for u in uid]
        tinv = [prod[u] + _dot(prod[u], n[u]) for u in uid]
        wu = [_dot(tinv[u], jnp.concatenate([kp[u], av[u][:c2]], axis=1)) for u in uid]
        rb = [_dot(a_rb[u], wu[u]) for u in uid]
        tn = [_dot_tn(jnp.concatenate([wu[u], vs[u]], axis=1),
                      jnp.concatenate([bt[u], kt[u]], axis=1)) for u in uid]
        r2 = [rt[u] - rb[u][:, :PAIR] for u in uid]
        y2 = [av[u][c2:] - rb[u][:, PAIR:] for u in uid]
        s_mix = [(eye - tn[u][:PAIR, :PAIR]) * gam_end[u] for u in uid]
        s_add = [(tn[u][2 * PAIR:, PAIR:] - tn[u][PAIR:2 * PAIR, :PAIR]) * gam_end[u] for u in uid]
        state = [s_ref[p] for p in range(n_pairs)]
        for cc in range(RWKV_UNROLL):
            us = [cc * n_pairs + p for p in range(n_pairs)]
            ys = [_dot_nt(r2[u], state[p]) + y2[u] for p, u in enumerate(us)]
            state = [_dot(state[p], s_mix[u]) + s_add[u] for p, u in enumerate(us)]
            for p in range(n_pairs):
                y_s[pl.ds(rows[cc], c), p * PAIR:(p + 1) * PAIR] = ys[p][:c] + ys[p][c:]
        for p in range(n_pairs):
            s_ref[p] = state[p]
        return carry

    lax.fori_loop(0, n_chunks // RWKV_UNROLL, chunk_body, 0)

    y = y_s[...]
    mean = _group_sum(y, g_ref) * (1.0 / HEAD_DIM)
    yc = y - mean
    var = _group_sum(yc * yc, g_ref) * (1.0 / HEAD_DIM)
    yn = yc * lax.rsqrt(var + RWKV_GN_EPS) * lnw_ref[...] + lnb_ref[...]
    o_ref[...] = (yn + bonus) * g


def _rwkv(rr, rk, rv, lora, w0, w2p, a0, a2p, g2p, k_k, k_a, r_k, ln_w, ln_b, gsum, tm):
    t = rr.shape[0]
    full = lambda shape: pl.BlockSpec(shape, lambda i: (0,) * len(shape))
    row = lambda width: pl.BlockSpec((tm, width), lambda i: (i, 0))
    vec = full((1, D_GROUP))
    big = pltpu.VMEM((tm, D_GROUP), F32)
    return pl.pallas_call(
        _rwkv_kernel,
        grid=(t // tm,),
        in_specs=[row(D_GROUP), row(D_GROUP), row(D_GROUP), row(LANES),
                  vec, full((LANES, D_GROUP)), vec, full((LANES, D_GROUP)),
                  full((LANES, D_GROUP)), vec, vec, vec, vec, vec, full((D_GROUP, D_GROUP))],
        out_specs=row(D_GROUP),
        out_shape=jax.ShapeDtypeStruct((t, D_GROUP), F32),
        scratch_shapes=[pltpu.VMEM((D_GROUP // PAIR, PAIR, PAIR), F32),
                        big, big, big, big, big, big],
        compiler_params=pltpu.CompilerParams(dimension_semantics=("arbitrary",),
                                             vmem_limit_bytes=VMEM_LIMIT),
        name="rwkv",
    )(rr, rk, rv, lora, w0, w2p, a0, a2p, g2p, k_k, k_a, r_k, ln_w, ln_b, gsum)


DSA_QB = 256
DSA_KB = 512
M_INIT = -5e29
DSA_HEAD_GROUP = 4
DSA_ACC_ROWS = 32
DSA_GROUPS = 256
DSA_TOP = 10
DSA_V_ROWS = HEAD_DIM + 16


def _dsa_kernel(qi_ref, kj_ref, qt_ref, k_ref, vt_ref, iqt_ref, ik_ref, iwt_ref, o_ref,
                key_s, thr_s, need_s, over_s, m_s, acc_s, bdq_s, top_s, *, n_sel):
    qb = qt_ref.shape[2]
    kb = k_ref.shape[0]
    step = pl.program_id(0)
    qi = qi_ref[step]
    kj = kj_ref[step]
    q0 = qi * qb
    j_last = (q0 + qb - 1) // kb
    n_kc = j_last + 1
    idx_scale = float((IDX_DIM * N_IDX_HEADS) ** -0.5)

    def causal(jblk):
        s_pos = jblk * kb + lax.broadcasted_iota(jnp.int32, (kb, qb), 0)
        t_pos = q0 + lax.broadcasted_iota(jnp.int32, (kb, qb), 1)
        return s_pos <= t_pos

    @pl.when(kj == 0)
    def _():
        iqt = iqt_ref[...].astype(BF16)
        iwt = iwt_ref[...] * idx_scale

        def score_chunk(kc, diagonal):
            ik = ik_ref[pl.ds(pl.multiple_of(kc * kb, kb), kb), :]
            score = jnp.zeros((kb, qb), F32)
            for h in range(N_IDX_HEADS):
                d = jnp.dot(ik, iqt[h * IDX_DIM:(h + 1) * IDX_DIM, :],
                            preferred_element_type=F32)
                score = score + jnp.maximum(d, 0.0) * iwt[h:h + 1, :]
            bits = pltpu.bitcast(score, jnp.int32)
            keys = jnp.where(bits < 0, bits ^ jnp.int32(0x7FFFFFFF), bits)
            if diagonal:
                keys = jnp.where(causal(kc), keys, jnp.int32(INT_MIN))
            key_s[kc] = keys
            xs = [keys[r * DSA_GROUPS:(r + 1) * DSA_GROUPS] for r in range(kb // DSA_GROUPS)]
            for lvl in range(DSA_TOP):
                s = top_s[lvl]
                for r in range(len(xs)):
                    s, xs[r] = jnp.maximum(s, xs[r]), jnp.minimum(s, xs[r])
                top_s[lvl] = s

        top_s[...] = jnp.full_like(top_s, INT_MIN)

        def full_chunk(kc, carry):
            score_chunk(kc, False)
            return carry

        lax.fori_loop(0, j_last, full_chunk, 0)
        score_chunk(j_last, True)

        def select(count_keys):
            def bit_step(b, cur):
                bit = lax.shift_left(jnp.int32(1), jnp.int32(31) - b)
                cand = (cur | bit) ^ jnp.int32(INT_MIN)
                cnt = jnp.sum(count_keys(cand), axis=0, keepdims=True)
                return jnp.where(cnt >= float(n_sel), cur | bit, cur)

            cur = lax.fori_loop(0, 32, bit_step, jnp.zeros((1, qb), jnp.int32))
            return jnp.maximum(cur ^ jnp.int32(INT_MIN), jnp.int32(INT_MIN + 1))

        def part_count(keys, cand):
            ind = jnp.where(keys >= cand, 1.0, 0.0)
            return jnp.sum(ind.reshape(-1, DSA_ACC_ROWS, qb), axis=0)

        def count_top(cand):
            return lax.fori_loop(0, DSA_TOP, lambda lvl, a: a + part_count(top_s[lvl], cand),
                                 jnp.zeros((DSA_ACC_ROWS, qb), F32))

        def count_all(cand):
            return lax.fori_loop(0, n_kc, lambda kc, a: a + part_count(key_s[kc], cand),
                                 jnp.zeros((DSA_ACC_ROWS, qb), F32))

        def settle(count_keys):
            thr = select(count_keys)
            thr_s[...] = thr
            above = jnp.sum(count_keys(thr + 1), axis=0, keepdims=True)
            upto = jnp.sum(count_keys(thr), axis=0, keepdims=True)
            need_s[...] = float(n_sel) - above
            over_s[...] = upto - float(n_sel)
            return thr

        thr = settle(count_top)
        hidden = jnp.where(top_s[DSA_TOP - 1] >= thr, 1.0, 0.0)

        @pl.when(jnp.max(hidden) > 0.0)
        def _():
            settle(count_all)

        @pl.when(jnp.max(over_s[...]) > 0.0)
        def _():
            thr = thr_s[...]
            need = need_s[...]
            sub = LANES
            earlier = jnp.where(lax.broadcasted_iota(jnp.int32, (sub, sub), 1)
                                < lax.broadcasted_iota(jnp.int32, (sub, sub), 0), 1.0, 0.0).astype(BF16)

            def drop_surplus(kc, seen):
                keys = key_s[kc]
                tie = keys == thr
                tie_f = jnp.where(tie, 1.0, 0.0)
                ranks = []
                for r in range(kb // sub):
                    part = tie_f[r * sub:(r + 1) * sub]
                    ranks.append(seen + jnp.dot(earlier, part.astype(BF16),
                                                preferred_element_type=F32))
                    seen = seen + jnp.sum(part, axis=0, keepdims=True)
                rank = jnp.concatenate(ranks, axis=0)
                key_s[kc] = jnp.where(tie & (rank >= need), thr - 1, keys)
                return seen

            lax.fori_loop(0, n_kc, drop_surplus, jnp.zeros((1, qb), F32))

        m_s[...] = jnp.full_like(m_s, M_INIT)
        acc_s[...] = jnp.zeros_like(acc_s)
        bdq_s[...] = jnp.zeros_like(bdq_s)
        for h in range(N_HEADS):
            g, hh = divmod(h, DSA_HEAD_GROUP)
            bdq_s[g, hh * HEAD_DIM:(hh + 1) * HEAD_DIM, hh * qb:(hh + 1) * qb] = (
                qt_ref[h] * jnp.asarray(HEAD_DIM ** -0.5, BF16))

    mask = key_s[kj] >= thr_s[...]
    hg = DSA_HEAD_GROUP
    st_all = [jnp.dot(k_ref[:, g * hg * HEAD_DIM:(g + 1) * hg * HEAD_DIM], bdq_s[g],
                      preferred_element_type=F32) for g in range(N_HEADS // hg)]
    for g in range(N_HEADS // hg):
        st_g = st_all[g]
        for hh in range(hg):
            h = g * hg + hh
            st = jnp.where(mask, st_g[:, hh * qb:(hh + 1) * qb], NEG_BIG)
            m_old = m_s[h]
            part = jnp.max(st.reshape(kb // DSA_ACC_ROWS, DSA_ACC_ROWS, qb), axis=0)
            m_new = jnp.maximum(m_old, jnp.max(part, axis=0, keepdims=True))
            p = jnp.exp((st - m_new).astype(BF16))
            acc_s[h] = jnp.exp(m_old - m_new) * acc_s[h] + jnp.dot(
                vt_ref[h], p, preferred_element_type=F32)
            m_s[h] = m_new

    @pl.when(kj == j_last)
    def _():
        for h in range(N_HEADS):
            acc = acc_s[h]
            o_ref[h] = acc[:HEAD_DIM] / acc[HEAD_DIM:HEAD_DIM + 1]


def _dsa(qt, k, vt, iqt, ik, iwt, n_sel):
    t = ik.shape[0]
    qb, kb = DSA_QB, DSA_KB
    assert n_sel <= DSA_GROUPS and kb % DSA_GROUPS == 0
    nq, nk = t // qb, t // kb
    pairs = [(i, j) for i in range(nq) for j in range((i * qb + qb - 1) // kb + 1)]
    qi = jnp.asarray(np.array([p[0] for p in pairs], np.int32))
    kj = jnp.asarray(np.array([p[1] for p in pairs], np.int32))
    grid_spec = pltpu.PrefetchScalarGridSpec(
        num_scalar_prefetch=2,
        grid=(len(pairs),),
        in_specs=[pl.BlockSpec((N_HEADS, HEAD_DIM, qb), lambda s, qi, kj: (0, 0, qi[s])),
                  pl.BlockSpec((kb, D_GROUP), lambda s, qi, kj: (kj[s], 0)),
                  pl.BlockSpec((N_HEADS, DSA_V_ROWS, kb), lambda s, qi, kj: (0, 0, kj[s])),
                  pl.BlockSpec((D_IDX, qb), lambda s, qi, kj: (0, qi[s])),
                  pl.BlockSpec((t, IDX_DIM), lambda s, qi, kj: (0, 0)),
                  pl.BlockSpec((N_IDX_HEADS, qb), lambda s, qi, kj: (0, qi[s]))],
        out_specs=pl.BlockSpec((N_HEADS, HEAD_DIM, qb), lambda s, qi, kj: (0, 0, qi[s])),
        scratch_shapes=[pltpu.VMEM((nk, kb, qb), jnp.int32),
                        pltpu.VMEM((1, qb), jnp.int32),
                        pltpu.VMEM((1, qb), F32),
                        pltpu.VMEM((1, qb), F32),
                        pltpu.VMEM((N_HEADS, 1, qb), F32),
                        pltpu.VMEM((N_HEADS, DSA_V_ROWS, qb), F32),
                        pltpu.VMEM((N_HEADS // DSA_HEAD_GROUP, DSA_HEAD_GROUP * HEAD_DIM,
                                    DSA_HEAD_GROUP * qb), BF16),
                        pltpu.VMEM((DSA_TOP, DSA_GROUPS, qb), jnp.int32)])
    return pl.pallas_call(
        functools.partial(_dsa_kernel, n_sel=n_sel),
        grid_spec=grid_spec,
        out_shape=jax.ShapeDtypeStruct((N_HEADS, HEAD_DIM, t), F32),
        compiler_params=pltpu.CompilerParams(dimension_semantics=("arbitrary",),
                                             vmem_limit_bytes=VMEM_LIMIT),
        name="dsa",
    )(qi, kj, qt, k, vt, iqt, ik, iwt)


def _out_proj_kernel(x_ref, att_ref, rw_ref, wo_ref, n2w_ref, rwt_ref, rb_ref,
                     h_ref, xn_ref, gate_ref):
    tm = x_ref.shape[0]
    att = att_ref[...].reshape(D_GROUP, tm).T
    mix = jnp.concatenate([att, rw_ref[...]], axis=1)
    acc = x_ref[...] + _dot(mix, wo_ref[...])
    h_ref[...] = acc
    xn = acc * lax.rsqrt(jnp.mean(acc * acc, axis=-1, keepdims=True) + RMS_EPS) * n2w_ref[...]
    xn_ref[...] = xn.astype(BF16)
    logits = _dotf(xn, rwt_ref[...]) + rb_ref[...]
    lane = lax.broadcasted_iota(jnp.int32, logits.shape, 1)
    work = logits
    vals, hots = [], []
    for _ in range(TOP_K_EXPERTS):
        m = jnp.max(work, axis=1, keepdims=True)
        idx = jnp.min(jnp.where(work == m, lane, LANES), axis=1, keepdims=True)
        hot = lane == idx
        vals.append(m)
        hots.append(hot)
        work = jnp.where(hot, -jnp.inf, work)
    es = [jnp.exp(vv - vals[0]) for vv in vals]
    denom = es[0] + es[1] + es[2] + es[3]
    gates = jnp.zeros_like(logits)
    for e, hot in zip(es, hots):
        gates = gates + jnp.where(hot, e / denom, 0.0)
    gate_ref[...] = gates


def _out_proj(x2, att_t, rw, wo, n2w, rwt, rb, tm):
    t = x2.shape[0]
    full = lambda shape: pl.BlockSpec(shape, lambda i: (0,) * len(shape))
    row = lambda width: pl.BlockSpec((tm, width), lambda i: (i, 0))
    return pl.pallas_call(
        _out_proj_kernel,
        grid=(t // tm,),
        in_specs=[row(D_MODEL), pl.BlockSpec((N_HEADS, HEAD_DIM, tm), lambda i: (0, 0, i)),
                  row(D_GROUP), full((2 * D_GROUP, D_MODEL)), full((1, D_MODEL)),
                  full((D_MODEL, LANES)), full((1, LANES))],
        out_specs=[row(D_MODEL), row(D_MODEL), row(LANES)],
        out_shape=(jax.ShapeDtypeStruct((t, D_MODEL), F32),
                   jax.ShapeDtypeStruct((t, D_MODEL), BF16),
                   jax.ShapeDtypeStruct((t, LANES), F32)),
        compiler_params=pltpu.CompilerParams(dimension_semantics=("arbitrary",),
                                             vmem_limit_bytes=VMEM_LIMIT),
        name="out_proj",
    )(x2, att_t, rw, wo, n2w, rwt, rb)


MOE_TM = 1024
MOE_RB = 144
MOE_FC = 512


def _route_kernel(gate_ref, rankt_ref, gatet_ref, cnt_ref):
    tm = gate_ref.shape[0]
    gates = gate_ref[...]
    hot = gates > 0.0
    ind = jnp.where(hot, 1.0, 0.0)
    before = (lax.broadcasted_iota(jnp.int32, (tm, tm), 1)
              < lax.broadcasted_iota(jnp.int32, (tm, tm), 0))
    rank = jnp.dot(jnp.where(before, 1.0, 0.0).astype(BF16), ind.astype(BF16),
                   preferred_element_type=F32)
    rankt_ref[...] = jnp.where(hot, rank, -1.0).T
    gatet_ref[...] = gates.T
    cnt = jnp.sum(ind, axis=0, keepdims=True)
    cnt_ref[...] = jnp.broadcast_to(cnt[None], cnt_ref.shape)


def _route(gates, tm):
    t = gates.shape[0]
    return pl.pallas_call(
        _route_kernel,
        grid=(t // tm,),
        in_specs=[pl.BlockSpec((tm, LANES), lambda i: (i, 0))],
        out_specs=[pl.BlockSpec((LANES, tm), lambda i: (0, i)),
                   pl.BlockSpec((LANES, tm), lambda i: (0, i)),
                   pl.BlockSpec((1, 8, LANES), lambda i: (i, 0, 0))],
        out_shape=(jax.ShapeDtypeStruct((LANES, t), F32),
                   jax.ShapeDtypeStruct((LANES, t), F32),
                   jax.ShapeDtypeStruct((t // tm, 8, LANES), F32)),
        compiler_params=pltpu.CompilerParams(dimension_semantics=("arbitrary",),
                                             vmem_limit_bytes=VMEM_LIMIT),
        name="route",
    )(gates)


def _moe_kernel(cnt_ref, h_ref, xn_ref, rankt_ref, gatet_ref, wgu_ref, bgu_ref,
                wdn_ref, bdn_ref, o_ref):
    i = pl.program_id(0)
    e = pl.program_id(1)
    tm = h_ref.shape[0]

    @pl.when(e == 0)
    def _():
        o_ref[...] = h_ref[...]

    n_rows = cnt_ref[i * N_EXPERTS + e]
    n_blocks = (n_rows + MOE_RB - 1) // MOE_RB
    rank_row = rankt_ref[pl.ds(e, 1), :]
    gate_row = gatet_ref[pl.ds(e, 1), :]

    def row_block(bi, carry):
        r0 = (bi * MOE_RB).astype(F32)
        rowid = lax.broadcasted_iota(jnp.int32, (MOE_RB, tm), 0).astype(F32) + r0
        hit = rank_row == rowid
        sel = jnp.where(hit, 1.0, 0.0).astype(BF16)
        gate = jnp.sum(jnp.where(hit, gate_row, 0.0), axis=1, keepdims=True)
        xb = jnp.dot(sel, xn_ref[...], preferred_element_type=F32).astype(BF16)
        y = jnp.zeros((MOE_RB, D_MODEL), F32)
        ups = []
        for fc in range(D_FF // MOE_FC):
            c0 = fc * MOE_FC
            ups.append((jnp.dot(xb, wgu_ref[0, :, c0:c0 + MOE_FC], preferred_element_type=F32),
                        jnp.dot(xb, wgu_ref[0, :, D_FF + c0:D_FF + c0 + MOE_FC],
                                preferred_element_type=F32)))
        for fc in range(D_FF // MOE_FC):
            c0 = fc * MOE_FC
            hg = ups[fc][0] + bgu_ref[0, :, c0:c0 + MOE_FC]
            hl = ups[fc][1] + bgu_ref[0, :, D_FF + c0:D_FF + c0 + MOE_FC]
            glu = jnp.minimum(hg, SWIGLU_LIMIT)
            lin = jnp.clip(hl, -SWIGLU_LIMIT, SWIGLU_LIMIT)
            act = (lin + 1.0) * glu * jax.nn.sigmoid(SWIGLU_ALPHA * glu)
            y = y + jnp.dot(act.astype(BF16), wdn_ref[0, c0:c0 + MOE_FC, :],
                            preferred_element_type=F32)
        y = ((y + bdn_ref[0]) * gate).astype(BF16)
        o_ref[...] += _dot_tn(sel, y)
        return carry

    lax.fori_loop(0, n_blocks, row_block, 0)


def _moe(counts, h1, xn2, rank_t, gate_t, wgu, bgu, wdn, bdn, tm):
    t = h1.shape[0]
    row = lambda width: pl.BlockSpec((tm, width), lambda i, e, c: (i, 0))
    grid_spec = pltpu.PrefetchScalarGridSpec(
        num_scalar_prefetch=1,
        grid=(t // tm, N_EXPERTS),
        in_specs=[row(D_MODEL), row(D_MODEL),
                  pl.BlockSpec((LANES, tm), lambda i, e, c: (0, i)),
                  pl.BlockSpec((LANES, tm), lambda i, e, c: (0, i)),
                  pl.BlockSpec((1, D_MODEL, 2 * D_FF), lambda i, e, c: (e, 0, 0)),
                  pl.BlockSpec((1, 1, 2 * D_FF), lambda i, e, c: (e, 0, 0)),
                  pl.BlockSpec((1, D_FF, D_MODEL), lambda i, e, c: (e, 0, 0)),
                  pl.BlockSpec((1, 1, D_MODEL), lambda i, e, c: (e, 0, 0))],
        out_specs=row(D_MODEL))
    return pl.pallas_call(
        _moe_kernel,
        grid_spec=grid_spec,
        out_shape=jax.ShapeDtypeStruct((t, D_MODEL), F32),
        compiler_params=pltpu.CompilerParams(dimension_semantics=("arbitrary", "arbitrary"),
                                             vmem_limit_bytes=VMEM_LIMIT),
        name="moe",
    )(counts, h1, xn2, rank_t, gate_t, wgu, bgu, wdn, bdn)


def _rope_tables():
    lane = jnp.arange(LANES)

    def rows(period, rot):
        half = rot // 2
        inv_freq = ROPE_THETA ** (-jnp.arange(half, dtype=F32) / half)
        jm = lane % period
        freq = jnp.where(jm < rot, inv_freq[jm % half], 0.0)
        sign = jnp.where(jm < half, -1.0, jnp.where(jm < rot, 1.0, 0.0))
        first = (jm < half).astype(F32)
        return [freq, sign, first]

    idx_rows = rows(IDX_DIM, IDX_DIM // 4)
    only_key = (lane < IDX_DIM).astype(F32)
    return jnp.stack(rows(HEAD_DIM, HEAD_DIM // 4) + idx_rows
                     + [only_key, jnp.zeros_like(only_key)]).astype(F32)


def _group_matrix():
    g = np.arange(D_GROUP) // HEAD_DIM
    return jnp.asarray((g[:, None] == g[None, :]).astype(np.float32), dtype=BF16)


def _pad_rows(w, r0, rows):
    return jnp.zeros((rows, w.shape[1]), w.dtype).at[r0:r0 + w.shape[0]].set(w)


def kernel(x, positions, norm1_w, w_in, q_norm_w, k_norm_w, rwkv_mu, rwkv_w0, rwkv_w2, rwkv_a0,
           rwkv_a2, rwkv_g2, rwkv_k_k, rwkv_k_a, rwkv_r_k, rwkv_ln_w, rwkv_ln_b, w_out, norm2_w,
           router_w, router_b, exp_w_gu, exp_b_gu, exp_w_down, exp_b_down):
    b, t, _ = x.shape
    assert b == 1 and w_in.shape[0] == 1, "single sequence, single layer"
    assert t % DSA_KB == 0 and t % DSA_QB == 0 and t % MOE_TM == 0
    x2 = x[0]
    pos_f = positions[0].astype(F32)[:, None]
    n_sel = min(TOPK_MAX, t // 4)

    w = w_in[0]
    a0 = 3 * D_GROUP
    att_cols = a0 + D_IDX + IDX_DIM + N_IDX_HEADS
    w_att, w_rw = w[:, :att_cols], w[:, att_cols:]
    w_sm = jnp.zeros((D_MODEL, LANES), F32).at[:, :IDX_DIM + N_IDX_HEADS].set(w_att[:, a0 + D_IDX:])
    w_packed = jnp.concatenate(
        [w_att[:, :a0], w_rw[:, :a0], w_att[:, a0:a0 + D_IDX], w_sm, w_rw[:, a0:]],
        axis=1).astype(BF16)
    mu = rwkv_mu[0][None, :]
    tile8 = lambda z: jnp.tile(z, N_HEADS)[None, :]
    gsum = _group_matrix()

    tm = 256
    q, k, v, rr, rk, rv, iq, sm, lora = _in_proj(
        x2, pos_f, norm1_w, w_packed, mu, tile8(q_norm_w[0]), tile8(k_norm_w[0]), gsum,
        _rope_tables(), tm)

    vec = lambda z: z.reshape(1, D_GROUP)
    rw = _rwkv(rr, rk, rv, lora, vec(rwkv_w0[0]),
               _pad_rows(rwkv_w2[0], 0, LANES), vec(rwkv_a0[0]),
               _pad_rows(rwkv_a2[0], D_DECAY_LORA, LANES),
               _pad_rows(rwkv_g2[0], D_DECAY_LORA + D_AAA_LORA, LANES),
               vec(rwkv_k_k[0]), vec(rwkv_k_a[0]), vec(rwkv_r_k[0]), vec(rwkv_ln_w[0]),
               vec(rwkv_ln_b[0]), gsum, tm)

    heads = lambda z: z.reshape(t, N_HEADS, HEAD_DIM)
    vt = jnp.concatenate([heads(v).transpose(1, 2, 0),
                          jnp.ones((N_HEADS, DSA_V_ROWS - HEAD_DIM, t), BF16)], axis=1)
    att_t = _dsa(heads(q).transpose(1, 2, 0), k, vt,
                 iq.T, sm[:, :IDX_DIM].astype(BF16),
                 sm[:, IDX_DIM:IDX_DIM + N_IDX_HEADS].T, n_sel)

    rwt = jnp.zeros((D_MODEL, LANES), F32).at[:, :N_EXPERTS].set(router_w[0])
    rb = jnp.full((1, LANES), NEG_BIG, F32).at[0, :N_EXPERTS].set(router_b[0])
    h1, xn2, gates = _out_proj(x2, att_t, rw, w_out[0].astype(BF16), norm2_w, rwt, rb, tm)

    rank_t, gate_t, cnt = _route(gates, MOE_TM)
    counts = cnt[:, 0, :N_EXPERTS].astype(jnp.int32).reshape(-1)
    out = _moe(counts, h1, xn2, rank_t, gate_t, exp_w_gu[0].astype(BF16),
               exp_b_gu[0][:, None, :], exp_w_down[0].astype(BF16), exp_b_down[0][:, None, :],
               MOE_TM)
    return out[None]
```

```python
import functools

import jax
import jax.numpy as jnp
import numpy as np
from jax import lax
from jax.experimental import pallas as pl
from jax.experimental.pallas import tpu as pltpu

F32 = jnp.float32
BF16 = jnp.bfloat16
HIGHEST = lax.Precision.HIGHEST

D_MODEL = 1024
HEAD_DIM = 64
N_HEADS = 8
D_GROUP = N_HEADS * HEAD_DIM
ROPE_THETA = 500000.0
N_IDX_HEADS = 8
IDX_DIM = 32
D_IDX = N_IDX_HEADS * IDX_DIM
TOPK_MAX = 256
D_DECAY_LORA = 32
D_AAA_LORA = 32
D_GATE_LORA = 64
RWKV_GN_EPS = 64e-5
N_EXPERTS = 32
TOP_K_EXPERTS = 4
D_FF = 1024
SWIGLU_LIMIT = 7.0
SWIGLU_ALPHA = 1.702
RMS_EPS = 1e-6

LANES = 128
VMEM_LIMIT = 56 * 1024 * 1024

NEG_BIG = -1e30
INT_MIN = -(2 ** 31)


def _dot(a, b):
    return jnp.dot(a.astype(BF16), b.astype(BF16), preferred_element_type=F32)


def _dotf(a, b):
    return jnp.dot(a, b, preferred_element_type=F32, precision=HIGHEST)


def _dot_nt(a, b):
    return lax.dot_general(a.astype(BF16), b.astype(BF16), (((1,), (1,)), ((), ())),
                           preferred_element_type=F32)


def _dot_tn(a, b):
    return lax.dot_general(a.astype(BF16), b.astype(BF16), (((0,), (0,)), ((), ())),
                           preferred_element_type=F32)


def _group_sum(z, g_ref):
    hi = z.astype(BF16)
    lo = (z - hi.astype(F32)).astype(BF16)
    g = g_ref[...]
    return (jnp.dot(hi, g, preferred_element_type=F32)
            + jnp.dot(lo, g, preferred_element_type=F32))


_C_Q, _C_K, _C_V, _C_RR, _C_RK, _C_RV = (i * D_GROUP for i in range(6))
_C_IQ = 6 * D_GROUP
_C_SM = _C_IQ + D_IDX
_C_LORA = _C_SM + LANES
D_IN_PACKED = _C_LORA + LANES
D_SHIFT = 3 * D_GROUP + LANES


def _in_proj_kernel(x_ref, pos_ref, n1w_ref, w_ref, mu_ref, qnw_ref, knw_ref, g_ref, rope_ref,
                    q_ref, k_ref, v_ref, rr_ref, rk_ref, rv_ref, iq_ref, sm_ref, lora_ref,
                    carry_ref):
    tm = x_ref.shape[0]

    @pl.when(pl.program_id(0) == 0)
    def _():
        carry_ref[...] = jnp.zeros_like(carry_ref)

    x = x_ref[...]
    xn = x * lax.rsqrt(jnp.mean(x * x, axis=-1, keepdims=True) + RMS_EPS) * n1w_ref[...]
    xb = xn.astype(BF16)
    pos = pos_ref[...]
    rope = rope_ref[...]

    def tables(frow, srow):
        ang = pos * rope[frow:frow + 1, :]
        return jnp.cos(ang), jnp.sin(ang) * rope[srow:srow + 1, :]

    def widen(z, reps):
        return jnp.concatenate([z] * reps, axis=1)

    def rotary(z, c, s, first_row, half):
        w = z.shape[1]
        first = jnp.concatenate([rope[first_row:first_row + 1, :]] * (w // LANES), axis=1) > 0.5
        partner = jnp.where(first, pltpu.roll(z, w - half, 1), pltpu.roll(z, half, 1))
        return z * c + partner * s

    def head_norm(z, w_row):
        ms = _group_sum(z * z, g_ref) * (1.0 / HEAD_DIM)
        return z * lax.rsqrt(ms + RMS_EPS) * w_row

    def proj(c0, width):
        return jnp.dot(xb, w_ref[:, c0:c0 + width], preferred_element_type=F32)

    def shift(z, c0):
        width = z.shape[1]
        row = lax.broadcasted_iota(jnp.int32, z.shape, 0)
        prev = jnp.where(row == 0, carry_ref[0:1, c0:c0 + width], pltpu.roll(z, 1, 0))
        carry_ref[0:1, c0:c0 + width] = z[tm - 1:tm, :]
        return z + (prev - z) * mu_ref[:, c0:c0 + width]

    cq, sq = (widen(z, D_GROUP // LANES) for z in tables(0, 1))
    q = rotary(head_norm(proj(_C_Q, D_GROUP), qnw_ref[...]), cq, sq, 2, HEAD_DIM // 8)
    q_ref[...] = q.astype(BF16)
    k = rotary(head_norm(proj(_C_K, D_GROUP), knw_ref[...]), cq, sq, 2, HEAD_DIM // 8)
    k_ref[...] = k.astype(BF16)
    v_ref[...] = proj(_C_V, D_GROUP).astype(BF16)

    rr_ref[...] = shift(proj(_C_RR, D_GROUP), 0)
    rk_ref[...] = shift(proj(_C_RK, D_GROUP), D_GROUP)
    rv_ref[...] = shift(proj(_C_RV, D_GROUP), 2 * D_GROUP)
    lora_ref[...] = shift(proj(_C_LORA, LANES), 3 * D_GROUP)

    ci, si = tables(3, 4)
    iq_ref[...] = rotary(proj(_C_IQ, D_IDX), widen(ci, D_IDX // LANES), widen(si, D_IDX // LANES),
                         5, IDX_DIM // 8)
    only_key = rope[6:7, :]
    sm_ref[...] = rotary(proj(_C_SM, LANES), ci * only_key + (1.0 - only_key), si * only_key,
                         5, IDX_DIM // 8)


def _in_proj(x2, pos_f, n1w, w_packed, mu_packed, qnw, knw, gmat, rope, tm):
    t = x2.shape[0]
    full = lambda shape: pl.BlockSpec(shape, lambda i: (0,) * len(shape))
    row = lambda width: pl.BlockSpec((tm, width), lambda i: (i, 0))
    out_shapes = (
        jax.ShapeDtypeStruct((t, D_GROUP), BF16),
        jax.ShapeDtypeStruct((t, D_GROUP), BF16),
        jax.ShapeDtypeStruct((t, D_GROUP), BF16),
        jax.ShapeDtypeStruct((t, D_GROUP), F32),
        jax.ShapeDtypeStruct((t, D_GROUP), F32),
        jax.ShapeDtypeStruct((t, D_GROUP), F32),
        jax.ShapeDtypeStruct((t, D_IDX), F32),
        jax.ShapeDtypeStruct((t, LANES), F32),
        jax.ShapeDtypeStruct((t, LANES), F32),
    )
    return pl.pallas_call(
        _in_proj_kernel,
        grid=(t // tm,),
        in_specs=[row(D_MODEL), row(1), full((1, D_MODEL)), full((D_MODEL, D_IN_PACKED)),
                  full((1, D_SHIFT)), full((1, D_GROUP)), full((1, D_GROUP)),
                  full((D_GROUP, D_GROUP)), full((8, LANES))],
        out_specs=[row(D_GROUP)] * 6 + [row(D_IDX), row(LANES), row(LANES)],
        out_shape=out_shapes,
        scratch_shapes=[pltpu.VMEM((8, D_SHIFT), F32)],
        compiler_params=pltpu.CompilerParams(dimension_semantics=("arbitrary",),
                                             vmem_limit_bytes=VMEM_LIMIT),
        name="in_proj",
    )(x2, pos_f, n1w, w_packed, mu_packed, qnw, knw, gmat, rope)


RWKV_CHUNK = 64
RWKV_UNROLL = 2
PAIR = 2 * HEAD_DIM


def _rwkv_kernel(r_ref, k_ref, v_ref, lora_ref, w0_ref, w2_ref, a0_ref, a2_ref, g2_ref,
                 kk_ref, ka_ref, rk_ref, lnw_ref, lnb_ref, g_ref,
                 o_ref,
                 s_ref, ld_s, r_s, k2_s, b_s, kk_s, y_s):
    tm = r_ref.shape[0]
    n_chunks = tm // RWKV_CHUNK
    n_pairs = D_GROUP // PAIR
    c = RWKV_CHUNK

    @pl.when(pl.program_id(0) == 0)
    def _():
        s_ref[...] = jnp.zeros_like(s_ref)

    lora = lora_ref[...]
    r = r_ref[...]
    k = k_ref[...]
    v = v_ref[...]
    zarg = w0_ref[...] + _dotf(jnp.tanh(lora), w2_ref[...])
    sp = jnp.maximum(-zarg, 0.0) + jnp.log1p(jnp.exp(-jnp.abs(zarg)))
    ld_s[...] = -jnp.exp(-sp - 0.5)
    a = jax.nn.sigmoid(a0_ref[...] + _dotf(lora, a2_ref[...]))
    g = _dotf(jax.nn.sigmoid(lora), g2_ref[...])
    kk = k * kk_ref[...]
    kk = kk * lax.rsqrt(jnp.maximum(_group_sum(kk * kk, g_ref), 1e-24))
    k2 = k * (1.0 + (a - 1.0) * ka_ref[...])
    bonus = _group_sum(r * k2 * rk_ref[...], g_ref) * v
    r_s[...] = r
    k2_s[...] = k2
    kk_s[...] = kk
    b_s[...] = kk * a

    row = lax.broadcasted_iota(jnp.int32, (2 * c, 2 * c), 0)
    col = lax.broadcasted_iota(jnp.int32, (2 * c, 2 * c), 1)
    same_head = (row >= c) == (col >= c)
    strict = same_head & (col < row)
    incl = same_head & (col <= row)
    eye = (row == col).astype(F32)
    tri = (lax.broadcasted_iota(jnp.int32, (c, c), 1)
           <= lax.broadcasted_iota(jnp.int32, (c, c), 0)).astype(F32)
    lane = lax.broadcasted_iota(jnp.int32, (c, PAIR), 1)
    head0 = lane < HEAD_DIM

    def stack(z):
        return jnp.concatenate([jnp.where(head0, z, 0.0), jnp.where(head0, 0.0, z)], axis=0)

    def chunk_body(ci, carry):
        c2 = 2 * c
        units = [(cc, p) for cc in range(RWKV_UNROLL) for p in range(n_pairs)]
        uid = range(len(units))
        rows = [pl.multiple_of((ci * RWKV_UNROLL + cc) * c, c) for cc in range(RWKV_UNROLL)]
        ld_all = [ld_s[pl.ds(r0, c), :] for r0 in rows]
        cum_all = [_dotf(tri, ld) for ld in ld_all]
        rt, kt, bt, kp, vs, lhs, rhs, gam_end = [], [], [], [], [], [], [], []
        for cc, p in units:
            r0 = rows[cc]
            cols = slice(p * PAIR, (p + 1) * PAIR)
            ld = ld_all[cc][:, cols]
            cum = cum_all[cc][:, cols]
            gam = jnp.exp(cum)
            inv = jnp.exp(-cum)
            gam_prev = jnp.exp(cum - ld)
            gam_end.append(gam[c - 1:c, :])
            rt.append(stack(r_s[pl.ds(r0, c), cols] * gam))
            kt_raw = k2_s[pl.ds(r0, c), cols] * inv
            bt_raw = b_s[pl.ds(r0, c), cols] * inv
            kt.append(stack(kt_raw))
            bt.append(stack(bt_raw))
            kp.append(stack(kk_s[pl.ds(r0, c), cols] * gam_prev))
            vs.append(stack(v_ref[pl.ds(r0, c), cols]))
            lhs.append(jnp.concatenate([kp[-1], rt[-1]], axis=0))
            rhs.append(jnp.concatenate([bt_raw, bt_raw, kt_raw, kt_raw], axis=0))
        aa = [_dot_nt(lhs[u], rhs[u]) for u in uid]
        a_kk = [jnp.where(strict, aa[u][:c2, c2:], 0.0) for u in uid]
        a_rb = [jnp.where(incl, aa[u][c2:, :c2], 0.0) for u in uid]
        a_rk = [jnp.where(incl, aa[u][c2:, c2:], 0.0) for u in uid]
        n = [-jnp.where(strict, aa[u][:c2, :c2], 0.0) for u in uid]
        prod = [eye + n[u] for u in uid]
        n = [_dot(n[u], n[u]) for u in uid]
        for _ in range(int(np.log2(c)) - 2):
            both = [_dot(jnp.concatenate([n[u], prod[u]], axis=0), n[u]) for u in uid]
            prod = [prod[u] + both[u][c2:] for u in uid]
            n = [both[u][:c2] for u in uid]
        av = [_dot(jnp.concatenate([a_kk[u], a_rk[u]], axis=0), vs[u]) for u in uid]
        tinv = [prod[u] + _dot(prod[u], n[u]) for u in uid]
        wu = [_dot(tinv[u], jnp.concatenate([kp[u], av[u][:c2]], axis=1)) for u in uid]
        rb = [_dot(a_rb[u], wu[u]) for u in uid]
        tn = [_dot_tn(jnp.concatenate([wu[u], vs[u]], axis=1),
                      jnp.concatenate([bt[u], kt[u]], axis=1)) for u in uid]
        r2 = [rt[u] - rb[u][:, :PAIR] for u in uid]
        y2 = [av[u][c2:] - rb[u][:, PAIR:] for u in uid]
        s_mix = [(eye - tn[u][:PAIR, :PAIR]) * gam_end[u] for u in uid]
        s_add = [(tn[u][2 * PAIR:, PAIR:] - tn[u][PAIR:2 * PAIR, :PAIR]) * gam_end[u] for u in uid]
        state = [s_ref[p] for p in range(n_pairs)]
        for cc in range(RWKV_UNROLL):
            us = [cc * n_pairs + p for p in range(n_pairs)]
            ys = [_dot_nt(r2[u], state[p]) + y2[u] for p, u in enumerate(us)]
            state = [_dot(state[p], s_mix[u]) + s_add[u] for p, u in enumerate(us)]
            for p in range(n_pairs):
                y_s[pl.ds(rows[cc], c), p * PAIR:(p + 1) * PAIR] = ys[p][:c] + ys[p][c:]
        for p in range(n_pairs):
            s_ref[p] = state[p]
        return carry

    lax.fori_loop(0, n_chunks // RWKV_UNROLL, chunk_body, 0)

    y = y_s[...]
    mean = _group_sum(y, g_ref) * (1.0 / HEAD_DIM)
    yc = y - mean
    var = _group_sum(yc * yc, g_ref) * (1.0 / HEAD_DIM)
    yn = yc * lax.rsqrt(var + RWKV_GN_EPS) * lnw_ref[...] + lnb_ref[...]
    o_ref[...] = (yn + bonus) * g


def _rwkv(rr, rk, rv, lora, w0, w2p, a0, a2p, g2p, k_k, k_a, r_k, ln_w, ln_b, gsum, tm):
    t = rr.shape[0]
    full = lambda shape: pl.BlockSpec(shape, lambda i: (0,) * len(shape))
    row = lambda width: pl.BlockSpec((tm, width), lambda i: (i, 0))
    vec = full((1, D_GROUP))
    big = pltpu.VMEM((tm, D_GROUP), F32)
    return pl.pallas_call(
        _rwkv_kernel,
        grid=(t // tm,),
        in_specs=[row(D_GROUP), row(D_GROUP), row(D_GROUP), row(LANES),
                  vec, full((LANES, D_GROUP)), vec, full((LANES, D_GROUP)),
                  full((LANES, D_GROUP)), vec, vec, vec, vec, vec, full((D_GROUP, D_GROUP))],
        out_specs=row(D_GROUP),
        out_shape=jax.ShapeDtypeStruct((t, D_GROUP), F32),
        scratch_shapes=[pltpu.VMEM((D_GROUP // PAIR, PAIR, PAIR), F32),
                        big, big, big, big, big, big],
        compiler_params=pltpu.CompilerParams(dimension_semantics=("arbitrary",),
                                             vmem_limit_bytes=VMEM_LIMIT),
        name="rwkv",
    )(rr, rk, rv, lora, w0, w2p, a0, a2p, g2p, k_k, k_a, r_k, ln_w, ln_b, gsum)


DSA_QB = 256
DSA_KB = 512
M_INIT = -5e29
DSA_HEAD_GROUP = 4
DSA_ACC_ROWS = 32
DSA_GROUPS = 256
DSA_TOP = 10
DSA_V_ROWS = HEAD_DIM + 16


def _dsa_kernel(qi_ref, kj_ref, qt_ref, k_ref, vt_ref, iqt_ref, ik_ref, iwt_ref, *rest, n_sel,
                n_cast):
    cast_in, o_ref, cast_out = rest[:n_cast], rest[n_cast], rest[n_cast + 1:2 * n_cast + 1]
    key_s, thr_s, need_s, over_s, m_s, acc_s, bdq_s, top_s = rest[2 * n_cast + 1:]
    for src, dst in zip(cast_in, cast_out):
        dst[...] = src[...].astype(BF16)
    qb = qt_ref.shape[2]
    kb = k_ref.shape[0]
    step = pl.program_id(0)
    qi = qi_ref[step]
    kj = kj_ref[step]
    q0 = qi * qb
    j_last = (q0 + qb - 1) // kb
    n_kc = j_last + 1
    idx_scale = float((IDX_DIM * N_IDX_HEADS) ** -0.5)

    def causal(jblk):
        s_pos = jblk * kb + lax.broadcasted_iota(jnp.int32, (kb, qb), 0)
        t_pos = q0 + lax.broadcasted_iota(jnp.int32, (kb, qb), 1)
        return s_pos <= t_pos

    @pl.when(kj == 0)
    def _():
        iqt = iqt_ref[...].astype(BF16)
        iwt = iwt_ref[...] * idx_scale

        def score_chunk(kc, diagonal):
            ik = ik_ref[pl.ds(pl.multiple_of(kc * kb, kb), kb), :]
            score = jnp.zeros((kb, qb), F32)
            for h in range(N_IDX_HEADS):
                d = jnp.dot(ik, iqt[h * IDX_DIM:(h + 1) * IDX_DIM, :],
                            preferred_element_type=F32)
                score = score + jnp.maximum(d, 0.0) * iwt[h:h + 1, :]
            bits = pltpu.bitcast(score, jnp.int32)
            keys = jnp.where(bits < 0, bits ^ jnp.int32(0x7FFFFFFF), bits)
            if diagonal:
                keys = jnp.where(causal(kc), keys, jnp.int32(INT_MIN))
            key_s[kc] = keys
            xs = [keys[r * DSA_GROUPS:(r + 1) * DSA_GROUPS] for r in range(kb // DSA_GROUPS)]
            for lvl in range(DSA_TOP):
                s = top_s[lvl]
                for r in range(len(xs)):
                    s, xs[r] = jnp.maximum(s, xs[r]), jnp.minimum(s, xs[r])
                top_s[lvl] = s

        top_s[...] = jnp.full_like(top_s, INT_MIN)

        def full_chunk(kc, carry):
            score_chunk(kc, False)
            return carry

        lax.fori_loop(0, j_last, full_chunk, 0)
        score_chunk(j_last, True)

        def select(count_keys):
            def bit_step(b, cur):
                bit = lax.shift_left(jnp.int32(1), jnp.int32(31) - b)
                cand = (cur | bit) ^ jnp.int32(INT_MIN)
                cnt = jnp.sum(count_keys(cand), axis=0, keepdims=True)
                return jnp.where(cnt >= float(n_sel), cur | bit, cur)

            cur = lax.fori_loop(0, 32, bit_step, jnp.zeros((1, qb), jnp.int32))
            return jnp.maximum(cur ^ jnp.int32(INT_MIN), jnp.int32(INT_MIN + 1))

        def part_count(keys, cand):
            ind = jnp.where(keys >= cand, 1.0, 0.0)
            return jnp.sum(ind.reshape(-1, DSA_ACC_ROWS, qb), axis=0)

        def count_top(cand):
            return lax.fori_loop(0, DSA_TOP, lambda lvl, a: a + part_count(top_s[lvl], cand),
                                 jnp.zeros((DSA_ACC_ROWS, qb), F32))

        def count_all(cand):
            return lax.fori_loop(0, n_kc, lambda kc, a: a + part_count(key_s[kc], cand),
                                 jnp.zeros((DSA_ACC_ROWS, qb), F32))

        def settle(count_keys):
            thr = select(count_keys)
            thr_s[...] = thr
            above = jnp.sum(count_keys(thr + 1), axis=0, keepdims=True)
            upto = jnp.sum(count_keys(thr), axis=0, keepdims=True)
            need_s[...] = float(n_sel) - above
            over_s[...] = upto - float(n_sel)
            return thr

        thr = settle(count_top)
        hidden = jnp.where(top_s[DSA_TOP - 1] >= thr, 1.0, 0.0)

        @pl.when(jnp.max(hidden) > 0.0)
        def _():
            settle(count_all)

        @pl.when(jnp.max(over_s[...]) > 0.0)
        def _():
            thr = thr_s[...]
            need = need_s[...]
            sub = LANES
            earlier = jnp.where(lax.broadcasted_iota(jnp.int32, (sub, sub), 1)
                                < lax.broadcasted_iota(jnp.int32, (sub, sub), 0), 1.0, 0.0).astype(BF16)

            def drop_surplus(kc, seen):
                keys = key_s[kc]
                tie = keys == thr
                tie_f = jnp.where(tie, 1.0, 0.0)
                ranks = []
                for r in range(kb // sub):
                    part = tie_f[r * sub:(r + 1) * sub]
                    ranks.append(seen + jnp.dot(earlier, part.astype(BF16),
                                                preferred_element_type=F32))
                    seen = seen + jnp.sum(part, axis=0, keepdims=True)
                rank = jnp.concatenate(ranks, axis=0)
                key_s[kc] = jnp.where(tie & (rank >= need), thr - 1, keys)
                return seen

            lax.fori_loop(0, n_kc, drop_surplus, jnp.zeros((1, qb), F32))

        m_s[...] = jnp.full_like(m_s, M_INIT)
        acc_s[...] = jnp.zeros_like(acc_s)
        bdq_s[...] = jnp.zeros_like(bdq_s)
        for h in range(N_HEADS):
            g, hh = divmod(h, DSA_HEAD_GROUP)
            bdq_s[g, hh * HEAD_DIM:(hh + 1) * HEAD_DIM, hh * qb:(hh + 1) * qb] = (
                qt_ref[h] * jnp.asarray(HEAD_DIM ** -0.5, BF16))

    mask = key_s[kj] >= thr_s[...]
    hg = DSA_HEAD_GROUP
    st_all = [jnp.dot(k_ref[:, g * hg * HEAD_DIM:(g + 1) * hg * HEAD_DIM], bdq_s[g],
                      preferred_element_type=F32) for g in range(N_HEADS // hg)]
    for g in range(N_HEADS // hg):
        st_g = st_all[g]
        for hh in range(hg):
            h = g * hg + hh
            st = jnp.where(mask, st_g[:, hh * qb:(hh + 1) * qb], NEG_BIG)
            m_old = m_s[h]
            part = jnp.max(st.reshape(kb // DSA_ACC_ROWS, DSA_ACC_ROWS, qb), axis=0)
            m_new = jnp.maximum(m_old, jnp.max(part, axis=0, keepdims=True))
            p = jnp.exp((st - m_new).astype(BF16))
            acc_s[h] = jnp.exp(m_old - m_new) * acc_s[h] + jnp.dot(
                vt_ref[h], p, preferred_element_type=F32)
            m_s[h] = m_new

    @pl.when(kj == j_last)
    def _():
        for h in range(N_HEADS):
            acc = acc_s[h]
            o_ref[h] = acc[:HEAD_DIM] / acc[HEAD_DIM:HEAD_DIM + 1]


def _dsa(qt, k, vt, iqt, ik, iwt, n_sel, to_cast=()):
    t = ik.shape[0]
    qb, kb = DSA_QB, DSA_KB
    assert n_sel <= DSA_GROUPS and kb % DSA_GROUPS == 0
    nq, nk = t // qb, t // kb
    pairs = [(i, j) for i in range(nq) for j in range((i * qb + qb - 1) // kb + 1)]
    qi = jnp.asarray(np.array([p[0] for p in pairs], np.int32))
    kj = jnp.asarray(np.array([p[1] for p in pairs], np.int32))
    n_slices = 1 << (len(pairs).bit_length() - 1)
    slabs = [w.reshape(n_slices, -1, w.shape[-1]) for w in to_cast]
    cast_specs = [pl.BlockSpec((1,) + w.shape[1:],
                               lambda s, qi, kj: (jnp.minimum(s, n_slices - 1), 0, 0))
                  for w in slabs]
    grid_spec = pltpu.PrefetchScalarGridSpec(
        num_scalar_prefetch=2,
        grid=(len(pairs),),
        in_specs=[pl.BlockSpec((N_HEADS, HEAD_DIM, qb), lambda s, qi, kj: (0, 0, qi[s])),
                  pl.BlockSpec((kb, D_GROUP), lambda s, qi, kj: (kj[s], 0)),
                  pl.BlockSpec((N_HEADS, DSA_V_ROWS, kb), lambda s, qi, kj: (0, 0, kj[s])),
                  pl.BlockSpec((D_IDX, qb), lambda s, qi, kj: (0, qi[s])),
                  pl.BlockSpec((t, IDX_DIM), lambda s, qi, kj: (0, 0)),
                  pl.BlockSpec((N_IDX_HEADS, qb), lambda s, qi, kj: (0, qi[s]))] + cast_specs,
        out_specs=[pl.BlockSpec((N_HEADS, HEAD_DIM, qb), lambda s, qi, kj: (0, 0, qi[s]))]
        + cast_specs,
        scratch_shapes=[pltpu.VMEM((nk, kb, qb), jnp.int32),
                        pltpu.VMEM((1, qb), jnp.int32),
                        pltpu.VMEM((1, qb), F32),
                        pltpu.VMEM((1, qb), F32),
                        pltpu.VMEM((N_HEADS, 1, qb), F32),
                        pltpu.VMEM((N_HEADS, DSA_V_ROWS, qb), F32),
                        pltpu.VMEM((N_HEADS // DSA_HEAD_GROUP, DSA_HEAD_GROUP * HEAD_DIM,
                                    DSA_HEAD_GROUP * qb), BF16),
                        pltpu.VMEM((DSA_TOP, DSA_GROUPS, qb), jnp.int32)])
    outs = pl.pallas_call(
        functools.partial(_dsa_kernel, n_sel=n_sel, n_cast=len(slabs)),
        grid_spec=grid_spec,
        out_shape=[jax.ShapeDtypeStruct((N_HEADS, HEAD_DIM, t), F32)]
        + [jax.ShapeDtypeStruct(w.shape, BF16) for w in slabs],
        compiler_params=pltpu.CompilerParams(dimension_semantics=("arbitrary",),
                                             vmem_limit_bytes=VMEM_LIMIT),
        name="dsa",
    )(qi, kj, qt, k, vt, iqt, ik, iwt, *slabs)
    return outs[0], [o.reshape(w.shape) for o, w in zip(outs[1:], to_cast)]


def _out_proj_kernel(x_ref, att_ref, rw_ref, wo_ref, n2w_ref, rwt_ref, rb_ref,
                     h_ref, xn_ref, gate_ref):
    tm = x_ref.shape[0]
    att = att_ref[...].reshape(D_GROUP, tm).T
    mix = jnp.concatenate([att, rw_ref[...]], axis=1)
    acc = x_ref[...] + _dot(mix, wo_ref[...])
    h_ref[...] = acc
    xn = acc * lax.rsqrt(jnp.mean(acc * acc, axis=-1, keepdims=True) + RMS_EPS) * n2w_ref[...]
    xn_ref[...] = xn.astype(BF16)
    logits = _dotf(xn, rwt_ref[...]) + rb_ref[...]
    lane = lax.broadcasted_iota(jnp.int32, logits.shape, 1)
    work = logits
    vals, hots = [], []
    for _ in range(TOP_K_EXPERTS):
        m = jnp.max(work, axis=1, keepdims=True)
        idx = jnp.min(jnp.where(work == m, lane, LANES), axis=1, keepdims=True)
        hot = lane == idx
        vals.append(m)
        hots.append(hot)
        work = jnp.where(hot, -jnp.inf, work)
    es = [jnp.exp(vv - vals[0]) for vv in vals]
    denom = es[0] + es[1] + es[2] + es[3]
    gates = jnp.zeros_like(logits)
    for e, hot in zip(es, hots):
        gates = gates + jnp.where(hot, e / denom, 0.0)
    gate_ref[...] = gates


def _out_proj(x2, att_t, rw, wo, n2w, rwt, rb, tm):
    t = x2.shape[0]
    full = lambda shape: pl.BlockSpec(shape, lambda i: (0,) * len(shape))
    row = lambda width: pl.BlockSpec((tm, width), lambda i: (i, 0))
    return pl.pallas_call(
        _out_proj_kernel,
        grid=(t // tm,),
        in_specs=[row(D_MODEL), pl.BlockSpec((N_HEADS, HEAD_DIM, tm), lambda i: (0, 0, i)),
                  row(D_GROUP), full((2 * D_GROUP, D_MODEL)), full((1, D_MODEL)),
                  full((D_MODEL, LANES)), full((1, LANES))],
        out_specs=[row(D_MODEL), row(D_MODEL), row(LANES)],
        out_shape=(jax.ShapeDtypeStruct((t, D_MODEL), F32),
                   jax.ShapeDtypeStruct((t, D_MODEL), BF16),
                   jax.ShapeDtypeStruct((t, LANES), F32)),
        compiler_params=pltpu.CompilerParams(dimension_semantics=("arbitrary",),
                                             vmem_limit_bytes=VMEM_LIMIT),
        name="out_proj",
    )(x2, att_t, rw, wo, n2w, rwt, rb)


MOE_TM = 1024
MOE_RB = 144
MOE_FC = 512


def _route_kernel(gate_ref, rankt_ref, gatet_ref, cnt_ref):
    tm = gate_ref.shape[0]
    gates = gate_ref[...]
    hot = gates > 0.0
    ind = jnp.where(hot, 1.0, 0.0)
    before = (lax.broadcasted_iota(jnp.int32, (tm, tm), 1)
              < lax.broadcasted_iota(jnp.int32, (tm, tm), 0))
    rank = jnp.dot(jnp.where(before, 1.0, 0.0).astype(BF16), ind.astype(BF16),
                   preferred_element_type=F32)
    rankt_ref[...] = jnp.where(hot, rank, -1.0).T
    gatet_ref[...] = gates.T
    cnt = jnp.sum(ind, axis=0, keepdims=True)
    cnt_ref[...] = jnp.broadcast_to(cnt[None], cnt_ref.shape)


def _route(gates, tm):
    t = gates.shape[0]
    return pl.pallas_call(
        _route_kernel,
        grid=(t // tm,),
        in_specs=[pl.BlockSpec((tm, LANES), lambda i: (i, 0))],
        out_specs=[pl.BlockSpec((LANES, tm), lambda i: (0, i)),
                   pl.BlockSpec((LANES, tm), lambda i: (0, i)),
                   pl.BlockSpec((1, 8, LANES), lambda i: (i, 0, 0))],
        out_shape=(jax.ShapeDtypeStruct((LANES, t), F32),
                   jax.ShapeDtypeStruct((LANES, t), F32),
                   jax.ShapeDtypeStruct((t // tm, 8, LANES), F32)),
        compiler_params=pltpu.CompilerParams(dimension_semantics=("arbitrary",),
                                             vmem_limit_bytes=VMEM_LIMIT),
        name="route",
    )(gates)


def _moe_kernel(cnt_ref, h_ref, xn_ref, rankt_ref, gatet_ref, wgu_ref, bgu_ref,
                wdn_ref, bdn_ref, o_ref):
    i = pl.program_id(0)
    e = pl.program_id(1)
    tm = h_ref.shape[0]

    @pl.when(e == 0)
    def _():
        o_ref[...] = h_ref[...]

    n_rows = cnt_ref[i * N_EXPERTS + e]
    n_blocks = (n_rows + MOE_RB - 1) // MOE_RB
    rank_row = rankt_ref[pl.ds(e, 1), :]
    gate_row = gatet_ref[pl.ds(e, 1), :]

    def row_block(bi, carry):
        r0 = (bi * MOE_RB).astype(F32)
        rowid = lax.broadcasted_iota(jnp.int32, (MOE_RB, tm), 0).astype(F32) + r0
        hit = rank_row == rowid
        sel = jnp.where(hit, 1.0, 0.0).astype(BF16)
        gate = jnp.sum(jnp.where(hit, gate_row, 0.0), axis=1, keepdims=True)
        xb = jnp.dot(sel, xn_ref[...], preferred_element_type=F32).astype(BF16)
        y = jnp.zeros((MOE_RB, D_MODEL), F32)
        ups = []
        for fc in range(D_FF // MOE_FC):
            c0 = fc * MOE_FC
            ups.append((jnp.dot(xb, wgu_ref[0, :, c0:c0 + MOE_FC], preferred_element_type=F32),
                        jnp.dot(xb, wgu_ref[0, :, D_FF + c0:D_FF + c0 + MOE_FC],
                                preferred_element_type=F32)))
        for fc in range(D_FF // MOE_FC):
            c0 = fc * MOE_FC
            hg = ups[fc][0] + bgu_ref[0, :, c0:c0 + MOE_FC]
            hl = ups[fc][1] + bgu_ref[0, :, D_FF + c0:D_FF + c0 + MOE_FC]
            glu = jnp.minimum(hg, SWIGLU_LIMIT)
            lin = jnp.clip(hl, -SWIGLU_LIMIT, SWIGLU_LIMIT)
            act = (lin + 1.0) * glu * jax.nn.sigmoid(SWIGLU_ALPHA * glu)
            y = y + jnp.dot(act.astype(BF16), wdn_ref[0, c0:c0 + MOE_FC, :],
                            preferred_element_type=F32)
        y = ((y + bdn_ref[0]) * gate).astype(BF16)
        o_ref[...] += _dot_tn(sel, y)
        return carry

    lax.fori_loop(0, n_blocks, row_block, 0)


def _moe(counts, h1, xn2, rank_t, gate_t, wgu, bgu, wdn, bdn, tm):
    t = h1.shape[0]
    row = lambda width: pl.BlockSpec((tm, width), lambda i, e, c: (i, 0))
    grid_spec = pltpu.PrefetchScalarGridSpec(
        num_scalar_prefetch=1,
        grid=(t // tm, N_EXPERTS),
        in_specs=[row(D_MODEL), row(D_MODEL),
                  pl.BlockSpec((LANES, tm), lambda i, e, c: (0, i)),
                  pl.BlockSpec((LANES, tm), lambda i, e, c: (0, i)),
                  pl.BlockSpec((1, D_MODEL, 2 * D_FF), lambda i, e, c: (e, 0, 0)),
                  pl.BlockSpec((1, 1, 2 * D_FF), lambda i, e, c: (e, 0, 0)),
                  pl.BlockSpec((1, D_FF, D_MODEL), lambda i, e, c: (e, 0, 0)),
                  pl.BlockSpec((1, 1, D_MODEL), lambda i, e, c: (e, 0, 0))],
        out_specs=row(D_MODEL))
    return pl.pallas_call(
        _moe_kernel,
        grid_spec=grid_spec,
        out_shape=jax.ShapeDtypeStruct((t, D_MODEL), F32),
        compiler_params=pltpu.CompilerParams(dimension_semantics=("arbitrary", "arbitrary"),
                                             vmem_limit_bytes=VMEM_LIMIT),
        name="moe",
    )(counts, h1, xn2, rank_t, gate_t, wgu, bgu, wdn, bdn)


def _rope_tables():
    lane = jnp.arange(LANES)

    def rows(period, rot):
        half = rot // 2
        inv_freq = ROPE_THETA ** (-jnp.arange(half, dtype=F32) / half)
        jm = lane % period
        freq = jnp.where(jm < rot, inv_freq[jm % half], 0.0)
        sign = jnp.where(jm < half, -1.0, jnp.where(jm < rot, 1.0, 0.0))
        first = (jm < half).astype(F32)
        return [freq, sign, first]

    idx_rows = rows(IDX_DIM, IDX_DIM // 4)
    only_key = (lane < IDX_DIM).astype(F32)
    return jnp.stack(rows(HEAD_DIM, HEAD_DIM // 4) + idx_rows
                     + [only_key, jnp.zeros_like(only_key)]).astype(F32)


def _group_matrix():
    g = np.arange(D_GROUP) // HEAD_DIM
    return jnp.asarray((g[:, None] == g[None, :]).astype(np.float32), dtype=BF16)


def _pad_rows(w, r0, rows):
    return jnp.zeros((rows, w.shape[1]), w.dtype).at[r0:r0 + w.shape[0]].set(w)


def kernel(x, positions, norm1_w, w_in, q_norm_w, k_norm_w, rwkv_mu, rwkv_w0, rwkv_w2, rwkv_a0,
           rwkv_a2, rwkv_g2, rwkv_k_k, rwkv_k_a, rwkv_r_k, rwkv_ln_w, rwkv_ln_b, w_out, norm2_w,
           router_w, router_b, exp_w_gu, exp_b_gu, exp_w_down, exp_b_down):
    b, t, _ = x.shape
    assert b == 1 and w_in.shape[0] == 1, "single sequence, single layer"
    assert t % DSA_KB == 0 and t % DSA_QB == 0 and t % MOE_TM == 0
    x2 = x[0]
    pos_f = positions[0].astype(F32)[:, None]
    n_sel = min(TOPK_MAX, t // 4)

    w = w_in[0]
    a0 = 3 * D_GROUP
    att_cols = a0 + D_IDX + IDX_DIM + N_IDX_HEADS
    w_att, w_rw = w[:, :att_cols], w[:, att_cols:]
    w_sm = jnp.zeros((D_MODEL, LANES), F32).at[:, :IDX_DIM + N_IDX_HEADS].set(w_att[:, a0 + D_IDX:])
    w_packed = jnp.concatenate(
        [w_att[:, :a0], w_rw[:, :a0], w_att[:, a0:a0 + D_IDX], w_sm, w_rw[:, a0:]],
        axis=1).astype(BF16)
    mu = rwkv_mu[0][None, :]
    tile8 = lambda z: jnp.tile(z, N_HEADS)[None, :]
    gsum = _group_matrix()

    tm = 256
    q, k, v, rr, rk, rv, iq, sm, lora = _in_proj(
        x2, pos_f, norm1_w, w_packed, mu, tile8(q_norm_w[0]), tile8(k_norm_w[0]), gsum,
        _rope_tables(), tm)

    vec = lambda z: z.reshape(1, D_GROUP)
    rw = _rwkv(rr, rk, rv, lora, vec(rwkv_w0[0]),
               _pad_rows(rwkv_w2[0], 0, LANES), vec(rwkv_a0[0]),
               _pad_rows(rwkv_a2[0], D_DECAY_LORA, LANES),
               _pad_rows(rwkv_g2[0], D_DECAY_LORA + D_AAA_LORA, LANES),
               vec(rwkv_k_k[0]), vec(rwkv_k_a[0]), vec(rwkv_r_k[0]), vec(rwkv_ln_w[0]),
               vec(rwkv_ln_b[0]), gsum, tm)

    heads = lambda z: z.reshape(t, N_HEADS, HEAD_DIM)
    vt = jnp.concatenate([heads(v).transpose(1, 2, 0),
                          jnp.ones((N_HEADS, DSA_V_ROWS - HEAD_DIM, t), BF16)], axis=1)
    att_t, (w_gu_bf, w_dn_bf) = _dsa(
        heads(q).transpose(1, 2, 0), k, vt, iq.T, sm[:, :IDX_DIM].astype(BF16),
        sm[:, IDX_DIM:IDX_DIM + N_IDX_HEADS].T, n_sel, to_cast=(exp_w_gu[0], exp_w_down[0]))

    rwt = jnp.zeros((D_MODEL, LANES), F32).at[:, :N_EXPERTS].set(router_w[0])
    rb = jnp.full((1, LANES), NEG_BIG, F32).at[0, :N_EXPERTS].set(router_b[0])
    h1, xn2, gates = _out_proj(x2, att_t, rw, w_out[0].astype(BF16), norm2_w, rwt, rb, tm)

    rank_t, gate_t, cnt = _route(gates, MOE_TM)
    counts = cnt[:, 0, :N_EXPERTS].astype(jnp.int32).reshape(-1)
    out = _moe(counts, h1, xn2, rank_t, gate_t, w_gu_bf, exp_b_gu[0][:, None, :], w_dn_bf,
               exp_b_down[0][:, None, :], MOE_TM)
    return out[None]
```

```python
import functools

import jax
import jax.numpy as jnp
import numpy as np
from jax import lax
from jax.experimental import pallas as pl
from jax.experimental.pallas import tpu as pltpu

F32 = jnp.float32
BF16 = jnp.bfloat16
HIGHEST = lax.Precision.HIGHEST

D_MODEL = 1024
HEAD_DIM = 64
N_HEADS = 8
D_GROUP = N_HEADS * HEAD_DIM
ROPE_THETA = 500000.0
N_IDX_HEADS = 8
IDX_DIM = 32
D_IDX = N_IDX_HEADS * IDX_DIM
TOPK_MAX = 256
D_DECAY_LORA = 32
D_AAA_LORA = 32
D_GATE_LORA = 64
RWKV_GN_EPS = 64e-5
N_EXPERTS = 32
TOP_K_EXPERTS = 4
D_FF = 1024
SWIGLU_LIMIT = 7.0
SWIGLU_ALPHA = 1.702
RMS_EPS = 1e-6

LANES = 128
VMEM_LIMIT = 56 * 1024 * 1024

NEG_BIG = -1e30
INT_MIN = -(2 ** 31)


def _dot(a, b):
    return jnp.dot(a.astype(BF16), b.astype(BF16), preferred_element_type=F32)


def _dotf(a, b):
    return jnp.dot(a, b, preferred_element_type=F32, precision=HIGHEST)


def _dot_nt(a, b):
    return lax.dot_general(a.astype(BF16), b.astype(BF16), (((1,), (1,)), ((), ())),
                           preferred_element_type=F32)


def _dot_tn(a, b):
    return lax.dot_general(a.astype(BF16), b.astype(BF16), (((0,), (0,)), ((), ())),
                           preferred_element_type=F32)


def _group_sum(z, g_ref):
    hi = z.astype(BF16)
    lo = (z - hi.astype(F32)).astype(BF16)
    g = g_ref[...]
    return (jnp.dot(hi, g, preferred_element_type=F32)
            + jnp.dot(lo, g, preferred_element_type=F32))


_C_Q, _C_K, _C_V, _C_RR, _C_RK, _C_RV = (i * D_GROUP for i in range(6))
_C_IQ = 6 * D_GROUP
_C_SM = _C_IQ + D_IDX
_C_LORA = _C_SM + LANES
D_IN_PACKED = _C_LORA + LANES
D_SHIFT = 3 * D_GROUP + LANES


def _in_proj_kernel(x_ref, pos_ref, n1w_ref, w_ref, mu_ref, qnw_ref, knw_ref, g_ref, rope_ref,
                    q_ref, k_ref, v_ref, rr_ref, rk_ref, rv_ref, iq_ref, sm_ref, lora_ref,
                    carry_ref):
    tm = x_ref.shape[0]

    @pl.when(pl.program_id(0) == 0)
    def _():
        carry_ref[...] = jnp.zeros_like(carry_ref)

    x = x_ref[...]
    xn = x * lax.rsqrt(jnp.mean(x * x, axis=-1, keepdims=True) + RMS_EPS) * n1w_ref[...]
    xb = xn.astype(BF16)
    pos = pos_ref[...]
    rope = rope_ref[...]

    def tables(frow, srow):
        ang = pos * rope[frow:frow + 1, :]
        return jnp.cos(ang), jnp.sin(ang) * rope[srow:srow + 1, :]

    def widen(z, reps):
        return jnp.concatenate([z] * reps, axis=1)

    def rotary(z, c, s, first_row, half):
        w = z.shape[1]
        first = jnp.concatenate([rope[first_row:first_row + 1, :]] * (w // LANES), axis=1) > 0.5
        partner = jnp.where(first, pltpu.roll(z, w - half, 1), pltpu.roll(z, half, 1))
        return z * c + partner * s

    def head_norm(z, w_row):
        ms = _group_sum(z * z, g_ref) * (1.0 / HEAD_DIM)
        return z * lax.rsqrt(ms + RMS_EPS) * w_row

    def proj(c0, width):
        return jnp.dot(xb, w_ref[:, c0:c0 + width], preferred_element_type=F32)

    def shift(z, c0):
        width = z.shape[1]
        row = lax.broadcasted_iota(jnp.int32, z.shape, 0)
        prev = jnp.where(row == 0, carry_ref[0:1, c0:c0 + width], pltpu.roll(z, 1, 0))
        carry_ref[0:1, c0:c0 + width] = z[tm - 1:tm, :]
        return z + (prev - z) * mu_ref[:, c0:c0 + width]

    cq, sq = (widen(z, D_GROUP // LANES) for z in tables(0, 1))
    q = rotary(head_norm(proj(_C_Q, D_GROUP), qnw_ref[...]), cq, sq, 2, HEAD_DIM // 8)
    q_ref[...] = q.astype(BF16)
    k = rotary(head_norm(proj(_C_K, D_GROUP), knw_ref[...]), cq, sq, 2, HEAD_DIM // 8)
    k_ref[...] = k.astype(BF16)
    v_ref[...] = proj(_C_V, D_GROUP).astype(BF16)

    rr_ref[...] = shift(proj(_C_RR, D_GROUP), 0)
    rk_ref[...] = shift(proj(_C_RK, D_GROUP), D_GROUP)
    rv_ref[...] = shift(proj(_C_RV, D_GROUP), 2 * D_GROUP)
    lora_ref[...] = shift(proj(_C_LORA, LANES), 3 * D_GROUP)

    ci, si = tables(3, 4)
    iq_ref[...] = rotary(proj(_C_IQ, D_IDX), widen(ci, D_IDX // LANES), widen(si, D_IDX // LANES),
                         5, IDX_DIM // 8)
    only_key = rope[6:7, :]
    sm_ref[...] = rotary(proj(_C_SM, LANES), ci * only_key + (1.0 - only_key), si * only_key,
                         5, IDX_DIM // 8)


def _in_proj(x2, pos_f, n1w, w_packed, mu_packed, qnw, knw, gmat, rope, tm):
    t = x2.shape[0]
    full = lambda shape: pl.BlockSpec(shape, lambda i: (0,) * len(shape))
    row = lambda width: pl.BlockSpec((tm, width), lambda i: (i, 0))
    out_shapes = (
        jax.ShapeDtypeStruct((t, D_GROUP), BF16),
        jax.ShapeDtypeStruct((t, D_GROUP), BF16),
        jax.ShapeDtypeStruct((t, D_GROUP), BF16),
        jax.ShapeDtypeStruct((t, D_GROUP), F32),
        jax.ShapeDtypeStruct((t, D_GROUP), F32),
        jax.ShapeDtypeStruct((t, D_GROUP), F32),
        jax.ShapeDtypeStruct((t, D_IDX), F32),
        jax.ShapeDtypeStruct((t, LANES), F32),
        jax.ShapeDtypeStruct((t, LANES), F32),
    )
    return pl.pallas_call(
        _in_proj_kernel,
        grid=(t // tm,),
        in_specs=[row(D_MODEL), row(1), full((1, D_MODEL)), full((D_MODEL, D_IN_PACKED)),
                  full((1, D_SHIFT)), full((1, D_GROUP)), full((1, D_GROUP)),
                  full((D_GROUP, D_GROUP)), full((8, LANES))],
        out_specs=[row(D_GROUP)] * 6 + [row(D_IDX), row(LANES), row(LANES)],
        out_shape=out_shapes,
        scratch_shapes=[pltpu.VMEM((8, D_SHIFT), F32)],
        compiler_params=pltpu.CompilerParams(dimension_semantics=("arbitrary",),
                                             vmem_limit_bytes=VMEM_LIMIT),
        name="in_proj",
    )(x2, pos_f, n1w, w_packed, mu_packed, qnw, knw, gmat, rope)


RWKV_CHUNK = 64
RWKV_UNROLL = 2
PAIR = 2 * HEAD_DIM


def _rwkv_kernel(r_ref, k_ref, v_ref, lora_ref, w0_ref, w2_ref, a0_ref, a2_ref, g2_ref,
                 kk_ref, ka_ref, rk_ref, lnw_ref, lnb_ref, g_ref,
                 o_ref,
                 s_ref, ld_s, r_s, k2_s, b_s, kk_s, y_s):
    tm = r_ref.shape[0]
    n_chunks = tm // RWKV_CHUNK
    n_pairs = D_GROUP // PAIR
    c = RWKV_CHUNK

    @pl.when(pl.program_id(0) == 0)
    def _():
        s_ref[...] = jnp.zeros_like(s_ref)

    lora = lora_ref[...]
    r = r_ref[...]
    k = k_ref[...]
    v = v_ref[...]
    zarg = w0_ref[...] + _dotf(jnp.tanh(lora), w2_ref[...])
    sp = jnp.maximum(-zarg, 0.0) + jnp.log1p(jnp.exp(-jnp.abs(zarg)))
    ld_s[...] = -jnp.exp(-sp - 0.5)
    a = jax.nn.sigmoid(a0_ref[...] + _dotf(lora, a2_ref[...]))
    g = _dotf(jax.nn.sigmoid(lora), g2_ref[...])
    kk = k * kk_ref[...]
    kk = kk * lax.rsqrt(jnp.maximum(_group_sum(kk * kk, g_ref), 1e-24))
    k2 = k * (1.0 + (a - 1.0) * ka_ref[...])
    bonus = _group_sum(r * k2 * rk_ref[...], g_ref) * v
    r_s[...] = r
    k2_s[...] = k2
    kk_s[...] = kk
    b_s[...] = kk * a

    row = lax.broadcasted_iota(jnp.int32, (2 * c, 2 * c), 0)
    col = lax.broadcasted_iota(jnp.int32, (2 * c, 2 * c), 1)
    same_head = (row >= c) == (col >= c)
    strict = same_head & (col < row)
    incl = same_head & (col <= row)
    eye = (row == col).astype(F32)
    tri = (lax.broadcasted_iota(jnp.int32, (c, c), 1)
           <= lax.broadcasted_iota(jnp.int32, (c, c), 0)).astype(F32)
    lane = lax.broadcasted_iota(jnp.int32, (c, PAIR), 1)
    head0 = lane < HEAD_DIM

    def stack(z):
        return jnp.concatenate([jnp.where(head0, z, 0.0), jnp.where(head0, 0.0, z)], axis=0)

    def chunk_body(ci, carry):
        c2 = 2 * c
        units = [(cc, p) for cc in range(RWKV_UNROLL) for p in range(n_pairs)]
        uid = range(len(units))
        rows = [pl.multiple_of((ci * RWKV_UNROLL + cc) * c, c) for cc in range(RWKV_UNROLL)]
        ld_all = [ld_s[pl.ds(r0, c), :] for r0 in rows]
        cum_all = [_dotf(tri, ld) for ld in ld_all]
        rt, kt, bt, kp, vs, lhs, rhs, gam_end = [], [], [], [], [], [], [], []
        for cc, p in units:
            r0 = rows[cc]
            cols = slice(p * PAIR, (p + 1) * PAIR)
            ld = ld_all[cc][:, cols]
            cum = cum_all[cc][:, cols]
            gam = jnp.exp(cum)
            inv = jnp.exp(-cum)
            gam_prev = jnp.exp(cum - ld)
            gam_end.append(gam[c - 1:c, :])
            rt.append(stack(r_s[pl.ds(r0, c), cols] * gam))
            kt_raw = k2_s[pl.ds(r0, c), cols] * inv
            bt_raw = b_s[pl.ds(r0, c), cols] * inv
            kt.append(stack(kt_raw))
            bt.append(stack(bt_raw))
            kp.append(stack(kk_s[pl.ds(r0, c), cols] * gam_prev))
            vs.append(stack(v_ref[pl.ds(r0, c), cols]))
            lhs.append(jnp.concatenate([kp[-1], rt[-1]], axis=0))
            rhs.append(jnp.concatenate([bt_raw, bt_raw, kt_raw, kt_raw], axis=0))
        aa = [_dot_nt(lhs[u], rhs[u]) for u in uid]
        a_kk = [jnp.where(strict, aa[u][:c2, c2:], 0.0) for u in uid]
        a_rb = [jnp.where(incl, aa[u][c2:, :c2], 0.0) for u in uid]
        a_rk = [jnp.where(incl, aa[u][c2:, c2:], 0.0) for u in uid]
        n = [-jnp.where(strict, aa[u][:c2, :c2], 0.0) for u in uid]
        prod = [eye + n[u] for u in uid]
        n = [_dot(n[u], n[u]) for u in uid]
        for _ in range(int(np.log2(c)) - 2):
            both = [_dot(jnp.concatenate([n[u], prod[u]], axis=0), n[u]) for u in uid]
            prod = [prod[u] + both[u][c2:] for u in uid]
            n = [both[u][:c2] for u in uid]
        av = [_dot(jnp.concatenate([a_kk[u], a_rk[u]], axis=0), vs[u]) for u in uid]
        tinv = [prod[u] + _dot(prod[u], n[u]) for u in uid]
        wu = [_dot(tinv[u], jnp.concatenate([kp[u], av[u][:c2]], axis=1)) for u in uid]
        rb = [_dot(a_rb[u], wu[u]) for u in uid]
        tn = [_dot_tn(jnp.concatenate([wu[u], vs[u]], axis=1),
                      jnp.concatenate([bt[u], kt[u]], axis=1)) for u in uid]
        r2 = [rt[u] - rb[u][:, :PAIR] for u in uid]
        y2 = [av[u][c2:] - rb[u][:, PAIR:] for u in uid]
        s_mix = [(eye - tn[u][:PAIR, :PAIR]) * gam_end[u] for u in uid]
        s_add = [(tn[u][2 * PAIR:, PAIR:] - tn[u][PAIR:2 * PAIR, :PAIR]) * gam_end[u] for u in uid]
        state = [s_ref[p] for p in range(n_pairs)]
        for cc in range(RWKV_UNROLL):
            us = [cc * n_pairs + p for p in range(n_pairs)]
            ys = [_dot_nt(r2[u], state[p]) + y2[u] for p, u in enumerate(us)]
            state = [_dot(state[p], s_mix[u]) + s_add[u] for p, u in enumerate(us)]
            for p in range(n_pairs):
                y_s[pl.ds(rows[cc], c), p * PAIR:(p + 1) * PAIR] = ys[p][:c] + ys[p][c:]
        for p in range(n_pairs):
            s_ref[p] = state[p]
        return carry

    lax.fori_loop(0, n_chunks // RWKV_UNROLL, chunk_body, 0)

    y = y_s[...]
    mean = _group_sum(y, g_ref) * (1.0 / HEAD_DIM)
    yc = y - mean
    var = _group_sum(yc * yc, g_ref) * (1.0 / HEAD_DIM)
    yn = yc * lax.rsqrt(var + RWKV_GN_EPS) * lnw_ref[...] + lnb_ref[...]
    o_ref[...] = (yn + bonus) * g


def _rwkv(rr, rk, rv, lora, w0, w2p, a0, a2p, g2p, k_k, k_a, r_k, ln_w, ln_b, gsum, tm):
    t = rr.shape[0]
    full = lambda shape: pl.BlockSpec(shape, lambda i: (0,) * len(shape))
    row = lambda width: pl.BlockSpec((tm, width), lambda i: (i, 0))
    vec = full((1, D_GROUP))
    big = pltpu.VMEM((tm, D_GROUP), F32)
    return pl.pallas_call(
        _rwkv_kernel,
        grid=(t // tm,),
        in_specs=[row(D_GROUP), row(D_GROUP), row(D_GROUP), row(LANES),
                  vec, full((LANES, D_GROUP)), vec, full((LANES, D_GROUP)),
                  full((LANES, D_GROUP)), vec, vec, vec, vec, vec, full((D_GROUP, D_GROUP))],
        out_specs=row(D_GROUP),
        out_shape=jax.ShapeDtypeStruct((t, D_GROUP), F32),
        scratch_shapes=[pltpu.VMEM((D_GROUP // PAIR, PAIR, PAIR), F32),
                        big, big, big, big, big, big],
        compiler_params=pltpu.CompilerParams(dimension_semantics=("arbitrary",),
                                             vmem_limit_bytes=VMEM_LIMIT),
        name="rwkv",
    )(rr, rk, rv, lora, w0, w2p, a0, a2p, g2p, k_k, k_a, r_k, ln_w, ln_b, gsum)


DSA_QB = 256
DSA_KB = 512
M_INIT = -5e29
DSA_HEAD_GROUP = 4
DSA_ACC_ROWS = 32
DSA_GROUPS = 256
DSA_TOP = 10
DSA_V_ROWS = HEAD_DIM + 16


def _dsa_kernel(qi_ref, kj_ref, qt_ref, k_ref, vt_ref, iqt_ref, ik_ref, iwt_ref, *rest, n_sel,
                n_cast):
    cast_in, o_ref, cast_out = rest[:n_cast], rest[n_cast], rest[n_cast + 1:2 * n_cast + 1]
    key_s, thr_s, need_s, over_s, m_s, acc_s, bdq_s, top_s = rest[2 * n_cast + 1:]
    for src, dst in zip(cast_in, cast_out):
        dst[...] = src[...].astype(BF16)
    qb = qt_ref.shape[2]
    kb = k_ref.shape[0]
    step = pl.program_id(0)
    qi = qi_ref[step]
    kj = kj_ref[step]
    q0 = qi * qb
    j_last = (q0 + qb - 1) // kb
    n_kc = j_last + 1
    idx_scale = float((IDX_DIM * N_IDX_HEADS) ** -0.5)

    def causal(jblk):
        s_pos = jblk * kb + lax.broadcasted_iota(jnp.int32, (kb, qb), 0)
        t_pos = q0 + lax.broadcasted_iota(jnp.int32, (kb, qb), 1)
        return s_pos <= t_pos

    @pl.when(kj == 0)
    def _():
        iqt = iqt_ref[...].astype(BF16)
        iwt = iwt_ref[...] * idx_scale

        def score_chunk(kc, diagonal):
            ik = ik_ref[pl.ds(pl.multiple_of(kc * kb, kb), kb), :]
            score = jnp.zeros((kb, qb), F32)
            for h in range(N_IDX_HEADS):
                d = jnp.dot(ik, iqt[h * IDX_DIM:(h + 1) * IDX_DIM, :],
                            preferred_element_type=F32)
                score = score + jnp.maximum(d, 0.0) * iwt[h:h + 1, :]
            bits = pltpu.bitcast(score, jnp.int32)
            keys = jnp.where(bits < 0, bits ^ jnp.int32(0x7FFFFFFF), bits)
            if diagonal:
                keys = jnp.where(causal(kc), keys, jnp.int32(INT_MIN))
            key_s[kc] = keys
            xs = [keys[r * DSA_GROUPS:(r + 1) * DSA_GROUPS] for r in range(kb // DSA_GROUPS)]
            for lvl in range(DSA_TOP):
                s = top_s[lvl]
                for r in range(len(xs)):
                    s, xs[r] = jnp.maximum(s, xs[r]), jnp.minimum(s, xs[r])
                top_s[lvl] = s

        top_s[...] = jnp.full_like(top_s, INT_MIN)

        def full_chunk(kc, carry):
            score_chunk(kc, False)
            return carry

        lax.fori_loop(0, j_last, full_chunk, 0)
        score_chunk(j_last, True)

        def select(count_keys):
            def bit_step(b, cur):
                bit = lax.shift_left(jnp.int32(1), jnp.int32(31) - b)
                cand = (cur | bit) ^ jnp.int32(INT_MIN)
                cnt = jnp.sum(count_keys(cand), axis=0, keepdims=True)
                return jnp.where(cnt >= float(n_sel), cur | bit, cur)

            cur = lax.fori_loop(0, 32, bit_step, jnp.zeros((1, qb), jnp.int32))
            return jnp.maximum(cur ^ jnp.int32(INT_MIN), jnp.int32(INT_MIN + 1))

        def part_count(keys, cand):
            ind = jnp.where(keys >= cand, 1.0, 0.0)
            return jnp.sum(ind.reshape(-1, DSA_ACC_ROWS, qb), axis=0)

        def count_top(cand):
            return lax.fori_loop(0, DSA_TOP, lambda lvl, a: a + part_count(top_s[lvl], cand),
                                 jnp.zeros((DSA_ACC_ROWS, qb), F32))

        def count_all(cand):
            return lax.fori_loop(0, n_kc, lambda kc, a: a + part_count(key_s[kc], cand),
                                 jnp.zeros((DSA_ACC_ROWS, qb), F32))

        def settle(count_keys):
            thr = select(count_keys)
            thr_s[...] = thr
            above = jnp.sum(count_keys(thr + 1), axis=0, keepdims=True)
            upto = jnp.sum(count_keys(thr), axis=0, keepdims=True)
            need_s[...] = float(n_sel) - above
            over_s[...] = upto - float(n_sel)
            return thr

        thr = settle(count_top)
        hidden = jnp.where(top_s[DSA_TOP - 1] >= thr, 1.0, 0.0)

        @pl.when(jnp.max(hidden) > 0.0)
        def _():
            settle(count_all)

        @pl.when(jnp.max(over_s[...]) > 0.0)
        def _():
            thr = thr_s[...]
            need = need_s[...]
            sub = LANES
            earlier = jnp.where(lax.broadcasted_iota(jnp.int32, (sub, sub), 1)
                                < lax.broadcasted_iota(jnp.int32, (sub, sub), 0), 1.0, 0.0).astype(BF16)

            def drop_surplus(kc, seen):
                keys = key_s[kc]
                tie = keys == thr
                tie_f = jnp.where(tie, 1.0, 0.0)
                ranks = []
                for r in range(kb // sub):
                    part = tie_f[r * sub:(r + 1) * sub]
                    ranks.append(seen + jnp.dot(earlier, part.astype(BF16),
                                                preferred_element_type=F32))
                    seen = seen + jnp.sum(part, axis=0, keepdims=True)
                rank = jnp.concatenate(ranks, axis=0)
                key_s[kc] = jnp.where(tie & (rank >= need), thr - 1, keys)
                return seen

            lax.fori_loop(0, n_kc, drop_surplus, jnp.zeros((1, qb), F32))

        m_s[...] = jnp.full_like(m_s, M_INIT)
        acc_s[...] = jnp.zeros_like(acc_s)
        bdq_s[...] = jnp.zeros_like(bdq_s)
        for h in range(N_HEADS):
            g, hh = divmod(h, DSA_HEAD_GROUP)
            bdq_s[g, hh * HEAD_DIM:(hh + 1) * HEAD_DIM, hh * qb:(hh + 1) * qb] = (
                qt_ref[h] * jnp.asarray(HEAD_DIM ** -0.5, BF16))

    mask = key_s[kj] >= thr_s[...]
    hg = DSA_HEAD_GROUP
    st_all = [jnp.dot(k_ref[:, g * hg * HEAD_DIM:(g + 1) * hg * HEAD_DIM], bdq_s[g],
                      preferred_element_type=F32) for g in range(N_HEADS // hg)]
    for g in range(N_HEADS // hg):
        st_g = st_all[g]
        for hh in range(hg):
            h = g * hg + hh
            st = jnp.where(mask, st_g[:, hh * qb:(hh + 1) * qb], NEG_BIG)
            m_old = m_s[h]
            part = jnp.max(st.reshape(kb // DSA_ACC_ROWS, DSA_ACC_ROWS, qb), axis=0)
            m_new = jnp.maximum(m_old, jnp.max(part, axis=0, keepdims=True))
            p = jnp.exp((st - m_new).astype(BF16))
            acc_s[h] = jnp.exp(m_old - m_new) * acc_s[h] + jnp.dot(
                vt_ref[h], p, preferred_element_type=F32)
            m_s[h] = m_new

    @pl.when(kj == j_last)
    def _():
        for h in range(N_HEADS):
            acc = acc_s[h]
            o_ref[h] = acc[:HEAD_DIM] / acc[HEAD_DIM:HEAD_DIM + 1]


def _dsa(qt, k, vt, iqt, ik, iwt, n_sel, to_cast=()):
    t = ik.shape[0]
    qb, kb = DSA_QB, DSA_KB
    assert n_sel <= DSA_GROUPS and kb % DSA_GROUPS == 0
    nq, nk = t // qb, t // kb
    pairs = [(i, j) for i in range(nq) for j in range((i * qb + qb - 1) // kb + 1)]
    qi = jnp.asarray(np.array([p[0] for p in pairs], np.int32))
    kj = jnp.asarray(np.array([p[1] for p in pairs], np.int32))
    n_slices = 1 << (len(pairs).bit_length() - 1)
    slabs = [w.reshape(n_slices, -1, w.shape[-1]) for w in to_cast]
    cast_specs = [pl.BlockSpec((1,) + w.shape[1:],
                               lambda s, qi, kj: (jnp.minimum(s, n_slices - 1), 0, 0))
                  for w in slabs]
    grid_spec = pltpu.PrefetchScalarGridSpec(
        num_scalar_prefetch=2,
        grid=(len(pairs),),
        in_specs=[pl.BlockSpec((N_HEADS, HEAD_DIM, qb), lambda s, qi, kj: (0, 0, qi[s])),
                  pl.BlockSpec((kb, D_GROUP), lambda s, qi, kj: (kj[s], 0)),
                  pl.BlockSpec((N_HEADS, DSA_V_ROWS, kb), lambda s, qi, kj: (0, 0, kj[s])),
                  pl.BlockSpec((D_IDX, qb), lambda s, qi, kj: (0, qi[s])),
                  pl.BlockSpec((t, IDX_DIM), lambda s, qi, kj: (0, 0)),
                  pl.BlockSpec((N_IDX_HEADS, qb), lambda s, qi, kj: (0, qi[s]))] + cast_specs,
        out_specs=[pl.BlockSpec((N_HEADS, HEAD_DIM, qb), lambda s, qi, kj: (0, 0, qi[s]))]
        + cast_specs,
        scratch_shapes=[pltpu.VMEM((nk, kb, qb), jnp.int32),
                        pltpu.VMEM((1, qb), jnp.int32),
                        pltpu.VMEM((1, qb), F32),
                        pltpu.VMEM((1, qb), F32),
                        pltpu.VMEM((N_HEADS, 1, qb), F32),
                        pltpu.VMEM((N_HEADS, DSA_V_ROWS, qb), F32),
                        pltpu.VMEM((N_HEADS // DSA_HEAD_GROUP, DSA_HEAD_GROUP * HEAD_DIM,
                                    DSA_HEAD_GROUP * qb), BF16),
                        pltpu.VMEM((DSA_TOP, DSA_GROUPS, qb), jnp.int32)])
    outs = pl.pallas_call(
        functools.partial(_dsa_kernel, n_sel=n_sel, n_cast=len(slabs)),
        grid_spec=grid_spec,
        out_shape=[jax.ShapeDtypeStruct((N_HEADS, HEAD_DIM, t), F32)]
        + [jax.ShapeDtypeStruct(w.shape, BF16) for w in slabs],
        compiler_params=pltpu.CompilerParams(dimension_semantics=("arbitrary",),
                                             vmem_limit_bytes=VMEM_LIMIT),
        name="dsa",
    )(qi, kj, qt, k, vt, iqt, ik, iwt, *slabs)
    return outs[0], [o.reshape(w.shape) for o, w in zip(outs[1:], to_cast)]


def _out_proj_kernel(x_ref, att_ref, rw_ref, wo_ref, n2w_ref, rwt_ref, rb_ref,
                     h_ref, xn_ref, gate_ref):
    tm = x_ref.shape[0]
    att = att_ref[...].reshape(D_GROUP, tm).T
    mix = jnp.concatenate([att, rw_ref[...]], axis=1)
    acc = x_ref[...] + _dot(mix, wo_ref[...])
    h_ref[...] = acc
    xn = acc * lax.rsqrt(jnp.mean(acc * acc, axis=-1, keepdims=True) + RMS_EPS) * n2w_ref[...]
    xn_ref[...] = xn.astype(BF16)
    logits = _dotf(xn, rwt_ref[...]) + rb_ref[...]
    lane = lax.broadcasted_iota(jnp.int32, logits.shape, 1)
    work = logits
    vals, hots = [], []
    for _ in range(TOP_K_EXPERTS):
        m = jnp.max(work, axis=1, keepdims=True)
        idx = jnp.min(jnp.where(work == m, lane, LANES), axis=1, keepdims=True)
        hot = lane == idx
        vals.append(m)
        hots.append(hot)
        work = jnp.where(hot, -jnp.inf, work)
    es = [jnp.exp(vv - vals[0]) for vv in vals]
    denom = es[0] + es[1] + es[2] + es[3]
    gates = jnp.zeros_like(logits)
    for e, hot in zip(es, hots):
        gates = gates + jnp.where(hot, e / denom, 0.0)
    gate_ref[...] = gates


def _out_proj(x2, att_t, rw, wo, n2w, rwt, rb, tm):
    t = x2.shape[0]
    full = lambda shape: pl.BlockSpec(shape, lambda i: (0,) * len(shape))
    row = lambda width: pl.BlockSpec((tm, width), lambda i: (i, 0))
    return pl.pallas_call(
        _out_proj_kernel,
        grid=(t // tm,),
        in_specs=[row(D_MODEL), pl.BlockSpec((N_HEADS, HEAD_DIM, tm), lambda i: (0, 0, i)),
                  row(D_GROUP), full((2 * D_GROUP, D_MODEL)), full((1, D_MODEL)),
                  full((D_MODEL, LANES)), full((1, LANES))],
        out_specs=[row(D_MODEL), row(D_MODEL), row(LANES)],
        out_shape=(jax.ShapeDtypeStruct((t, D_MODEL), F32),
                   jax.ShapeDtypeStruct((t, D_MODEL), BF16),
                   jax.ShapeDtypeStruct((t, LANES), F32)),
        compiler_params=pltpu.CompilerParams(dimension_semantics=("arbitrary",),
                                             vmem_limit_bytes=VMEM_LIMIT),
        name="out_proj",
    )(x2, att_t, rw, wo, n2w, rwt, rb)


MOE_TM = 1024
MOE_RB = 144
MOE_FC = 512
MOE_EXPERTS_PER_STEP = 2


def _route_kernel(gate_ref, rankt_ref, gatet_ref, cnt_ref):
    tm = gate_ref.shape[0]
    gates = gate_ref[...]
    hot = gates > 0.0
    ind = jnp.where(hot, 1.0, 0.0)
    before = (lax.broadcasted_iota(jnp.int32, (tm, tm), 1)
              < lax.broadcasted_iota(jnp.int32, (tm, tm), 0))
    rank = jnp.dot(jnp.where(before, 1.0, 0.0).astype(BF16), ind.astype(BF16),
                   preferred_element_type=F32)
    rankt_ref[...] = jnp.where(hot, rank, -1.0).T
    gatet_ref[...] = gates.T
    cnt = jnp.sum(ind, axis=0, keepdims=True)
    cnt_ref[...] = jnp.broadcast_to(cnt[None], cnt_ref.shape)


def _route(gates, tm):
    t = gates.shape[0]
    return pl.pallas_call(
        _route_kernel,
        grid=(t // tm,),
        in_specs=[pl.BlockSpec((tm, LANES), lambda i: (i, 0))],
        out_specs=[pl.BlockSpec((LANES, tm), lambda i: (0, i)),
                   pl.BlockSpec((LANES, tm), lambda i: (0, i)),
                   pl.BlockSpec((1, 8, LANES), lambda i: (i, 0, 0))],
        out_shape=(jax.ShapeDtypeStruct((LANES, t), F32),
                   jax.ShapeDtypeStruct((LANES, t), F32),
                   jax.ShapeDtypeStruct((t // tm, 8, LANES), F32)),
        compiler_params=pltpu.CompilerParams(dimension_semantics=("arbitrary",),
                                             vmem_limit_bytes=VMEM_LIMIT),
        name="route",
    )(gates)


def _moe_kernel(cnt_ref, h_ref, xn_ref, rankt_ref, gatet_ref, wgu_ref, bgu_ref,
                wdn_ref, bdn_ref, o_ref):
    i = pl.program_id(0)
    tm = h_ref.shape[0]

    @pl.when(pl.program_id(1) == 0)
    def _():
        o_ref[...] = h_ref[...]

    for sub in range(MOE_EXPERTS_PER_STEP):
        _moe_expert(pl.program_id(1) * MOE_EXPERTS_PER_STEP + sub, sub, i, tm, cnt_ref, xn_ref,
                    rankt_ref, gatet_ref, wgu_ref, bgu_ref, wdn_ref, bdn_ref, o_ref)


def _moe_expert(e, slot, i, tm, cnt_ref, xn_ref, rankt_ref, gatet_ref, wgu_ref, bgu_ref,
                wdn_ref, bdn_ref, o_ref):
    n_rows = cnt_ref[i * N_EXPERTS + e]
    n_blocks = (n_rows + MOE_RB - 1) // MOE_RB
    rank_row = rankt_ref[pl.ds(e, 1), :]
    gate_row = gatet_ref[pl.ds(e, 1), :]

    def row_block(bi, carry):
        r0 = (bi * MOE_RB).astype(F32)
        rowid = lax.broadcasted_iota(jnp.int32, (MOE_RB, tm), 0).astype(F32) + r0
        hit = rank_row == rowid
        sel = jnp.where(hit, 1.0, 0.0).astype(BF16)
        gate = jnp.sum(jnp.where(hit, gate_row, 0.0), axis=1, keepdims=True)
        xb = jnp.dot(sel, xn_ref[...], preferred_element_type=F32).astype(BF16)
        y = jnp.zeros((MOE_RB, D_MODEL), F32)
        ups = []
        for fc in range(D_FF // MOE_FC):
            c0 = fc * MOE_FC
            ups.append((jnp.dot(xb, wgu_ref[slot, :, c0:c0 + MOE_FC], preferred_element_type=F32),
                        jnp.dot(xb, wgu_ref[slot, :, D_FF + c0:D_FF + c0 + MOE_FC],
                                preferred_element_type=F32)))
        for fc in range(D_FF // MOE_FC):
            c0 = fc * MOE_FC
            hg = ups[fc][0] + bgu_ref[slot, :, c0:c0 + MOE_FC]
            hl = ups[fc][1] + bgu_ref[slot, :, D_FF + c0:D_FF + c0 + MOE_FC]
            glu = jnp.minimum(hg, SWIGLU_LIMIT)
            lin = jnp.clip(hl, -SWIGLU_LIMIT, SWIGLU_LIMIT)
            act = (lin + 1.0) * glu * jax.nn.sigmoid(SWIGLU_ALPHA * glu)
            y = y + jnp.dot(act.astype(BF16), wdn_ref[slot, c0:c0 + MOE_FC, :],
                            preferred_element_type=F32)
        y = ((y + bdn_ref[slot]) * gate).astype(BF16)
        o_ref[...] += _dot_tn(sel, y)
        return carry

    lax.fori_loop(0, n_blocks, row_block, 0)


def _moe(counts, h1, xn2, rank_t, gate_t, wgu, bgu, wdn, bdn, tm):
    t = h1.shape[0]
    eps = MOE_EXPERTS_PER_STEP
    row = lambda width: pl.BlockSpec((tm, width), lambda i, e, c: (i, 0))
    grid_spec = pltpu.PrefetchScalarGridSpec(
        num_scalar_prefetch=1,
        grid=(t // tm, N_EXPERTS // eps),
        in_specs=[row(D_MODEL), row(D_MODEL),
                  pl.BlockSpec((LANES, tm), lambda i, e, c: (0, i)),
                  pl.BlockSpec((LANES, tm), lambda i, e, c: (0, i)),
                  pl.BlockSpec((eps, D_MODEL, 2 * D_FF), lambda i, e, c: (e, 0, 0)),
                  pl.BlockSpec((eps, 1, 2 * D_FF), lambda i, e, c: (e, 0, 0)),
                  pl.BlockSpec((eps, D_FF, D_MODEL), lambda i, e, c: (e, 0, 0)),
                  pl.BlockSpec((eps, 1, D_MODEL), lambda i, e, c: (e, 0, 0))],
        out_specs=row(D_MODEL))
    return pl.pallas_call(
        _moe_kernel,
        grid_spec=grid_spec,
        out_shape=jax.ShapeDtypeStruct((t, D_MODEL), F32),
        compiler_params=pltpu.CompilerParams(dimension_semantics=("arbitrary", "arbitrary"),
                                             vmem_limit_bytes=VMEM_LIMIT),
        name="moe",
    )(counts, h1, xn2, rank_t, gate_t, wgu, bgu, wdn, bdn)


def _rope_tables():
    lane = jnp.arange(LANES)

    def rows(period, rot):
        half = rot // 2
        inv_freq = ROPE_THETA ** (-jnp.arange(half, dtype=F32) / half)
        jm = lane % period
        freq = jnp.where(jm < rot, inv_freq[jm % half], 0.0)
        sign = jnp.where(jm < half, -1.0, jnp.where(jm < rot, 1.0, 0.0))
        first = (jm < half).astype(F32)
        return [freq, sign, first]

    idx_rows = rows(IDX_DIM, IDX_DIM // 4)
    only_key = (lane < IDX_DIM).astype(F32)
    return jnp.stack(rows(HEAD_DIM, HEAD_DIM // 4) + idx_rows
                     + [only_key, jnp.zeros_like(only_key)]).astype(F32)


def _group_matrix():
    g = np.arange(D_GROUP) // HEAD_DIM
    return jnp.asarray((g[:, None] == g[None, :]).astype(np.float32), dtype=BF16)


def _pad_rows(w, r0, rows):
    return jnp.zeros((rows, w.shape[1]), w.dtype).at[r0:r0 + w.shape[0]].set(w)


def kernel(x, positions, norm1_w, w_in, q_norm_w, k_norm_w, rwkv_mu, rwkv_w0, rwkv_w2, rwkv_a0,
           rwkv_a2, rwkv_g2, rwkv_k_k, rwkv_k_a, rwkv_r_k, rwkv_ln_w, rwkv_ln_b, w_out, norm2_w,
           router_w, router_b, exp_w_gu, exp_b_gu, exp_w_down, exp_b_down):
    b, t, _ = x.shape
    assert b == 1 and w_in.shape[0] == 1, "single sequence, single layer"
    assert t % DSA_KB == 0 and t % DSA_QB == 0 and t % MOE_TM == 0
    x2 = x[0]
    pos_f = positions[0].astype(F32)[:, None]
    n_sel = min(TOPK_MAX, t // 4)

    w = w_in[0]
    a0 = 3 * D_GROUP
    att_cols = a0 + D_IDX + IDX_DIM + N_IDX_HEADS
    w_att, w_rw = w[:, :att_cols], w[:, att_cols:]
    w_sm = jnp.zeros((D_MODEL, LANES), F32).at[:, :IDX_DIM + N_IDX_HEADS].set(w_att[:, a0 + D_IDX:])
    w_packed = jnp.concatenate(
        [w_att[:, :a0], w_rw[:, :a0], w_att[:, a0:a0 + D_IDX], w_sm, w_rw[:, a0:]],
        axis=1).astype(BF16)
    mu = rwkv_mu[0][None, :]
    tile8 = lambda z: jnp.tile(z, N_HEADS)[None, :]
    gsum = _group_matrix()

    tm = 256
    q, k, v, rr, rk, rv, iq, sm, lora = _in_proj(
        x2, pos_f, norm1_w, w_packed, mu, tile8(q_norm_w[0]), tile8(k_norm_w[0]), gsum,
        _rope_tables(), tm)

    vec = lambda z: z.reshape(1, D_GROUP)
    rw = _rwkv(rr, rk, rv, lora, vec(rwkv_w0[0]),
               _pad_rows(rwkv_w2[0], 0, LANES), vec(rwkv_a0[0]),
               _pad_rows(rwkv_a2[0], D_DECAY_LORA, LANES),
               _pad_rows(rwkv_g2[0], D_DECAY_LORA + D_AAA_LORA, LANES),
               vec(rwkv_k_k[0]), vec(rwkv_k_a[0]), vec(rwkv_r_k[0]), vec(rwkv_ln_w[0]),
               vec(rwkv_ln_b[0]), gsum, tm)

    heads = lambda z: z.reshape(t, N_HEADS, HEAD_DIM)
    vt = jnp.concatenate([heads(v).transpose(1, 2, 0),
                          jnp.ones((N_HEADS, DSA_V_ROWS - HEAD_DIM, t), BF16)], axis=1)
    att_t, (w_gu_bf, w_dn_bf) = _dsa(
        heads(q).transpose(1, 2, 0), k, vt, iq.T, sm[:, :IDX_DIM].astype(BF16),
        sm[:, IDX_DIM:IDX_DIM + N_IDX_HEADS].T, n_sel, to_cast=(exp_w_gu[0], exp_w_down[0]))

    rwt = jnp.zeros((D_MODEL, LANES), F32).at[:, :N_EXPERTS].set(router_w[0])
    rb = jnp.full((1, LANES), NEG_BIG, F32).at[0, :N_EXPERTS].set(router_b[0])
    h1, xn2, gates = _out_proj(x2, att_t, rw, w_out[0].astype(BF16), norm2_w, rwt, rb, tm)

    rank_t, gate_t, cnt = _route(gates, MOE_TM)
    counts = cnt[:, 0, :N_EXPERTS].astype(jnp.int32).reshape(-1)
    out = _moe(counts, h1, xn2, rank_t, gate_t, w_gu_bf, exp_b_gu[0][:, None, :], w_dn_bf,
               exp_b_down[0][:, None, :], MOE_TM)
    return out[None]
```

```python
import functools

import jax
import jax.numpy as jnp
import numpy as np
from jax import lax
from jax.experimental import pallas as pl
from jax.experimental.pallas import tpu as pltpu

F32 = jnp.float32
BF16 = jnp.bfloat16
HIGHEST = lax.Precision.HIGHEST

D_MODEL = 1024
HEAD_DIM = 64
N_HEADS = 8
D_GROUP = N_HEADS * HEAD_DIM
ROPE_THETA = 500000.0
N_IDX_HEADS = 8
IDX_DIM = 32
D_IDX = N_IDX_HEADS * IDX_DIM
TOPK_MAX = 256
D_DECAY_LORA = 32
D_AAA_LORA = 32
D_GATE_LORA = 64
RWKV_GN_EPS = 64e-5
N_EXPERTS = 32
TOP_K_EXPERTS = 4
D_FF = 1024
SWIGLU_LIMIT = 7.0
SWIGLU_ALPHA = 1.702
RMS_EPS = 1e-6

LANES = 128
VMEM_LIMIT = 56 * 1024 * 1024

NEG_BIG = -1e30
INT_MIN = -(2 ** 31)


def _dot(a, b):
    return jnp.dot(a.astype(BF16), b.astype(BF16), preferred_element_type=F32)


def _dotf(a, b):
    return jnp.dot(a, b, preferred_element_type=F32, precision=HIGHEST)


def _dot_nt(a, b):
    return lax.dot_general(a.astype(BF16), b.astype(BF16), (((1,), (1,)), ((), ())),
                           preferred_element_type=F32)


def _dot_tn(a, b):
    return lax.dot_general(a.astype(BF16), b.astype(BF16), (((0,), (0,)), ((), ())),
                           preferred_element_type=F32)


def _group_sum(z, g_ref):
    hi = z.astype(BF16)
    lo = (z - hi.astype(F32)).astype(BF16)
    g = g_ref[...]
    return (jnp.dot(hi, g, preferred_element_type=F32)
            + jnp.dot(lo, g, preferred_element_type=F32))


_C_Q, _C_K, _C_V, _C_RR, _C_RK, _C_RV = (i * D_GROUP for i in range(6))
_C_IQ = 6 * D_GROUP
_C_SM = _C_IQ + D_IDX
_C_LORA = _C_SM + LANES
D_IN_PACKED = _C_LORA + LANES
D_SHIFT = 3 * D_GROUP + LANES


def _in_proj_kernel(x_ref, pos_ref, n1w_ref, w_ref, mu_ref, qnw_ref, knw_ref, g_ref, rope_ref,
                    q_ref, k_ref, v_ref, rr_ref, rk_ref, rv_ref, iq_ref, sm_ref, lora_ref,
                    carry_ref):
    tm = x_ref.shape[0]

    @pl.when(pl.program_id(0) == 0)
    def _():
        carry_ref[...] = jnp.zeros_like(carry_ref)

    x = x_ref[...]
    xn = x * lax.rsqrt(jnp.mean(x * x, axis=-1, keepdims=True) + RMS_EPS) * n1w_ref[...]
    xb = xn.astype(BF16)
    pos = pos_ref[...]
    rope = rope_ref[...]

    def tables(frow, srow):
        ang = pos * rope[frow:frow + 1, :]
        return jnp.cos(ang), jnp.sin(ang) * rope[srow:srow + 1, :]

    def widen(z, reps):
        return jnp.concatenate([z] * reps, axis=1)

    def rotary(z, c, s, first_row, half):
        w = z.shape[1]
        first = jnp.concatenate([rope[first_row:first_row + 1, :]] * (w // LANES), axis=1) > 0.5
        partner = jnp.where(first, pltpu.roll(z, w - half, 1), pltpu.roll(z, half, 1))
        return z * c + partner * s

    def head_norm(z, w_row):
        ms = _group_sum(z * z, g_ref) * (1.0 / HEAD_DIM)
        return z * lax.rsqrt(ms + RMS_EPS) * w_row

    def proj(c0, width):
        return jnp.dot(xb, w_ref[:, c0:c0 + width], preferred_element_type=F32)

    def shift(z, c0):
        width = z.shape[1]
        row = lax.broadcasted_iota(jnp.int32, z.shape, 0)
        prev = jnp.where(row == 0, carry_ref[0:1, c0:c0 + width], pltpu.roll(z, 1, 0))
        carry_ref[0:1, c0:c0 + width] = z[tm - 1:tm, :]
        return z + (prev - z) * mu_ref[:, c0:c0 + width]

    cq, sq = (widen(z, D_GROUP // LANES) for z in tables(0, 1))
    q = rotary(head_norm(proj(_C_Q, D_GROUP), qnw_ref[...]), cq, sq, 2, HEAD_DIM // 8)
    q_ref[...] = q.astype(BF16)
    k = rotary(head_norm(proj(_C_K, D_GROUP), knw_ref[...]), cq, sq, 2, HEAD_DIM // 8)
    k_ref[...] = k.astype(BF16)
    v_ref[...] = proj(_C_V, D_GROUP).astype(BF16)

    rr_ref[...] = shift(proj(_C_RR, D_GROUP), 0)
    rk_ref[...] = shift(proj(_C_RK, D_GROUP), D_GROUP)
    rv_ref[...] = shift(proj(_C_RV, D_GROUP), 2 * D_GROUP)
    lora_ref[...] = shift(proj(_C_LORA, LANES), 3 * D_GROUP)

    ci, si = tables(3, 4)
    iq_ref[...] = rotary(proj(_C_IQ, D_IDX), widen(ci, D_IDX // LANES), widen(si, D_IDX // LANES),
                         5, IDX_DIM // 8)
    only_key = rope[6:7, :]
    sm_ref[...] = rotary(proj(_C_SM, LANES), ci * only_key + (1.0 - only_key), si * only_key,
                         5, IDX_DIM // 8)


def _in_proj(x2, pos_f, n1w, w_packed, mu_packed, qnw, knw, gmat, rope, tm):
    t = x2.shape[0]
    full = lambda shape: pl.BlockSpec(shape, lambda i: (0,) * len(shape))
    row = lambda width: pl.BlockSpec((tm, width), lambda i: (i, 0))
    out_shapes = (
        jax.ShapeDtypeStruct((t, D_GROUP), BF16),
        jax.ShapeDtypeStruct((t, D_GROUP), BF16),
        jax.ShapeDtypeStruct((t, D_GROUP), BF16),
        jax.ShapeDtypeStruct((t, D_GROUP), F32),
        jax.ShapeDtypeStruct((t, D_GROUP), F32),
        jax.ShapeDtypeStruct((t, D_GROUP), F32),
        jax.ShapeDtypeStruct((t, D_IDX), F32),
        jax.ShapeDtypeStruct((t, LANES), F32),
        jax.ShapeDtypeStruct((t, LANES), F32),
    )
    return pl.pallas_call(
        _in_proj_kernel,
        grid=(t // tm,),
        in_specs=[row(D_MODEL), row(1), full((1, D_MODEL)), full((D_MODEL, D_IN_PACKED)),
                  full((1, D_SHIFT)), full((1, D_GROUP)), full((1, D_GROUP)),
                  full((D_GROUP, D_GROUP)), full((8, LANES))],
        out_specs=[row(D_GROUP)] * 6 + [row(D_IDX), row(LANES), row(LANES)],
        out_shape=out_shapes,
        scratch_shapes=[pltpu.VMEM((8, D_SHIFT), F32)],
        compiler_params=pltpu.CompilerParams(dimension_semantics=("arbitrary",),
                                             vmem_limit_bytes=VMEM_LIMIT),
        name="in_proj",
    )(x2, pos_f, n1w, w_packed, mu_packed, qnw, knw, gmat, rope)


RWKV_CHUNK = 64
RWKV_UNROLL = 2
PAIR = 2 * HEAD_DIM


def _rwkv_kernel(r_ref, k_ref, v_ref, lora_ref, w0_ref, w2_ref, a0_ref, a2_ref, g2_ref,
                 kk_ref, ka_ref, rk_ref, lnw_ref, lnb_ref, g_ref,
                 o_ref,
                 s_ref, ld_s, r_s, k2_s, b_s, kk_s, y_s):
    tm = r_ref.shape[0]
    n_chunks = tm // RWKV_CHUNK
    n_pairs = D_GROUP // PAIR
    c = RWKV_CHUNK

    @pl.when(pl.program_id(0) == 0)
    def _():
        s_ref[...] = jnp.zeros_like(s_ref)

    lora = lora_ref[...]
    r = r_ref[...]
    k = k_ref[...]
    v = v_ref[...]
    zarg = w0_ref[...] + _dotf(jnp.tanh(lora), w2_ref[...])
    sp = jnp.maximum(-zarg, 0.0) + jnp.log1p(jnp.exp(-jnp.abs(zarg)))
    ld_s[...] = -jnp.exp(-sp - 0.5)
    a = jax.nn.sigmoid(a0_ref[...] + _dotf(lora, a2_ref[...]))
    g = _dotf(jax.nn.sigmoid(lora), g2_ref[...])
    kk = k * kk_ref[...]
    kk = kk * lax.rsqrt(jnp.maximum(_group_sum(kk * kk, g_ref), 1e-24))
    k2 = k * (1.0 + (a - 1.0) * ka_ref[...])
    bonus = _group_sum(r * k2 * rk_ref[...], g_ref) * v
    r_s[...] = r
    k2_s[...] = k2
    kk_s[...] = kk
    b_s[...] = kk * a

    row = lax.broadcasted_iota(jnp.int32, (2 * c, 2 * c), 0)
    col = lax.broadcasted_iota(jnp.int32, (2 * c, 2 * c), 1)
    same_head = (row >= c) == (col >= c)
    strict = same_head & (col < row)
    incl = same_head & (col <= row)
    eye = (row == col).astype(F32)
    tri = (lax.broadcasted_iota(jnp.int32, (c, c), 1)
           <= lax.broadcasted_iota(jnp.int32, (c, c), 0)).astype(F32)
    lane = lax.broadcasted_iota(jnp.int32, (c, PAIR), 1)
    head0 = lane < HEAD_DIM

    def stack(z):
        return jnp.concatenate([jnp.where(head0, z, 0.0), jnp.where(head0, 0.0, z)], axis=0)

    def chunk_body(ci, carry):
        c2 = 2 * c
        units = [(cc, p) for cc in range(RWKV_UNROLL) for p in range(n_pairs)]
        uid = range(len(units))
        rows = [pl.multiple_of((ci * RWKV_UNROLL + cc) * c, c) for cc in range(RWKV_UNROLL)]
        ld_all = [ld_s[pl.ds(r0, c), :] for r0 in rows]
        cum_all = [_dotf(tri, ld) for ld in ld_all]
        rt, kt, bt, kp, vs, lhs, rhs, gam_end = [], [], [], [], [], [], [], []
        for cc, p in units:
            r0 = rows[cc]
            cols = slice(p * PAIR, (p + 1) * PAIR)
            ld = ld_all[cc][:, cols]
            cum = cum_all[cc][:, cols]
            gam = jnp.exp(cum)
            inv = jnp.exp(-cum)
            gam_prev = jnp.exp(cum - ld)
            gam_end.append(gam[c - 1:c, :])
            rt.append(stack(r_s[pl.ds(r0, c), cols] * gam))
            kt_raw = k2_s[pl.ds(r0, c), cols] * inv
            bt_raw = b_s[pl.ds(r0, c), cols] * inv
            kt.append(stack(kt_raw))
            bt.append(stack(bt_raw))
            kp.append(stack(kk_s[pl.ds(r0, c), cols] * gam_prev))
            vs.append(stack(v_ref[pl.ds(r0, c), cols]))
            lhs.append(jnp.concatenate([kp[-1], rt[-1]], axis=0))
            rhs.append(jnp.concatenate([bt_raw, bt_raw, kt_raw, kt_raw], axis=0))
        aa = [_dot_nt(lhs[u], rhs[u]) for u in uid]
        a_kk = [jnp.where(strict, aa[u][:c2, c2:], 0.0) for u in uid]
        a_rb = [jnp.where(incl, aa[u][c2:, :c2], 0.0) for u in uid]
        a_rk = [jnp.where(incl, aa[u][c2:, c2:], 0.0) for u in uid]
        n = [-jnp.where(strict, aa[u][:c2, :c2], 0.0) for u in uid]
        prod = [eye + n[u] for u in uid]
        n = [_dot(n[u], n[u]) for u in uid]
        for _ in range(int(np.log2(c)) - 2):
            both = [_dot(jnp.concatenate([n[u], prod[u]], axis=0), n[u]) for u in uid]
            prod = [prod[u] + both[u][c2:] for u in uid]
            n = [both[u][:c2] for u in uid]
        av = [_dot(jnp.concatenate([a_kk[u], a_rk[u]], axis=0), vs[u]) for u in uid]
        tinv = [prod[u] + _dot(prod[u], n[u]) for u in uid]
        wu = [_dot(tinv[u], jnp.concatenate([kp[u], av[u][:c2]], axis=1)) for u in uid]
        rb = [_dot(a_rb[u], wu[u]) for u in uid]
        tn = [_dot_tn(jnp.concatenate([wu[u], vs[u]], axis=1),
                      jnp.concatenate([bt[u], kt[u]], axis=1)) for u in uid]
        r2 = [rt[u] - rb[u][:, :PAIR] for u in uid]
        y2 = [av[u][c2:] - rb[u][:, PAIR:] for u in uid]
        s_mix = [(eye - tn[u][:PAIR, :PAIR]) * gam_end[u] for u in uid]
        s_add = [(tn[u][2 * PAIR:, PAIR:] - tn[u][PAIR:2 * PAIR, :PAIR]) * gam_end[u] for u in uid]
        state = [s_ref[p] for p in range(n_pairs)]
        for cc in range(RWKV_UNROLL):
            us = [cc * n_pairs + p for p in range(n_pairs)]
            ys = [_dot_nt(r2[u], state[p]) + y2[u] for p, u in enumerate(us)]
            state = [_dot(state[p], s_mix[u]) + s_add[u] for p, u in enumerate(us)]
            for p in range(n_pairs):
                y_s[pl.ds(rows[cc], c), p * PAIR:(p + 1) * PAIR] = ys[p][:c] + ys[p][c:]
        for p in range(n_pairs):
            s_ref[p] = state[p]
        return carry

    lax.fori_loop(0, n_chunks // RWKV_UNROLL, chunk_body, 0)

    y = y_s[...]
    mean = _group_sum(y, g_ref) * (1.0 / HEAD_DIM)
    yc = y - mean
    var = _group_sum(yc * yc, g_ref) * (1.0 / HEAD_DIM)
    yn = yc * lax.rsqrt(var + RWKV_GN_EPS) * lnw_ref[...] + lnb_ref[...]
    o_ref[...] = (yn + bonus) * g


def _rwkv(rr, rk, rv, lora, w0, w2p, a0, a2p, g2p, k_k, k_a, r_k, ln_w, ln_b, gsum, tm):
    t = rr.shape[0]
    full = lambda shape: pl.BlockSpec(shape, lambda i: (0,) * len(shape))
    row = lambda width: pl.BlockSpec((tm, width), lambda i: (i, 0))
    vec = full((1, D_GROUP))
    big = pltpu.VMEM((tm, D_GROUP), F32)
    return pl.pallas_call(
        _rwkv_kernel,
        grid=(t // tm,),
        in_specs=[row(D_GROUP), row(D_GROUP), row(D_GROUP), row(LANES),
                  vec, full((LANES, D_GROUP)), vec, full((LANES, D_GROUP)),
                  full((LANES, D_GROUP)), vec, vec, vec, vec, vec, full((D_GROUP, D_GROUP))],
        out_specs=row(D_GROUP),
        out_shape=jax.ShapeDtypeStruct((t, D_GROUP), F32),
        scratch_shapes=[pltpu.VMEM((D_GROUP // PAIR, PAIR, PAIR), F32),
                        big, big, big, big, big, big],
        compiler_params=pltpu.CompilerParams(dimension_semantics=("arbitrary",),
                                             vmem_limit_bytes=VMEM_LIMIT),
        name="rwkv",
    )(rr, rk, rv, lora, w0, w2p, a0, a2p, g2p, k_k, k_a, r_k, ln_w, ln_b, gsum)


DSA_QB = 256
DSA_KB = 1024
M_INIT = -5e29
DSA_HEAD_GROUP = 4
DSA_ACC_ROWS = 32
DSA_GROUPS = 256
DSA_TOP = 10
DSA_V_ROWS = HEAD_DIM + 16


def _dsa_kernel(qi_ref, kj_ref, qt_ref, k_ref, vt_ref, iqt_ref, ik_ref, iwt_ref, *rest, n_sel,
                n_cast):
    cast_in, o_ref, cast_out = rest[:n_cast], rest[n_cast], rest[n_cast + 1:2 * n_cast + 1]
    key_s, thr_s, need_s, over_s, m_s, acc_s, bdq_s, top_s = rest[2 * n_cast + 1:]
    for src, dst in zip(cast_in, cast_out):
        dst[...] = src[...].astype(BF16)
    qb = qt_ref.shape[2]
    kb = k_ref.shape[0]
    step = pl.program_id(0)
    qi = qi_ref[step]
    kj = kj_ref[step]
    q0 = qi * qb
    j_last = (q0 + qb - 1) // kb
    n_kc = j_last + 1
    idx_scale = float((IDX_DIM * N_IDX_HEADS) ** -0.5)

    def causal(jblk):
        s_pos = jblk * kb + lax.broadcasted_iota(jnp.int32, (kb, qb), 0)
        t_pos = q0 + lax.broadcasted_iota(jnp.int32, (kb, qb), 1)
        return s_pos <= t_pos

    @pl.when(kj == 0)
    def _():
        iqt = iqt_ref[...].astype(BF16)
        iwt = iwt_ref[...] * idx_scale

        def score_chunk(kc, diagonal):
            ik = ik_ref[pl.ds(pl.multiple_of(kc * kb, kb), kb), :]
            score = jnp.zeros((kb, qb), F32)
            for h in range(N_IDX_HEADS):
                d = jnp.dot(ik, iqt[h * IDX_DIM:(h + 1) * IDX_DIM, :],
                            preferred_element_type=F32)
                score = score + jnp.maximum(d, 0.0) * iwt[h:h + 1, :]
            bits = pltpu.bitcast(score, jnp.int32)
            keys = jnp.where(bits < 0, bits ^ jnp.int32(0x7FFFFFFF), bits)
            if diagonal:
                keys = jnp.where(causal(kc), keys, jnp.int32(INT_MIN))
            key_s[kc] = keys
            xs = [keys[r * DSA_GROUPS:(r + 1) * DSA_GROUPS] for r in range(kb // DSA_GROUPS)]
            for lvl in range(DSA_TOP):
                s = top_s[lvl]
                for r in range(len(xs)):
                    s, xs[r] = jnp.maximum(s, xs[r]), jnp.minimum(s, xs[r])
                top_s[lvl] = s

        top_s[...] = jnp.full_like(top_s, INT_MIN)

        def full_chunk(kc, carry):
            score_chunk(kc, False)
            return carry

        lax.fori_loop(0, j_last, full_chunk, 0)
        score_chunk(j_last, True)

        def select(count_keys):
            def bit_step(b, cur):
                bit = lax.shift_left(jnp.int32(1), jnp.int32(31) - b)
                cand = (cur | bit) ^ jnp.int32(INT_MIN)
                cnt = jnp.sum(count_keys(cand), axis=0, keepdims=True)
                return jnp.where(cnt >= float(n_sel), cur | bit, cur)

            cur = lax.fori_loop(0, 32, bit_step, jnp.zeros((1, qb), jnp.int32))
            return jnp.maximum(cur ^ jnp.int32(INT_MIN), jnp.int32(INT_MIN + 1))

        def part_count(keys, cand):
            ind = jnp.where(keys >= cand, 1.0, 0.0)
            return jnp.sum(ind.reshape(-1, DSA_ACC_ROWS, qb), axis=0)

        def count_top(cand):
            return lax.fori_loop(0, DSA_TOP, lambda lvl, a: a + part_count(top_s[lvl], cand),
                                 jnp.zeros((DSA_ACC_ROWS, qb), F32))

        def count_all(cand):
            return lax.fori_loop(0, n_kc, lambda kc, a: a + part_count(key_s[kc], cand),
                                 jnp.zeros((DSA_ACC_ROWS, qb), F32))

        def settle(count_keys):
            thr = select(count_keys)
            thr_s[...] = thr
            above = jnp.sum(count_keys(thr + 1), axis=0, keepdims=True)
            upto = jnp.sum(count_keys(thr), axis=0, keepdims=True)
            need_s[...] = float(n_sel) - above
            over_s[...] = upto - float(n_sel)
            return thr

        thr = settle(count_top)
        hidden = jnp.where(top_s[DSA_TOP - 1] >= thr, 1.0, 0.0)

        @pl.when(jnp.max(hidden) > 0.0)
        def _():
            settle(count_all)

        @pl.when(jnp.max(over_s[...]) > 0.0)
        def _():
            thr = thr_s[...]
            need = need_s[...]
            sub = LANES
            earlier = jnp.where(lax.broadcasted_iota(jnp.int32, (sub, sub), 1)
                                < lax.broadcasted_iota(jnp.int32, (sub, sub), 0), 1.0, 0.0).astype(BF16)

            def drop_surplus(kc, seen):
                keys = key_s[kc]
                tie = keys == thr
                tie_f = jnp.where(tie, 1.0, 0.0)
                ranks = []
                for r in range(kb // sub):
                    part = tie_f[r * sub:(r + 1) * sub]
                    ranks.append(seen + jnp.dot(earlier, part.astype(BF16),
                                                preferred_element_type=F32))
                    seen = seen + jnp.sum(part, axis=0, keepdims=True)
                rank = jnp.concatenate(ranks, axis=0)
                key_s[kc] = jnp.where(tie & (rank >= need), thr - 1, keys)
                return seen

            lax.fori_loop(0, n_kc, drop_surplus, jnp.zeros((1, qb), F32))

        m_s[...] = jnp.full_like(m_s, M_INIT)
        acc_s[...] = jnp.zeros_like(acc_s)
        bdq_s[...] = jnp.zeros_like(bdq_s)
        for h in range(N_HEADS):
            g, hh = divmod(h, DSA_HEAD_GROUP)
            bdq_s[g, hh * HEAD_DIM:(hh + 1) * HEAD_DIM, hh * qb:(hh + 1) * qb] = (
                qt_ref[h] * jnp.asarray(HEAD_DIM ** -0.5, BF16))

    mask = key_s[kj] >= thr_s[...]
    hg = DSA_HEAD_GROUP
    st_all = [jnp.dot(k_ref[:, g * hg * HEAD_DIM:(g + 1) * hg * HEAD_DIM], bdq_s[g],
                      preferred_element_type=F32) for g in range(N_HEADS // hg)]
    for g in range(N_HEADS // hg):
        st_g = st_all[g]
        for hh in range(hg):
            h = g * hg + hh
            st = jnp.where(mask, st_g[:, hh * qb:(hh + 1) * qb], NEG_BIG)
            m_old = m_s[h]
            part = jnp.max(st.reshape(kb // DSA_ACC_ROWS, DSA_ACC_ROWS, qb), axis=0)
            m_new = jnp.maximum(m_old, jnp.max(part, axis=0, keepdims=True))
            p = jnp.exp((st - m_new).astype(BF16))
            acc_s[h] = jnp.exp(m_old - m_new) * acc_s[h] + jnp.dot(
                vt_ref[h], p, preferred_element_type=F32)
            m_s[h] = m_new

    @pl.when(kj == j_last)
    def _():
        for h in range(N_HEADS):
            acc = acc_s[h]
            o_ref[h] = acc[:HEAD_DIM] / acc[HEAD_DIM:HEAD_DIM + 1]


def _dsa(qt, k, vt, iqt, ik, iwt, n_sel, to_cast=()):
    t = ik.shape[0]
    qb, kb = DSA_QB, DSA_KB
    assert n_sel <= DSA_GROUPS and kb % DSA_GROUPS == 0
    nq, nk = t // qb, t // kb
    pairs = [(i, j) for i in range(nq) for j in range((i * qb + qb - 1) // kb + 1)]
    qi = jnp.asarray(np.array([p[0] for p in pairs], np.int32))
    kj = jnp.asarray(np.array([p[1] for p in pairs], np.int32))
    n_slices = 1 << (len(pairs).bit_length() - 1)
    slabs = [w.reshape(n_slices, -1, w.shape[-1]) for w in to_cast]
    cast_specs = [pl.BlockSpec((1,) + w.shape[1:],
                               lambda s, qi, kj: (jnp.minimum(s, n_slices - 1), 0, 0))
                  for w in slabs]
    grid_spec = pltpu.PrefetchScalarGridSpec(
        num_scalar_prefetch=2,
        grid=(len(pairs),),
        in_specs=[pl.BlockSpec((N_HEADS, HEAD_DIM, qb), lambda s, qi, kj: (0, 0, qi[s])),
                  pl.BlockSpec((kb, D_GROUP), lambda s, qi, kj: (kj[s], 0)),
                  pl.BlockSpec((N_HEADS, DSA_V_ROWS, kb), lambda s, qi, kj: (0, 0, kj[s])),
                  pl.BlockSpec((D_IDX, qb), lambda s, qi, kj: (0, qi[s])),
                  pl.BlockSpec((t, IDX_DIM), lambda s, qi, kj: (0, 0)),
                  pl.BlockSpec((N_IDX_HEADS, qb), lambda s, qi, kj: (0, qi[s]))] + cast_specs,
        out_specs=[pl.BlockSpec((N_HEADS, HEAD_DIM, qb), lambda s, qi, kj: (0, 0, qi[s]))]
        + cast_specs,
        scratch_shapes=[pltpu.VMEM((nk, kb, qb), jnp.int32),
                        pltpu.VMEM((1, qb), jnp.int32),
                        pltpu.VMEM((1, qb), F32),
                        pltpu.VMEM((1, qb), F32),
                        pltpu.VMEM((N_HEADS, 1, qb), F32),
                        pltpu.VMEM((N_HEADS, DSA_V_ROWS, qb), F32),
                        pltpu.VMEM((N_HEADS // DSA_HEAD_GROUP, DSA_HEAD_GROUP * HEAD_DIM,
                                    DSA_HEAD_GROUP * qb), BF16),
                        pltpu.VMEM((DSA_TOP, DSA_GROUPS, qb), jnp.int32)])
    outs = pl.pallas_call(
        functools.partial(_dsa_kernel, n_sel=n_sel, n_cast=len(slabs)),
        grid_spec=grid_spec,
        out_shape=[jax.ShapeDtypeStruct((N_HEADS, HEAD_DIM, t), F32)]
        + [jax.ShapeDtypeStruct(w.shape, BF16) for w in slabs],
        compiler_params=pltpu.CompilerParams(dimension_semantics=("arbitrary",),
                                             vmem_limit_bytes=VMEM_LIMIT),
        name="dsa",
    )(qi, kj, qt, k, vt, iqt, ik, iwt, *slabs)
    return outs[0], [o.reshape(w.shape) for o, w in zip(outs[1:], to_cast)]


def _out_proj_kernel(x_ref, att_ref, rw_ref, wo_ref, n2w_ref, rwt_ref, rb_ref,
                     h_ref, xn_ref, gate_ref):
    tm = x_ref.shape[0]
    att = att_ref[...].reshape(D_GROUP, tm).T
    mix = jnp.concatenate([att, rw_ref[...]], axis=1)
    acc = x_ref[...] + _dot(mix, wo_ref[...])
    h_ref[...] = acc
    xn = acc * lax.rsqrt(jnp.mean(acc * acc, axis=-1, keepdims=True) + RMS_EPS) * n2w_ref[...]
    xn_ref[...] = xn.astype(BF16)
    logits = _dotf(xn, rwt_ref[...]) + rb_ref[...]
    lane = lax.broadcasted_iota(jnp.int32, logits.shape, 1)
    work = logits
    vals, hots = [], []
    for _ in range(TOP_K_EXPERTS):
        m = jnp.max(work, axis=1, keepdims=True)
        idx = jnp.min(jnp.where(work == m, lane, LANES), axis=1, keepdims=True)
        hot = lane == idx
        vals.append(m)
        hots.append(hot)
        work = jnp.where(hot, -jnp.inf, work)
    es = [jnp.exp(vv - vals[0]) for vv in vals]
    denom = es[0] + es[1] + es[2] + es[3]
    gates = jnp.zeros_like(logits)
    for e, hot in zip(es, hots):
        gates = gates + jnp.where(hot, e / denom, 0.0)
    gate_ref[...] = gates


def _out_proj(x2, att_t, rw, wo, n2w, rwt, rb, tm):
    t = x2.shape[0]
    full = lambda shape: pl.BlockSpec(shape, lambda i: (0,) * len(shape))
    row = lambda width: pl.BlockSpec((tm, width), lambda i: (i, 0))
    return pl.pallas_call(
        _out_proj_kernel,
        grid=(t // tm,),
        in_specs=[row(D_MODEL), pl.BlockSpec((N_HEADS, HEAD_DIM, tm), lambda i: (0, 0, i)),
                  row(D_GROUP), full((2 * D_GROUP, D_MODEL)), full((1, D_MODEL)),
                  full((D_MODEL, LANES)), full((1, LANES))],
        out_specs=[row(D_MODEL), row(D_MODEL), row(LANES)],
        out_shape=(jax.ShapeDtypeStruct((t, D_MODEL), F32),
                   jax.ShapeDtypeStruct((t, D_MODEL), BF16),
                   jax.ShapeDtypeStruct((t, LANES), F32)),
        compiler_params=pltpu.CompilerParams(dimension_semantics=("arbitrary",),
                                             vmem_limit_bytes=VMEM_LIMIT),
        name="out_proj",
    )(x2, att_t, rw, wo, n2w, rwt, rb)


MOE_TM = 1024
MOE_RB = 144
MOE_FC = 512
MOE_EXPERTS_PER_STEP = 2


def _route_kernel(gate_ref, rankt_ref, gatet_ref, cnt_ref):
    tm = gate_ref.shape[0]
    gates = gate_ref[...]
    hot = gates > 0.0
    ind = jnp.where(hot, 1.0, 0.0)
    before = (lax.broadcasted_iota(jnp.int32, (tm, tm), 1)
              < lax.broadcasted_iota(jnp.int32, (tm, tm), 0))
    rank = jnp.dot(jnp.where(before, 1.0, 0.0).astype(BF16), ind.astype(BF16),
                   preferred_element_type=F32)
    rankt_ref[...] = jnp.where(hot, rank, -1.0).T
    gatet_ref[...] = gates.T
    cnt = jnp.sum(ind, axis=0, keepdims=True)
    cnt_ref[...] = jnp.broadcast_to(cnt[None], cnt_ref.shape)


def _route(gates, tm):
    t = gates.shape[0]
    return pl.pallas_call(
        _route_kernel,
        grid=(t // tm,),
        in_specs=[pl.BlockSpec((tm, LANES), lambda i: (i, 0))],
        out_specs=[pl.BlockSpec((LANES, tm), lambda i: (0, i)),
                   pl.BlockSpec((LANES, tm), lambda i: (0, i)),
                   pl.BlockSpec((1, 8, LANES), lambda i: (i, 0, 0))],
        out_shape=(jax.ShapeDtypeStruct((LANES, t), F32),
                   jax.ShapeDtypeStruct((LANES, t), F32),
                   jax.ShapeDtypeStruct((t // tm, 8, LANES), F32)),
        compiler_params=pltpu.CompilerParams(dimension_semantics=("arbitrary",),
                                             vmem_limit_bytes=VMEM_LIMIT),
        name="route",
    )(gates)


def _moe_kernel(cnt_ref, h_ref, xn_ref, rankt_ref, gatet_ref, wgu_ref, bgu_ref,
                wdn_ref, bdn_ref, o_ref):
    i = pl.program_id(0)
    tm = h_ref.shape[0]

    @pl.when(pl.program_id(1) == 0)
    def _():
        o_ref[...] = h_ref[...]

    for sub in range(MOE_EXPERTS_PER_STEP):
        _moe_expert(pl.program_id(1) * MOE_EXPERTS_PER_STEP + sub, sub, i, tm, cnt_ref, xn_ref,
                    rankt_ref, gatet_ref, wgu_ref, bgu_ref, wdn_ref, bdn_ref, o_ref)


def _moe_expert(e, slot, i, tm, cnt_ref, xn_ref, rankt_ref, gatet_ref, wgu_ref, bgu_ref,
                wdn_ref, bdn_ref, o_ref):
    n_rows = cnt_ref[i * N_EXPERTS + e]
    n_blocks = (n_rows + MOE_RB - 1) // MOE_RB
    rank_row = rankt_ref[pl.ds(e, 1), :]
    gate_row = gatet_ref[pl.ds(e, 1), :]

    def row_block(bi, carry):
        r0 = (bi * MOE_RB).astype(F32)
        rowid = lax.broadcasted_iota(jnp.int32, (MOE_RB, tm), 0).astype(F32) + r0
        hit = rank_row == rowid
        sel = jnp.where(hit, 1.0, 0.0).astype(BF16)
        gate = jnp.sum(jnp.where(hit, gate_row, 0.0), axis=1, keepdims=True)
        xb = jnp.dot(sel, xn_ref[...], preferred_element_type=F32).astype(BF16)
        y = jnp.zeros((MOE_RB, D_MODEL), F32)
        ups = []
        for fc in range(D_FF // MOE_FC):
            c0 = fc * MOE_FC
            ups.append((jnp.dot(xb, wgu_ref[slot, :, c0:c0 + MOE_FC], preferred_element_type=F32),
                        jnp.dot(xb, wgu_ref[slot, :, D_FF + c0:D_FF + c0 + MOE_FC],
                                preferred_element_type=F32)))
        for fc in range(D_FF // MOE_FC):
            c0 = fc * MOE_FC
            hg = ups[fc][0] + bgu_ref[slot, :, c0:c0 + MOE_FC]
            hl = ups[fc][1] + bgu_ref[slot, :, D_FF + c0:D_FF + c0 + MOE_FC]
            glu = jnp.minimum(hg, SWIGLU_LIMIT)
            lin = jnp.clip(hl, -SWIGLU_LIMIT, SWIGLU_LIMIT)
            act = (lin + 1.0) * glu * jax.nn.sigmoid(SWIGLU_ALPHA * glu)
            y = y + jnp.dot(act.astype(BF16), wdn_ref[slot, c0:c0 + MOE_FC, :],
                            preferred_element_type=F32)
        y = ((y + bdn_ref[slot]) * gate).astype(BF16)
        o_ref[...] += _dot_tn(sel, y)
        return carry

    lax.fori_loop(0, n_blocks, row_block, 0)


def _moe(counts, h1, xn2, rank_t, gate_t, wgu, bgu, wdn, bdn, tm):
    t = h1.shape[0]
    eps = MOE_EXPERTS_PER_STEP
    row = lambda width: pl.BlockSpec((tm, width), lambda i, e, c: (i, 0))
    grid_spec = pltpu.PrefetchScalarGridSpec(
        num_scalar_prefetch=1,
        grid=(t // tm, N_EXPERTS // eps),
        in_specs=[row(D_MODEL), row(D_MODEL),
                  pl.BlockSpec((LANES, tm), lambda i, e, c: (0, i)),
                  pl.BlockSpec((LANES, tm), lambda i, e, c: (0, i)),
                  pl.BlockSpec((eps, D_MODEL, 2 * D_FF), lambda i, e, c: (e, 0, 0)),
                  pl.BlockSpec((eps, 1, 2 * D_FF), lambda i, e, c: (e, 0, 0)),
                  pl.BlockSpec((eps, D_FF, D_MODEL), lambda i, e, c: (e, 0, 0)),
                  pl.BlockSpec((eps, 1, D_MODEL), lambda i, e, c: (e, 0, 0))],
        out_specs=row(D_MODEL))
    return pl.pallas_call(
        _moe_kernel,
        grid_spec=grid_spec,
        out_shape=jax.ShapeDtypeStruct((t, D_MODEL), F32),
        compiler_params=pltpu.CompilerParams(dimension_semantics=("arbitrary", "arbitrary"),
                                             vmem_limit_bytes=VMEM_LIMIT),
        name="moe",
    )(counts, h1, xn2, rank_t, gate_t, wgu, bgu, wdn, bdn)


def _rope_tables():
    lane = jnp.arange(LANES)

    def rows(period, rot):
        half = rot // 2
        inv_freq = ROPE_THETA ** (-jnp.arange(half, dtype=F32) / half)
        jm = lane % period
        freq = jnp.where(jm < rot, inv_freq[jm % half], 0.0)
        sign = jnp.where(jm < half, -1.0, jnp.where(jm < rot, 1.0, 0.0))
        first = (jm < half).astype(F32)
        return [freq, sign, first]

    idx_rows = rows(IDX_DIM, IDX_DIM // 4)
    only_key = (lane < IDX_DIM).astype(F32)
    return jnp.stack(rows(HEAD_DIM, HEAD_DIM // 4) + idx_rows
                     + [only_key, jnp.zeros_like(only_key)]).astype(F32)


def _group_matrix():
    g = np.arange(D_GROUP) // HEAD_DIM
    return jnp.asarray((g[:, None] == g[None, :]).astype(np.float32), dtype=BF16)


def _pad_rows(w, r0, rows):
    return jnp.zeros((rows, w.shape[1]), w.dtype).at[r0:r0 + w.shape[0]].set(w)


def kernel(x, positions, norm1_w, w_in, q_norm_w, k_norm_w, rwkv_mu, rwkv_w0, rwkv_w2, rwkv_a0,
           rwkv_a2, rwkv_g2, rwkv_k_k, rwkv_k_a, rwkv_r_k, rwkv_ln_w, rwkv_ln_b, w_out, norm2_w,
           router_w, router_b, exp_w_gu, exp_b_gu, exp_w_down, exp_b_down):
    b, t, _ = x.shape
    assert b == 1 and w_in.shape[0] == 1, "single sequence, single layer"
    assert t % DSA_KB == 0 and t % DSA_QB == 0 and t % MOE_TM == 0
    x2 = x[0]
    pos_f = positions[0].astype(F32)[:, None]
    n_sel = min(TOPK_MAX, t // 4)

    w = w_in[0]
    a0 = 3 * D_GROUP
    att_cols = a0 + D_IDX + IDX_DIM + N_IDX_HEADS
    w_att, w_rw = w[:, :att_cols], w[:, att_cols:]
    w_sm = jnp.zeros((D_MODEL, LANES), F32).at[:, :IDX_DIM + N_IDX_HEADS].set(w_att[:, a0 + D_IDX:])
    w_packed = jnp.concatenate(
        [w_att[:, :a0], w_rw[:, :a0], w_att[:, a0:a0 + D_IDX], w_sm, w_rw[:, a0:]],
        axis=1).astype(BF16)
    mu = rwkv_mu[0][None, :]
    tile8 = lambda z: jnp.tile(z, N_HEADS)[None, :]
    gsum = _group_matrix()

    tm = 256
    q, k, v, rr, rk, rv, iq, sm, lora = _in_proj(
        x2, pos_f, norm1_w, w_packed, mu, tile8(q_norm_w[0]), tile8(k_norm_w[0]), gsum,
        _rope_tables(), tm)

    vec = lambda z: z.reshape(1, D_GROUP)
    rw = _rwkv(rr, rk, rv, lora, vec(rwkv_w0[0]),
               _pad_rows(rwkv_w2[0], 0, LANES), vec(rwkv_a0[0]),
               _pad_rows(rwkv_a2[0], D_DECAY_LORA, LANES),
               _pad_rows(rwkv_g2[0], D_DECAY_LORA + D_AAA_LORA, LANES),
               vec(rwkv_k_k[0]), vec(rwkv_k_a[0]), vec(rwkv_r_k[0]), vec(rwkv_ln_w[0]),
               vec(rwkv_ln_b[0]), gsum, tm)

    heads = lambda z: z.reshape(t, N_HEADS, HEAD_DIM)
    vt = jnp.concatenate([heads(v).transpose(1, 2, 0),
                          jnp.ones((N_HEADS, DSA_V_ROWS - HEAD_DIM, t), BF16)], axis=1)
    att_t, (w_gu_bf, w_dn_bf) = _dsa(
        heads(q).transpose(1, 2, 0), k, vt, iq.T, sm[:, :IDX_DIM].astype(BF16),
        sm[:, IDX_DIM:IDX_DIM + N_IDX_HEADS].T, n_sel, to_cast=(exp_w_gu[0], exp_w_down[0]))

    rwt = jnp.zeros((D_MODEL, LANES), F32).at[:, :N_EXPERTS].set(router_w[0])
    rb = jnp.full((1, LANES), NEG_BIG, F32).at[0, :N_EXPERTS].set(router_b[0])
    h1, xn2, gates = _out_proj(x2, att_t, rw, w_out[0].astype(BF16), norm2_w, rwt, rb, tm)

    rank_t, gate_t, cnt = _route(gates, MOE_TM)
    counts = cnt[:, 0, :N_EXPERTS].astype(jnp.int32).reshape(-1)
    out = _moe(counts, h1, xn2, rank_t, gate_t, w_gu_bf, exp_b_gu[0][:, None, :], w_dn_bf,
               exp_b_down[0][:, None, :], MOE_TM)
    return out[None]
```

```python
import functools

import jax
import jax.numpy as jnp
import numpy as np
from jax import lax
from jax.experimental import pallas as pl
from jax.experimental.pallas import tpu as pltpu

F32 = jnp.float32
BF16 = jnp.bfloat16
HIGHEST = lax.Precision.HIGHEST

D_MODEL = 1024
HEAD_DIM = 64
N_HEADS = 8
D_GROUP = N_HEADS * HEAD_DIM
ROPE_THETA = 500000.0
N_IDX_HEADS = 8
IDX_DIM = 32
D_IDX = N_IDX_HEADS * IDX_DIM
TOPK_MAX = 256
D_DECAY_LORA = 32
D_AAA_LORA = 32
D_GATE_LORA = 64
RWKV_GN_EPS = 64e-5
N_EXPERTS = 32
TOP_K_EXPERTS = 4
D_FF = 1024
SWIGLU_LIMIT = 7.0
SWIGLU_ALPHA = 1.702
RMS_EPS = 1e-6

LANES = 128
VMEM_LIMIT = 56 * 1024 * 1024

NEG_BIG = -1e30
INT_MIN = -(2 ** 31)


def _dot(a, b):
    return jnp.dot(a.astype(BF16), b.astype(BF16), preferred_element_type=F32)


def _dotf(a, b):
    return jnp.dot(a, b, preferred_element_type=F32, precision=HIGHEST)


def _dot3(a, b):
    a_hi = a.astype(BF16)
    b_hi = b.astype(BF16)
    a_lo = (a - a_hi.astype(F32)).astype(BF16)
    b_lo = (b - b_hi.astype(F32)).astype(BF16)
    dot = functools.partial(jnp.dot, preferred_element_type=F32)
    return dot(a_hi, b_hi) + dot(a_hi, b_lo) + dot(a_lo, b_hi)


def _dot_nt(a, b):
    return lax.dot_general(a.astype(BF16), b.astype(BF16), (((1,), (1,)), ((), ())),
                           preferred_element_type=F32)


def _dot_tn(a, b):
    return lax.dot_general(a.astype(BF16), b.astype(BF16), (((0,), (0,)), ((), ())),
                           preferred_element_type=F32)


def _group_sum(z, g_ref):
    hi = z.astype(BF16)
    lo = (z - hi.astype(F32)).astype(BF16)
    g = g_ref[...]
    return (jnp.dot(hi, g, preferred_element_type=F32)
            + jnp.dot(lo, g, preferred_element_type=F32))


_C_Q, _C_K, _C_V, _C_RR, _C_RK, _C_RV = (i * D_GROUP for i in range(6))
_C_IQ = 6 * D_GROUP
_C_SM = _C_IQ + D_IDX
_C_LORA = _C_SM + LANES
D_IN_PACKED = _C_LORA + LANES
D_SHIFT = 3 * D_GROUP + LANES


def _in_proj_kernel(x_ref, pos_ref, n1w_ref, w_ref, mu_ref, qnw_ref, knw_ref, g_ref, rope_ref,
                    q_ref, k_ref, v_ref, rr_ref, rk_ref, rv_ref, iq_ref, sm_ref, lora_ref,
                    carry_ref):
    tm = x_ref.shape[0]

    @pl.when(pl.program_id(0) == 0)
    def _():
        carry_ref[...] = jnp.zeros_like(carry_ref)

    x = x_ref[...]
    xn = x * lax.rsqrt(jnp.mean(x * x, axis=-1, keepdims=True) + RMS_EPS) * n1w_ref[...]
    xb = xn.astype(BF16)
    pos = pos_ref[...]
    rope = rope_ref[...]

    def tables(frow, srow):
        ang = pos * rope[frow:frow + 1, :]
        return jnp.cos(ang), jnp.sin(ang) * rope[srow:srow + 1, :]

    def widen(z, reps):
        return jnp.concatenate([z] * reps, axis=1)

    def rotary(z, c, s, first_row, half):
        w = z.shape[1]
        first = jnp.concatenate([rope[first_row:first_row + 1, :]] * (w // LANES), axis=1) > 0.5
        partner = jnp.where(first, pltpu.roll(z, w - half, 1), pltpu.roll(z, half, 1))
        return z * c + partner * s

    def head_norm(z, w_row):
        ms = _group_sum(z * z, g_ref) * (1.0 / HEAD_DIM)
        return z * lax.rsqrt(ms + RMS_EPS) * w_row

    def proj(c0, width):
        return jnp.dot(xb, w_ref[:, c0:c0 + width], preferred_element_type=F32)

    def shift(z, c0):
        width = z.shape[1]
        row = lax.broadcasted_iota(jnp.int32, z.shape, 0)
        prev = jnp.where(row == 0, carry_ref[0:1, c0:c0 + width], pltpu.roll(z, 1, 0))
        carry_ref[0:1, c0:c0 + width] = z[tm - 1:tm, :]
        return z + (prev - z) * mu_ref[:, c0:c0 + width]

    cq, sq = (widen(z, D_GROUP // LANES) for z in tables(0, 1))
    q = rotary(head_norm(proj(_C_Q, D_GROUP), qnw_ref[...]), cq, sq, 2, HEAD_DIM // 8)
    q_ref[...] = q.astype(BF16)
    k = rotary(head_norm(proj(_C_K, D_GROUP), knw_ref[...]), cq, sq, 2, HEAD_DIM // 8)
    k_ref[...] = k.astype(BF16)
    v_ref[...] = proj(_C_V, D_GROUP).astype(BF16)

    rr_ref[...] = shift(proj(_C_RR, D_GROUP), 0)
    rk_ref[...] = shift(proj(_C_RK, D_GROUP), D_GROUP)
    rv_ref[...] = shift(proj(_C_RV, D_GROUP), 2 * D_GROUP)
    lora_ref[...] = shift(proj(_C_LORA, LANES), 3 * D_GROUP)

    ci, si = tables(3, 4)
    iq_ref[...] = rotary(proj(_C_IQ, D_IDX), widen(ci, D_IDX // LANES), widen(si, D_IDX // LANES),
                         5, IDX_DIM // 8)
    only_key = rope[6:7, :]
    sm_ref[...] = rotary(proj(_C_SM, LANES), ci * only_key + (1.0 - only_key), si * only_key,
                         5, IDX_DIM // 8)


def _in_proj(x2, pos_f, n1w, w_packed, mu_packed, qnw, knw, gmat, rope, tm):
    t = x2.shape[0]
    full = lambda shape: pl.BlockSpec(shape, lambda i: (0,) * len(shape))
    row = lambda width: pl.BlockSpec((tm, width), lambda i: (i, 0))
    out_shapes = (
        jax.ShapeDtypeStruct((t, D_GROUP), BF16),
        jax.ShapeDtypeStruct((t, D_GROUP), BF16),
        jax.ShapeDtypeStruct((t, D_GROUP), BF16),
        jax.ShapeDtypeStruct((t, D_GROUP), F32),
        jax.ShapeDtypeStruct((t, D_GROUP), F32),
        jax.ShapeDtypeStruct((t, D_GROUP), F32),
        jax.ShapeDtypeStruct((t, D_IDX), F32),
        jax.ShapeDtypeStruct((t, LANES), F32),
        jax.ShapeDtypeStruct((t, LANES), F32),
    )
    return pl.pallas_call(
        _in_proj_kernel,
        grid=(t // tm,),
        in_specs=[row(D_MODEL), row(1), full((1, D_MODEL)), full((D_MODEL, D_IN_PACKED)),
                  full((1, D_SHIFT)), full((1, D_GROUP)), full((1, D_GROUP)),
                  full((D_GROUP, D_GROUP)), full((8, LANES))],
        out_specs=[row(D_GROUP)] * 6 + [row(D_IDX), row(LANES), row(LANES)],
        out_shape=out_shapes,
        scratch_shapes=[pltpu.VMEM((8, D_SHIFT), F32)],
        compiler_params=pltpu.CompilerParams(dimension_semantics=("arbitrary",),
                                             vmem_limit_bytes=VMEM_LIMIT),
        name="in_proj",
    )(x2, pos_f, n1w, w_packed, mu_packed, qnw, knw, gmat, rope)


RWKV_CHUNK = 64
RWKV_UNROLL = 4
PAIR = 2 * HEAD_DIM


def _rwkv_kernel(r_ref, k_ref, v_ref, lora_ref, w0_ref, w2_ref, a0_ref, a2_ref, g2_ref,
                 kk_ref, ka_ref, rk_ref, lnw_ref, lnb_ref, g_ref,
                 o_ref,
                 s_ref, ld_s, r_s, k2_s, b_s, kk_s, y_s):
    tm = r_ref.shape[0]
    n_chunks = tm // RWKV_CHUNK
    n_pairs = D_GROUP // PAIR
    c = RWKV_CHUNK

    @pl.when(pl.program_id(0) == 0)
    def _():
        s_ref[...] = jnp.zeros_like(s_ref)

    lora = lora_ref[...]
    r = r_ref[...]
    k = k_ref[...]
    v = v_ref[...]
    zarg = w0_ref[...] + _dot3(jnp.tanh(lora), w2_ref[...])
    sp = jnp.maximum(-zarg, 0.0) + jnp.log1p(jnp.exp(-jnp.abs(zarg)))
    ld_s[...] = -jnp.exp(-sp - 0.5)
    a = jax.nn.sigmoid(a0_ref[...] + _dot3(lora, a2_ref[...]))
    g = _dot3(jax.nn.sigmoid(lora), g2_ref[...])
    kk = k * kk_ref[...]
    kk = kk * lax.rsqrt(jnp.maximum(_group_sum(kk * kk, g_ref), 1e-24))
    k2 = k * (1.0 + (a - 1.0) * ka_ref[...])
    bonus = _group_sum(r * k2 * rk_ref[...], g_ref) * v
    r_s[...] = r
    k2_s[...] = k2
    kk_s[...] = kk
    b_s[...] = kk * a

    row = lax.broadcasted_iota(jnp.int32, (2 * c, 2 * c), 0)
    col = lax.broadcasted_iota(jnp.int32, (2 * c, 2 * c), 1)
    same_head = (row >= c) == (col >= c)
    strict = same_head & (col < row)
    incl = same_head & (col <= row)
    eye = (row == col).astype(F32)
    tri = (lax.broadcasted_iota(jnp.int32, (c, c), 1)
           <= lax.broadcasted_iota(jnp.int32, (c, c), 0)).astype(F32)
    lane = lax.broadcasted_iota(jnp.int32, (c, PAIR), 1)
    head0 = lane < HEAD_DIM

    def stack(z):
        return jnp.concatenate([jnp.where(head0, z, 0.0), jnp.where(head0, 0.0, z)], axis=0)

    def chunk_body(ci, carry):
        c2 = 2 * c
        units = [(cc, p) for cc in range(RWKV_UNROLL) for p in range(n_pairs)]
        uid = range(len(units))
        rows = [pl.multiple_of((ci * RWKV_UNROLL + cc) * c, c) for cc in range(RWKV_UNROLL)]
        ld_all = [ld_s[pl.ds(r0, c), :] for r0 in rows]
        cum_all = [_dotf(tri, ld) for ld in ld_all]
        rt, kt, bt, kp, vs, lhs, rhs, gam_end = [], [], [], [], [], [], [], []
        for cc, p in units:
            r0 = rows[cc]
            cols = slice(p * PAIR, (p + 1) * PAIR)
            ld = ld_all[cc][:, cols]
            cum = cum_all[cc][:, cols]
            gam = jnp.exp(cum)
            inv = jnp.exp(-cum)
            gam_prev = jnp.exp(cum - ld)
            gam_end.append(gam[c - 1:c, :])
            rt.append(stack(r_s[pl.ds(r0, c), cols] * gam))
            kt_raw = k2_s[pl.ds(r0, c), cols] * inv
            bt_raw = b_s[pl.ds(r0, c), cols] * inv
            kt.append(stack(kt_raw))
            bt.append(stack(bt_raw))
            kp.append(stack(kk_s[pl.ds(r0, c), cols] * gam_prev))
            vs.append(stack(v_ref[pl.ds(r0, c), cols]))
            lhs.append(jnp.concatenate([kp[-1], rt[-1]], axis=0))
            rhs.append(jnp.concatenate([bt_raw, bt_raw, kt_raw, kt_raw], axis=0))
        aa = [_dot_nt(lhs[u], rhs[u]) for u in uid]
        a_kk = [jnp.where(strict, aa[u][:c2, c2:], 0.0) for u in uid]
        a_rb = [jnp.where(incl, aa[u][c2:, :c2], 0.0) for u in uid]
        a_rk = [jnp.where(incl, aa[u][c2:, c2:], 0.0) for u in uid]
        n = [-jnp.where(strict, aa[u][:c2, :c2], 0.0) for u in uid]
        prod = [eye + n[u] for u in uid]
        n = [_dot(n[u], n[u]) for u in uid]
        for _ in range(int(np.log2(c)) - 2):
            both = [_dot(jnp.concatenate([n[u], prod[u]], axis=0), n[u]) for u in uid]
            prod = [prod[u] + both[u][c2:] for u in uid]
            n = [both[u][:c2] for u in uid]
        av = [_dot(jnp.concatenate([a_kk[u], a_rk[u]], axis=0), vs[u]) for u in uid]
        tinv = [prod[u] + _dot(prod[u], n[u]) for u in uid]
        wu = [_dot(tinv[u], jnp.concatenate([kp[u], av[u][:c2]], axis=1)) for u in uid]
        rb = [_dot(a_rb[u], wu[u]) for u in uid]
        tn = [_dot_tn(jnp.concatenate([wu[u], vs[u]], axis=1),
                      jnp.concatenate([bt[u], kt[u]], axis=1)) for u in uid]
        r2 = [rt[u] - rb[u][:, :PAIR] for u in uid]
        y2 = [av[u][c2:] - rb[u][:, PAIR:] for u in uid]
        s_mix = [(eye - tn[u][:PAIR, :PAIR]) * gam_end[u] for u in uid]
        s_add = [(tn[u][2 * PAIR:, PAIR:] - tn[u][PAIR:2 * PAIR, :PAIR]) * gam_end[u] for u in uid]
        state = [s_ref[p] for p in range(n_pairs)]
        for cc in range(RWKV_UNROLL):
            us = [cc * n_pairs + p for p in range(n_pairs)]
            ys = [_dot_nt(r2[u], state[p]) + y2[u] for p, u in enumerate(us)]
            state = [_dot(state[p], s_mix[u]) + s_add[u] for p, u in enumerate(us)]
            for p in range(n_pairs):
                y_s[pl.ds(rows[cc], c), p * PAIR:(p + 1) * PAIR] = ys[p][:c] + ys[p][c:]
        for p in range(n_pairs):
            s_ref[p] = state[p]
        return carry

    lax.fori_loop(0, n_chunks // RWKV_UNROLL, chunk_body, 0)

    y = y_s[...]
    mean = _group_sum(y, g_ref) * (1.0 / HEAD_DIM)
    yc = y - mean
    var = _group_sum(yc * yc, g_ref) * (1.0 / HEAD_DIM)
    yn = yc * lax.rsqrt(var + RWKV_GN_EPS) * lnw_ref[...] + lnb_ref[...]
    o_ref[...] = (yn + bonus) * g


def _rwkv(rr, rk, rv, lora, w0, w2p, a0, a2p, g2p, k_k, k_a, r_k, ln_w, ln_b, gsum, tm):
    t = rr.shape[0]
    full = lambda shape: pl.BlockSpec(shape, lambda i: (0,) * len(shape))
    row = lambda width: pl.BlockSpec((tm, width), lambda i: (i, 0))
    vec = full((1, D_GROUP))
    big = pltpu.VMEM((tm, D_GROUP), F32)
    return pl.pallas_call(
        _rwkv_kernel,
        grid=(t // tm,),
        in_specs=[row(D_GROUP), row(D_GROUP), row(D_GROUP), row(LANES),
                  vec, full((LANES, D_GROUP)), vec, full((LANES, D_GROUP)),
                  full((LANES, D_GROUP)), vec, vec, vec, vec, vec, full((D_GROUP, D_GROUP))],
        out_specs=row(D_GROUP),
        out_shape=jax.ShapeDtypeStruct((t, D_GROUP), F32),
        scratch_shapes=[pltpu.VMEM((D_GROUP // PAIR, PAIR, PAIR), F32),
                        big, big, big, big, big, big],
        compiler_params=pltpu.CompilerParams(dimension_semantics=("arbitrary",),
                                             vmem_limit_bytes=VMEM_LIMIT),
        name="rwkv",
    )(rr, rk, rv, lora, w0, w2p, a0, a2p, g2p, k_k, k_a, r_k, ln_w, ln_b, gsum)


DSA_QB = 256
DSA_KB = 512
M_INIT = -5e29
DSA_HEAD_GROUP = 4
DSA_ACC_ROWS = 32
DSA_GROUPS = 256
DSA_TOP = 10
DSA_V_ROWS = HEAD_DIM + 16


def _dsa_kernel(qi_ref, kj_ref, qt_ref, k_ref, vt_ref, iqt_ref, ik_ref, iwt_ref, *rest, n_sel,
                n_cast):
    cast_in, o_ref, cast_out = rest[:n_cast], rest[n_cast], rest[n_cast + 1:2 * n_cast + 1]
    key_s, thr_s, need_s, over_s, m_s, acc_s, bdq_s, top_s = rest[2 * n_cast + 1:]
    for src, dst in zip(cast_in, cast_out):
        dst[...] = src[...].astype(BF16)
    qb = qt_ref.shape[2]
    kb = k_ref.shape[0]
    step = pl.program_id(0)
    qi = qi_ref[step]
    kj = kj_ref[step]
    q0 = qi * qb
    j_last = (q0 + qb - 1) // kb
    n_kc = j_last + 1
    idx_scale = float((IDX_DIM * N_IDX_HEADS) ** -0.5)

    def causal(jblk):
        s_pos = jblk * kb + lax.broadcasted_iota(jnp.int32, (kb, qb), 0)
        t_pos = q0 + lax.broadcasted_iota(jnp.int32, (kb, qb), 1)
        return s_pos <= t_pos

    @pl.when(kj == 0)
    def _():
        iqt = iqt_ref[...].astype(BF16)
        iwt = iwt_ref[...] * idx_scale

        def score_chunk(kc, diagonal):
            ik = ik_ref[pl.ds(pl.multiple_of(kc * kb, kb), kb), :]
            score = jnp.zeros((kb, qb), F32)
            for h in range(N_IDX_HEADS):
                d = jnp.dot(ik, iqt[h * IDX_DIM:(h + 1) * IDX_DIM, :],
                            preferred_element_type=F32)
                score = score + jnp.maximum(d, 0.0) * iwt[h:h + 1, :]
            bits = pltpu.bitcast(score, jnp.int32)
            keys = jnp.where(bits < 0, bits ^ jnp.int32(0x7FFFFFFF), bits)
            if diagonal:
                keys = jnp.where(causal(kc), keys, jnp.int32(INT_MIN))
            key_s[kc] = keys
            xs = [keys[r * DSA_GROUPS:(r + 1) * DSA_GROUPS] for r in range(kb // DSA_GROUPS)]
            for lvl in range(DSA_TOP):
                s = top_s[lvl]
                for r in range(len(xs)):
                    s, xs[r] = jnp.maximum(s, xs[r]), jnp.minimum(s, xs[r])
                top_s[lvl] = s

        top_s[...] = jnp.full_like(top_s, INT_MIN)

        def full_chunk(kc, carry):
            score_chunk(kc, False)
            return carry

        lax.fori_loop(0, j_last, full_chunk, 0)
        score_chunk(j_last, True)

        def select(count_keys):
            def bit_step(b, cur):
                bit = lax.shift_left(jnp.int32(1), jnp.int32(31) - b)
                cand = (cur | bit) ^ jnp.int32(INT_MIN)
                cnt = jnp.sum(count_keys(cand), axis=0, keepdims=True)
                return jnp.where(cnt >= float(n_sel), cur | bit, cur)

            cur = lax.fori_loop(0, 32, bit_step, jnp.zeros((1, qb), jnp.int32))
            return jnp.maximum(cur ^ jnp.int32(INT_MIN), jnp.int32(INT_MIN + 1))

        def part_count(keys, cand):
            ind = jnp.where(keys >= cand, 1.0, 0.0)
            return jnp.sum(ind.reshape(-1, DSA_ACC_ROWS, qb), axis=0)

        def count_top(cand):
            return lax.fori_loop(0, DSA_TOP, lambda lvl, a: a + part_count(top_s[lvl], cand),
                                 jnp.zeros((DSA_ACC_ROWS, qb), F32))

        def count_all(cand):
            return lax.fori_loop(0, n_kc, lambda kc, a: a + part_count(key_s[kc], cand),
                                 jnp.zeros((DSA_ACC_ROWS, qb), F32))

        def settle(count_keys):
            thr = select(count_keys)
            thr_s[...] = thr
            above = jnp.sum(count_keys(thr + 1), axis=0, keepdims=True)
            upto = jnp.sum(count_keys(thr), axis=0, keepdims=True)
            need_s[...] = float(n_sel) - above
            over_s[...] = upto - float(n_sel)
            return thr

        thr = settle(count_top)
        hidden = jnp.where(top_s[DSA_TOP - 1] >= thr, 1.0, 0.0)

        @pl.when(jnp.max(hidden) > 0.0)
        def _():
            settle(count_all)

        @pl.when(jnp.max(over_s[...]) > 0.0)
        def _():
            thr = thr_s[...]
            need = need_s[...]
            sub = LANES
            earlier = jnp.where(lax.broadcasted_iota(jnp.int32, (sub, sub), 1)
                                < lax.broadcasted_iota(jnp.int32, (sub, sub), 0), 1.0, 0.0).astype(BF16)

            def drop_surplus(kc, seen):
                keys = key_s[kc]
                tie = keys == thr
                tie_f = jnp.where(tie, 1.0, 0.0)
                ranks = []
                for r in range(kb // sub):
                    part = tie_f[r * sub:(r + 1) * sub]
                    ranks.append(seen + jnp.dot(earlier, part.astype(BF16),
                                                preferred_element_type=F32))
                    seen = seen + jnp.sum(part, axis=0, keepdims=True)
                rank = jnp.concatenate(ranks, axis=0)
                key_s[kc] = jnp.where(tie & (rank >= need), thr - 1, keys)
                return seen

            lax.fori_loop(0, n_kc, drop_surplus, jnp.zeros((1, qb), F32))

        m_s[...] = jnp.full_like(m_s, M_INIT)
        acc_s[...] = jnp.zeros_like(acc_s)
        bdq_s[...] = jnp.zeros_like(bdq_s)
        for h in range(N_HEADS):
            g, hh = divmod(h, DSA_HEAD_GROUP)
            bdq_s[g, hh * HEAD_DIM:(hh + 1) * HEAD_DIM, hh * qb:(hh + 1) * qb] = (
                qt_ref[h] * jnp.asarray(HEAD_DIM ** -0.5, BF16))

    mask = key_s[kj] >= thr_s[...]
    hg = DSA_HEAD_GROUP
    st_all = [jnp.dot(k_ref[:, g * hg * HEAD_DIM:(g + 1) * hg * HEAD_DIM], bdq_s[g],
                      preferred_element_type=F32) for g in range(N_HEADS // hg)]
    for g in range(N_HEADS // hg):
        st_g = st_all[g]
        for hh in range(hg):
            h = g * hg + hh
            st = jnp.where(mask, st_g[:, hh * qb:(hh + 1) * qb], NEG_BIG)
            m_old = m_s[h]
            part = jnp.max(st.reshape(kb // DSA_ACC_ROWS, DSA_ACC_ROWS, qb), axis=0)
            m_new = jnp.maximum(m_old, jnp.max(part, axis=0, keepdims=True))
            p = jnp.exp((st - m_new).astype(BF16))
            acc_s[h] = jnp.exp(m_old - m_new) * acc_s[h] + jnp.dot(
                vt_ref[h], p, preferred_element_type=F32)
            m_s[h] = m_new

    @pl.when(kj == j_last)
    def _():
        for h in range(N_HEADS):
            acc = acc_s[h]
            o_ref[h] = acc[:HEAD_DIM] / acc[HEAD_DIM:HEAD_DIM + 1]


def _dsa(qt, k, vt, iqt, ik, iwt, n_sel, to_cast=()):
    t = ik.shape[0]
    qb, kb = DSA_QB, DSA_KB
    assert n_sel <= DSA_GROUPS and kb % DSA_GROUPS == 0
    nq, nk = t // qb, t // kb
    pairs = [(i, j) for i in range(nq) for j in range((i * qb + qb - 1) // kb + 1)]
    qi = jnp.asarray(np.array([p[0] for p in pairs], np.int32))
    kj = jnp.asarray(np.array([p[1] for p in pairs], np.int32))
    n_slices = 1 << (len(pairs).bit_length() - 1)
    slabs = [w.reshape(n_slices, -1, w.shape[-1]) for w in to_cast]
    cast_specs = [pl.BlockSpec((1,) + w.shape[1:],
                               lambda s, qi, kj: (jnp.minimum(s, n_slices - 1), 0, 0))
                  for w in slabs]
    grid_spec = pltpu.PrefetchScalarGridSpec(
        num_scalar_prefetch=2,
        grid=(len(pairs),),
        in_specs=[pl.BlockSpec((N_HEADS, HEAD_DIM, qb), lambda s, qi, kj: (0, 0, qi[s])),
                  pl.BlockSpec((kb, D_GROUP), lambda s, qi, kj: (kj[s], 0)),
                  pl.BlockSpec((N_HEADS, DSA_V_ROWS, kb), lambda s, qi, kj: (0, 0, kj[s])),
                  pl.BlockSpec((D_IDX, qb), lambda s, qi, kj: (0, qi[s])),
                  pl.BlockSpec((t, IDX_DIM), lambda s, qi, kj: (0, 0)),
                  pl.BlockSpec((N_IDX_HEADS, qb), lambda s, qi, kj: (0, qi[s]))] + cast_specs,
        out_specs=[pl.BlockSpec((N_HEADS, HEAD_DIM, qb), lambda s, qi, kj: (0, 0, qi[s]))]
        + cast_specs,
        scratch_shapes=[pltpu.VMEM((nk, kb, qb), jnp.int32),
                        pltpu.VMEM((1, qb), jnp.int32),
                        pltpu.VMEM((1, qb), F32),
                        pltpu.VMEM((1, qb), F32),
                        pltpu.VMEM((N_HEADS, 1, qb), F32),
                        pltpu.VMEM((N_HEADS, DSA_V_ROWS, qb), F32),
                        pltpu.VMEM((N_HEADS // DSA_HEAD_GROUP, DSA_HEAD_GROUP * HEAD_DIM,
                                    DSA_HEAD_GROUP * qb), BF16),
                        pltpu.VMEM((DSA_TOP, DSA_GROUPS, qb), jnp.int32)])
    outs = pl.pallas_call(
        functools.partial(_dsa_kernel, n_sel=n_sel, n_cast=len(slabs)),
        grid_spec=grid_spec,
        out_shape=[jax.ShapeDtypeStruct((N_HEADS, HEAD_DIM, t), F32)]
        + [jax.ShapeDtypeStruct(w.shape, BF16) for w in slabs],
        compiler_params=pltpu.CompilerParams(dimension_semantics=("arbitrary",),
                                             vmem_limit_bytes=VMEM_LIMIT),
        name="dsa",
    )(qi, kj, qt, k, vt, iqt, ik, iwt, *slabs)
    return outs[0], [o.reshape(w.shape) for o, w in zip(outs[1:], to_cast)]


def _out_proj_kernel(x_ref, att_ref, rw_ref, wo_ref, n2w_ref, rwt_ref, rb_ref,
                     h_ref, xn_ref, gate_ref):
    tm = x_ref.shape[0]
    att = att_ref[...].reshape(D_GROUP, tm).T
    mix = jnp.concatenate([att, rw_ref[...]], axis=1)
    acc = x_ref[...] + _dot(mix, wo_ref[...])
    h_ref[...] = acc
    xn = acc * lax.rsqrt(jnp.mean(acc * acc, axis=-1, keepdims=True) + RMS_EPS) * n2w_ref[...]
    xn_ref[...] = xn.astype(BF16)
    logits = _dotf(xn, rwt_ref[...]) + rb_ref[...]
    lane = lax.broadcasted_iota(jnp.int32, logits.shape, 1)
    work = logits
    vals, hots = [], []
    for _ in range(TOP_K_EXPERTS):
        m = jnp.max(work, axis=1, keepdims=True)
        idx = jnp.min(jnp.where(work == m, lane, LANES), axis=1, keepdims=True)
        hot = lane == idx
        vals.append(m)
        hots.append(hot)
        work = jnp.where(hot, -jnp.inf, work)
    es = [jnp.exp(vv - vals[0]) for vv in vals]
    denom = es[0] + es[1] + es[2] + es[3]
    gates = jnp.zeros_like(logits)
    for e, hot in zip(es, hots):
        gates = gates + jnp.where(hot, e / denom, 0.0)
    gate_ref[...] = gates


def _out_proj(x2, att_t, rw, wo, n2w, rwt, rb, tm):
    t = x2.shape[0]
    full = lambda shape: pl.BlockSpec(shape, lambda i: (0,) * len(shape))
    row = lambda width: pl.BlockSpec((tm, width), lambda i: (i, 0))
    return pl.pallas_call(
        _out_proj_kernel,
        grid=(t // tm,),
        in_specs=[row(D_MODEL), pl.BlockSpec((N_HEADS, HEAD_DIM, tm), lambda i: (0, 0, i)),
                  row(D_GROUP), full((2 * D_GROUP, D_MODEL)), full((1, D_MODEL)),
                  full((D_MODEL, LANES)), full((1, LANES))],
        out_specs=[row(D_MODEL), row(D_MODEL), row(LANES)],
        out_shape=(jax.ShapeDtypeStruct((t, D_MODEL), F32),
                   jax.ShapeDtypeStruct((t, D_MODEL), BF16),
                   jax.ShapeDtypeStruct((t, LANES), F32)),
        compiler_params=pltpu.CompilerParams(dimension_semantics=("arbitrary",),
                                             vmem_limit_bytes=VMEM_LIMIT),
        name="out_proj",
    )(x2, att_t, rw, wo, n2w, rwt, rb)


MOE_TM = 1024
MOE_RB = 144
MOE_FC = 512
MOE_EXPERTS_PER_STEP = 2


def _route_kernel(gate_ref, rankt_ref, gatet_ref, cnt_ref):
    tm = gate_ref.shape[0]
    gates = gate_ref[...]
    hot = gates > 0.0
    ind = jnp.where(hot, 1.0, 0.0)
    before = (lax.broadcasted_iota(jnp.int32, (tm, tm), 1)
              < lax.broadcasted_iota(jnp.int32, (tm, tm), 0))
    rank = jnp.dot(jnp.where(before, 1.0, 0.0).astype(BF16), ind.astype(BF16),
                   preferred_element_type=F32)
    rankt_ref[...] = jnp.where(hot, rank, -1.0).T
    gatet_ref[...] = gates.T
    cnt = jnp.sum(ind, axis=0, keepdims=True)
    cnt_ref[...] = jnp.broadcast_to(cnt[None], cnt_ref.shape)


def _route(gates, tm):
    t = gates.shape[0]
    return pl.pallas_call(
        _route_kernel,
        grid=(t // tm,),
        in_specs=[pl.BlockSpec((tm, LANES), lambda i: (i, 0))],
        out_specs=[pl.BlockSpec((LANES, tm), lambda i: (0, i)),
                   pl.BlockSpec((LANES, tm), lambda i: (0, i)),
                   pl.BlockSpec((1, 8, LANES), lambda i: (i, 0, 0))],
        out_shape=(jax.ShapeDtypeStruct((LANES, t), F32),
                   jax.ShapeDtypeStruct((LANES, t), F32),
                   jax.ShapeDtypeStruct((t // tm, 8, LANES), F32)),
        compiler_params=pltpu.CompilerParams(dimension_semantics=("arbitrary",),
                                             vmem_limit_bytes=VMEM_LIMIT),
        name="route",
    )(gates)


def _moe_kernel(cnt_ref, h_ref, xn_ref, rankt_ref, gatet_ref, wgu_ref, bgu_ref,
                wdn_ref, bdn_ref, o_ref):
    i = pl.program_id(0)
    tm = h_ref.shape[0]

    @pl.when(pl.program_id(1) == 0)
    def _():
        o_ref[...] = h_ref[...]

    for sub in range(MOE_EXPERTS_PER_STEP):
        _moe_expert(pl.program_id(1) * MOE_EXPERTS_PER_STEP + sub, sub, i, tm, cnt_ref, xn_ref,
                    rankt_ref, gatet_ref, wgu_ref, bgu_ref, wdn_ref, bdn_ref, o_ref)


def _moe_expert(e, slot, i, tm, cnt_ref, xn_ref, rankt_ref, gatet_ref, wgu_ref, bgu_ref,
                wdn_ref, bdn_ref, o_ref):
    n_rows = cnt_ref[i * N_EXPERTS + e]
    n_blocks = (n_rows + MOE_RB - 1) // MOE_RB
    rank_row = rankt_ref[pl.ds(e, 1), :]
    gate_row = gatet_ref[pl.ds(e, 1), :]

    def row_block(bi, carry):
        r0 = (bi * MOE_RB).astype(F32)
        rowid = lax.broadcasted_iota(jnp.int32, (MOE_RB, tm), 0).astype(F32) + r0
        hit = rank_row == rowid
        sel = jnp.where(hit, 1.0, 0.0).astype(BF16)
        gate = jnp.sum(jnp.where(hit, gate_row, 0.0), axis=1, keepdims=True)
        xb = jnp.dot(sel, xn_ref[...], preferred_element_type=F32).astype(BF16)
        y = jnp.zeros((MOE_RB, D_MODEL), F32)
        ups = []
        for fc in range(D_FF // MOE_FC):
            c0 = fc * MOE_FC
            ups.append((jnp.dot(xb, wgu_ref[slot, :, c0:c0 + MOE_FC], preferred_element_type=F32),
                        jnp.dot(xb, wgu_ref[slot, :, D_FF + c0:D_FF + c0 + MOE_FC],
                                preferred_element_type=F32)))
        for fc in range(D_FF // MOE_FC):
            c0 = fc * MOE_FC
            hg = ups[fc][0] + bgu_ref[slot, :, c0:c0 + MOE_FC]
            hl = ups[fc][1] + bgu_ref[slot, :, D_FF + c0:D_FF + c0 + MOE_FC]
            glu = jnp.minimum(hg, SWIGLU_LIMIT)
            lin = jnp.clip(hl, -SWIGLU_LIMIT, SWIGLU_LIMIT)
            act = (lin + 1.0) * glu * jax.nn.sigmoid(SWIGLU_ALPHA * glu)
            y = y + jnp.dot(act.astype(BF16), wdn_ref[slot, c0:c0 + MOE_FC, :],
                            preferred_element_type=F32)
        y = ((y + bdn_ref[slot]) * gate).astype(BF16)
        o_ref[...] += _dot_tn(sel, y)
        return carry

    lax.fori_loop(0, n_blocks, row_block, 0)


def _moe(counts, h1, xn2, rank_t, gate_t, wgu, bgu, wdn, bdn, tm):
    t = h1.shape[0]
    eps = MOE_EXPERTS_PER_STEP
    row = lambda width: pl.BlockSpec((tm, width), lambda i, e, c: (i, 0))
    grid_spec = pltpu.PrefetchScalarGridSpec(
        num_scalar_prefetch=1,
        grid=(t // tm, N_EXPERTS // eps),
        in_specs=[row(D_MODEL), row(D_MODEL),
                  pl.BlockSpec((LANES, tm), lambda i, e, c: (0, i)),
                  pl.BlockSpec((LANES, tm), lambda i, e, c: (0, i)),
                  pl.BlockSpec((eps, D_MODEL, 2 * D_FF), lambda i, e, c: (e, 0, 0)),
                  pl.BlockSpec((eps, 1, 2 * D_FF), lambda i, e, c: (e, 0, 0)),
                  pl.BlockSpec((eps, D_FF, D_MODEL), lambda i, e, c: (e, 0, 0)),
                  pl.BlockSpec((eps, 1, D_MODEL), lambda i, e, c: (e, 0, 0))],
        out_specs=row(D_MODEL))
    return pl.pallas_call(
        _moe_kernel,
        grid_spec=grid_spec,
        out_shape=jax.ShapeDtypeStruct((t, D_MODEL), F32),
        compiler_params=pltpu.CompilerParams(dimension_semantics=("arbitrary", "arbitrary"),
                                             vmem_limit_bytes=VMEM_LIMIT),
        name="moe",
    )(counts, h1, xn2, rank_t, gate_t, wgu, bgu, wdn, bdn)


def _rope_tables():
    lane = jnp.arange(LANES)

    def rows(period, rot):
        half = rot // 2
        inv_freq = ROPE_THETA ** (-jnp.arange(half, dtype=F32) / half)
        jm = lane % period
        freq = jnp.where(jm < rot, inv_freq[jm % half], 0.0)
        sign = jnp.where(jm < half, -1.0, jnp.where(jm < rot, 1.0, 0.0))
        first = (jm < half).astype(F32)
        return [freq, sign, first]

    idx_rows = rows(IDX_DIM, IDX_DIM // 4)
    only_key = (lane < IDX_DIM).astype(F32)
    return jnp.stack(rows(HEAD_DIM, HEAD_DIM // 4) + idx_rows
                     + [only_key, jnp.zeros_like(only_key)]).astype(F32)


def _group_matrix():
    g = np.arange(D_GROUP) // HEAD_DIM
    return jnp.asarray((g[:, None] == g[None, :]).astype(np.float32), dtype=BF16)


def _pad_rows(w, r0, rows):
    return jnp.zeros((rows, w.shape[1]), w.dtype).at[r0:r0 + w.shape[0]].set(w)


def kernel(x, positions, norm1_w, w_in, q_norm_w, k_norm_w, rwkv_mu, rwkv_w0, rwkv_w2, rwkv_a0,
           rwkv_a2, rwkv_g2, rwkv_k_k, rwkv_k_a, rwkv_r_k, rwkv_ln_w, rwkv_ln_b, w_out, norm2_w,
           router_w, router_b, exp_w_gu, exp_b_gu, exp_w_down, exp_b_down):
    b, t, _ = x.shape
    assert b == 1 and w_in.shape[0] == 1, "single sequence, single layer"
    assert t % DSA_KB == 0 and t % DSA_QB == 0 and t % MOE_TM == 0
    x2 = x[0]
    pos_f = positions[0].astype(F32)[:, None]
    n_sel = min(TOPK_MAX, t // 4)

    w = w_in[0]
    a0 = 3 * D_GROUP
    att_cols = a0 + D_IDX + IDX_DIM + N_IDX_HEADS
    w_att, w_rw = w[:, :att_cols], w[:, att_cols:]
    w_sm = jnp.zeros((D_MODEL, LANES), F32).at[:, :IDX_DIM + N_IDX_HEADS].set(w_att[:, a0 + D_IDX:])
    w_packed = jnp.concatenate(
        [w_att[:, :a0], w_rw[:, :a0], w_att[:, a0:a0 + D_IDX], w_sm, w_rw[:, a0:]],
        axis=1).astype(BF16)
    mu = rwkv_mu[0][None, :]
    tile8 = lambda z: jnp.tile(z, N_HEADS)[None, :]
    gsum = _group_matrix()

    tm = 256
    q, k, v, rr, rk, rv, iq, sm, lora = _in_proj(
        x2, pos_f, norm1_w, w_packed, mu, tile8(q_norm_w[0]), tile8(k_norm_w[0]), gsum,
        _rope_tables(), tm)

    vec = lambda z: z.reshape(1, D_GROUP)
    rw = _rwkv(rr, rk, rv, lora, vec(rwkv_w0[0]),
               _pad_rows(rwkv_w2[0], 0, LANES), vec(rwkv_a0[0]),
               _pad_rows(rwkv_a2[0], D_DECAY_LORA, LANES),
               _pad_rows(rwkv_g2[0], D_DECAY_LORA + D_AAA_LORA, LANES),
               vec(rwkv_k_k[0]), vec(rwkv_k_a[0]), vec(rwkv_r_k[0]), vec(rwkv_ln_w[0]),
               vec(rwkv_ln_b[0]), gsum, tm)

    heads = lambda z: z.reshape(t, N_HEADS, HEAD_DIM)
    vt = jnp.concatenate([heads(v).transpose(1, 2, 0),
                          jnp.ones((N_HEADS, DSA_V_ROWS - HEAD_DIM, t), BF16)], axis=1)
    att_t, (w_gu_bf, w_dn_bf) = _dsa(
        heads(q).transpose(1, 2, 0), k, vt, iq.T, sm[:, :IDX_DIM].astype(BF16),
        sm[:, IDX_DIM:IDX_DIM + N_IDX_HEADS].T, n_sel, to_cast=(exp_w_gu[0], exp_w_down[0]))

    rwt = jnp.zeros((D_MODEL, LANES), F32).at[:, :N_EXPERTS].set(router_w[0])
    rb = jnp.full((1, LANES), NEG_BIG, F32).at[0, :N_EXPERTS].set(router_b[0])
    h1, xn2, gates = _out_proj(x2, att_t, rw, w_out[0].astype(BF16), norm2_w, rwt, rb, tm)

    rank_t, gate_t, cnt = _route(gates, MOE_TM)
    counts = cnt[:, 0, :N_EXPERTS].astype(jnp.int32).reshape(-1)
    out = _moe(counts, h1, xn2, rank_t, gate_t, w_gu_bf, exp_b_gu[0][:, None, :], w_dn_bf,
               exp_b_down[0][:, None, :], MOE_TM)
    return out[None]
```

```python
import functools

import jax
import jax.numpy as jnp
import numpy as np
from jax import lax
from jax.experimental import pallas as pl
from jax.experimental.pallas import tpu as pltpu

F32 = jnp.float32
BF16 = jnp.bfloat16
HIGHEST = lax.Precision.HIGHEST

D_MODEL = 1024
HEAD_DIM = 64
N_HEADS = 8
D_GROUP = N_HEADS * HEAD_DIM
ROPE_THETA = 500000.0
N_IDX_HEADS = 8
IDX_DIM = 32
D_IDX = N_IDX_HEADS * IDX_DIM
TOPK_MAX = 256
D_DECAY_LORA = 32
D_AAA_LORA = 32
D_GATE_LORA = 64
RWKV_GN_EPS = 64e-5
N_EXPERTS = 32
TOP_K_EXPERTS = 4
D_FF = 1024
SWIGLU_LIMIT = 7.0
SWIGLU_ALPHA = 1.702
RMS_EPS = 1e-6

LANES = 128
VMEM_LIMIT = 56 * 1024 * 1024

NEG_BIG = -1e30
INT_MIN = -(2 ** 31)


def _dot(a, b):
    return jnp.dot(a.astype(BF16), b.astype(BF16), preferred_element_type=F32)


def _dotf(a, b):
    return jnp.dot(a, b, preferred_element_type=F32, precision=HIGHEST)


def _dot3(a, b):
    a_hi = a.astype(BF16)
    b_hi = b.astype(BF16)
    a_lo = (a - a_hi.astype(F32)).astype(BF16)
    b_lo = (b - b_hi.astype(F32)).astype(BF16)
    dot = functools.partial(jnp.dot, preferred_element_type=F32)
    return dot(a_hi, b_hi) + dot(a_hi, b_lo) + dot(a_lo, b_hi)


def _dot_nt(a, b):
    return lax.dot_general(a.astype(BF16), b.astype(BF16), (((1,), (1,)), ((), ())),
                           preferred_element_type=F32)


def _dot_tn(a, b):
    return lax.dot_general(a.astype(BF16), b.astype(BF16), (((0,), (0,)), ((), ())),
                           preferred_element_type=F32)


def _group_sum(z, g_ref):
    hi = z.astype(BF16)
    lo = (z - hi.astype(F32)).astype(BF16)
    g = g_ref[...]
    return (jnp.dot(hi, g, preferred_element_type=F32)
            + jnp.dot(lo, g, preferred_element_type=F32))


_C_Q, _C_K, _C_V, _C_RR, _C_RK, _C_RV = (i * D_GROUP for i in range(6))
_C_IQ = 6 * D_GROUP
_C_SM = _C_IQ + D_IDX
_C_LORA = _C_SM + LANES
D_IN_PACKED = _C_LORA + LANES
D_SHIFT = 3 * D_GROUP + LANES


def _in_proj_kernel(x_ref, pos_ref, n1w_ref, w_ref, mu_ref, qnw_ref, knw_ref, g_ref, rope_ref,
                    q_ref, k_ref, v_ref, rr_ref, rk_ref, rv_ref, iq_ref, sm_ref, lora_ref,
                    carry_ref):
    tm = x_ref.shape[0]

    @pl.when(pl.program_id(0) == 0)
    def _():
        carry_ref[...] = jnp.zeros_like(carry_ref)

    x = x_ref[...]
    xn = x * lax.rsqrt(jnp.mean(x * x, axis=-1, keepdims=True) + RMS_EPS) * n1w_ref[...]
    xb = xn.astype(BF16)
    pos = pos_ref[...]
    rope = rope_ref[...]

    def tables(frow, srow):
        ang = pos * rope[frow:frow + 1, :]
        return jnp.cos(ang), jnp.sin(ang) * rope[srow:srow + 1, :]

    def widen(z, reps):
        return jnp.concatenate([z] * reps, axis=1)

    def rotary(z, c, s, first_row, half):
        w = z.shape[1]
        first = jnp.concatenate([rope[first_row:first_row + 1, :]] * (w // LANES), axis=1) > 0.5
        partner = jnp.where(first, pltpu.roll(z, w - half, 1), pltpu.roll(z, half, 1))
        return z * c + partner * s

    def head_norm(z, w_row):
        ms = _group_sum(z * z, g_ref) * (1.0 / HEAD_DIM)
        return z * lax.rsqrt(ms + RMS_EPS) * w_row

    def proj(c0, width):
        return jnp.dot(xb, w_ref[:, c0:c0 + width], preferred_element_type=F32)

    def shift(z, c0):
        width = z.shape[1]
        row = lax.broadcasted_iota(jnp.int32, z.shape, 0)
        prev = jnp.where(row == 0, carry_ref[0:1, c0:c0 + width], pltpu.roll(z, 1, 0))
        carry_ref[0:1, c0:c0 + width] = z[tm - 1:tm, :]
        return z + (prev - z) * mu_ref[:, c0:c0 + width]

    cq, sq = (widen(z, D_GROUP // LANES) for z in tables(0, 1))
    q = rotary(head_norm(proj(_C_Q, D_GROUP), qnw_ref[...]), cq, sq, 2, HEAD_DIM // 8)
    q_ref[...] = q.astype(BF16)
    k = rotary(head_norm(proj(_C_K, D_GROUP), knw_ref[...]), cq, sq, 2, HEAD_DIM // 8)
    k_ref[...] = k.astype(BF16)
    v_ref[...] = proj(_C_V, D_GROUP).astype(BF16)

    rr_ref[...] = shift(proj(_C_RR, D_GROUP), 0)
    rk_ref[...] = shift(proj(_C_RK, D_GROUP), D_GROUP)
    rv_ref[...] = shift(proj(_C_RV, D_GROUP), 2 * D_GROUP)
    lora_ref[...] = shift(proj(_C_LORA, LANES), 3 * D_GROUP)

    ci, si = tables(3, 4)
    iq_ref[...] = rotary(proj(_C_IQ, D_IDX), widen(ci, D_IDX // LANES), widen(si, D_IDX // LANES),
                         5, IDX_DIM // 8)
    only_key = rope[6:7, :]
    sm_ref[...] = rotary(proj(_C_SM, LANES), ci * only_key + (1.0 - only_key), si * only_key,
                         5, IDX_DIM // 8)


def _in_proj(x2, pos_f, n1w, w_packed, mu_packed, qnw, knw, gmat, rope, tm):
    t = x2.shape[0]
    full = lambda shape: pl.BlockSpec(shape, lambda i: (0,) * len(shape))
    row = lambda width: pl.BlockSpec((tm, width), lambda i: (i, 0))
    out_shapes = (
        jax.ShapeDtypeStruct((t, D_GROUP), BF16),
        jax.ShapeDtypeStruct((t, D_GROUP), BF16),
        jax.ShapeDtypeStruct((t, D_GROUP), BF16),
        jax.ShapeDtypeStruct((t, D_GROUP), F32),
        jax.ShapeDtypeStruct((t, D_GROUP), F32),
        jax.ShapeDtypeStruct((t, D_GROUP), F32),
        jax.ShapeDtypeStruct((t, D_IDX), F32),
        jax.ShapeDtypeStruct((t, LANES), F32),
        jax.ShapeDtypeStruct((t, LANES), F32),
    )
    return pl.pallas_call(
        _in_proj_kernel,
        grid=(t // tm,),
        in_specs=[row(D_MODEL), row(1), full((1, D_MODEL)), full((D_MODEL, D_IN_PACKED)),
                  full((1, D_SHIFT)), full((1, D_GROUP)), full((1, D_GROUP)),
                  full((D_GROUP, D_GROUP)), full((8, LANES))],
        out_specs=[row(D_GROUP)] * 6 + [row(D_IDX), row(LANES), row(LANES)],
        out_shape=out_shapes,
        scratch_shapes=[pltpu.VMEM((8, D_SHIFT), F32)],
        compiler_params=pltpu.CompilerParams(dimension_semantics=("arbitrary",),
                                             vmem_limit_bytes=VMEM_LIMIT),
        name="in_proj",
    )(x2, pos_f, n1w, w_packed, mu_packed, qnw, knw, gmat, rope)


RWKV_CHUNK = 64
RWKV_UNROLL = 4
PAIR = 2 * HEAD_DIM


def _rwkv_kernel(r_ref, k_ref, v_ref, lora_ref, w0_ref, w2_ref, a0_ref, a2_ref, g2_ref,
                 kk_ref, ka_ref, rk_ref, lnw_ref, lnb_ref, g_ref, *rest, n_cast):
    cast_in, o_ref, cast_out = rest[:n_cast], rest[n_cast], rest[n_cast + 1:2 * n_cast + 1]
    s_ref, ld_s, r_s, k2_s, b_s, kk_s, y_s = rest[2 * n_cast + 1:]
    for src, dst in zip(cast_in, cast_out):
        dst[...] = src[...].astype(BF16)
    tm = r_ref.shape[0]
    n_chunks = tm // RWKV_CHUNK
    n_pairs = D_GROUP // PAIR
    c = RWKV_CHUNK

    @pl.when(pl.program_id(0) == 0)
    def _():
        s_ref[...] = jnp.zeros_like(s_ref)

    lora = lora_ref[...]
    r = r_ref[...]
    k = k_ref[...]
    v = v_ref[...]
    zarg = w0_ref[...] + _dot3(jnp.tanh(lora), w2_ref[...])
    sp = jnp.maximum(-zarg, 0.0) + jnp.log1p(jnp.exp(-jnp.abs(zarg)))
    ld_s[...] = -jnp.exp(-sp - 0.5)
    a = jax.nn.sigmoid(a0_ref[...] + _dot3(lora, a2_ref[...]))
    g = _dot3(jax.nn.sigmoid(lora), g2_ref[...])
    kk = k * kk_ref[...]
    kk = kk * lax.rsqrt(jnp.maximum(_group_sum(kk * kk, g_ref), 1e-24))
    k2 = k * (1.0 + (a - 1.0) * ka_ref[...])
    bonus = _group_sum(r * k2 * rk_ref[...], g_ref) * v
    r_s[...] = r
    k2_s[...] = k2
    kk_s[...] = kk
    b_s[...] = kk * a

    row = lax.broadcasted_iota(jnp.int32, (2 * c, 2 * c), 0)
    col = lax.broadcasted_iota(jnp.int32, (2 * c, 2 * c), 1)
    same_head = (row >= c) == (col >= c)
    strict = same_head & (col < row)
    incl = same_head & (col <= row)
    eye = (row == col).astype(F32)
    tri = (lax.broadcasted_iota(jnp.int32, (c, c), 1)
           <= lax.broadcasted_iota(jnp.int32, (c, c), 0)).astype(F32)
    lane = lax.broadcasted_iota(jnp.int32, (c, PAIR), 1)
    head0 = lane < HEAD_DIM

    def stack(z):
        return jnp.concatenate([jnp.where(head0, z, 0.0), jnp.where(head0, 0.0, z)], axis=0)

    def chunk_body(ci, carry):
        c2 = 2 * c
        units = [(cc, p) for cc in range(RWKV_UNROLL) for p in range(n_pairs)]
        uid = range(len(units))
        rows = [pl.multiple_of((ci * RWKV_UNROLL + cc) * c, c) for cc in range(RWKV_UNROLL)]
        ld_all = [ld_s[pl.ds(r0, c), :] for r0 in rows]
        cum_all = [_dotf(tri, ld) for ld in ld_all]
        rt, kt, bt, kp, vs, lhs, rhs, gam_end = [], [], [], [], [], [], [], []
        for cc, p in units:
            r0 = rows[cc]
            cols = slice(p * PAIR, (p + 1) * PAIR)
            ld = ld_all[cc][:, cols]
            cum = cum_all[cc][:, cols]
            gam = jnp.exp(cum)
            inv = jnp.exp(-cum)
            gam_prev = jnp.exp(cum - ld)
            gam_end.append(gam[c - 1:c, :])
            rt.append(stack(r_s[pl.ds(r0, c), cols] * gam))
            kt_raw = k2_s[pl.ds(r0, c), cols] * inv
            bt_raw = b_s[pl.ds(r0, c), cols] * inv
            kt.append(stack(kt_raw))
            bt.append(stack(bt_raw))
            kp.append(stack(kk_s[pl.ds(r0, c), cols] * gam_prev))
            vs.append(stack(v_ref[pl.ds(r0, c), cols]))
            lhs.append(jnp.concatenate([kp[-1], rt[-1]], axis=0))
            rhs.append(jnp.concatenate([bt_raw, bt_raw, kt_raw, kt_raw], axis=0))
        aa = [_dot_nt(lhs[u], rhs[u]) for u in uid]
        a_kk = [jnp.where(strict, aa[u][:c2, c2:], 0.0) for u in uid]
        a_rb = [jnp.where(incl, aa[u][c2:, :c2], 0.0) for u in uid]
        a_rk = [jnp.where(incl, aa[u][c2:, c2:], 0.0) for u in uid]
        n = [-jnp.where(strict, aa[u][:c2, :c2], 0.0) for u in uid]
        prod = [eye + n[u] for u in uid]
        n = [_dot(n[u], n[u]) for u in uid]
        for _ in range(int(np.log2(c)) - 2):
            both = [_dot(jnp.concatenate([n[u], prod[u]], axis=0), n[u]) for u in uid]
            prod = [prod[u] + both[u][c2:] for u in uid]
            n = [both[u][:c2] for u in uid]
        av = [_dot(jnp.concatenate([a_kk[u], a_rk[u]], axis=0), vs[u]) for u in uid]
        tinv = [prod[u] + _dot(prod[u], n[u]) for u in uid]
        wu = [_dot(tinv[u], jnp.concatenate([kp[u], av[u][:c2]], axis=1)) for u in uid]
        rb = [_dot(a_rb[u], wu[u]) for u in uid]
        tn = [_dot_tn(jnp.concatenate([wu[u], vs[u]], axis=1),
                      jnp.concatenate([bt[u], kt[u]], axis=1)) for u in uid]
        r2 = [rt[u] - rb[u][:, :PAIR] for u in uid]
        y2 = [av[u][c2:] - rb[u][:, PAIR:] for u in uid]
        s_mix = [(eye - tn[u][:PAIR, :PAIR]) * gam_end[u] for u in uid]
        s_add = [(tn[u][2 * PAIR:, PAIR:] - tn[u][PAIR:2 * PAIR, :PAIR]) * gam_end[u] for u in uid]
        state = [s_ref[p] for p in range(n_pairs)]
        for cc in range(RWKV_UNROLL):
            us = [cc * n_pairs + p for p in range(n_pairs)]
            ys = [_dot_nt(r2[u], state[p]) + y2[u] for p, u in enumerate(us)]
            state = [_dot(state[p], s_mix[u]) + s_add[u] for p, u in enumerate(us)]
            for p in range(n_pairs):
                y_s[pl.ds(rows[cc], c), p * PAIR:(p + 1) * PAIR] = ys[p][:c] + ys[p][c:]
        for p in range(n_pairs):
            s_ref[p] = state[p]
        return carry

    lax.fori_loop(0, n_chunks // RWKV_UNROLL, chunk_body, 0)

    y = y_s[...]
    mean = _group_sum(y, g_ref) * (1.0 / HEAD_DIM)
    yc = y - mean
    var = _group_sum(yc * yc, g_ref) * (1.0 / HEAD_DIM)
    yn = yc * lax.rsqrt(var + RWKV_GN_EPS) * lnw_ref[...] + lnb_ref[...]
    o_ref[...] = (yn + bonus) * g


def _rwkv(rr, rk, rv, lora, w0, w2p, a0, a2p, g2p, k_k, k_a, r_k, ln_w, ln_b, gsum, tm,
          to_cast=()):
    t = rr.shape[0]
    full = lambda shape: pl.BlockSpec(shape, lambda i: (0,) * len(shape))
    row = lambda width: pl.BlockSpec((tm, width), lambda i: (i, 0))
    vec = full((1, D_GROUP))
    big = pltpu.VMEM((tm, D_GROUP), F32)
    slabs = [w.reshape(t // tm, -1, w.shape[-1]) for w in to_cast]
    cast_specs = [pl.BlockSpec((1,) + w.shape[1:], lambda i: (i, 0, 0)) for w in slabs]
    outs = pl.pallas_call(
        functools.partial(_rwkv_kernel, n_cast=len(slabs)),
        grid=(t // tm,),
        in_specs=[row(D_GROUP), row(D_GROUP), row(D_GROUP), row(LANES),
                  vec, full((LANES, D_GROUP)), vec, full((LANES, D_GROUP)),
                  full((LANES, D_GROUP)), vec, vec, vec, vec, vec, full((D_GROUP, D_GROUP))]
        + cast_specs,
        out_specs=[row(D_GROUP)] + cast_specs,
        out_shape=[jax.ShapeDtypeStruct((t, D_GROUP), F32)]
        + [jax.ShapeDtypeStruct(w.shape, BF16) for w in slabs],
        scratch_shapes=[pltpu.VMEM((D_GROUP // PAIR, PAIR, PAIR), F32),
                        big, big, big, big, big, big],
        compiler_params=pltpu.CompilerParams(dimension_semantics=("arbitrary",),
                                             vmem_limit_bytes=VMEM_LIMIT),
        name="rwkv",
    )(rr, rk, rv, lora, w0, w2p, a0, a2p, g2p, k_k, k_a, r_k, ln_w, ln_b, gsum, *slabs)
    return outs[0], [o.reshape(w.shape) for o, w in zip(outs[1:], to_cast)]


DSA_QB = 256
DSA_KB = 512
M_INIT = -5e29
DSA_HEAD_GROUP = 4
DSA_ACC_ROWS = 32
DSA_GROUPS = 256
DSA_TOP = 10
DSA_V_ROWS = HEAD_DIM + 16


def _dsa_kernel(qt_ref, k_hbm, vt_hbm, iqt_ref, ik_ref, iwt_ref, o_ref,
                key_s, thr_s, need_s, over_s, m_s, acc_s, bdq_s, top_s, k_buf, vt_buf, kv_sem,
                *, n_sel):
    qb = qt_ref.shape[2]
    kb = k_buf.shape[1]
    qi = pl.program_id(0)
    q0 = qi * qb
    j_last = (q0 + qb - 1) // kb
    n_kc = j_last + 1
    idx_scale = float((IDX_DIM * N_IDX_HEADS) ** -0.5)

    def kv_copies(j, slot):
        k0 = pl.multiple_of(j * kb, kb)
        return (pltpu.make_async_copy(k_hbm.at[pl.ds(k0, kb), :], k_buf.at[slot], kv_sem.at[0, slot]),
                pltpu.make_async_copy(vt_hbm.at[:, :, pl.ds(k0, kb)], vt_buf.at[slot],
                                      kv_sem.at[1, slot]))

    for cp in kv_copies(0, 0):
        cp.start()

    def causal(jblk):
        s_pos = jblk * kb + lax.broadcasted_iota(jnp.int32, (kb, qb), 0)
        t_pos = q0 + lax.broadcasted_iota(jnp.int32, (kb, qb), 1)
        return s_pos <= t_pos

    def select_keys():
        iqt = iqt_ref[...].astype(BF16)
        iwt = iwt_ref[...] * idx_scale

        def score_chunk(kc, diagonal):
            ik = ik_ref[pl.ds(pl.multiple_of(kc * kb, kb), kb), :]
            score = jnp.zeros((kb, qb), F32)
            for h in range(N_IDX_HEADS):
                d = jnp.dot(ik, iqt[h * IDX_DIM:(h + 1) * IDX_DIM, :],
                            preferred_element_type=F32)
                score = score + jnp.maximum(d, 0.0) * iwt[h:h + 1, :]
            bits = pltpu.bitcast(score, jnp.int32)
            keys = jnp.where(bits < 0, bits ^ jnp.int32(0x7FFFFFFF), bits)
            if diagonal:
                keys = jnp.where(causal(kc), keys, jnp.int32(INT_MIN))
            key_s[kc] = keys
            xs = [keys[r * DSA_GROUPS:(r + 1) * DSA_GROUPS] for r in range(kb // DSA_GROUPS)]
            for lvl in range(DSA_TOP):
                s = top_s[lvl]
                for r in range(len(xs)):
                    s, xs[r] = jnp.maximum(s, xs[r]), jnp.minimum(s, xs[r])
                top_s[lvl] = s

        top_s[...] = jnp.full_like(top_s, INT_MIN)

        def full_chunk(kc, carry):
            score_chunk(kc, False)
            return carry

        lax.fori_loop(0, j_last, full_chunk, 0)
        score_chunk(j_last, True)

        def select(count_keys):
            def bit_step(b, cur):
                bit = lax.shift_left(jnp.int32(1), jnp.int32(31) - b)
                cand = (cur | bit) ^ jnp.int32(INT_MIN)
                cnt = jnp.sum(count_keys(cand), axis=0, keepdims=True)
                return jnp.where(cnt >= float(n_sel), cur | bit, cur)

            cur = lax.fori_loop(0, 32, bit_step, jnp.zeros((1, qb), jnp.int32))
            return jnp.maximum(cur ^ jnp.int32(INT_MIN), jnp.int32(INT_MIN + 1))

        def part_count(keys, cand):
            ind = jnp.where(keys >= cand, 1.0, 0.0)
            return jnp.sum(ind.reshape(-1, DSA_ACC_ROWS, qb), axis=0)

        def count_top(cand):
            return lax.fori_loop(0, DSA_TOP, lambda lvl, a: a + part_count(top_s[lvl], cand),
                                 jnp.zeros((DSA_ACC_ROWS, qb), F32))

        def count_all(cand):
            return lax.fori_loop(0, n_kc, lambda kc, a: a + part_count(key_s[kc], cand),
                                 jnp.zeros((DSA_ACC_ROWS, qb), F32))

        def settle(count_keys):
            thr = select(count_keys)
            thr_s[...] = thr
            above = jnp.sum(count_keys(thr + 1), axis=0, keepdims=True)
            upto = jnp.sum(count_keys(thr), axis=0, keepdims=True)
            need_s[...] = float(n_sel) - above
            over_s[...] = upto - float(n_sel)
            return thr

        thr = settle(count_top)
        hidden = jnp.where(top_s[DSA_TOP - 1] >= thr, 1.0, 0.0)

        @pl.when(jnp.max(hidden) > 0.0)
        def _():
            settle(count_all)

        @pl.when(jnp.max(over_s[...]) > 0.0)
        def _():
            thr = thr_s[...]
            need = need_s[...]
            sub = LANES
            earlier = jnp.where(lax.broadcasted_iota(jnp.int32, (sub, sub), 1)
                                < lax.broadcasted_iota(jnp.int32, (sub, sub), 0), 1.0, 0.0).astype(BF16)

            def drop_surplus(kc, seen):
                keys = key_s[kc]
                tie = keys == thr
                tie_f = jnp.where(tie, 1.0, 0.0)
                ranks = []
                for r in range(kb // sub):
                    part = tie_f[r * sub:(r + 1) * sub]
                    ranks.append(seen + jnp.dot(earlier, part.astype(BF16),
                                                preferred_element_type=F32))
                    seen = seen + jnp.sum(part, axis=0, keepdims=True)
                rank = jnp.concatenate(ranks, axis=0)
                key_s[kc] = jnp.where(tie & (rank >= need), thr - 1, keys)
                return seen

            lax.fori_loop(0, n_kc, drop_surplus, jnp.zeros((1, qb), F32))

    select_keys()
    m_s[...] = jnp.full_like(m_s, M_INIT)
    acc_s[...] = jnp.zeros_like(acc_s)
    bdq_s[...] = jnp.zeros_like(bdq_s)
    for h in range(N_HEADS):
        g, hh = divmod(h, DSA_HEAD_GROUP)
        bdq_s[g, hh * HEAD_DIM:(hh + 1) * HEAD_DIM, hh * qb:(hh + 1) * qb] = (
            qt_ref[h] * jnp.asarray(HEAD_DIM ** -0.5, BF16))

    def attend(kj, carry):
        slot = lax.rem(kj, 2)
        for cp in kv_copies(kj, slot):
            cp.wait()

        @pl.when(kj + 1 < n_kc)
        def _():
            for cp in kv_copies(kj + 1, 1 - slot):
                cp.start()

        k_blk = k_buf.at[slot]
        vt_blk = vt_buf.at[slot]
        mask = key_s[kj] >= thr_s[...]
        hg = DSA_HEAD_GROUP
        st_all = [jnp.dot(k_blk[:, g * hg * HEAD_DIM:(g + 1) * hg * HEAD_DIM], bdq_s[g],
                          preferred_element_type=F32) for g in range(N_HEADS // hg)]
        for g in range(N_HEADS // hg):
            st_g = st_all[g]
            for hh in range(hg):
                h = g * hg + hh
                st = jnp.where(mask, st_g[:, hh * qb:(hh + 1) * qb], NEG_BIG)
                m_old = m_s[h]
                part = jnp.max(st.reshape(kb // DSA_ACC_ROWS, DSA_ACC_ROWS, qb), axis=0)
                m_new = jnp.maximum(m_old, jnp.max(part, axis=0, keepdims=True))
                p = jnp.exp((st - m_new).astype(BF16))
                acc_s[h] = jnp.exp(m_old - m_new) * acc_s[h] + jnp.dot(
                    vt_blk[h], p, preferred_element_type=F32)
                m_s[h] = m_new
        return carry

    lax.fori_loop(0, n_kc, attend, 0)
    for h in range(N_HEADS):
        acc = acc_s[h]
        o_ref[h] = acc[:HEAD_DIM] / acc[HEAD_DIM:HEAD_DIM + 1]


def _dsa(qt, k, vt, iqt, ik, iwt, n_sel):
    t = ik.shape[0]
    qb, kb = DSA_QB, DSA_KB
    assert n_sel <= DSA_GROUPS and kb % DSA_GROUPS == 0
    nq, nk = t // qb, t // kb
    return pl.pallas_call(
        functools.partial(_dsa_kernel, n_sel=n_sel),
        grid=(nq,),
        in_specs=[pl.BlockSpec((N_HEADS, HEAD_DIM, qb), lambda i: (0, 0, i)),
                  pl.BlockSpec(memory_space=pl.ANY),
                  pl.BlockSpec(memory_space=pl.ANY),
                  pl.BlockSpec((D_IDX, qb), lambda i: (0, i)),
                  pl.BlockSpec((t, IDX_DIM), lambda i: (0, 0)),
                  pl.BlockSpec((N_IDX_HEADS, qb), lambda i: (0, i))],
        out_specs=pl.BlockSpec((N_HEADS, HEAD_DIM, qb), lambda i: (0, 0, i)),
        out_shape=jax.ShapeDtypeStruct((N_HEADS, HEAD_DIM, t), F32),
        scratch_shapes=[pltpu.VMEM((nk, kb, qb), jnp.int32),
                        pltpu.VMEM((1, qb), jnp.int32),
                        pltpu.VMEM((1, qb), F32),
                        pltpu.VMEM((1, qb), F32),
                        pltpu.VMEM((N_HEADS, 1, qb), F32),
                        pltpu.VMEM((N_HEADS, DSA_V_ROWS, qb), F32),
                        pltpu.VMEM((N_HEADS // DSA_HEAD_GROUP, DSA_HEAD_GROUP * HEAD_DIM,
                                    DSA_HEAD_GROUP * qb), BF16),
                        pltpu.VMEM((DSA_TOP, DSA_GROUPS, qb), jnp.int32),
                        pltpu.VMEM((2, kb, D_GROUP), BF16),
                        pltpu.VMEM((2, N_HEADS, DSA_V_ROWS, kb), BF16),
                        pltpu.SemaphoreType.DMA((2, 2))],
        compiler_params=pltpu.CompilerParams(dimension_semantics=("arbitrary",),
                                             vmem_limit_bytes=VMEM_LIMIT),
        name="dsa",
    )(qt, k, vt, iqt, ik, iwt)


def _out_proj_kernel(x_ref, att_ref, rw_ref, wo_ref, n2w_ref, rwt_ref, rb_ref,
                     h_ref, xn_ref, gate_ref):
    tm = x_ref.shape[0]
    att = att_ref[...].reshape(D_GROUP, tm).T
    mix = jnp.concatenate([att, rw_ref[...]], axis=1)
    acc = x_ref[...] + _dot(mix, wo_ref[...])
    h_ref[...] = acc
    xn = acc * lax.rsqrt(jnp.mean(acc * acc, axis=-1, keepdims=True) + RMS_EPS) * n2w_ref[...]
    xn_ref[...] = xn.astype(BF16)
    logits = _dotf(xn, rwt_ref[...]) + rb_ref[...]
    lane = lax.broadcasted_iota(jnp.int32, logits.shape, 1)
    work = logits
    vals, hots = [], []
    for _ in range(TOP_K_EXPERTS):
        m = jnp.max(work, axis=1, keepdims=True)
        idx = jnp.min(jnp.where(work == m, lane, LANES), axis=1, keepdims=True)
        hot = lane == idx
        vals.append(m)
        hots.append(hot)
        work = jnp.where(hot, -jnp.inf, work)
    es = [jnp.exp(vv - vals[0]) for vv in vals]
    denom = es[0] + es[1] + es[2] + es[3]
    gates = jnp.zeros_like(logits)
    for e, hot in zip(es, hots):
        gates = gates + jnp.where(hot, e / denom, 0.0)
    gate_ref[...] = gates


def _out_proj(x2, att_t, rw, wo, n2w, rwt, rb, tm):
    t = x2.shape[0]
    full = lambda shape: pl.BlockSpec(shape, lambda i: (0,) * len(shape))
    row = lambda width: pl.BlockSpec((tm, width), lambda i: (i, 0))
    return pl.pallas_call(
        _out_proj_kernel,
        grid=(t // tm,),
        in_specs=[row(D_MODEL), pl.BlockSpec((N_HEADS, HEAD_DIM, tm), lambda i: (0, 0, i)),
                  row(D_GROUP), full((2 * D_GROUP, D_MODEL)), full((1, D_MODEL)),
                  full((D_MODEL, LANES)), full((1, LANES))],
        out_specs=[row(D_MODEL), row(D_MODEL), row(LANES)],
        out_shape=(jax.ShapeDtypeStruct((t, D_MODEL), F32),
                   jax.ShapeDtypeStruct((t, D_MODEL), BF16),
                   jax.ShapeDtypeStruct((t, LANES), F32)),
        compiler_params=pltpu.CompilerParams(dimension_semantics=("arbitrary",),
                                             vmem_limit_bytes=VMEM_LIMIT),
        name="out_proj",
    )(x2, att_t, rw, wo, n2w, rwt, rb)


MOE_TM = 1024
MOE_RB = 144
MOE_FC = 512
MOE_EXPERTS_PER_STEP = 2


def _route_kernel(gate_ref, rankt_ref, gatet_ref, cnt_ref):
    tm = gate_ref.shape[0]
    gates = gate_ref[...]
    hot = gates > 0.0
    ind = jnp.where(hot, 1.0, 0.0)
    before = (lax.broadcasted_iota(jnp.int32, (tm, tm), 1)
              < lax.broadcasted_iota(jnp.int32, (tm, tm), 0))
    rank = jnp.dot(jnp.where(before, 1.0, 0.0).astype(BF16), ind.astype(BF16),
                   preferred_element_type=F32)
    rankt_ref[...] = jnp.where(hot, rank, -1.0).T
    gatet_ref[...] = gates.T
    cnt = jnp.sum(ind, axis=0, keepdims=True)
    cnt_ref[...] = jnp.broadcast_to(cnt[None], cnt_ref.shape)


def _route(gates, tm):
    t = gates.shape[0]
    return pl.pallas_call(
        _route_kernel,
        grid=(t // tm,),
        in_specs=[pl.BlockSpec((tm, LANES), lambda i: (i, 0))],
        out_specs=[pl.BlockSpec((LANES, tm), lambda i: (0, i)),
                   pl.BlockSpec((LANES, tm), lambda i: (0, i)),
                   pl.BlockSpec((1, 8, LANES), lambda i: (i, 0, 0))],
        out_shape=(jax.ShapeDtypeStruct((LANES, t), F32),
                   jax.ShapeDtypeStruct((LANES, t), F32),
                   jax.ShapeDtypeStruct((t // tm, 8, LANES), F32)),
        compiler_params=pltpu.CompilerParams(dimension_semantics=("arbitrary",),
                                             vmem_limit_bytes=VMEM_LIMIT),
        name="route",
    )(gates)


def _moe_kernel(cnt_ref, h_ref, xn_ref, rankt_ref, gatet_ref, wgu_ref, bgu_ref,
                wdn_ref, bdn_ref, o_ref):
    i = pl.program_id(0)
    tm = h_ref.shape[0]

    @pl.when(pl.program_id(1) == 0)
    def _():
        o_ref[...] = h_ref[...]

    for sub in range(MOE_EXPERTS_PER_STEP):
        _moe_expert(pl.program_id(1) * MOE_EXPERTS_PER_STEP + sub, sub, i, tm, cnt_ref, xn_ref,
                    rankt_ref, gatet_ref, wgu_ref, bgu_ref, wdn_ref, bdn_ref, o_ref)


def _moe_expert(e, slot, i, tm, cnt_ref, xn_ref, rankt_ref, gatet_ref, wgu_ref, bgu_ref,
                wdn_ref, bdn_ref, o_ref):
    n_rows = cnt_ref[i * N_EXPERTS + e]
    n_blocks = (n_rows + MOE_RB - 1) // MOE_RB
    rank_row = rankt_ref[pl.ds(e, 1), :]
    gate_row = gatet_ref[pl.ds(e, 1), :]

    def row_block(bi, carry):
        r0 = (bi * MOE_RB).astype(F32)
        rowid = lax.broadcasted_iota(jnp.int32, (MOE_RB, tm), 0).astype(F32) + r0
        hit = rank_row == rowid
        sel = jnp.where(hit, 1.0, 0.0).astype(BF16)
        gate = jnp.sum(jnp.where(hit, gate_row, 0.0), axis=1, keepdims=True)
        xb = jnp.dot(sel, xn_ref[...], preferred_element_type=F32).astype(BF16)
        y = jnp.zeros((MOE_RB, D_MODEL), F32)
        ups = []
        for fc in range(D_FF // MOE_FC):
            c0 = fc * MOE_FC
            ups.append((jnp.dot(xb, wgu_ref[slot, :, c0:c0 + MOE_FC], preferred_element_type=F32),
                        jnp.dot(xb, wgu_ref[slot, :, D_FF + c0:D_FF + c0 + MOE_FC],
                                preferred_element_type=F32)))
        for fc in range(D_FF // MOE_FC):
            c0 = fc * MOE_FC
            hg = ups[fc][0] + bgu_ref[slot, :, c0:c0 + MOE_FC]
            hl = ups[fc][1] + bgu_ref[slot, :, D_FF + c0:D_FF + c0 + MOE_FC]
            glu = jnp.minimum(hg, SWIGLU_LIMIT)
            lin = jnp.clip(hl, -SWIGLU_LIMIT, SWIGLU_LIMIT)
            act = (lin + 1.0) * glu * jax.nn.sigmoid(SWIGLU_ALPHA * glu)
            y = y + jnp.dot(act.astype(BF16), wdn_ref[slot, c0:c0 + MOE_FC, :],
                            preferred_element_type=F32)
        y = ((y + bdn_ref[slot]) * gate).astype(BF16)
        o_ref[...] += _dot_tn(sel, y)
        return carry

    lax.fori_loop(0, n_blocks, row_block, 0)


def _moe(counts, h1, xn2, rank_t, gate_t, wgu, bgu, wdn, bdn, tm):
    t = h1.shape[0]
    eps = MOE_EXPERTS_PER_STEP
    row = lambda width: pl.BlockSpec((tm, width), lambda i, e, c: (i, 0))
    grid_spec = pltpu.PrefetchScalarGridSpec(
        num_scalar_prefetch=1,
        grid=(t // tm, N_EXPERTS // eps),
        in_specs=[row(D_MODEL), row(D_MODEL),
                  pl.BlockSpec((LANES, tm), lambda i, e, c: (0, i)),
                  pl.BlockSpec((LANES, tm), lambda i, e, c: (0, i)),
                  pl.BlockSpec((eps, D_MODEL, 2 * D_FF), lambda i, e, c: (e, 0, 0)),
                  pl.BlockSpec((eps, 1, 2 * D_FF), lambda i, e, c: (e, 0, 0)),
                  pl.BlockSpec((eps, D_FF, D_MODEL), lambda i, e, c: (e, 0, 0)),
                  pl.BlockSpec((eps, 1, D_MODEL), lambda i, e, c: (e, 0, 0))],
        out_specs=row(D_MODEL))
    return pl.pallas_call(
        _moe_kernel,
        grid_spec=grid_spec,
        out_shape=jax.ShapeDtypeStruct((t, D_MODEL), F32),
        compiler_params=pltpu.CompilerParams(dimension_semantics=("arbitrary", "arbitrary"),
                                             vmem_limit_bytes=VMEM_LIMIT),
        name="moe",
    )(counts, h1, xn2, rank_t, gate_t, wgu, bgu, wdn, bdn)


def _rope_tables():
    lane = jnp.arange(LANES)

    def rows(period, rot):
        half = rot // 2
        inv_freq = ROPE_THETA ** (-jnp.arange(half, dtype=F32) / half)
        jm = lane % period
        freq = jnp.where(jm < rot, inv_freq[jm % half], 0.0)
        sign = jnp.where(jm < half, -1.0, jnp.where(jm < rot, 1.0, 0.0))
        first = (jm < half).astype(F32)
        return [freq, sign, first]

    idx_rows = rows(IDX_DIM, IDX_DIM // 4)
    only_key = (lane < IDX_DIM).astype(F32)
    return jnp.stack(rows(HEAD_DIM, HEAD_DIM // 4) + idx_rows
                     + [only_key, jnp.zeros_like(only_key)]).astype(F32)


def _group_matrix():
    g = np.arange(D_GROUP) // HEAD_DIM
    return jnp.asarray((g[:, None] == g[None, :]).astype(np.float32), dtype=BF16)


def _pad_rows(w, r0, rows):
    return jnp.zeros((rows, w.shape[1]), w.dtype).at[r0:r0 + w.shape[0]].set(w)


def kernel(x, positions, norm1_w, w_in, q_norm_w, k_norm_w, rwkv_mu, rwkv_w0, rwkv_w2, rwkv_a0,
           rwkv_a2, rwkv_g2, rwkv_k_k, rwkv_k_a, rwkv_r_k, rwkv_ln_w, rwkv_ln_b, w_out, norm2_w,
           router_w, router_b, exp_w_gu, exp_b_gu, exp_w_down, exp_b_down):
    b, t, _ = x.shape
    assert b == 1 and w_in.shape[0] == 1, "single sequence, single layer"
    assert t % DSA_KB == 0 and t % DSA_QB == 0 and t % MOE_TM == 0
    x2 = x[0]
    pos_f = positions[0].astype(F32)[:, None]
    n_sel = min(TOPK_MAX, t // 4)

    w = w_in[0]
    a0 = 3 * D_GROUP
    att_cols = a0 + D_IDX + IDX_DIM + N_IDX_HEADS
    w_att, w_rw = w[:, :att_cols], w[:, att_cols:]
    w_sm = jnp.zeros((D_MODEL, LANES), F32).at[:, :IDX_DIM + N_IDX_HEADS].set(w_att[:, a0 + D_IDX:])
    w_packed = jnp.concatenate(
        [w_att[:, :a0], w_rw[:, :a0], w_att[:, a0:a0 + D_IDX], w_sm, w_rw[:, a0:]],
        axis=1).astype(BF16)
    mu = rwkv_mu[0][None, :]
    tile8 = lambda z: jnp.tile(z, N_HEADS)[None, :]
    gsum = _group_matrix()

    tm = 256
    q, k, v, rr, rk, rv, iq, sm, lora = _in_proj(
        x2, pos_f, norm1_w, w_packed, mu, tile8(q_norm_w[0]), tile8(k_norm_w[0]), gsum,
        _rope_tables(), tm)

    vec = lambda z: z.reshape(1, D_GROUP)
    rw, (w_gu_bf, w_dn_bf) = _rwkv(
        rr, rk, rv, lora, vec(rwkv_w0[0]), _pad_rows(rwkv_w2[0], 0, LANES), vec(rwkv_a0[0]),
        _pad_rows(rwkv_a2[0], D_DECAY_LORA, LANES),
        _pad_rows(rwkv_g2[0], D_DECAY_LORA + D_AAA_LORA, LANES),
        vec(rwkv_k_k[0]), vec(rwkv_k_a[0]), vec(rwkv_r_k[0]), vec(rwkv_ln_w[0]),
        vec(rwkv_ln_b[0]), gsum, tm, to_cast=(exp_w_gu[0], exp_w_down[0]))

    heads = lambda z: z.reshape(t, N_HEADS, HEAD_DIM)
    vt = jnp.concatenate([heads(v).transpose(1, 2, 0),
                          jnp.ones((N_HEADS, DSA_V_ROWS - HEAD_DIM, t), BF16)], axis=1)
    att_t = _dsa(heads(q).transpose(1, 2, 0), k, vt, iq.T, sm[:, :IDX_DIM].astype(BF16),
                 sm[:, IDX_DIM:IDX_DIM + N_IDX_HEADS].T, n_sel)

    rwt = jnp.zeros((D_MODEL, LANES), F32).at[:, :N_EXPERTS].set(router_w[0])
    rb = jnp.full((1, LANES), NEG_BIG, F32).at[0, :N_EXPERTS].set(router_b[0])
    h1, xn2, gates = _out_proj(x2, att_t, rw, w_out[0].astype(BF16), norm2_w, rwt, rb, tm)

    rank_t, gate_t, cnt = _route(gates, MOE_TM)
    counts = cnt[:, 0, :N_EXPERTS].astype(jnp.int32).reshape(-1)
    out = _moe(counts, h1, xn2, rank_t, gate_t, w_gu_bf, exp_b_gu[0][:, None, :], w_dn_bf,
               exp_b_down[0][:, None, :], MOE_TM)
    return out[None]
```

```python
import functools

import jax
import jax.numpy as jnp
import numpy as np
from jax import lax
from jax.experimental import pallas as pl
from jax.experimental.pallas import tpu as pltpu

F32 = jnp.float32
BF16 = jnp.bfloat16
HIGHEST = lax.Precision.HIGHEST

D_MODEL = 1024
HEAD_DIM = 64
N_HEADS = 8
D_GROUP = N_HEADS * HEAD_DIM
ROPE_THETA = 500000.0
N_IDX_HEADS = 8
IDX_DIM = 32
D_IDX = N_IDX_HEADS * IDX_DIM
TOPK_MAX = 256
D_DECAY_LORA = 32
D_AAA_LORA = 32
D_GATE_LORA = 64
RWKV_GN_EPS = 64e-5
N_EXPERTS = 32
TOP_K_EXPERTS = 4
D_FF = 1024
SWIGLU_LIMIT = 7.0
SWIGLU_ALPHA = 1.702
RMS_EPS = 1e-6

LANES = 128
VMEM_LIMIT = 56 * 1024 * 1024

NEG_BIG = -1e30
INT_MIN = -(2 ** 31)


def _dot(a, b):
    return jnp.dot(a.astype(BF16), b.astype(BF16), preferred_element_type=F32)


def _dotf(a, b):
    return jnp.dot(a, b, preferred_element_type=F32, precision=HIGHEST)


def _dot3(a, b):
    a_hi = a.astype(BF16)
    b_hi = b.astype(BF16)
    a_lo = (a - a_hi.astype(F32)).astype(BF16)
    b_lo = (b - b_hi.astype(F32)).astype(BF16)
    dot = functools.partial(jnp.dot, preferred_element_type=F32)
    return dot(a_hi, b_hi) + dot(a_hi, b_lo) + dot(a_lo, b_hi)


def _dot_nt(a, b):
    return lax.dot_general(a.astype(BF16), b.astype(BF16), (((1,), (1,)), ((), ())),
                           preferred_element_type=F32)


def _dot_tn(a, b):
    return lax.dot_general(a.astype(BF16), b.astype(BF16), (((0,), (0,)), ((), ())),
                           preferred_element_type=F32)


def _group_sum(z, g_ref):
    hi = z.astype(BF16)
    lo = (z - hi.astype(F32)).astype(BF16)
    g = g_ref[...]
    return (jnp.dot(hi, g, preferred_element_type=F32)
            + jnp.dot(lo, g, preferred_element_type=F32))


_C_Q, _C_K, _C_V, _C_RR, _C_RK, _C_RV = (i * D_GROUP for i in range(6))
_C_IQ = 6 * D_GROUP
_C_SM = _C_IQ + D_IDX
_C_LORA = _C_SM + LANES
D_IN_PACKED = _C_LORA + LANES
D_SHIFT = 3 * D_GROUP + LANES


def _in_proj_kernel(x_ref, pos_ref, n1w_ref, w_ref, mu_ref, qnw_ref, knw_ref, g_ref, rope_ref,
                    q_ref, k_ref, v_ref, rr_ref, rk_ref, rv_ref, iq_ref, sm_ref, lora_ref,
                    carry_ref):
    tm = x_ref.shape[0]

    @pl.when(pl.program_id(0) == 0)
    def _():
        carry_ref[...] = jnp.zeros_like(carry_ref)

    x = x_ref[...]
    xn = x * lax.rsqrt(jnp.mean(x * x, axis=-1, keepdims=True) + RMS_EPS) * n1w_ref[...]
    xb = xn.astype(BF16)
    pos = pos_ref[...]
    rope = rope_ref[...]

    def tables(frow, srow):
        ang = pos * rope[frow:frow + 1, :]
        return jnp.cos(ang), jnp.sin(ang) * rope[srow:srow + 1, :]

    def widen(z, reps):
        return jnp.concatenate([z] * reps, axis=1)

    def rotary(z, c, s, first_row, half):
        w = z.shape[1]
        first = jnp.concatenate([rope[first_row:first_row + 1, :]] * (w // LANES), axis=1) > 0.5
        partner = jnp.where(first, pltpu.roll(z, w - half, 1), pltpu.roll(z, half, 1))
        return z * c + partner * s

    def head_norm(z, w_row):
        ms = _group_sum(z * z, g_ref) * (1.0 / HEAD_DIM)
        return z * lax.rsqrt(ms + RMS_EPS) * w_row

    def proj(c0, width):
        return jnp.dot(xb, w_ref[:, c0:c0 + width], preferred_element_type=F32)

    def shift(z, c0):
        width = z.shape[1]
        row = lax.broadcasted_iota(jnp.int32, z.shape, 0)
        prev = jnp.where(row == 0, carry_ref[0:1, c0:c0 + width], pltpu.roll(z, 1, 0))
        carry_ref[0:1, c0:c0 + width] = z[tm - 1:tm, :]
        return z + (prev - z) * mu_ref[:, c0:c0 + width]

    cq, sq = (widen(z, D_GROUP // LANES) for z in tables(0, 1))
    q = rotary(head_norm(proj(_C_Q, D_GROUP), qnw_ref[...]), cq, sq, 2, HEAD_DIM // 8)
    q_ref[...] = q.astype(BF16)
    k = rotary(head_norm(proj(_C_K, D_GROUP), knw_ref[...]), cq, sq, 2, HEAD_DIM // 8)
    k_ref[...] = k.astype(BF16)
    v_ref[...] = proj(_C_V, D_GROUP).astype(BF16)

    rr_ref[...] = shift(proj(_C_RR, D_GROUP), 0)
    rk_ref[...] = shift(proj(_C_RK, D_GROUP), D_GROUP)
    rv_ref[...] = shift(proj(_C_RV, D_GROUP), 2 * D_GROUP)
    lora_ref[...] = shift(proj(_C_LORA, LANES), 3 * D_GROUP)

    ci, si = tables(3, 4)
    iq_ref[...] = rotary(proj(_C_IQ, D_IDX), widen(ci, D_IDX // LANES), widen(si, D_IDX // LANES),
                         5, IDX_DIM // 8)
    only_key = rope[6:7, :]
    sm_ref[...] = rotary(proj(_C_SM, LANES), ci * only_key + (1.0 - only_key), si * only_key,
                         5, IDX_DIM // 8)


def _in_proj(x2, pos_f, n1w, w_packed, mu_packed, qnw, knw, gmat, rope, tm):
    t = x2.shape[0]
    full = lambda shape: pl.BlockSpec(shape, lambda i: (0,) * len(shape))
    row = lambda width: pl.BlockSpec((tm, width), lambda i: (i, 0))
    out_shapes = (
        jax.ShapeDtypeStruct((t, D_GROUP), BF16),
        jax.ShapeDtypeStruct((t, D_GROUP), BF16),
        jax.ShapeDtypeStruct((t, D_GROUP), BF16),
        jax.ShapeDtypeStruct((t, D_GROUP), F32),
        jax.ShapeDtypeStruct((t, D_GROUP), F32),
        jax.ShapeDtypeStruct((t, D_GROUP), F32),
        jax.ShapeDtypeStruct((t, D_IDX), F32),
        jax.ShapeDtypeStruct((t, LANES), F32),
        jax.ShapeDtypeStruct((t, LANES), F32),
    )
    return pl.pallas_call(
        _in_proj_kernel,
        grid=(t // tm,),
        in_specs=[row(D_MODEL), row(1), full((1, D_MODEL)), full((D_MODEL, D_IN_PACKED)),
                  full((1, D_SHIFT)), full((1, D_GROUP)), full((1, D_GROUP)),
                  full((D_GROUP, D_GROUP)), full((8, LANES))],
        out_specs=[row(D_GROUP)] * 6 + [row(D_IDX), row(LANES), row(LANES)],
        out_shape=out_shapes,
        scratch_shapes=[pltpu.VMEM((8, D_SHIFT), F32)],
        compiler_params=pltpu.CompilerParams(dimension_semantics=("arbitrary",),
                                             vmem_limit_bytes=VMEM_LIMIT),
        name="in_proj",
    )(x2, pos_f, n1w, w_packed, mu_packed, qnw, knw, gmat, rope)


RWKV_CHUNK = 64
RWKV_UNROLL = 4
PAIR = 2 * HEAD_DIM


def _rwkv_kernel(r_ref, k_ref, v_ref, lora_ref, w0_ref, w2_ref, a0_ref, a2_ref, g2_ref,
                 kk_ref, ka_ref, rk_ref, lnw_ref, lnb_ref, g_ref, *rest, n_cast):
    cast_in, o_ref, cast_out = rest[:n_cast], rest[n_cast], rest[n_cast + 1:2 * n_cast + 1]
    s_ref, ld_s, r_s, k2_s, b_s, kk_s, y_s = rest[2 * n_cast + 1:]
    for src, dst in zip(cast_in, cast_out):
        dst[...] = src[...].astype(BF16)
    tm = r_ref.shape[0]
    n_chunks = tm // RWKV_CHUNK
    n_pairs = D_GROUP // PAIR
    c = RWKV_CHUNK

    @pl.when(pl.program_id(0) == 0)
    def _():
        s_ref[...] = jnp.zeros_like(s_ref)

    lora = lora_ref[...]
    r = r_ref[...]
    k = k_ref[...]
    v = v_ref[...]
    zarg = w0_ref[...] + _dot3(jnp.tanh(lora), w2_ref[...])
    sp = jnp.maximum(-zarg, 0.0) + jnp.log1p(jnp.exp(-jnp.abs(zarg)))
    ld_s[...] = -jnp.exp(-sp - 0.5)
    a = jax.nn.sigmoid(a0_ref[...] + _dot3(lora, a2_ref[...]))
    g = _dot3(jax.nn.sigmoid(lora), g2_ref[...])
    kk = k * kk_ref[...]
    kk = kk * lax.rsqrt(jnp.maximum(_group_sum(kk * kk, g_ref), 1e-24))
    k2 = k * (1.0 + (a - 1.0) * ka_ref[...])
    bonus = _group_sum(r * k2 * rk_ref[...], g_ref) * v
    r_s[...] = r
    k2_s[...] = k2
    kk_s[...] = kk
    b_s[...] = kk * a

    row = lax.broadcasted_iota(jnp.int32, (2 * c, 2 * c), 0)
    col = lax.broadcasted_iota(jnp.int32, (2 * c, 2 * c), 1)
    same_head = (row >= c) == (col >= c)
    strict = same_head & (col < row)
    incl = same_head & (col <= row)
    eye = (row == col).astype(F32)
    tri = (lax.broadcasted_iota(jnp.int32, (c, c), 1)
           <= lax.broadcasted_iota(jnp.int32, (c, c), 0)).astype(F32)
    lane = lax.broadcasted_iota(jnp.int32, (c, PAIR), 1)
    head0 = lane < HEAD_DIM

    def stack(z):
        return jnp.concatenate([jnp.where(head0, z, 0.0), jnp.where(head0, 0.0, z)], axis=0)

    def chunk_body(ci, carry):
        c2 = 2 * c
        units = [(cc, p) for cc in range(RWKV_UNROLL) for p in range(n_pairs)]
        uid = range(len(units))
        rows = [pl.multiple_of((ci * RWKV_UNROLL + cc) * c, c) for cc in range(RWKV_UNROLL)]
        ld_all = [ld_s[pl.ds(r0, c), :] for r0 in rows]
        cum_all = [_dotf(tri, ld) for ld in ld_all]
        rt, kt, bt, kp, vs, lhs, rhs, gam_end = [], [], [], [], [], [], [], []
        for cc, p in units:
            r0 = rows[cc]
            cols = slice(p * PAIR, (p + 1) * PAIR)
            ld = ld_all[cc][:, cols]
            cum = cum_all[cc][:, cols]
            gam = jnp.exp(cum)
            inv = jnp.exp(-cum)
            gam_prev = jnp.exp(cum - ld)
            gam_end.append(gam[c - 1:c, :])
            rt.append(stack(r_s[pl.ds(r0, c), cols] * gam))
            kt_raw = k2_s[pl.ds(r0, c), cols] * inv
            bt_raw = b_s[pl.ds(r0, c), cols] * inv
            kt.append(stack(kt_raw))
            bt.append(stack(bt_raw))
            kp.append(stack(kk_s[pl.ds(r0, c), cols] * gam_prev))
            vs.append(stack(v_ref[pl.ds(r0, c), cols]))
            lhs.append(jnp.concatenate([kp[-1], rt[-1]], axis=0))
            rhs.append(jnp.concatenate([bt_raw, bt_raw, kt_raw, kt_raw], axis=0))
        aa = [_dot_nt(lhs[u], rhs[u]) for u in uid]
        a_kk = [jnp.where(strict, aa[u][:c2, c2:], 0.0) for u in uid]
        a_rb = [jnp.where(incl, aa[u][c2:, :c2], 0.0) for u in uid]
        a_rk = [jnp.where(incl, aa[u][c2:, c2:], 0.0) for u in uid]
        n = [-jnp.where(strict, aa[u][:c2, :c2], 0.0) for u in uid]
        prod = [eye + n[u] for u in uid]
        n = [_dot(n[u], n[u]) for u in uid]
        for _ in range(int(np.log2(c)) - 2):
            both = [_dot(jnp.concatenate([n[u], prod[u]], axis=0), n[u]) for u in uid]
            prod = [prod[u] + both[u][c2:] for u in uid]
            n = [both[u][:c2] for u in uid]
        av = [_dot(jnp.concatenate([a_kk[u], a_rk[u]], axis=0), vs[u]) for u in uid]
        tinv = [prod[u] + _dot(prod[u], n[u]) for u in uid]
        wu = [_dot(tinv[u], jnp.concatenate([kp[u], av[u][:c2]], axis=1)) for u in uid]
        rb = [_dot(a_rb[u], wu[u]) for u in uid]
        tn = [_dot_tn(jnp.concatenate([wu[u], vs[u]], axis=1),
                      jnp.concatenate([bt[u], kt[u]], axis=1)) for u in uid]
        r2 = [rt[u] - rb[u][:, :PAIR] for u in uid]
        y2 = [av[u][c2:] - rb[u][:, PAIR:] for u in uid]
        s_mix = [(eye - tn[u][:PAIR, :PAIR]) * gam_end[u] for u in uid]
        s_add = [(tn[u][2 * PAIR:, PAIR:] - tn[u][PAIR:2 * PAIR, :PAIR]) * gam_end[u] for u in uid]
        state = [s_ref[p] for p in range(n_pairs)]
        for cc in range(RWKV_UNROLL):
            us = [cc * n_pairs + p for p in range(n_pairs)]
            ys = [_dot_nt(r2[u], state[p]) + y2[u] for p, u in enumerate(us)]
            state = [_dot(state[p], s_mix[u]) + s_add[u] for p, u in enumerate(us)]
            for p in range(n_pairs):
                y_s[pl.ds(rows[cc], c), p * PAIR:(p + 1) * PAIR] = ys[p][:c] + ys[p][c:]
        for p in range(n_pairs):
            s_ref[p] = state[p]
        return carry

    lax.fori_loop(0, n_chunks // RWKV_UNROLL, chunk_body, 0)

    y = y_s[...]
    mean = _group_sum(y, g_ref) * (1.0 / HEAD_DIM)
    yc = y - mean
    var = _group_sum(yc * yc, g_ref) * (1.0 / HEAD_DIM)
    yn = yc * lax.rsqrt(var + RWKV_GN_EPS) * lnw_ref[...] + lnb_ref[...]
    o_ref[...] = (yn + bonus) * g


def _rwkv(rr, rk, rv, lora, w0, w2p, a0, a2p, g2p, k_k, k_a, r_k, ln_w, ln_b, gsum, tm,
          to_cast=()):
    t = rr.shape[0]
    full = lambda shape: pl.BlockSpec(shape, lambda i: (0,) * len(shape))
    row = lambda width: pl.BlockSpec((tm, width), lambda i: (i, 0))
    vec = full((1, D_GROUP))
    big = pltpu.VMEM((tm, D_GROUP), F32)
    slabs = [w.reshape(t // tm, -1, w.shape[-1]) for w in to_cast]
    cast_specs = [pl.BlockSpec((1,) + w.shape[1:], lambda i: (i, 0, 0)) for w in slabs]
    outs = pl.pallas_call(
        functools.partial(_rwkv_kernel, n_cast=len(slabs)),
        grid=(t // tm,),
        in_specs=[row(D_GROUP), row(D_GROUP), row(D_GROUP), row(LANES),
                  vec, full((LANES, D_GROUP)), vec, full((LANES, D_GROUP)),
                  full((LANES, D_GROUP)), vec, vec, vec, vec, vec, full((D_GROUP, D_GROUP))]
        + cast_specs,
        out_specs=[row(D_GROUP)] + cast_specs,
        out_shape=[jax.ShapeDtypeStruct((t, D_GROUP), F32)]
        + [jax.ShapeDtypeStruct(w.shape, BF16) for w in slabs],
        scratch_shapes=[pltpu.VMEM((D_GROUP // PAIR, PAIR, PAIR), F32),
                        big, big, big, big, big, big],
        compiler_params=pltpu.CompilerParams(dimension_semantics=("arbitrary",),
                                             vmem_limit_bytes=VMEM_LIMIT),
        name="rwkv",
    )(rr, rk, rv, lora, w0, w2p, a0, a2p, g2p, k_k, k_a, r_k, ln_w, ln_b, gsum, *slabs)
    return outs[0], [o.reshape(w.shape) for o, w in zip(outs[1:], to_cast)]


DSA_QB = 256
DSA_KB = 512
M_INIT = -5e29
DSA_HEAD_GROUP = 4
DSA_ACC_ROWS = 32
DSA_GROUPS = 256
DSA_TOP = 10
DSA_V_ROWS = HEAD_DIM + 16


def _dsa_kernel(qi_ref, kj_ref, qt_ref, k_ref, vt_ref, iqt_ref, ik_ref, iwt_ref, o_ref,
                key_s, thr_s, need_s, over_s, m_s, acc_s, bdq_s, top_s, *, n_sel):
    qb = qt_ref.shape[2]
    kb = k_ref.shape[0]
    step = pl.program_id(0)
    qi = qi_ref[step]
    kj = kj_ref[step]
    q0 = qi * qb
    j_last = (q0 + qb - 1) // kb
    n_kc = j_last + 1
    idx_scale = float((IDX_DIM * N_IDX_HEADS) ** -0.5)

    def causal(jblk):
        s_pos = jblk * kb + lax.broadcasted_iota(jnp.int32, (kb, qb), 0)
        t_pos = q0 + lax.broadcasted_iota(jnp.int32, (kb, qb), 1)
        return s_pos <= t_pos

    @pl.when(kj == 0)
    def _():
        iqt = iqt_ref[...].astype(BF16)
        iwt = iwt_ref[...] * idx_scale

        def score_chunk(kc, diagonal):
            ik = ik_ref[pl.ds(pl.multiple_of(kc * kb, kb), kb), :]
            score = jnp.zeros((kb, qb), F32)
            for h in range(N_IDX_HEADS):
                d = jnp.dot(ik, iqt[h * IDX_DIM:(h + 1) * IDX_DIM, :],
                            preferred_element_type=F32)
                score = score + jnp.maximum(d, 0.0) * iwt[h:h + 1, :]
            bits = pltpu.bitcast(score, jnp.int32)
            keys = jnp.where(bits < 0, bits ^ jnp.int32(0x7FFFFFFF), bits)
            if diagonal:
                keys = jnp.where(causal(kc), keys, jnp.int32(INT_MIN))
            key_s[kc] = keys
            xs = [keys[r * DSA_GROUPS:(r + 1) * DSA_GROUPS] for r in range(kb // DSA_GROUPS)]
            for lvl in range(DSA_TOP):
                s = top_s[lvl]
                for r in range(len(xs)):
                    s, xs[r] = jnp.maximum(s, xs[r]), jnp.minimum(s, xs[r])
                top_s[lvl] = s

        top_s[...] = jnp.full_like(top_s, INT_MIN)

        def full_chunk(kc, carry):
            score_chunk(kc, False)
            return carry

        lax.fori_loop(0, j_last, full_chunk, 0)
        score_chunk(j_last, True)

        def select(count_keys):
            def bit_step(b, cur):
                bit = lax.shift_left(jnp.int32(1), jnp.int32(31) - b)
                cand = (cur | bit) ^ jnp.int32(INT_MIN)
                cnt = jnp.sum(count_keys(cand), axis=0, keepdims=True)
                return jnp.where(cnt >= float(n_sel), cur | bit, cur)

            cur = lax.fori_loop(0, 32, bit_step, jnp.zeros((1, qb), jnp.int32))
            return jnp.maximum(cur ^ jnp.int32(INT_MIN), jnp.int32(INT_MIN + 1))

        def part_count(keys, cand):
            ind = jnp.where(keys >= cand, 1.0, 0.0)
            return jnp.sum(ind.reshape(-1, DSA_ACC_ROWS, qb), axis=0)

        def count_top(cand):
            return lax.fori_loop(0, DSA_TOP, lambda lvl, a: a + part_count(top_s[lvl], cand),
                                 jnp.zeros((DSA_ACC_ROWS, qb), F32))

        def count_all(cand):
            return lax.fori_loop(0, n_kc, lambda kc, a: a + part_count(key_s[kc], cand),
                                 jnp.zeros((DSA_ACC_ROWS, qb), F32))

        def settle(count_keys):
            thr = select(count_keys)
            thr_s[...] = thr
            above = jnp.sum(count_keys(thr + 1), axis=0, keepdims=True)
            upto = jnp.sum(count_keys(thr), axis=0, keepdims=True)
            need_s[...] = float(n_sel) - above
            over_s[...] = upto - float(n_sel)
            return thr

        thr = settle(count_top)
        hidden = jnp.where(top_s[DSA_TOP - 1] >= thr, 1.0, 0.0)

        @pl.when(jnp.max(hidden) > 0.0)
        def _():
            settle(count_all)

        @pl.when(jnp.max(over_s[...]) > 0.0)
        def _():
            thr = thr_s[...]
            need = need_s[...]
            sub = LANES
            earlier = jnp.where(lax.broadcasted_iota(jnp.int32, (sub, sub), 1)
                                < lax.broadcasted_iota(jnp.int32, (sub, sub), 0), 1.0, 0.0).astype(BF16)

            def drop_surplus(kc, seen):
                keys = key_s[kc]
                tie = keys == thr
                tie_f = jnp.where(tie, 1.0, 0.0)
                ranks = []
                for r in range(kb // sub):
                    part = tie_f[r * sub:(r + 1) * sub]
                    ranks.append(seen + jnp.dot(earlier, part.astype(BF16),
                                                preferred_element_type=F32))
                    seen = seen + jnp.sum(part, axis=0, keepdims=True)
                rank = jnp.concatenate(ranks, axis=0)
                key_s[kc] = jnp.where(tie & (rank >= need), thr - 1, keys)
                return seen

            lax.fori_loop(0, n_kc, drop_surplus, jnp.zeros((1, qb), F32))

        m_s[...] = jnp.full_like(m_s, M_INIT)
        acc_s[...] = jnp.zeros_like(acc_s)
        bdq_s[...] = jnp.zeros_like(bdq_s)
        for h in range(N_HEADS):
            g, hh = divmod(h, DSA_HEAD_GROUP)
            bdq_s[g, hh * HEAD_DIM:(hh + 1) * HEAD_DIM, hh * qb:(hh + 1) * qb] = (
                qt_ref[h] * jnp.asarray(HEAD_DIM ** -0.5, BF16))

    mask = key_s[kj] >= thr_s[...]
    hg = DSA_HEAD_GROUP
    st_all = [jnp.dot(k_ref[:, g * hg * HEAD_DIM:(g + 1) * hg * HEAD_DIM], bdq_s[g],
                      preferred_element_type=F32) for g in range(N_HEADS // hg)]
    for g in range(N_HEADS // hg):
        st_g = st_all[g]
        for hh in range(hg):
            h = g * hg + hh
            st = jnp.where(mask, st_g[:, hh * qb:(hh + 1) * qb], NEG_BIG)
            m_old = m_s[h]
            part = jnp.max(st.reshape(kb // DSA_ACC_ROWS, DSA_ACC_ROWS, qb), axis=0)
            m_new = jnp.maximum(m_old, jnp.max(part, axis=0, keepdims=True))
            p = jnp.exp((st - m_new).astype(BF16))
            acc_s[h] = jnp.exp(m_old - m_new) * acc_s[h] + jnp.dot(
                vt_ref[h], p, preferred_element_type=F32)
            m_s[h] = m_new

    @pl.when(kj == j_last)
    def _():
        for h in range(N_HEADS):
            acc = acc_s[h]
            o_ref[h] = acc[:HEAD_DIM] / acc[HEAD_DIM:HEAD_DIM + 1]


def _dsa(qt, k, vt, iqt, ik, iwt, n_sel):
    t = ik.shape[0]
    qb, kb = DSA_QB, DSA_KB
    assert n_sel <= DSA_GROUPS and kb % DSA_GROUPS == 0
    nq, nk = t // qb, t // kb
    pairs = [(i, j) for i in range(nq) for j in range((i * qb + qb - 1) // kb + 1)]
    qi = jnp.asarray(np.array([p[0] for p in pairs], np.int32))
    kj = jnp.asarray(np.array([p[1] for p in pairs], np.int32))
    grid_spec = pltpu.PrefetchScalarGridSpec(
        num_scalar_prefetch=2,
        grid=(len(pairs),),
        in_specs=[pl.BlockSpec((N_HEADS, HEAD_DIM, qb), lambda s, qi, kj: (0, 0, qi[s])),
                  pl.BlockSpec((kb, D_GROUP), lambda s, qi, kj: (kj[s], 0)),
                  pl.BlockSpec((N_HEADS, DSA_V_ROWS, kb), lambda s, qi, kj: (0, 0, kj[s])),
                  pl.BlockSpec((D_IDX, qb), lambda s, qi, kj: (0, qi[s])),
                  pl.BlockSpec((t, IDX_DIM), lambda s, qi, kj: (0, 0)),
                  pl.BlockSpec((N_IDX_HEADS, qb), lambda s, qi, kj: (0, qi[s]))],
        out_specs=pl.BlockSpec((N_HEADS, HEAD_DIM, qb), lambda s, qi, kj: (0, 0, qi[s])),
        scratch_shapes=[pltpu.VMEM((nk, kb, qb), jnp.int32),
                        pltpu.VMEM((1, qb), jnp.int32),
                        pltpu.VMEM((1, qb), F32),
                        pltpu.VMEM((1, qb), F32),
                        pltpu.VMEM((N_HEADS, 1, qb), F32),
                        pltpu.VMEM((N_HEADS, DSA_V_ROWS, qb), F32),
                        pltpu.VMEM((N_HEADS // DSA_HEAD_GROUP, DSA_HEAD_GROUP * HEAD_DIM,
                                    DSA_HEAD_GROUP * qb), BF16),
                        pltpu.VMEM((DSA_TOP, DSA_GROUPS, qb), jnp.int32)])
    return pl.pallas_call(
        functools.partial(_dsa_kernel, n_sel=n_sel),
        grid_spec=grid_spec,
        out_shape=jax.ShapeDtypeStruct((N_HEADS, HEAD_DIM, t), F32),
        compiler_params=pltpu.CompilerParams(dimension_semantics=("arbitrary",),
                                             vmem_limit_bytes=VMEM_LIMIT),
        name="dsa",
    )(qi, kj, qt, k, vt, iqt, ik, iwt)


def _out_proj_kernel(x_ref, att_ref, rw_ref, wo_ref, n2w_ref, rwt_ref, rb_ref,
                     h_ref, xn_ref, gate_ref):
    tm = x_ref.shape[0]
    att = att_ref[...].reshape(D_GROUP, tm).T
    mix = jnp.concatenate([att, rw_ref[...]], axis=1)
    acc = x_ref[...] + _dot(mix, wo_ref[...])
    h_ref[...] = acc
    xn = acc * lax.rsqrt(jnp.mean(acc * acc, axis=-1, keepdims=True) + RMS_EPS) * n2w_ref[...]
    xn_ref[...] = xn.astype(BF16)
    logits = _dotf(xn, rwt_ref[...]) + rb_ref[...]
    lane = lax.broadcasted_iota(jnp.int32, logits.shape, 1)
    work = logits
    vals, hots = [], []
    for _ in range(TOP_K_EXPERTS):
        m = jnp.max(work, axis=1, keepdims=True)
        idx = jnp.min(jnp.where(work == m, lane, LANES), axis=1, keepdims=True)
        hot = lane == idx
        vals.append(m)
        hots.append(hot)
        work = jnp.where(hot, -jnp.inf, work)
    es = [jnp.exp(vv - vals[0]) for vv in vals]
    denom = es[0] + es[1] + es[2] + es[3]
    gates = jnp.zeros_like(logits)
    for e, hot in zip(es, hots):
        gates = gates + jnp.where(hot, e / denom, 0.0)
    gate_ref[...] = gates


def _out_proj(x2, att_t, rw, wo, n2w, rwt, rb, tm):
    t = x2.shape[0]
    full = lambda shape: pl.BlockSpec(shape, lambda i: (0,) * len(shape))
    row = lambda width: pl.BlockSpec((tm, width), lambda i: (i, 0))
    return pl.pallas_call(
        _out_proj_kernel,
        grid=(t // tm,),
        in_specs=[row(D_MODEL), pl.BlockSpec((N_HEADS, HEAD_DIM, tm), lambda i: (0, 0, i)),
                  row(D_GROUP), full((2 * D_GROUP, D_MODEL)), full((1, D_MODEL)),
                  full((D_MODEL, LANES)), full((1, LANES))],
        out_specs=[row(D_MODEL), row(D_MODEL), row(LANES)],
        out_shape=(jax.ShapeDtypeStruct((t, D_MODEL), F32),
                   jax.ShapeDtypeStruct((t, D_MODEL), BF16),
                   jax.ShapeDtypeStruct((t, LANES), F32)),
        compiler_params=pltpu.CompilerParams(dimension_semantics=("arbitrary",),
                                             vmem_limit_bytes=VMEM_LIMIT),
        name="out_proj",
    )(x2, att_t, rw, wo, n2w, rwt, rb)


MOE_TM = 1024
MOE_RB = 144
MOE_FC = 512
MOE_EXPERTS_PER_STEP = 2


def _route_kernel(gate_ref, rankt_ref, gatet_ref, cnt_ref):
    tm = gate_ref.shape[0]
    gates = gate_ref[...]
    hot = gates > 0.0
    ind = jnp.where(hot, 1.0, 0.0)
    before = (lax.broadcasted_iota(jnp.int32, (tm, tm), 1)
              < lax.broadcasted_iota(jnp.int32, (tm, tm), 0))
    rank = jnp.dot(jnp.where(before, 1.0, 0.0).astype(BF16), ind.astype(BF16),
                   preferred_element_type=F32)
    rankt_ref[...] = jnp.where(hot, rank, -1.0).T
    gatet_ref[...] = gates.T
    cnt = jnp.sum(ind, axis=0, keepdims=True)
    cnt_ref[...] = jnp.broadcast_to(cnt[None], cnt_ref.shape)


def _route(gates, tm):
    t = gates.shape[0]
    return pl.pallas_call(
        _route_kernel,
        grid=(t // tm,),
        in_specs=[pl.BlockSpec((tm, LANES), lambda i: (i, 0))],
        out_specs=[pl.BlockSpec((LANES, tm), lambda i: (0, i)),
                   pl.BlockSpec((LANES, tm), lambda i: (0, i)),
                   pl.BlockSpec((1, 8, LANES), lambda i: (i, 0, 0))],
        out_shape=(jax.ShapeDtypeStruct((LANES, t), F32),
                   jax.ShapeDtypeStruct((LANES, t), F32),
                   jax.ShapeDtypeStruct((t // tm, 8, LANES), F32)),
        compiler_params=pltpu.CompilerParams(dimension_semantics=("arbitrary",),
                                             vmem_limit_bytes=VMEM_LIMIT),
        name="route",
    )(gates)


def _moe_kernel(cnt_ref, h_ref, xn_ref, rankt_ref, gatet_ref, wgu_ref, bgu_ref,
                wdn_ref, bdn_ref, o_ref):
    i = pl.program_id(0)
    tm = h_ref.shape[0]

    @pl.when(pl.program_id(1) == 0)
    def _():
        o_ref[...] = h_ref[...]

    for sub in range(MOE_EXPERTS_PER_STEP):
        _moe_expert(pl.program_id(1) * MOE_EXPERTS_PER_STEP + sub, sub, i, tm, cnt_ref, xn_ref,
                    rankt_ref, gatet_ref, wgu_ref, bgu_ref, wdn_ref, bdn_ref, o_ref)


def _moe_expert(e, slot, i, tm, cnt_ref, xn_ref, rankt_ref, gatet_ref, wgu_ref, bgu_ref,
                wdn_ref, bdn_ref, o_ref):
    n_rows = cnt_ref[i * N_EXPERTS + e]
    n_blocks = (n_rows + MOE_RB - 1) // MOE_RB
    rank_row = rankt_ref[pl.ds(e, 1), :]
    gate_row = gatet_ref[pl.ds(e, 1), :]

    def row_block(bi, carry):
        r0 = (bi * MOE_RB).astype(F32)
        rowid = lax.broadcasted_iota(jnp.int32, (MOE_RB, tm), 0).astype(F32) + r0
        hit = rank_row == rowid
        sel = jnp.where(hit, 1.0, 0.0).astype(BF16)
        gate = jnp.sum(jnp.where(hit, gate_row, 0.0), axis=1, keepdims=True)
        xb = jnp.dot(sel, xn_ref[...], preferred_element_type=F32).astype(BF16)
        y = jnp.zeros((MOE_RB, D_MODEL), F32)
        ups = []
        for fc in range(D_FF // MOE_FC):
            c0 = fc * MOE_FC
            ups.append((jnp.dot(xb, wgu_ref[slot, :, c0:c0 + MOE_FC], preferred_element_type=F32),
                        jnp.dot(xb, wgu_ref[slot, :, D_FF + c0:D_FF + c0 + MOE_FC],
                                preferred_element_type=F32)))
        for fc in range(D_FF // MOE_FC):
            c0 = fc * MOE_FC
            hg = ups[fc][0] + bgu_ref[slot, :, c0:c0 + MOE_FC]
            hl = ups[fc][1] + bgu_ref[slot, :, D_FF + c0:D_FF + c0 + MOE_FC]
            glu = jnp.minimum(hg, SWIGLU_LIMIT)
            lin = jnp.clip(hl, -SWIGLU_LIMIT, SWIGLU_LIMIT)
            act = (lin + 1.0) * glu * jax.nn.sigmoid(SWIGLU_ALPHA * glu)
            y = y + jnp.dot(act.astype(BF16), wdn_ref[slot, c0:c0 + MOE_FC, :],
                            preferred_element_type=F32)
        y = ((y + bdn_ref[slot]) * gate).astype(BF16)
        o_ref[...] += _dot_tn(sel, y)
        return carry

    lax.fori_loop(0, n_blocks, row_block, 0)


def _moe(counts, h1, xn2, rank_t, gate_t, wgu, bgu, wdn, bdn, tm):
    t = h1.shape[0]
    eps = MOE_EXPERTS_PER_STEP
    row = lambda width: pl.BlockSpec((tm, width), lambda i, e, c: (i, 0))
    grid_spec = pltpu.PrefetchScalarGridSpec(
        num_scalar_prefetch=1,
        grid=(t // tm, N_EXPERTS // eps),
        in_specs=[row(D_MODEL), row(D_MODEL),
                  pl.BlockSpec((LANES, tm), lambda i, e, c: (0, i)),
                  pl.BlockSpec((LANES, tm), lambda i, e, c: (0, i)),
                  pl.BlockSpec((eps, D_MODEL, 2 * D_FF), lambda i, e, c: (e, 0, 0)),
                  pl.BlockSpec((eps, 1, 2 * D_FF), lambda i, e, c: (e, 0, 0)),
                  pl.BlockSpec((eps, D_FF, D_MODEL), lambda i, e, c: (e, 0, 0)),
                  pl.BlockSpec((eps, 1, D_MODEL), lambda i, e, c: (e, 0, 0))],
        out_specs=row(D_MODEL))
    return pl.pallas_call(
        _moe_kernel,
        grid_spec=grid_spec,
        out_shape=jax.ShapeDtypeStruct((t, D_MODEL), F32),
        compiler_params=pltpu.CompilerParams(dimension_semantics=("arbitrary", "arbitrary"),
                                             vmem_limit_bytes=VMEM_LIMIT),
        name="moe",
    )(counts, h1, xn2, rank_t, gate_t, wgu, bgu, wdn, bdn)


def _rope_tables():
    lane = jnp.arange(LANES)

    def rows(period, rot):
        half = rot // 2
        inv_freq = ROPE_THETA ** (-jnp.arange(half, dtype=F32) / half)
        jm = lane % period
        freq = jnp.where(jm < rot, inv_freq[jm % half], 0.0)
        sign = jnp.where(jm < half, -1.0, jnp.where(jm < rot, 1.0, 0.0))
        first = (jm < half).astype(F32)
        return [freq, sign, first]

    idx_rows = rows(IDX_DIM, IDX_DIM // 4)
    only_key = (lane < IDX_DIM).astype(F32)
    return jnp.stack(rows(HEAD_DIM, HEAD_DIM // 4) + idx_rows
                     + [only_key, jnp.zeros_like(only_key)]).astype(F32)


def _group_matrix():
    g = np.arange(D_GROUP) // HEAD_DIM
    return jnp.asarray((g[:, None] == g[None, :]).astype(np.float32), dtype=BF16)


def _pad_rows(w, r0, rows):
    return jnp.zeros((rows, w.shape[1]), w.dtype).at[r0:r0 + w.shape[0]].set(w)


def kernel(x, positions, norm1_w, w_in, q_norm_w, k_norm_w, rwkv_mu, rwkv_w0, rwkv_w2, rwkv_a0,
           rwkv_a2, rwkv_g2, rwkv_k_k, rwkv_k_a, rwkv_r_k, rwkv_ln_w, rwkv_ln_b, w_out, norm2_w,
           router_w, router_b, exp_w_gu, exp_b_gu, exp_w_down, exp_b_down):
    b, t, _ = x.shape
    assert b == 1 and w_in.shape[0] == 1, "single sequence, single layer"
    assert t % DSA_KB == 0 and t % DSA_QB == 0 and t % MOE_TM == 0
    x2 = x[0]
    pos_f = positions[0].astype(F32)[:, None]
    n_sel = min(TOPK_MAX, t // 4)

    w = w_in[0]
    a0 = 3 * D_GROUP
    att_cols = a0 + D_IDX + IDX_DIM + N_IDX_HEADS
    w_att, w_rw = w[:, :att_cols], w[:, att_cols:]
    w_sm = jnp.zeros((D_MODEL, LANES), F32).at[:, :IDX_DIM + N_IDX_HEADS].set(w_att[:, a0 + D_IDX:])
    w_packed = jnp.concatenate(
        [w_att[:, :a0], w_rw[:, :a0], w_att[:, a0:a0 + D_IDX], w_sm, w_rw[:, a0:]],
        axis=1).astype(BF16)
    mu = rwkv_mu[0][None, :]
    tile8 = lambda z: jnp.tile(z, N_HEADS)[None, :]
    gsum = _group_matrix()

    tm = 256
    q, k, v, rr, rk, rv, iq, sm, lora = _in_proj(
        x2, pos_f, norm1_w, w_packed, mu, tile8(q_norm_w[0]), tile8(k_norm_w[0]), gsum,
        _rope_tables(), tm)

    vec = lambda z: z.reshape(1, D_GROUP)
    rw, (w_gu_bf, w_dn_bf) = _rwkv(
        rr, rk, rv, lora, vec(rwkv_w0[0]), _pad_rows(rwkv_w2[0], 0, LANES), vec(rwkv_a0[0]),
        _pad_rows(rwkv_a2[0], D_DECAY_LORA, LANES),
        _pad_rows(rwkv_g2[0], D_DECAY_LORA + D_AAA_LORA, LANES),
        vec(rwkv_k_k[0]), vec(rwkv_k_a[0]), vec(rwkv_r_k[0]), vec(rwkv_ln_w[0]),
        vec(rwkv_ln_b[0]), gsum, tm, to_cast=(exp_w_gu[0], exp_w_down[0]))

    heads = lambda z: z.reshape(t, N_HEADS, HEAD_DIM)
    vt = jnp.concatenate([heads(v).transpose(1, 2, 0),
                          jnp.ones((N_HEADS, DSA_V_ROWS - HEAD_DIM, t), BF16)], axis=1)
    att_t = _dsa(heads(q).transpose(1, 2, 0), k, vt, iq.T, sm[:, :IDX_DIM].astype(BF16),
                 sm[:, IDX_DIM:IDX_DIM + N_IDX_HEADS].T, n_sel)

    rwt = jnp.zeros((D_MODEL, LANES), F32).at[:, :N_EXPERTS].set(router_w[0])
    rb = jnp.full((1, LANES), NEG_BIG, F32).at[0, :N_EXPERTS].set(router_b[0])
    h1, xn2, gates = _out_proj(x2, att_t, rw, w_out[0].astype(BF16), norm2_w, rwt, rb, tm)

    rank_t, gate_t, cnt = _route(gates, MOE_TM)
    counts = cnt[:, 0, :N_EXPERTS].astype(jnp.int32).reshape(-1)
    out = _moe(counts, h1, xn2, rank_t, gate_t, w_gu_bf, exp_b_gu[0][:, None, :], w_dn_bf,
               exp_b_down[0][:, None, :], MOE_TM)
    return out[None]
```

```python
import functools

import jax
import jax.numpy as jnp
import numpy as np
from jax import lax
from jax.experimental import pallas as pl
from jax.experimental.pallas import tpu as pltpu

F32 = jnp.float32
BF16 = jnp.bfloat16
HIGHEST = lax.Precision.HIGHEST

D_MODEL = 1024
HEAD_DIM = 64
N_HEADS = 8
D_GROUP = N_HEADS * HEAD_DIM
ROPE_THETA = 500000.0
N_IDX_HEADS = 8
IDX_DIM = 32
D_IDX = N_IDX_HEADS * IDX_DIM
TOPK_MAX = 256
D_DECAY_LORA = 32
D_AAA_LORA = 32
D_GATE_LORA = 64
RWKV_GN_EPS = 64e-5
N_EXPERTS = 32
TOP_K_EXPERTS = 4
D_FF = 1024
SWIGLU_LIMIT = 7.0
SWIGLU_ALPHA = 1.702
RMS_EPS = 1e-6

LANES = 128
SUBLANES = 8
VMEM_LIMIT = 56 * 1024 * 1024

NEG_BIG = -1e30
INT_MIN = -(2 ** 31)


def _dot(a, b):
    return jnp.dot(a.astype(BF16), b.astype(BF16), preferred_element_type=F32)


def _dotf(a, b):
    return jnp.dot(a, b, preferred_element_type=F32, precision=HIGHEST)


def _dot3(a, b):
    a_hi = a.astype(BF16)
    b_hi = b.astype(BF16)
    a_lo = (a - a_hi.astype(F32)).astype(BF16)
    b_lo = (b - b_hi.astype(F32)).astype(BF16)
    dot = functools.partial(jnp.dot, preferred_element_type=F32)
    return dot(a_hi, b_hi) + dot(a_hi, b_lo) + dot(a_lo, b_hi)


def _dot_nt(a, b):
    return lax.dot_general(a.astype(BF16), b.astype(BF16), (((1,), (1,)), ((), ())),
                           preferred_element_type=F32)


def _dot_tn(a, b):
    return lax.dot_general(a.astype(BF16), b.astype(BF16), (((0,), (0,)), ((), ())),
                           preferred_element_type=F32)


def _group_sum(z, g_ref):
    hi = z.astype(BF16)
    lo = (z - hi.astype(F32)).astype(BF16)
    g = g_ref[...]
    return (jnp.dot(hi, g, preferred_element_type=F32)
            + jnp.dot(lo, g, preferred_element_type=F32))


_C_Q, _C_K, _C_V, _C_RR, _C_RK, _C_RV = (i * D_GROUP for i in range(6))
_C_IQ = 6 * D_GROUP
_C_SM = _C_IQ + D_IDX
_C_LORA = _C_SM + LANES
D_IN_PACKED = _C_LORA + LANES
D_SHIFT = 3 * D_GROUP + LANES


def _in_proj_kernel(x_ref, pos_ref, n1w_ref, w_ref, mu_ref, qnw_ref, knw_ref, g_ref, rope_ref,
                    q_ref, k_ref, v_ref, rr_ref, rk_ref, rv_ref, iq_ref, sm_ref, lora_ref,
                    carry_ref):
    tm = x_ref.shape[0]

    @pl.when(pl.program_id(0) == 0)
    def _():
        carry_ref[...] = jnp.zeros_like(carry_ref)

    x = x_ref[...]
    xn = x * lax.rsqrt(jnp.mean(x * x, axis=-1, keepdims=True) + RMS_EPS) * n1w_ref[...]
    xb = xn.astype(BF16)
    pos = pos_ref[...]
    rope = rope_ref[...]

    def tables(frow, srow):
        ang = pos * rope[frow:frow + 1, :]
        return jnp.cos(ang), jnp.sin(ang) * rope[srow:srow + 1, :]

    def widen(z, reps):
        return jnp.concatenate([z] * reps, axis=1)

    def rotary(z, c, s, first_row, half):
        w = z.shape[1]
        first = jnp.concatenate([rope[first_row:first_row + 1, :]] * (w // LANES), axis=1) > 0.5
        partner = jnp.where(first, pltpu.roll(z, w - half, 1), pltpu.roll(z, half, 1))
        return z * c + partner * s

    def head_norm(z, w_row):
        ms = _group_sum(z * z, g_ref) * (1.0 / HEAD_DIM)
        return z * lax.rsqrt(ms + RMS_EPS) * w_row

    def proj(c0, width):
        return jnp.dot(xb, w_ref[:, c0:c0 + width], preferred_element_type=F32)

    def shift(z, c0):
        width = z.shape[1]
        row = lax.broadcasted_iota(jnp.int32, z.shape, 0)
        prev = jnp.where(row == 0, carry_ref[0:1, c0:c0 + width], pltpu.roll(z, 1, 0))
        carry_ref[0:1, c0:c0 + width] = z[tm - 1:tm, :]
        return z + (prev - z) * mu_ref[:, c0:c0 + width]

    cq, sq = (widen(z, D_GROUP // LANES) for z in tables(0, 1))
    q = rotary(head_norm(proj(_C_Q, D_GROUP), qnw_ref[...]), cq, sq, 2, HEAD_DIM // 8)
    q_ref[...] = q.astype(BF16)
    k = rotary(head_norm(proj(_C_K, D_GROUP), knw_ref[...]), cq, sq, 2, HEAD_DIM // 8)
    k_ref[...] = k.astype(BF16)
    v_ref[...] = proj(_C_V, D_GROUP).astype(BF16)

    rr_ref[...] = shift(proj(_C_RR, D_GROUP), 0)
    rk_ref[...] = shift(proj(_C_RK, D_GROUP), D_GROUP)
    rv_ref[...] = shift(proj(_C_RV, D_GROUP), 2 * D_GROUP)
    lora_ref[...] = shift(proj(_C_LORA, LANES), 3 * D_GROUP)

    ci, si = tables(3, 4)
    iq_ref[...] = rotary(proj(_C_IQ, D_IDX), widen(ci, D_IDX // LANES), widen(si, D_IDX // LANES),
                         5, IDX_DIM // 8)
    only_key = rope[6:7, :]
    sm_ref[...] = rotary(proj(_C_SM, LANES), ci * only_key + (1.0 - only_key), si * only_key,
                         5, IDX_DIM // 8)


def _in_proj(x2, pos_f, n1w, w_packed, mu_packed, qnw, knw, gmat, rope, tm):
    t = x2.shape[0]
    full = lambda shape: pl.BlockSpec(shape, lambda i: (0,) * len(shape))
    row = lambda width: pl.BlockSpec((tm, width), lambda i: (i, 0))
    out_shapes = (
        jax.ShapeDtypeStruct((t, D_GROUP), BF16),
        jax.ShapeDtypeStruct((t, D_GROUP), BF16),
        jax.ShapeDtypeStruct((t, D_GROUP), BF16),
        jax.ShapeDtypeStruct((t, D_GROUP), F32),
        jax.ShapeDtypeStruct((t, D_GROUP), F32),
        jax.ShapeDtypeStruct((t, D_GROUP), F32),
        jax.ShapeDtypeStruct((t, D_IDX), F32),
        jax.ShapeDtypeStruct((t, LANES), F32),
        jax.ShapeDtypeStruct((t, LANES), F32),
    )
    return pl.pallas_call(
        _in_proj_kernel,
        grid=(t // tm,),
        in_specs=[row(D_MODEL), row(1), full((1, D_MODEL)), full((D_MODEL, D_IN_PACKED)),
                  full((1, D_SHIFT)), full((1, D_GROUP)), full((1, D_GROUP)),
                  full((D_GROUP, D_GROUP)), full((SUBLANES, LANES))],
        out_specs=[row(D_GROUP)] * 6 + [row(D_IDX), row(LANES), row(LANES)],
        out_shape=out_shapes,
        scratch_shapes=[pltpu.VMEM((SUBLANES, D_SHIFT), F32)],
        compiler_params=pltpu.CompilerParams(dimension_semantics=("arbitrary",),
                                             vmem_limit_bytes=VMEM_LIMIT),
        name="in_proj",
    )(x2, pos_f, n1w, w_packed, mu_packed, qnw, knw, gmat, rope)


RWKV_CHUNK = 64
RWKV_UNROLL = 4
PAIR = 2 * HEAD_DIM


def _rwkv_kernel(r_ref, k_ref, v_ref, lora_ref, w0_ref, w2_ref, a0_ref, a2_ref, g2_ref,
                 kk_ref, ka_ref, rk_ref, lnw_ref, lnb_ref, g_ref, *rest, n_cast):
    cast_in, o_ref, cast_out = rest[:n_cast], rest[n_cast], rest[n_cast + 1:2 * n_cast + 1]
    s_ref, ld_s, r_s, k2_s, b_s, kk_s, y_s = rest[2 * n_cast + 1:]
    for src, dst in zip(cast_in, cast_out):
        dst[...] = src[...].astype(BF16)
    tm = r_ref.shape[0]
    n_chunks = tm // RWKV_CHUNK
    n_pairs = D_GROUP // PAIR
    c = RWKV_CHUNK

    @pl.when(pl.program_id(0) == 0)
    def _():
        s_ref[...] = jnp.zeros_like(s_ref)

    lora = lora_ref[...]
    r = r_ref[...]
    k = k_ref[...]
    v = v_ref[...]
    zarg = w0_ref[...] + _dot3(jnp.tanh(lora), w2_ref[...])
    sp = jnp.maximum(-zarg, 0.0) + jnp.log1p(jnp.exp(-jnp.abs(zarg)))
    ld_s[...] = -jnp.exp(-sp - 0.5)
    a = jax.nn.sigmoid(a0_ref[...] + _dot3(lora, a2_ref[...]))
    g = _dot3(jax.nn.sigmoid(lora), g2_ref[...])
    kk = k * kk_ref[...]
    kk = kk * lax.rsqrt(jnp.maximum(_group_sum(kk * kk, g_ref), 1e-24))
    k2 = k * (1.0 + (a - 1.0) * ka_ref[...])
    bonus = _group_sum(r * k2 * rk_ref[...], g_ref) * v
    r_s[...] = r
    k2_s[...] = k2
    kk_s[...] = kk
    b_s[...] = kk * a

    row = lax.broadcasted_iota(jnp.int32, (2 * c, 2 * c), 0)
    col = lax.broadcasted_iota(jnp.int32, (2 * c, 2 * c), 1)
    same_head = (row >= c) == (col >= c)
    strict = same_head & (col < row)
    incl = same_head & (col <= row)
    eye = (row == col).astype(F32)
    tri = (lax.broadcasted_iota(jnp.int32, (c, c), 1)
           <= lax.broadcasted_iota(jnp.int32, (c, c), 0)).astype(F32)
    lane = lax.broadcasted_iota(jnp.int32, (c, PAIR), 1)
    head0 = lane < HEAD_DIM

    def stack(z):
        return jnp.concatenate([jnp.where(head0, z, 0.0), jnp.where(head0, 0.0, z)], axis=0)

    def chunk_body(ci, carry):
        c2 = 2 * c
        units = [(cc, p) for cc in range(RWKV_UNROLL) for p in range(n_pairs)]
        uid = range(len(units))
        rows = [pl.multiple_of((ci * RWKV_UNROLL + cc) * c, c) for cc in range(RWKV_UNROLL)]
        ld_all = [ld_s[pl.ds(r0, c), :] for r0 in rows]
        cum_all = [_dotf(tri, ld) for ld in ld_all]
        rt, kt, bt, kp, vs, lhs, rhs, gam_end = [], [], [], [], [], [], [], []
        for cc, p in units:
            r0 = rows[cc]
            cols = slice(p * PAIR, (p + 1) * PAIR)
            ld = ld_all[cc][:, cols]
            cum = cum_all[cc][:, cols]
            gam = jnp.exp(cum)
            inv = jnp.exp(-cum)
            gam_prev = jnp.exp(cum - ld)
            gam_end.append(gam[c - 1:c, :])
            rt.append(stack(r_s[pl.ds(r0, c), cols] * gam))
            kt_raw = k2_s[pl.ds(r0, c), cols] * inv
            bt_raw = b_s[pl.ds(r0, c), cols] * inv
            kt.append(stack(kt_raw))
            bt.append(stack(bt_raw))
            kp.append(stack(kk_s[pl.ds(r0, c), cols] * gam_prev))
            vs.append(stack(v_ref[pl.ds(r0, c), cols]))
            lhs.append(jnp.concatenate([kp[-1], rt[-1]], axis=0))
            rhs.append(jnp.concatenate([bt_raw, bt_raw, kt_raw, kt_raw], axis=0))
        aa = [_dot_nt(lhs[u], rhs[u]) for u in uid]
        a_kk = [jnp.where(strict, aa[u][:c2, c2:], 0.0) for u in uid]
        a_rb = [jnp.where(incl, aa[u][c2:, :c2], 0.0) for u in uid]
        a_rk = [jnp.where(incl, aa[u][c2:, c2:], 0.0) for u in uid]
        n = [-jnp.where(strict, aa[u][:c2, :c2], 0.0) for u in uid]
        prod = [eye + n[u] for u in uid]
        n = [_dot(n[u], n[u]) for u in uid]
        for _ in range(int(np.log2(c)) - 2):
            both = [_dot(jnp.concatenate([n[u], prod[u]], axis=0), n[u]) for u in uid]
            prod = [prod[u] + both[u][c2:] for u in uid]
            n = [both[u][:c2] for u in uid]
        av = [_dot(jnp.concatenate([a_kk[u], a_rk[u]], axis=0), vs[u]) for u in uid]
        tinv = [prod[u] + _dot(prod[u], n[u]) for u in uid]
        wu = [_dot(tinv[u], jnp.concatenate([kp[u], av[u][:c2]], axis=1)) for u in uid]
        rb = [_dot(a_rb[u], wu[u]) for u in uid]
        tn = [_dot_tn(jnp.concatenate([wu[u], vs[u]], axis=1),
                      jnp.concatenate([bt[u], kt[u]], axis=1)) for u in uid]
        r2 = [rt[u] - rb[u][:, :PAIR] for u in uid]
        y2 = [av[u][c2:] - rb[u][:, PAIR:] for u in uid]
        s_mix = [(eye - tn[u][:PAIR, :PAIR]) * gam_end[u] for u in uid]
        s_add = [(tn[u][2 * PAIR:, PAIR:] - tn[u][PAIR:2 * PAIR, :PAIR]) * gam_end[u] for u in uid]
        state = [s_ref[p] for p in range(n_pairs)]
        for cc in range(RWKV_UNROLL):
            us = [cc * n_pairs + p for p in range(n_pairs)]
            ys = [_dot_nt(r2[u], state[p]) + y2[u] for p, u in enumerate(us)]
            state = [_dot(state[p], s_mix[u]) + s_add[u] for p, u in enumerate(us)]
            for p in range(n_pairs):
                y_s[pl.ds(rows[cc], c), p * PAIR:(p + 1) * PAIR] = ys[p][:c] + ys[p][c:]
        for p in range(n_pairs):
            s_ref[p] = state[p]
        return carry

    lax.fori_loop(0, n_chunks // RWKV_UNROLL, chunk_body, 0)

    y = y_s[...]
    mean = _group_sum(y, g_ref) * (1.0 / HEAD_DIM)
    yc = y - mean
    var = _group_sum(yc * yc, g_ref) * (1.0 / HEAD_DIM)
    yn = yc * lax.rsqrt(var + RWKV_GN_EPS) * lnw_ref[...] + lnb_ref[...]
    o_ref[...] = (yn + bonus) * g


def _rwkv(rr, rk, rv, lora, w0, w2p, a0, a2p, g2p, k_k, k_a, r_k, ln_w, ln_b, gsum, tm,
          to_cast=()):
    t = rr.shape[0]
    full = lambda shape: pl.BlockSpec(shape, lambda i: (0,) * len(shape))
    row = lambda width: pl.BlockSpec((tm, width), lambda i: (i, 0))
    vec = full((1, D_GROUP))
    big = pltpu.VMEM((tm, D_GROUP), F32)
    slabs = [w.reshape(t // tm, -1, w.shape[-1]) for w in to_cast]
    cast_specs = [pl.BlockSpec((1,) + w.shape[1:], lambda i: (i, 0, 0)) for w in slabs]
    outs = pl.pallas_call(
        functools.partial(_rwkv_kernel, n_cast=len(slabs)),
        grid=(t // tm,),
        in_specs=[row(D_GROUP), row(D_GROUP), row(D_GROUP), row(LANES),
                  vec, full((LANES, D_GROUP)), vec, full((LANES, D_GROUP)),
                  full((LANES, D_GROUP)), vec, vec, vec, vec, vec, full((D_GROUP, D_GROUP))]
        + cast_specs,
        out_specs=[row(D_GROUP)] + cast_specs,
        out_shape=[jax.ShapeDtypeStruct((t, D_GROUP), F32)]
        + [jax.ShapeDtypeStruct(w.shape, BF16) for w in slabs],
        scratch_shapes=[pltpu.VMEM((D_GROUP // PAIR, PAIR, PAIR), F32),
                        big, big, big, big, big, big],
        compiler_params=pltpu.CompilerParams(dimension_semantics=("arbitrary",),
                                             vmem_limit_bytes=VMEM_LIMIT),
        name="rwkv",
    )(rr, rk, rv, lora, w0, w2p, a0, a2p, g2p, k_k, k_a, r_k, ln_w, ln_b, gsum, *slabs)
    return outs[0], [o.reshape(w.shape) for o, w in zip(outs[1:], to_cast)]


DSA_QB = 256
DSA_KB = 512
M_INIT = -5e29
DSA_HEAD_GROUP = 4
DSA_ACC_ROWS = 32
DSA_GROUPS = 256
DSA_TOP = 10
DSA_V_ROWS = HEAD_DIM + 16


def _dsa_kernel(qi_ref, kj_ref, qt_ref, k_ref, vt_ref, iqt_ref, ik_ref, iwt_ref, o_ref,
                key_s, thr_s, need_s, over_s, m_s, acc_s, bdq_s, top_s, tops_s, *, n_sel):
    qb = qt_ref.shape[2]
    kb = k_ref.shape[0]
    step = pl.program_id(0)
    qi = qi_ref[step]
    kj = kj_ref[step]
    q0 = qi * qb
    j_last = (q0 + qb - 1) // kb
    n_kc = j_last + 1
    idx_scale = float((IDX_DIM * N_IDX_HEADS) ** -0.5)

    def causal(jblk):
        s_pos = jblk * kb + lax.broadcasted_iota(jnp.int32, (kb, qb), 0)
        t_pos = q0 + lax.broadcasted_iota(jnp.int32, (kb, qb), 1)
        return s_pos <= t_pos

    @pl.when(kj == 0)
    def _():
        iqt = iqt_ref[...].astype(BF16)
        iwt = iwt_ref[...] * idx_scale

        def score_chunk(kc, diagonal):
            ik = ik_ref[pl.ds(pl.multiple_of(kc * kb, kb), kb), :]
            score = jnp.zeros((kb, qb), F32)
            for h in range(N_IDX_HEADS):
                d = jnp.dot(ik, iqt[h * IDX_DIM:(h + 1) * IDX_DIM, :],
                            preferred_element_type=F32)
                score = score + jnp.maximum(d, 0.0) * iwt[h:h + 1, :]
            keys = to_key(score)
            if diagonal:
                ok = causal(kc)
                score = jnp.where(ok, score, -jnp.inf)
                keys = jnp.where(ok, keys, jnp.int32(INT_MIN))
            key_s[kc] = keys
            xs = [score[r * DSA_GROUPS:(r + 1) * DSA_GROUPS] for r in range(kb // DSA_GROUPS)]
            for lvl in range(DSA_TOP):
                s = tops_s[lvl]
                for r in range(len(xs)):
                    s, xs[r] = jnp.maximum(s, xs[r]), jnp.minimum(s, xs[r])
                tops_s[lvl] = s

        def to_key(score):
            bits = pltpu.bitcast(score, jnp.int32)
            return jnp.where(bits < 0, bits ^ jnp.int32(0x7FFFFFFF), bits)

        tops_s[...] = jnp.full_like(tops_s, -jnp.inf)

        def full_chunk(kc, carry):
            score_chunk(kc, False)
            return carry

        lax.fori_loop(0, j_last, full_chunk, 0)
        score_chunk(j_last, True)
        for lvl in range(DSA_TOP):
            kept = tops_s[lvl]
            top_s[lvl] = jnp.where(kept == -jnp.inf, jnp.int32(INT_MIN), to_key(kept))

        def select(count_keys):
            def bit_step(b, cur):
                bit = lax.shift_left(jnp.int32(1), jnp.int32(31) - b)
                cand = (cur | bit) ^ jnp.int32(INT_MIN)
                cnt = jnp.sum(count_keys(cand), axis=0, keepdims=True)
                return jnp.where(cnt >= float(n_sel), cur | bit, cur)

            cur = lax.fori_loop(0, 32, bit_step, jnp.zeros((1, qb), jnp.int32))
            return jnp.maximum(cur ^ jnp.int32(INT_MIN), jnp.int32(INT_MIN + 1))

        def part_count(keys, cand):
            ind = jnp.where(keys >= cand, 1.0, 0.0)
            return jnp.sum(ind.reshape(-1, DSA_ACC_ROWS, qb), axis=0)

        def count_top(cand):
            return lax.fori_loop(0, DSA_TOP, lambda lvl, a: a + part_count(top_s[lvl], cand),
                                 jnp.zeros((DSA_ACC_ROWS, qb), F32))

        def count_all(cand):
            return lax.fori_loop(0, n_kc, lambda kc, a: a + part_count(key_s[kc], cand),
                                 jnp.zeros((DSA_ACC_ROWS, qb), F32))

        def settle(count_keys):
            thr = select(count_keys)
            thr_s[...] = thr
            above = jnp.sum(count_keys(thr + 1), axis=0, keepdims=True)
            upto = jnp.sum(count_keys(thr), axis=0, keepdims=True)
            need_s[...] = float(n_sel) - above
            over_s[...] = upto - float(n_sel)
            return thr

        thr = settle(count_top)
        hidden = jnp.where(top_s[DSA_TOP - 1] >= thr, 1.0, 0.0)

        @pl.when(jnp.max(hidden) > 0.0)
        def _():
            settle(count_all)

        @pl.when(jnp.max(over_s[...]) > 0.0)
        def _():
            thr = thr_s[...]
            need = need_s[...]
            sub = LANES
            earlier = jnp.where(lax.broadcasted_iota(jnp.int32, (sub, sub), 1)
                                < lax.broadcasted_iota(jnp.int32, (sub, sub), 0), 1.0, 0.0).astype(BF16)

            def drop_surplus(kc, seen):
                keys = key_s[kc]
                tie = keys == thr
                tie_f = jnp.where(tie, 1.0, 0.0)
                ranks = []
                for r in range(kb // sub):
                    part = tie_f[r * sub:(r + 1) * sub]
                    ranks.append(seen + jnp.dot(earlier, part.astype(BF16),
                                                preferred_element_type=F32))
                    seen = seen + jnp.sum(part, axis=0, keepdims=True)
                rank = jnp.concatenate(ranks, axis=0)
                key_s[kc] = jnp.where(tie & (rank >= need), thr - 1, keys)
                return seen

            lax.fori_loop(0, n_kc, drop_surplus, jnp.zeros((1, qb), F32))

        m_s[...] = jnp.full_like(m_s, M_INIT)
        acc_s[...] = jnp.zeros_like(acc_s)
        bdq_s[...] = jnp.zeros_like(bdq_s)
        for h in range(N_HEADS):
            g, hh = divmod(h, DSA_HEAD_GROUP)
            bdq_s[g, hh * HEAD_DIM:(hh + 1) * HEAD_DIM, hh * qb:(hh + 1) * qb] = (
                qt_ref[h] * jnp.asarray(HEAD_DIM ** -0.5, BF16))

    mask = key_s[kj] >= thr_s[...]
    hg = DSA_HEAD_GROUP
    st_all = [jnp.dot(k_ref[:, g * hg * HEAD_DIM:(g + 1) * hg * HEAD_DIM], bdq_s[g],
                      preferred_element_type=F32) for g in range(N_HEADS // hg)]
    for g in range(N_HEADS // hg):
        st_g = st_all[g]
        for hh in range(hg):
            h = g * hg + hh
            st = jnp.where(mask, st_g[:, hh * qb:(hh + 1) * qb], NEG_BIG)
            m_old = m_s[h]
            part = jnp.max(st.reshape(kb // DSA_ACC_ROWS, DSA_ACC_ROWS, qb), axis=0)
            m_new = jnp.maximum(m_old, jnp.max(part, axis=0, keepdims=True))
            p = jnp.exp((st - m_new).astype(BF16))
            acc_s[h] = jnp.exp(m_old - m_new) * acc_s[h] + jnp.dot(
                vt_ref[h], p, preferred_element_type=F32)
            m_s[h] = m_new

    @pl.when(kj == j_last)
    def _():
        for h in range(N_HEADS):
            acc = acc_s[h]
            o_ref[h] = acc[:HEAD_DIM] / acc[HEAD_DIM:HEAD_DIM + 1]


def _dsa(qt, k, vt, iqt, ik, iwt, n_sel):
    t = ik.shape[0]
    qb, kb = DSA_QB, DSA_KB
    assert n_sel <= DSA_GROUPS and kb % DSA_GROUPS == 0
    nq, nk = t // qb, t // kb
    pairs = [(i, j) for i in range(nq) for j in range((i * qb + qb - 1) // kb + 1)]
    qi = jnp.asarray(np.array([p[0] for p in pairs], np.int32))
    kj = jnp.asarray(np.array([p[1] for p in pairs], np.int32))
    grid_spec = pltpu.PrefetchScalarGridSpec(
        num_scalar_prefetch=2,
        grid=(len(pairs),),
        in_specs=[pl.BlockSpec((N_HEADS, HEAD_DIM, qb), lambda s, qi, kj: (0, 0, qi[s])),
                  pl.BlockSpec((kb, D_GROUP), lambda s, qi, kj: (kj[s], 0)),
                  pl.BlockSpec((N_HEADS, DSA_V_ROWS, kb), lambda s, qi, kj: (0, 0, kj[s])),
                  pl.BlockSpec((D_IDX, qb), lambda s, qi, kj: (0, qi[s])),
                  pl.BlockSpec((t, IDX_DIM), lambda s, qi, kj: (0, 0)),
                  pl.BlockSpec((N_IDX_HEADS, qb), lambda s, qi, kj: (0, qi[s]))],
        out_specs=pl.BlockSpec((N_HEADS, HEAD_DIM, qb), lambda s, qi, kj: (0, 0, qi[s])),
        scratch_shapes=[pltpu.VMEM((nk, kb, qb), jnp.int32),
                        pltpu.VMEM((1, qb), jnp.int32),
                        pltpu.VMEM((1, qb), F32),
                        pltpu.VMEM((1, qb), F32),
                        pltpu.VMEM((N_HEADS, 1, qb), F32),
                        pltpu.VMEM((N_HEADS, DSA_V_ROWS, qb), F32),
                        pltpu.VMEM((N_HEADS // DSA_HEAD_GROUP, DSA_HEAD_GROUP * HEAD_DIM,
                                    DSA_HEAD_GROUP * qb), BF16),
                        pltpu.VMEM((DSA_TOP, DSA_GROUPS, qb), jnp.int32),
                        pltpu.VMEM((DSA_TOP, DSA_GROUPS, qb), F32)])
    return pl.pallas_call(
        functools.partial(_dsa_kernel, n_sel=n_sel),
        grid_spec=grid_spec,
        out_shape=jax.ShapeDtypeStruct((N_HEADS, HEAD_DIM, t), F32),
        compiler_params=pltpu.CompilerParams(dimension_semantics=("arbitrary",),
                                             vmem_limit_bytes=VMEM_LIMIT),
        name="dsa",
    )(qi, kj, qt, k, vt, iqt, ik, iwt)


def _out_proj_kernel(x_ref, att_ref, rw_ref, wo_ref, n2w_ref, rwt_ref, rb_ref,
                     h_ref, xn_ref, gate_ref):
    tm = x_ref.shape[0]
    att = att_ref[...].reshape(D_GROUP, tm).T
    mix = jnp.concatenate([att, rw_ref[...]], axis=1)
    acc = x_ref[...] + _dot(mix, wo_ref[...])
    h_ref[...] = acc
    xn = acc * lax.rsqrt(jnp.mean(acc * acc, axis=-1, keepdims=True) + RMS_EPS) * n2w_ref[...]
    xn_ref[...] = xn.astype(BF16)
    logits = _dotf(xn, rwt_ref[...]) + rb_ref[...]
    lane = lax.broadcasted_iota(jnp.int32, logits.shape, 1)
    work = logits
    vals, hots = [], []
    for _ in range(TOP_K_EXPERTS):
        m = jnp.max(work, axis=1, keepdims=True)
        idx = jnp.min(jnp.where(work == m, lane, LANES), axis=1, keepdims=True)
        hot = lane == idx
        vals.append(m)
        hots.append(hot)
        work = jnp.where(hot, -jnp.inf, work)
    es = [jnp.exp(vv - vals[0]) for vv in vals]
    denom = es[0] + es[1] + es[2] + es[3]
    gates = jnp.zeros_like(logits)
    for e, hot in zip(es, hots):
        gates = gates + jnp.where(hot, e / denom, 0.0)
    gate_ref[...] = gates


def _out_proj(x2, att_t, rw, wo, n2w, rwt, rb, tm):
    t = x2.shape[0]
    full = lambda shape: pl.BlockSpec(shape, lambda i: (0,) * len(shape))
    row = lambda width: pl.BlockSpec((tm, width), lambda i: (i, 0))
    return pl.pallas_call(
        _out_proj_kernel,
        grid=(t // tm,),
        in_specs=[row(D_MODEL), pl.BlockSpec((N_HEADS, HEAD_DIM, tm), lambda i: (0, 0, i)),
                  row(D_GROUP), full((2 * D_GROUP, D_MODEL)), full((1, D_MODEL)),
                  full((D_MODEL, LANES)), full((1, LANES))],
        out_specs=[row(D_MODEL), row(D_MODEL), row(LANES)],
        out_shape=(jax.ShapeDtypeStruct((t, D_MODEL), F32),
                   jax.ShapeDtypeStruct((t, D_MODEL), BF16),
                   jax.ShapeDtypeStruct((t, LANES), F32)),
        compiler_params=pltpu.CompilerParams(dimension_semantics=("arbitrary",),
                                             vmem_limit_bytes=VMEM_LIMIT),
        name="out_proj",
    )(x2, att_t, rw, wo, n2w, rwt, rb)


MOE_TM = 1024
MOE_RB = 144
MOE_FC = 512
MOE_EXPERTS_PER_STEP = 2


def _route_kernel(gate_ref, rankt_ref, gatet_ref, cnt_ref):
    tm = gate_ref.shape[0]
    gates = gate_ref[...]
    hot = gates > 0.0
    ind = jnp.where(hot, 1.0, 0.0)
    before = (lax.broadcasted_iota(jnp.int32, (tm, tm), 1)
              < lax.broadcasted_iota(jnp.int32, (tm, tm), 0))
    rank = jnp.dot(jnp.where(before, 1.0, 0.0).astype(BF16), ind.astype(BF16),
                   preferred_element_type=F32)
    rankt_ref[...] = jnp.where(hot, rank, -1.0).T
    gatet_ref[...] = gates.T
    cnt = jnp.sum(ind, axis=0, keepdims=True)
    cnt_ref[...] = jnp.broadcast_to(cnt[None], cnt_ref.shape)


def _route(gates, tm):
    t = gates.shape[0]
    return pl.pallas_call(
        _route_kernel,
        grid=(t // tm,),
        in_specs=[pl.BlockSpec((tm, LANES), lambda i: (i, 0))],
        out_specs=[pl.BlockSpec((LANES, tm), lambda i: (0, i)),
                   pl.BlockSpec((LANES, tm), lambda i: (0, i)),
                   pl.BlockSpec((1, SUBLANES, LANES), lambda i: (i, 0, 0))],
        out_shape=(jax.ShapeDtypeStruct((LANES, t), F32),
                   jax.ShapeDtypeStruct((LANES, t), F32),
                   jax.ShapeDtypeStruct((t // tm, SUBLANES, LANES), F32)),
        compiler_params=pltpu.CompilerParams(dimension_semantics=("arbitrary",),
                                             vmem_limit_bytes=VMEM_LIMIT),
        name="route",
    )(gates)


def _moe_kernel(cnt_ref, h_ref, xn_ref, rankt_ref, gatet_ref, wgu_ref, bgu_ref,
                wdn_ref, bdn_ref, o_ref):
    i = pl.program_id(0)
    tm = h_ref.shape[0]

    @pl.when(pl.program_id(1) == 0)
    def _():
        o_ref[...] = h_ref[...]

    for sub in range(MOE_EXPERTS_PER_STEP):
        _moe_expert(pl.program_id(1) * MOE_EXPERTS_PER_STEP + sub, sub, i, tm, cnt_ref, xn_ref,
                    rankt_ref, gatet_ref, wgu_ref, bgu_ref, wdn_ref, bdn_ref, o_ref)


def _moe_expert(e, slot, i, tm, cnt_ref, xn_ref, rankt_ref, gatet_ref, wgu_ref, bgu_ref,
                wdn_ref, bdn_ref, o_ref):
    n_rows = cnt_ref[i * N_EXPERTS + e]
    n_blocks = (n_rows + MOE_RB - 1) // MOE_RB
    rank_row = rankt_ref[pl.ds(e, 1), :]
    gate_row = gatet_ref[pl.ds(e, 1), :]

    def row_block(bi, carry):
        r0 = (bi * MOE_RB).astype(F32)
        rowid = lax.broadcasted_iota(jnp.int32, (MOE_RB, tm), 0).astype(F32) + r0
        hit = rank_row == rowid
        sel = jnp.where(hit, 1.0, 0.0).astype(BF16)
        gate = jnp.sum(jnp.where(hit, gate_row, 0.0), axis=1, keepdims=True)
        xb = jnp.dot(sel, xn_ref[...], preferred_element_type=F32).astype(BF16)
        y = jnp.zeros((MOE_RB, D_MODEL), F32)
        ups = []
        for fc in range(D_FF // MOE_FC):
            c0 = fc * MOE_FC
            ups.append((jnp.dot(xb, wgu_ref[slot, :, c0:c0 + MOE_FC], preferred_element_type=F32),
                        jnp.dot(xb, wgu_ref[slot, :, D_FF + c0:D_FF + c0 + MOE_FC],
                                preferred_element_type=F32)))
        for fc in range(D_FF // MOE_FC):
            c0 = fc * MOE_FC
            hg = ups[fc][0] + bgu_ref[slot, :, c0:c0 + MOE_FC]
            hl = ups[fc][1] + bgu_ref[slot, :, D_FF + c0:D_FF + c0 + MOE_FC]
            glu = jnp.minimum(hg, SWIGLU_LIMIT)
            lin = jnp.clip(hl, -SWIGLU_LIMIT, SWIGLU_LIMIT)
            act = (lin + 1.0) * glu * jax.nn.sigmoid(SWIGLU_ALPHA * glu)
            y = y + jnp.dot(act.astype(BF16), wdn_ref[slot, c0:c0 + MOE_FC, :],
                            preferred_element_type=F32)
        y = ((y + bdn_ref[slot]) * gate).astype(BF16)
        o_ref[...] += _dot_tn(sel, y)
        return carry

    lax.fori_loop(0, n_blocks, row_block, 0)


def _moe(counts, h1, xn2, rank_t, gate_t, wgu, bgu, wdn, bdn, tm):
    t = h1.shape[0]
    eps = MOE_EXPERTS_PER_STEP
    row = lambda width: pl.BlockSpec((tm, width), lambda i, e, c: (i, 0))
    grid_spec = pltpu.PrefetchScalarGridSpec(
        num_scalar_prefetch=1,
        grid=(t // tm, N_EXPERTS // eps),
        in_specs=[row(D_MODEL), row(D_MODEL),
                  pl.BlockSpec((LANES, tm), lambda i, e, c: (0, i)),
                  pl.BlockSpec((LANES, tm), lambda i, e, c: (0, i)),
                  pl.BlockSpec((eps, D_MODEL, 2 * D_FF), lambda i, e, c: (e, 0, 0)),
                  pl.BlockSpec((eps, 1, 2 * D_FF), lambda i, e, c: (e, 0, 0)),
                  pl.BlockSpec((eps, D_FF, D_MODEL), lambda i, e, c: (e, 0, 0)),
                  pl.BlockSpec((eps, 1, D_MODEL), lambda i, e, c: (e, 0, 0))],
        out_specs=row(D_MODEL))
    return pl.pallas_call(
        _moe_kernel,
        grid_spec=grid_spec,
        out_shape=jax.ShapeDtypeStruct((t, D_MODEL), F32),
        compiler_params=pltpu.CompilerParams(dimension_semantics=("arbitrary", "arbitrary"),
                                             vmem_limit_bytes=VMEM_LIMIT),
        name="moe",
    )(counts, h1, xn2, rank_t, gate_t, wgu, bgu, wdn, bdn)


def _rope_tables():
    lane = jnp.arange(LANES)

    def rows(period, rot):
        half = rot // 2
        inv_freq = ROPE_THETA ** (-jnp.arange(half, dtype=F32) / half)
        jm = lane % period
        freq = jnp.where(jm < rot, inv_freq[jm % half], 0.0)
        sign = jnp.where(jm < half, -1.0, jnp.where(jm < rot, 1.0, 0.0))
        first = (jm < half).astype(F32)
        return [freq, sign, first]

    idx_rows = rows(IDX_DIM, IDX_DIM // 4)
    only_key = (lane < IDX_DIM).astype(F32)
    return jnp.stack(rows(HEAD_DIM, HEAD_DIM // 4) + idx_rows
                     + [only_key, jnp.zeros_like(only_key)]).astype(F32)


def _group_matrix():
    g = np.arange(D_GROUP) // HEAD_DIM
    return jnp.asarray((g[:, None] == g[None, :]).astype(np.float32), dtype=BF16)


def _pad_rows(w, r0, rows):
    return jnp.zeros((rows, w.shape[1]), w.dtype).at[r0:r0 + w.shape[0]].set(w)


def kernel(x, positions, norm1_w, w_in, q_norm_w, k_norm_w, rwkv_mu, rwkv_w0, rwkv_w2, rwkv_a0,
           rwkv_a2, rwkv_g2, rwkv_k_k, rwkv_k_a, rwkv_r_k, rwkv_ln_w, rwkv_ln_b, w_out, norm2_w,
           router_w, router_b, exp_w_gu, exp_b_gu, exp_w_down, exp_b_down):
    b, t, _ = x.shape
    assert b == 1 and w_in.shape[0] == 1, "single sequence, single layer"
    assert t % DSA_KB == 0 and t % DSA_QB == 0 and t % MOE_TM == 0
    x2 = x[0]
    pos_f = positions[0].astype(F32)[:, None]
    n_sel = min(TOPK_MAX, t // 4)

    w = w_in[0]
    a0 = 3 * D_GROUP
    att_cols = a0 + D_IDX + IDX_DIM + N_IDX_HEADS
    w_att, w_rw = w[:, :att_cols], w[:, att_cols:]
    w_sm = jnp.zeros((D_MODEL, LANES), F32).at[:, :IDX_DIM + N_IDX_HEADS].set(w_att[:, a0 + D_IDX:])
    w_packed = jnp.concatenate(
        [w_att[:, :a0], w_rw[:, :a0], w_att[:, a0:a0 + D_IDX], w_sm, w_rw[:, a0:]],
        axis=1).astype(BF16)
    mu = rwkv_mu[0][None, :]
    tile8 = lambda z: jnp.tile(z, N_HEADS)[None, :]
    gsum = _group_matrix()

    tm = 256
    q, k, v, rr, rk, rv, iq, sm, lora = _in_proj(
        x2, pos_f, norm1_w, w_packed, mu, tile8(q_norm_w[0]), tile8(k_norm_w[0]), gsum,
        _rope_tables(), tm)

    vec = lambda z: z.reshape(1, D_GROUP)
    rw, (w_gu_bf, w_dn_bf) = _rwkv(
        rr, rk, rv, lora, vec(rwkv_w0[0]), _pad_rows(rwkv_w2[0], 0, LANES), vec(rwkv_a0[0]),
        _pad_rows(rwkv_a2[0], D_DECAY_LORA, LANES),
        _pad_rows(rwkv_g2[0], D_DECAY_LORA + D_AAA_LORA, LANES),
        vec(rwkv_k_k[0]), vec(rwkv_k_a[0]), vec(rwkv_r_k[0]), vec(rwkv_ln_w[0]),
        vec(rwkv_ln_b[0]), gsum, tm, to_cast=(exp_w_gu[0], exp_w_down[0]))

    heads = lambda z: z.reshape(t, N_HEADS, HEAD_DIM)
    vt = jnp.concatenate([heads(v).transpose(1, 2, 0),
                          jnp.ones((N_HEADS, DSA_V_ROWS - HEAD_DIM, t), BF16)], axis=1)
    att_t = _dsa(heads(q).transpose(1, 2, 0), k, vt, iq.T, sm[:, :IDX_DIM].astype(BF16),
                 sm[:, IDX_DIM:IDX_DIM + N_IDX_HEADS].T, n_sel)

    rwt = jnp.zeros((D_MODEL, LANES), F32).at[:, :N_EXPERTS].set(router_w[0])
    rb = jnp.full((1, LANES), NEG_BIG, F32).at[0, :N_EXPERTS].set(router_b[0])
    h1, xn2, gates = _out_proj(x2, att_t, rw, w_out[0].astype(BF16), norm2_w, rwt, rb, tm)

    rank_t, gate_t, cnt = _route(gates, MOE_TM)
    counts = cnt[:, 0, :N_EXPERTS].astype(jnp.int32).reshape(-1)
    out = _moe(counts, h1, xn2, rank_t, gate_t, w_gu_bf, exp_b_gu[0][:, None, :], w_dn_bf,
               exp_b_down[0][:, None, :], MOE_TM)
    return out[None]
```

```python
import functools

import jax
import jax.numpy as jnp
import numpy as np
from jax import lax
from jax.experimental import pallas as pl
from jax.experimental.pallas import tpu as pltpu

F32 = jnp.float32
BF16 = jnp.bfloat16
HIGHEST = lax.Precision.HIGHEST

D_MODEL = 1024
HEAD_DIM = 64
N_HEADS = 8
D_GROUP = N_HEADS * HEAD_DIM
ROPE_THETA = 500000.0
N_IDX_HEADS = 8
IDX_DIM = 32
D_IDX = N_IDX_HEADS * IDX_DIM
TOPK_MAX = 256
D_DECAY_LORA = 32
D_AAA_LORA = 32
D_GATE_LORA = 64
RWKV_GN_EPS = 64e-5
N_EXPERTS = 32
TOP_K_EXPERTS = 4
D_FF = 1024
SWIGLU_LIMIT = 7.0
SWIGLU_ALPHA = 1.702
RMS_EPS = 1e-6

LANES = 128
SUBLANES = 8
VMEM_LIMIT = 56 * 1024 * 1024

NEG_BIG = -1e30
LOG2_E = 1.4426950408889634
INT_MIN = -(2 ** 31)


def _dot(a, b):
    return jnp.dot(a.astype(BF16), b.astype(BF16), preferred_element_type=F32)


def _dotf(a, b):
    return jnp.dot(a, b, preferred_element_type=F32, precision=HIGHEST)


def _dot3(a, b):
    a_hi = a.astype(BF16)
    b_hi = b.astype(BF16)
    a_lo = (a - a_hi.astype(F32)).astype(BF16)
    b_lo = (b - b_hi.astype(F32)).astype(BF16)
    dot = functools.partial(jnp.dot, preferred_element_type=F32)
    return dot(a_hi, b_hi) + dot(a_hi, b_lo) + dot(a_lo, b_hi)


def _dot_nt(a, b):
    return lax.dot_general(a.astype(BF16), b.astype(BF16), (((1,), (1,)), ((), ())),
                           preferred_element_type=F32)


def _dot_tn(a, b):
    return lax.dot_general(a.astype(BF16), b.astype(BF16), (((0,), (0,)), ((), ())),
                           preferred_element_type=F32)


def _group_sum(z, g_ref):
    hi = z.astype(BF16)
    lo = (z - hi.astype(F32)).astype(BF16)
    g = g_ref[...]
    return (jnp.dot(hi, g, preferred_element_type=F32)
            + jnp.dot(lo, g, preferred_element_type=F32))


_C_Q, _C_K, _C_V, _C_RR, _C_RK, _C_RV = (i * D_GROUP for i in range(6))
_C_IQ = 6 * D_GROUP
_C_SM = _C_IQ + D_IDX
_C_LORA = _C_SM + LANES
D_IN_PACKED = _C_LORA + LANES
D_SHIFT = 3 * D_GROUP + LANES


def _in_proj_kernel(x_ref, pos_ref, n1w_ref, w_ref, mu_ref, qnw_ref, knw_ref, g_ref, rope_ref,
                    q_ref, k_ref, v_ref, rr_ref, rk_ref, rv_ref, iq_ref, sm_ref, lora_ref,
                    carry_ref):
    tm = x_ref.shape[0]

    @pl.when(pl.program_id(0) == 0)
    def _():
        carry_ref[...] = jnp.zeros_like(carry_ref)

    x = x_ref[...]
    xn = x * lax.rsqrt(jnp.mean(x * x, axis=-1, keepdims=True) + RMS_EPS) * n1w_ref[...]
    xb = xn.astype(BF16)
    pos = pos_ref[...]
    rope = rope_ref[...]

    def tables(frow, srow):
        ang = pos * rope[frow:frow + 1, :]
        return jnp.cos(ang), jnp.sin(ang) * rope[srow:srow + 1, :]

    def widen(z, reps):
        return jnp.concatenate([z] * reps, axis=1)

    def rotary(z, c, s, first_row, half):
        w = z.shape[1]
        first = jnp.concatenate([rope[first_row:first_row + 1, :]] * (w // LANES), axis=1) > 0.5
        partner = jnp.where(first, pltpu.roll(z, w - half, 1), pltpu.roll(z, half, 1))
        return z * c + partner * s

    def head_norm(z, w_row):
        ms = _group_sum(z * z, g_ref) * (1.0 / HEAD_DIM)
        return z * lax.rsqrt(ms + RMS_EPS) * w_row

    def proj(c0, width):
        return jnp.dot(xb, w_ref[:, c0:c0 + width], preferred_element_type=F32)

    def shift(z, c0):
        width = z.shape[1]
        row = lax.broadcasted_iota(jnp.int32, z.shape, 0)
        prev = jnp.where(row == 0, carry_ref[0:1, c0:c0 + width], pltpu.roll(z, 1, 0))
        carry_ref[0:1, c0:c0 + width] = z[tm - 1:tm, :]
        return z + (prev - z) * mu_ref[:, c0:c0 + width]

    cq, sq = (widen(z, D_GROUP // LANES) for z in tables(0, 1))
    q = rotary(head_norm(proj(_C_Q, D_GROUP), qnw_ref[...]), cq, sq, 2, HEAD_DIM // 8)
    q_ref[...] = q.astype(BF16)
    k = rotary(head_norm(proj(_C_K, D_GROUP), knw_ref[...]), cq, sq, 2, HEAD_DIM // 8)
    k_ref[...] = k.astype(BF16)
    v_ref[...] = proj(_C_V, D_GROUP).astype(BF16)

    rr_ref[...] = shift(proj(_C_RR, D_GROUP), 0)
    rk_ref[...] = shift(proj(_C_RK, D_GROUP), D_GROUP)
    rv_ref[...] = shift(proj(_C_RV, D_GROUP), 2 * D_GROUP)
    lora_ref[...] = shift(proj(_C_LORA, LANES), 3 * D_GROUP)

    ci, si = tables(3, 4)
    iq_ref[...] = rotary(proj(_C_IQ, D_IDX), widen(ci, D_IDX // LANES), widen(si, D_IDX // LANES),
                         5, IDX_DIM // 8)
    only_key = rope[6:7, :]
    sm_ref[...] = rotary(proj(_C_SM, LANES), ci * only_key + (1.0 - only_key), si * only_key,
                         5, IDX_DIM // 8)


def _in_proj(x2, pos_f, n1w, w_packed, mu_packed, qnw, knw, gmat, rope, tm):
    t = x2.shape[0]
    full = lambda shape: pl.BlockSpec(shape, lambda i: (0,) * len(shape))
    row = lambda width: pl.BlockSpec((tm, width), lambda i: (i, 0))
    out_shapes = (
        jax.ShapeDtypeStruct((t, D_GROUP), BF16),
        jax.ShapeDtypeStruct((t, D_GROUP), BF16),
        jax.ShapeDtypeStruct((t, D_GROUP), BF16),
        jax.ShapeDtypeStruct((t, D_GROUP), F32),
        jax.ShapeDtypeStruct((t, D_GROUP), F32),
        jax.ShapeDtypeStruct((t, D_GROUP), F32),
        jax.ShapeDtypeStruct((t, D_IDX), F32),
        jax.ShapeDtypeStruct((t, LANES), F32),
        jax.ShapeDtypeStruct((t, LANES), F32),
    )
    return pl.pallas_call(
        _in_proj_kernel,
        grid=(t // tm,),
        in_specs=[row(D_MODEL), row(1), full((1, D_MODEL)), full((D_MODEL, D_IN_PACKED)),
                  full((1, D_SHIFT)), full((1, D_GROUP)), full((1, D_GROUP)),
                  full((D_GROUP, D_GROUP)), full((SUBLANES, LANES))],
        out_specs=[row(D_GROUP)] * 6 + [row(D_IDX), row(LANES), row(LANES)],
        out_shape=out_shapes,
        scratch_shapes=[pltpu.VMEM((SUBLANES, D_SHIFT), F32)],
        compiler_params=pltpu.CompilerParams(dimension_semantics=("arbitrary",),
                                             vmem_limit_bytes=VMEM_LIMIT),
        name="in_proj",
    )(x2, pos_f, n1w, w_packed, mu_packed, qnw, knw, gmat, rope)


RWKV_CHUNK = 64
RWKV_UNROLL = 4
PAIR = 2 * HEAD_DIM


def _rwkv_kernel(r_ref, k_ref, v_ref, lora_ref, w0_ref, w2_ref, a0_ref, a2_ref, g2_ref,
                 kk_ref, ka_ref, rk_ref, lnw_ref, lnb_ref, g_ref, *rest, n_cast):
    cast_in, o_ref, cast_out = rest[:n_cast], rest[n_cast], rest[n_cast + 1:2 * n_cast + 1]
    s_ref, ld_s, r_s, k2_s, b_s, kk_s, y_s = rest[2 * n_cast + 1:]
    for src, dst in zip(cast_in, cast_out):
        dst[...] = src[...].astype(BF16)
    tm = r_ref.shape[0]
    n_chunks = tm // RWKV_CHUNK
    n_pairs = D_GROUP // PAIR
    c = RWKV_CHUNK

    @pl.when(pl.program_id(0) == 0)
    def _():
        s_ref[...] = jnp.zeros_like(s_ref)

    lora = lora_ref[...]
    r = r_ref[...]
    k = k_ref[...]
    v = v_ref[...]
    zarg = w0_ref[...] + _dot3(jnp.tanh(lora), w2_ref[...])
    sp = jnp.maximum(-zarg, 0.0) + jnp.log1p(jnp.exp(-jnp.abs(zarg)))
    ld_s[...] = -jnp.exp(-sp - 0.5)
    a = jax.nn.sigmoid(a0_ref[...] + _dot3(lora, a2_ref[...]))
    g = _dot3(jax.nn.sigmoid(lora), g2_ref[...])
    kk = k * kk_ref[...]
    kk = kk * lax.rsqrt(jnp.maximum(_group_sum(kk * kk, g_ref), 1e-24))
    k2 = k * (1.0 + (a - 1.0) * ka_ref[...])
    bonus = _group_sum(r * k2 * rk_ref[...], g_ref) * v
    r_s[...] = r
    k2_s[...] = k2
    kk_s[...] = kk
    b_s[...] = kk * a

    row = lax.broadcasted_iota(jnp.int32, (2 * c, 2 * c), 0)
    col = lax.broadcasted_iota(jnp.int32, (2 * c, 2 * c), 1)
    same_head = (row >= c) == (col >= c)
    strict = same_head & (col < row)
    incl = same_head & (col <= row)
    eye = (row == col).astype(F32)
    tri = (lax.broadcasted_iota(jnp.int32, (c, c), 1)
           <= lax.broadcasted_iota(jnp.int32, (c, c), 0)).astype(F32)
    lane = lax.broadcasted_iota(jnp.int32, (c, PAIR), 1)
    head0 = lane < HEAD_DIM

    def stack(z):
        return jnp.concatenate([jnp.where(head0, z, 0.0), jnp.where(head0, 0.0, z)], axis=0)

    def chunk_body(ci, carry):
        c2 = 2 * c
        units = [(cc, p) for cc in range(RWKV_UNROLL) for p in range(n_pairs)]
        uid = range(len(units))
        rows = [pl.multiple_of((ci * RWKV_UNROLL + cc) * c, c) for cc in range(RWKV_UNROLL)]
        ld_all = [ld_s[pl.ds(r0, c), :] for r0 in rows]
        cum_all = [_dotf(tri, ld) for ld in ld_all]
        rt, kt, bt, kp, vs, lhs, rhs, gam_end = [], [], [], [], [], [], [], []
        for cc, p in units:
            r0 = rows[cc]
            cols = slice(p * PAIR, (p + 1) * PAIR)
            ld = ld_all[cc][:, cols]
            cum = cum_all[cc][:, cols]
            gam = jnp.exp(cum)
            inv = jnp.exp(-cum)
            gam_prev = jnp.exp(cum - ld)
            gam_end.append(gam[c - 1:c, :])
            rt.append(stack(r_s[pl.ds(r0, c), cols] * gam))
            kt_raw = k2_s[pl.ds(r0, c), cols] * inv
            bt_raw = b_s[pl.ds(r0, c), cols] * inv
            kt.append(stack(kt_raw))
            bt.append(stack(bt_raw))
            kp.append(stack(kk_s[pl.ds(r0, c), cols] * gam_prev))
            vs.append(stack(v_ref[pl.ds(r0, c), cols]))
            lhs.append(jnp.concatenate([kp[-1], rt[-1]], axis=0))
            rhs.append(jnp.concatenate([bt_raw, bt_raw, kt_raw, kt_raw], axis=0))
        aa = [_dot_nt(lhs[u], rhs[u]) for u in uid]
        a_kk = [jnp.where(strict, aa[u][:c2, c2:], 0.0) for u in uid]
        a_rb = [jnp.where(incl, aa[u][c2:, :c2], 0.0) for u in uid]
        a_rk = [jnp.where(incl, aa[u][c2:, c2:], 0.0) for u in uid]
        n = [-jnp.where(strict, aa[u][:c2, :c2], 0.0) for u in uid]
        prod = [eye + n[u] for u in uid]
        n = [_dot(n[u], n[u]) for u in uid]
        for _ in range(int(np.log2(c)) - 2):
            both = [_dot(jnp.concatenate([n[u], prod[u]], axis=0), n[u]) for u in uid]
            prod = [prod[u] + both[u][c2:] for u in uid]
            n = [both[u][:c2] for u in uid]
        av = [_dot(jnp.concatenate([a_kk[u], a_rk[u]], axis=0), vs[u]) for u in uid]
        tinv = [prod[u] + _dot(prod[u], n[u]) for u in uid]
        wu = [_dot(tinv[u], jnp.concatenate([kp[u], av[u][:c2]], axis=1)) for u in uid]
        rb = [_dot(a_rb[u], wu[u]) for u in uid]
        tn = [_dot_tn(jnp.concatenate([wu[u], vs[u]], axis=1),
                      jnp.concatenate([bt[u], kt[u]], axis=1)) for u in uid]
        r2 = [rt[u] - rb[u][:, :PAIR] for u in uid]
        y2 = [av[u][c2:] - rb[u][:, PAIR:] for u in uid]
        s_mix = [(eye - tn[u][:PAIR, :PAIR]) * gam_end[u] for u in uid]
        s_add = [(tn[u][2 * PAIR:, PAIR:] - tn[u][PAIR:2 * PAIR, :PAIR]) * gam_end[u] for u in uid]
        state = [s_ref[p] for p in range(n_pairs)]
        for cc in range(RWKV_UNROLL):
            us = [cc * n_pairs + p for p in range(n_pairs)]
            ys = [_dot_nt(r2[u], state[p]) + y2[u] for p, u in enumerate(us)]
            state = [_dot(state[p], s_mix[u]) + s_add[u] for p, u in enumerate(us)]
            for p in range(n_pairs):
                y_s[pl.ds(rows[cc], c), p * PAIR:(p + 1) * PAIR] = ys[p][:c] + ys[p][c:]
        for p in range(n_pairs):
            s_ref[p] = state[p]
        return carry

    lax.fori_loop(0, n_chunks // RWKV_UNROLL, chunk_body, 0)

    y = y_s[...]
    mean = _group_sum(y, g_ref) * (1.0 / HEAD_DIM)
    yc = y - mean
    var = _group_sum(yc * yc, g_ref) * (1.0 / HEAD_DIM)
    yn = yc * lax.rsqrt(var + RWKV_GN_EPS) * lnw_ref[...] + lnb_ref[...]
    o_ref[...] = (yn + bonus) * g


def _rwkv(rr, rk, rv, lora, w0, w2p, a0, a2p, g2p, k_k, k_a, r_k, ln_w, ln_b, gsum, tm,
          to_cast=()):
    t = rr.shape[0]
    full = lambda shape: pl.BlockSpec(shape, lambda i: (0,) * len(shape))
    row = lambda width: pl.BlockSpec((tm, width), lambda i: (i, 0))
    vec = full((1, D_GROUP))
    big = pltpu.VMEM((tm, D_GROUP), F32)
    slabs = [w.reshape(t // tm, -1, w.shape[-1]) for w in to_cast]
    cast_specs = [pl.BlockSpec((1,) + w.shape[1:], lambda i: (i, 0, 0)) for w in slabs]
    outs = pl.pallas_call(
        functools.partial(_rwkv_kernel, n_cast=len(slabs)),
        grid=(t // tm,),
        in_specs=[row(D_GROUP), row(D_GROUP), row(D_GROUP), row(LANES),
                  vec, full((LANES, D_GROUP)), vec, full((LANES, D_GROUP)),
                  full((LANES, D_GROUP)), vec, vec, vec, vec, vec, full((D_GROUP, D_GROUP))]
        + cast_specs,
        out_specs=[row(D_GROUP)] + cast_specs,
        out_shape=[jax.ShapeDtypeStruct((t, D_GROUP), F32)]
        + [jax.ShapeDtypeStruct(w.shape, BF16) for w in slabs],
        scratch_shapes=[pltpu.VMEM((D_GROUP // PAIR, PAIR, PAIR), F32),
                        big, big, big, big, big, big],
        compiler_params=pltpu.CompilerParams(dimension_semantics=("arbitrary",),
                                             vmem_limit_bytes=VMEM_LIMIT),
        name="rwkv",
    )(rr, rk, rv, lora, w0, w2p, a0, a2p, g2p, k_k, k_a, r_k, ln_w, ln_b, gsum, *slabs)
    return outs[0], [o.reshape(w.shape) for o, w in zip(outs[1:], to_cast)]


DSA_QB = 256
DSA_KB = 512
M_INIT = -5e29
DSA_HEAD_GROUP = 4
DSA_ACC_ROWS = 32
DSA_GROUPS = 256
DSA_TOP = 10
DSA_V_ROWS = HEAD_DIM + 16


def _dsa_kernel(qi_ref, kj_ref, qt_ref, k_ref, vt_ref, iqt_ref, ik_ref, iwt_ref, o_ref,
                key_s, thr_s, need_s, over_s, m_s, acc_s, bdq_s, top_s, tops_s, *, n_sel):
    qb = qt_ref.shape[2]
    kb = k_ref.shape[0]
    step = pl.program_id(0)
    qi = qi_ref[step]
    kj = kj_ref[step]
    q0 = qi * qb
    j_last = (q0 + qb - 1) // kb
    n_kc = j_last + 1
    idx_scale = float((IDX_DIM * N_IDX_HEADS) ** -0.5)

    def causal(jblk):
        s_pos = jblk * kb + lax.broadcasted_iota(jnp.int32, (kb, qb), 0)
        t_pos = q0 + lax.broadcasted_iota(jnp.int32, (kb, qb), 1)
        return s_pos <= t_pos

    @pl.when(kj == 0)
    def _():
        iqt = iqt_ref[...].astype(BF16)
        iwt = iwt_ref[...] * idx_scale

        def score_chunk(kc, diagonal):
            ik = ik_ref[pl.ds(pl.multiple_of(kc * kb, kb), kb), :]
            score = jnp.zeros((kb, qb), F32)
            for h in range(N_IDX_HEADS):
                d = jnp.dot(ik, iqt[h * IDX_DIM:(h + 1) * IDX_DIM, :],
                            preferred_element_type=F32)
                score = score + jnp.maximum(d, 0.0) * iwt[h:h + 1, :]
            keys = to_key(score)
            if diagonal:
                ok = causal(kc)
                score = jnp.where(ok, score, -jnp.inf)
                keys = jnp.where(ok, keys, jnp.int32(INT_MIN))
            key_s[kc] = keys
            xs = [score[r * DSA_GROUPS:(r + 1) * DSA_GROUPS] for r in range(kb // DSA_GROUPS)]
            for lvl in range(DSA_TOP):
                s = tops_s[lvl]
                for r in range(len(xs)):
                    s, xs[r] = jnp.maximum(s, xs[r]), jnp.minimum(s, xs[r])
                tops_s[lvl] = s

        def to_key(score):
            bits = pltpu.bitcast(score, jnp.int32)
            return jnp.where(bits < 0, bits ^ jnp.int32(0x7FFFFFFF), bits)

        tops_s[...] = jnp.full_like(tops_s, -jnp.inf)

        def full_chunk(kc, carry):
            score_chunk(kc, False)
            return carry

        lax.fori_loop(0, j_last, full_chunk, 0)
        score_chunk(j_last, True)
        for lvl in range(DSA_TOP):
            kept = tops_s[lvl]
            top_s[lvl] = jnp.where(kept == -jnp.inf, jnp.int32(INT_MIN), to_key(kept))

        def select(count_keys):
            def bit_step(b, cur):
                bit = lax.shift_left(jnp.int32(1), jnp.int32(31) - b)
                cand = (cur | bit) ^ jnp.int32(INT_MIN)
                cnt = jnp.sum(count_keys(cand), axis=0, keepdims=True)
                return jnp.where(cnt >= float(n_sel), cur | bit, cur)

            cur = lax.fori_loop(0, 32, bit_step, jnp.zeros((1, qb), jnp.int32))
            return jnp.maximum(cur ^ jnp.int32(INT_MIN), jnp.int32(INT_MIN + 1))

        def part_count(keys, cand):
            ind = jnp.where(keys >= cand, 1.0, 0.0)
            return jnp.sum(ind.reshape(-1, DSA_ACC_ROWS, qb), axis=0)

        def count_top(cand):
            return lax.fori_loop(0, DSA_TOP, lambda lvl, a: a + part_count(top_s[lvl], cand),
                                 jnp.zeros((DSA_ACC_ROWS, qb), F32))

        def count_all(cand):
            return lax.fori_loop(0, n_kc, lambda kc, a: a + part_count(key_s[kc], cand),
                                 jnp.zeros((DSA_ACC_ROWS, qb), F32))

        def settle(count_keys):
            thr = select(count_keys)
            thr_s[...] = thr
            above = jnp.sum(count_keys(thr + 1), axis=0, keepdims=True)
            upto = jnp.sum(count_keys(thr), axis=0, keepdims=True)
            need_s[...] = float(n_sel) - above
            over_s[...] = upto - float(n_sel)
            return thr

        thr = settle(count_top)
        hidden = jnp.where(top_s[DSA_TOP - 1] >= thr, 1.0, 0.0)

        @pl.when(jnp.max(hidden) > 0.0)
        def _():
            settle(count_all)

        @pl.when(jnp.max(over_s[...]) > 0.0)
        def _():
            thr = thr_s[...]
            need = need_s[...]
            sub = LANES
            earlier = jnp.where(lax.broadcasted_iota(jnp.int32, (sub, sub), 1)
                                < lax.broadcasted_iota(jnp.int32, (sub, sub), 0), 1.0, 0.0).astype(BF16)

            def drop_surplus(kc, seen):
                keys = key_s[kc]
                tie = keys == thr
                tie_f = jnp.where(tie, 1.0, 0.0)
                ranks = []
                for r in range(kb // sub):
                    part = tie_f[r * sub:(r + 1) * sub]
                    ranks.append(seen + jnp.dot(earlier, part.astype(BF16),
                                                preferred_element_type=F32))
                    seen = seen + jnp.sum(part, axis=0, keepdims=True)
                rank = jnp.concatenate(ranks, axis=0)
                key_s[kc] = jnp.where(tie & (rank >= need), thr - 1, keys)
                return seen

            lax.fori_loop(0, n_kc, drop_surplus, jnp.zeros((1, qb), F32))

        m_s[...] = jnp.full_like(m_s, M_INIT)
        acc_s[...] = jnp.zeros_like(acc_s)
        bdq_s[...] = jnp.zeros_like(bdq_s)
        for h in range(N_HEADS):
            g, hh = divmod(h, DSA_HEAD_GROUP)
            bdq_s[g, hh * HEAD_DIM:(hh + 1) * HEAD_DIM, hh * qb:(hh + 1) * qb] = (
                qt_ref[h].astype(F32) * (HEAD_DIM ** -0.5 * LOG2_E)).astype(BF16)

    mask = key_s[kj] >= thr_s[...]
    hg = DSA_HEAD_GROUP
    st_all = [jnp.dot(k_ref[:, g * hg * HEAD_DIM:(g + 1) * hg * HEAD_DIM], bdq_s[g],
                      preferred_element_type=F32) for g in range(N_HEADS // hg)]
    for g in range(N_HEADS // hg):
        st_g = st_all[g]
        for hh in range(hg):
            h = g * hg + hh
            st = jnp.where(mask, st_g[:, hh * qb:(hh + 1) * qb], NEG_BIG)
            m_old = m_s[h]
            part = jnp.max(st.reshape(kb // DSA_ACC_ROWS, DSA_ACC_ROWS, qb), axis=0)
            m_new = jnp.maximum(m_old, jnp.max(part, axis=0, keepdims=True))
            p = jnp.exp2((st - m_new).astype(BF16))
            acc_s[h] = jnp.exp2(m_old - m_new) * acc_s[h] + jnp.dot(
                vt_ref[h], p, preferred_element_type=F32)
            m_s[h] = m_new

    @pl.when(kj == j_last)
    def _():
        for h in range(N_HEADS):
            acc = acc_s[h]
            o_ref[h] = acc[:HEAD_DIM] / acc[HEAD_DIM:HEAD_DIM + 1]


def _dsa(qt, k, vt, iqt, ik, iwt, n_sel):
    t = ik.shape[0]
    qb, kb = DSA_QB, DSA_KB
    assert n_sel <= DSA_GROUPS and kb % DSA_GROUPS == 0
    nq, nk = t // qb, t // kb
    pairs = [(i, j) for i in range(nq) for j in range((i * qb + qb - 1) // kb + 1)]
    qi = jnp.asarray(np.array([p[0] for p in pairs], np.int32))
    kj = jnp.asarray(np.array([p[1] for p in pairs], np.int32))
    grid_spec = pltpu.PrefetchScalarGridSpec(
        num_scalar_prefetch=2,
        grid=(len(pairs),),
        in_specs=[pl.BlockSpec((N_HEADS, HEAD_DIM, qb), lambda s, qi, kj: (0, 0, qi[s])),
                  pl.BlockSpec((kb, D_GROUP), lambda s, qi, kj: (kj[s], 0)),
                  pl.BlockSpec((N_HEADS, DSA_V_ROWS, kb), lambda s, qi, kj: (0, 0, kj[s])),
                  pl.BlockSpec((D_IDX, qb), lambda s, qi, kj: (0, qi[s])),
                  pl.BlockSpec((t, IDX_DIM), lambda s, qi, kj: (0, 0)),
                  pl.BlockSpec((N_IDX_HEADS, qb), lambda s, qi, kj: (0, qi[s]))],
        out_specs=pl.BlockSpec((N_HEADS, HEAD_DIM, qb), lambda s, qi, kj: (0, 0, qi[s])),
        scratch_shapes=[pltpu.VMEM((nk, kb, qb), jnp.int32),
                        pltpu.VMEM((1, qb), jnp.int32),
                        pltpu.VMEM((1, qb), F32),
                        pltpu.VMEM((1, qb), F32),
                        pltpu.VMEM((N_HEADS, 1, qb), F32),
                        pltpu.VMEM((N_HEADS, DSA_V_ROWS, qb), F32),
                        pltpu.VMEM((N_HEADS // DSA_HEAD_GROUP, DSA_HEAD_GROUP * HEAD_DIM,
                                    DSA_HEAD_GROUP * qb), BF16),
                        pltpu.VMEM((DSA_TOP, DSA_GROUPS, qb), jnp.int32),
                        pltpu.VMEM((DSA_TOP, DSA_GROUPS, qb), F32)])
    return pl.pallas_call(
        functools.partial(_dsa_kernel, n_sel=n_sel),
        grid_spec=grid_spec,
        out_shape=jax.ShapeDtypeStruct((N_HEADS, HEAD_DIM, t), F32),
        compiler_params=pltpu.CompilerParams(dimension_semantics=("arbitrary",),
                                             vmem_limit_bytes=VMEM_LIMIT),
        name="dsa",
    )(qi, kj, qt, k, vt, iqt, ik, iwt)


def _out_proj_kernel(x_ref, att_ref, rw_ref, wo_ref, n2w_ref, rwt_ref, rb_ref,
                     h_ref, xn_ref, gate_ref):
    tm = x_ref.shape[0]
    att = att_ref[...].reshape(D_GROUP, tm).T
    mix = jnp.concatenate([att, rw_ref[...]], axis=1)
    acc = x_ref[...] + _dot(mix, wo_ref[...])
    h_ref[...] = acc
    xn = acc * lax.rsqrt(jnp.mean(acc * acc, axis=-1, keepdims=True) + RMS_EPS) * n2w_ref[...]
    xn_ref[...] = xn.astype(BF16)
    logits = _dotf(xn, rwt_ref[...]) + rb_ref[...]
    lane = lax.broadcasted_iota(jnp.int32, logits.shape, 1)
    work = logits
    vals, hots = [], []
    for _ in range(TOP_K_EXPERTS):
        m = jnp.max(work, axis=1, keepdims=True)
        idx = jnp.min(jnp.where(work == m, lane, LANES), axis=1, keepdims=True)
        hot = lane == idx
        vals.append(m)
        hots.append(hot)
        work = jnp.where(hot, -jnp.inf, work)
    es = [jnp.exp(vv - vals[0]) for vv in vals]
    denom = es[0] + es[1] + es[2] + es[3]
    gates = jnp.zeros_like(logits)
    for e, hot in zip(es, hots):
        gates = gates + jnp.where(hot, e / denom, 0.0)
    gate_ref[...] = gates


def _out_proj(x2, att_t, rw, wo, n2w, rwt, rb, tm):
    t = x2.shape[0]
    full = lambda shape: pl.BlockSpec(shape, lambda i: (0,) * len(shape))
    row = lambda width: pl.BlockSpec((tm, width), lambda i: (i, 0))
    return pl.pallas_call(
        _out_proj_kernel,
        grid=(t // tm,),
        in_specs=[row(D_MODEL), pl.BlockSpec((N_HEADS, HEAD_DIM, tm), lambda i: (0, 0, i)),
                  row(D_GROUP), full((2 * D_GROUP, D_MODEL)), full((1, D_MODEL)),
                  full((D_MODEL, LANES)), full((1, LANES))],
        out_specs=[row(D_MODEL), row(D_MODEL), row(LANES)],
        out_shape=(jax.ShapeDtypeStruct((t, D_MODEL), F32),
                   jax.ShapeDtypeStruct((t, D_MODEL), BF16),
                   jax.ShapeDtypeStruct((t, LANES), F32)),
        compiler_params=pltpu.CompilerParams(dimension_semantics=("arbitrary",),
                                             vmem_limit_bytes=VMEM_LIMIT),
        name="out_proj",
    )(x2, att_t, rw, wo, n2w, rwt, rb)


MOE_TM = 1024
MOE_RB = 144
MOE_FC = 512
MOE_EXPERTS_PER_STEP = 2


def _route_kernel(gate_ref, rankt_ref, gatet_ref, cnt_ref):
    tm = gate_ref.shape[0]
    gates = gate_ref[...]
    hot = gates > 0.0
    ind = jnp.where(hot, 1.0, 0.0)
    before = (lax.broadcasted_iota(jnp.int32, (tm, tm), 1)
              < lax.broadcasted_iota(jnp.int32, (tm, tm), 0))
    rank = jnp.dot(jnp.where(before, 1.0, 0.0).astype(BF16), ind.astype(BF16),
                   preferred_element_type=F32)
    rankt_ref[...] = jnp.where(hot, rank, -1.0).T
    gatet_ref[...] = gates.T
    cnt = jnp.sum(ind, axis=0, keepdims=True)
    cnt_ref[...] = jnp.broadcast_to(cnt[None], cnt_ref.shape)


def _route(gates, tm):
    t = gates.shape[0]
    return pl.pallas_call(
        _route_kernel,
        grid=(t // tm,),
        in_specs=[pl.BlockSpec((tm, LANES), lambda i: (i, 0))],
        out_specs=[pl.BlockSpec((LANES, tm), lambda i: (0, i)),
                   pl.BlockSpec((LANES, tm), lambda i: (0, i)),
                   pl.BlockSpec((1, SUBLANES, LANES), lambda i: (i, 0, 0))],
        out_shape=(jax.ShapeDtypeStruct((LANES, t), F32),
                   jax.ShapeDtypeStruct((LANES, t), F32),
                   jax.ShapeDtypeStruct((t // tm, SUBLANES, LANES), F32)),
        compiler_params=pltpu.CompilerParams(dimension_semantics=("arbitrary",),
                                             vmem_limit_bytes=VMEM_LIMIT),
        name="route",
    )(gates)


def _moe_kernel(cnt_ref, h_ref, xn_ref, rankt_ref, gatet_ref, wgu_ref, bgu_ref,
                wdn_ref, bdn_ref, o_ref):
    i = pl.program_id(0)
    tm = h_ref.shape[0]

    @pl.when(pl.program_id(1) == 0)
    def _():
        o_ref[...] = h_ref[...]

    for sub in range(MOE_EXPERTS_PER_STEP):
        _moe_expert(pl.program_id(1) * MOE_EXPERTS_PER_STEP + sub, sub, i, tm, cnt_ref, xn_ref,
                    rankt_ref, gatet_ref, wgu_ref, bgu_ref, wdn_ref, bdn_ref, o_ref)


def _moe_expert(e, slot, i, tm, cnt_ref, xn_ref, rankt_ref, gatet_ref, wgu_ref, bgu_ref,
                wdn_ref, bdn_ref, o_ref):
    n_rows = cnt_ref[i * N_EXPERTS + e]
    n_blocks = (n_rows + MOE_RB - 1) // MOE_RB
    rank_row = rankt_ref[pl.ds(e, 1), :]
    gate_row = gatet_ref[pl.ds(e, 1), :]

    def row_block(bi, carry):
        r0 = (bi * MOE_RB).astype(F32)
        rowid = lax.broadcasted_iota(jnp.int32, (MOE_RB, tm), 0).astype(F32) + r0
        hit = rank_row == rowid
        sel = jnp.where(hit, 1.0, 0.0).astype(BF16)
        gate = jnp.sum(jnp.where(hit, gate_row, 0.0), axis=1, keepdims=True)
        xb = jnp.dot(sel, xn_ref[...], preferred_element_type=F32).astype(BF16)
        y = jnp.zeros((MOE_RB, D_MODEL), F32)
        ups = []
        for fc in range(D_FF // MOE_FC):
            c0 = fc * MOE_FC
            ups.append((jnp.dot(xb, wgu_ref[slot, :, c0:c0 + MOE_FC], preferred_element_type=F32),
                        jnp.dot(xb, wgu_ref[slot, :, D_FF + c0:D_FF + c0 + MOE_FC],
                                preferred_element_type=F32)))
        for fc in range(D_FF // MOE_FC):
            c0 = fc * MOE_FC
            hg = ups[fc][0] + bgu_ref[slot, :, c0:c0 + MOE_FC]
            hl = ups[fc][1] + bgu_ref[slot, :, D_FF + c0:D_FF + c0 + MOE_FC]
            glu = jnp.minimum(hg, SWIGLU_LIMIT)
            lin = jnp.clip(hl, -SWIGLU_LIMIT, SWIGLU_LIMIT)
            act = (lin + 1.0) * glu * jax.nn.sigmoid(SWIGLU_ALPHA * glu)
            y = y + jnp.dot(act.astype(BF16), wdn_ref[slot, c0:c0 + MOE_FC, :],
                            preferred_element_type=F32)
        y = ((y + bdn_ref[slot]) * gate).astype(BF16)
        o_ref[...] += _dot_tn(sel, y)
        return carry

    lax.fori_loop(0, n_blocks, row_block, 0)


def _moe(counts, h1, xn2, rank_t, gate_t, wgu, bgu, wdn, bdn, tm):
    t = h1.shape[0]
    eps = MOE_EXPERTS_PER_STEP
    row = lambda width: pl.BlockSpec((tm, width), lambda i, e, c: (i, 0))
    grid_spec = pltpu.PrefetchScalarGridSpec(
        num_scalar_prefetch=1,
        grid=(t // tm, N_EXPERTS // eps),
        in_specs=[row(D_MODEL), row(D_MODEL),
                  pl.BlockSpec((LANES, tm), lambda i, e, c: (0, i)),
                  pl.BlockSpec((LANES, tm), lambda i, e, c: (0, i)),
                  pl.BlockSpec((eps, D_MODEL, 2 * D_FF), lambda i, e, c: (e, 0, 0)),
                  pl.BlockSpec((eps, 1, 2 * D_FF), lambda i, e, c: (e, 0, 0)),
                  pl.BlockSpec((eps, D_FF, D_MODEL), lambda i, e, c: (e, 0, 0)),
                  pl.BlockSpec((eps, 1, D_MODEL), lambda i, e, c: (e, 0, 0))],
        out_specs=row(D_MODEL))
    return pl.pallas_call(
        _moe_kernel,
        grid_spec=grid_spec,
        out_shape=jax.ShapeDtypeStruct((t, D_MODEL), F32),
        compiler_params=pltpu.CompilerParams(dimension_semantics=("arbitrary", "arbitrary"),
                                             vmem_limit_bytes=VMEM_LIMIT),
        name="moe",
    )(counts, h1, xn2, rank_t, gate_t, wgu, bgu, wdn, bdn)


def _rope_tables():
    lane = jnp.arange(LANES)

    def rows(period, rot):
        half = rot // 2
        inv_freq = ROPE_THETA ** (-jnp.arange(half, dtype=F32) / half)
        jm = lane % period
        freq = jnp.where(jm < rot, inv_freq[jm % half], 0.0)
        sign = jnp.where(jm < half, -1.0, jnp.where(jm < rot, 1.0, 0.0))
        first = (jm < half).astype(F32)
        return [freq, sign, first]

    idx_rows = rows(IDX_DIM, IDX_DIM // 4)
    only_key = (lane < IDX_DIM).astype(F32)
    return jnp.stack(rows(HEAD_DIM, HEAD_DIM // 4) + idx_rows
                     + [only_key, jnp.zeros_like(only_key)]).astype(F32)


def _group_matrix():
    g = np.arange(D_GROUP) // HEAD_DIM
    return jnp.asarray((g[:, None] == g[None, :]).astype(np.float32), dtype=BF16)


def _pad_rows(w, r0, rows):
    return jnp.zeros((rows, w.shape[1]), w.dtype).at[r0:r0 + w.shape[0]].set(w)


def kernel(x, positions, norm1_w, w_in, q_norm_w, k_norm_w, rwkv_mu, rwkv_w0, rwkv_w2, rwkv_a0,
           rwkv_a2, rwkv_g2, rwkv_k_k, rwkv_k_a, rwkv_r_k, rwkv_ln_w, rwkv_ln_b, w_out, norm2_w,
           router_w, router_b, exp_w_gu, exp_b_gu, exp_w_down, exp_b_down):
    b, t, _ = x.shape
    assert b == 1 and w_in.shape[0] == 1, "single sequence, single layer"
    assert t % DSA_KB == 0 and t % DSA_QB == 0 and t % MOE_TM == 0
    x2 = x[0]
    pos_f = positions[0].astype(F32)[:, None]
    n_sel = min(TOPK_MAX, t // 4)

    w = w_in[0]
    a0 = 3 * D_GROUP
    att_cols = a0 + D_IDX + IDX_DIM + N_IDX_HEADS
    w_att, w_rw = w[:, :att_cols], w[:, att_cols:]
    w_sm = jnp.zeros((D_MODEL, LANES), F32).at[:, :IDX_DIM + N_IDX_HEADS].set(w_att[:, a0 + D_IDX:])
    w_packed = jnp.concatenate(
        [w_att[:, :a0], w_rw[:, :a0], w_att[:, a0:a0 + D_IDX], w_sm, w_rw[:, a0:]],
        axis=1).astype(BF16)
    mu = rwkv_mu[0][None, :]
    tile8 = lambda z: jnp.tile(z, N_HEADS)[None, :]
    gsum = _group_matrix()

    tm = 256
    q, k, v, rr, rk, rv, iq, sm, lora = _in_proj(
        x2, pos_f, norm1_w, w_packed, mu, tile8(q_norm_w[0]), tile8(k_norm_w[0]), gsum,
        _rope_tables(), tm)

    vec = lambda z: z.reshape(1, D_GROUP)
    rw, (w_gu_bf, w_dn_bf) = _rwkv(
        rr, rk, rv, lora, vec(rwkv_w0[0]), _pad_rows(rwkv_w2[0], 0, LANES), vec(rwkv_a0[0]),
        _pad_rows(rwkv_a2[0], D_DECAY_LORA, LANES),
        _pad_rows(rwkv_g2[0], D_DECAY_LORA + D_AAA_LORA, LANES),
        vec(rwkv_k_k[0]), vec(rwkv_k_a[0]), vec(rwkv_r_k[0]), vec(rwkv_ln_w[0]),
        vec(rwkv_ln_b[0]), gsum, tm, to_cast=(exp_w_gu[0], exp_w_down[0]))

    heads = lambda z: z.reshape(t, N_HEADS, HEAD_DIM)
    vt = jnp.concatenate([heads(v).transpose(1, 2, 0),
                          jnp.ones((N_HEADS, DSA_V_ROWS - HEAD_DIM, t), BF16)], axis=1)
    att_t = _dsa(heads(q).transpose(1, 2, 0), k, vt, iq.T, sm[:, :IDX_DIM].astype(BF16),
                 sm[:, IDX_DIM:IDX_DIM + N_IDX_HEADS].T, n_sel)

    rwt = jnp.zeros((D_MODEL, LANES), F32).at[:, :N_EXPERTS].set(router_w[0])
    rb = jnp.full((1, LANES), NEG_BIG, F32).at[0, :N_EXPERTS].set(router_b[0])
    h1, xn2, gates = _out_proj(x2, att_t, rw, w_out[0].astype(BF16), norm2_w, rwt, rb, tm)

    rank_t, gate_t, cnt = _route(gates, MOE_TM)
    counts = cnt[:, 0, :N_EXPERTS].astype(jnp.int32).reshape(-1)
    out = _moe(counts, h1, xn2, rank_t, gate_t, w_gu_bf, exp_b_gu[0][:, None, :], w_dn_bf,
               exp_b_down[0][:, None, :], MOE_TM)
    return out[None]
```

```python
import functools

import jax
import jax.numpy as jnp
import numpy as np
from jax import lax
from jax.experimental import pallas as pl
from jax.experimental.pallas import tpu as pltpu

F32 = jnp.float32
BF16 = jnp.bfloat16
HIGHEST = lax.Precision.HIGHEST

D_MODEL = 1024
HEAD_DIM = 64
N_HEADS = 8
D_GROUP = N_HEADS * HEAD_DIM
ROPE_THETA = 500000.0
N_IDX_HEADS = 8
IDX_DIM = 32
D_IDX = N_IDX_HEADS * IDX_DIM
TOPK_MAX = 256
D_DECAY_LORA = 32
D_AAA_LORA = 32
D_GATE_LORA = 64
RWKV_GN_EPS = 64e-5
N_EXPERTS = 32
TOP_K_EXPERTS = 4
D_FF = 1024
SWIGLU_LIMIT = 7.0
SWIGLU_ALPHA = 1.702
RMS_EPS = 1e-6

LANES = 128
SUBLANES = 8
VMEM_LIMIT = 56 * 1024 * 1024

NEG_BIG = -1e30
LOG2_E = 1.4426950408889634
INT_MIN = -(2 ** 31)


def _dot(a, b):
    return jnp.dot(a.astype(BF16), b.astype(BF16), preferred_element_type=F32)


def _dotf(a, b):
    return jnp.dot(a, b, preferred_element_type=F32, precision=HIGHEST)


def _dot3(a, b):
    a_hi = a.astype(BF16)
    b_hi = b.astype(BF16)
    a_lo = (a - a_hi.astype(F32)).astype(BF16)
    b_lo = (b - b_hi.astype(F32)).astype(BF16)
    dot = functools.partial(jnp.dot, preferred_element_type=F32)
    return dot(a_hi, b_hi) + dot(a_hi, b_lo) + dot(a_lo, b_hi)


def _dot_nt(a, b):
    return lax.dot_general(a.astype(BF16), b.astype(BF16), (((1,), (1,)), ((), ())),
                           preferred_element_type=F32)


def _dot_tn(a, b):
    return lax.dot_general(a.astype(BF16), b.astype(BF16), (((0,), (0,)), ((), ())),
                           preferred_element_type=F32)


def _group_sum(z, g_ref):
    hi = z.astype(BF16)
    lo = (z - hi.astype(F32)).astype(BF16)
    g = g_ref[...]
    return (jnp.dot(hi, g, preferred_element_type=F32)
            + jnp.dot(lo, g, preferred_element_type=F32))


_C_Q, _C_K, _C_V, _C_RR, _C_RK, _C_RV = (i * D_GROUP for i in range(6))
_C_IQ = 6 * D_GROUP
_C_SM = _C_IQ + D_IDX
_C_LORA = _C_SM + LANES
D_IN_PACKED = _C_LORA + LANES
D_SHIFT = 3 * D_GROUP + LANES


def _in_proj_kernel(x_ref, pos_ref, n1w_ref, w_ref, mu_ref, qnw_ref, knw_ref, g_ref, rope_ref,
                    q_ref, k_ref, v_ref, rr_ref, rk_ref, rv_ref, iq_ref, sm_ref, lora_ref,
                    carry_ref):
    tm = x_ref.shape[0]

    @pl.when(pl.program_id(0) == 0)
    def _():
        carry_ref[...] = jnp.zeros_like(carry_ref)

    x = x_ref[...]
    xn = x * lax.rsqrt(jnp.mean(x * x, axis=-1, keepdims=True) + RMS_EPS) * n1w_ref[...]
    xb = xn.astype(BF16)
    pos = pos_ref[...]
    rope = rope_ref[...]

    def tables(frow, srow):
        ang = pos * rope[frow:frow + 1, :]
        return jnp.cos(ang), jnp.sin(ang) * rope[srow:srow + 1, :]

    def widen(z, reps):
        return jnp.concatenate([z] * reps, axis=1)

    def rotary(z, c, s, first_row, half):
        w = z.shape[1]
        first = jnp.concatenate([rope[first_row:first_row + 1, :]] * (w // LANES), axis=1) > 0.5
        partner = jnp.where(first, pltpu.roll(z, w - half, 1), pltpu.roll(z, half, 1))
        return z * c + partner * s

    def head_norm(z, w_row):
        ms = _group_sum(z * z, g_ref) * (1.0 / HEAD_DIM)
        return z * lax.rsqrt(ms + RMS_EPS) * w_row

    def proj(c0, width):
        return jnp.dot(xb, w_ref[:, c0:c0 + width], preferred_element_type=F32)

    def shift(z, c0):
        width = z.shape[1]
        row = lax.broadcasted_iota(jnp.int32, z.shape, 0)
        prev = jnp.where(row == 0, carry_ref[0:1, c0:c0 + width], pltpu.roll(z, 1, 0))
        carry_ref[0:1, c0:c0 + width] = z[tm - 1:tm, :]
        return z + (prev - z) * mu_ref[:, c0:c0 + width]

    cq, sq = (widen(z, D_GROUP // LANES) for z in tables(0, 1))
    q = rotary(head_norm(proj(_C_Q, D_GROUP), qnw_ref[...]), cq, sq, 2, HEAD_DIM // 8)
    q_ref[...] = q.astype(BF16)
    k = rotary(head_norm(proj(_C_K, D_GROUP), knw_ref[...]), cq, sq, 2, HEAD_DIM // 8)
    k_ref[...] = k.astype(BF16)
    v_ref[...] = proj(_C_V, D_GROUP).astype(BF16)

    rr_ref[...] = shift(proj(_C_RR, D_GROUP), 0)
    rk_ref[...] = shift(proj(_C_RK, D_GROUP), D_GROUP)
    rv_ref[...] = shift(proj(_C_RV, D_GROUP), 2 * D_GROUP)
    lora_ref[...] = shift(proj(_C_LORA, LANES), 3 * D_GROUP)

    ci, si = tables(3, 4)
    iq_ref[...] = rotary(proj(_C_IQ, D_IDX), widen(ci, D_IDX // LANES), widen(si, D_IDX // LANES),
                         5, IDX_DIM // 8)
    only_key = rope[6:7, :]
    sm_ref[...] = rotary(proj(_C_SM, LANES), ci * only_key + (1.0 - only_key), si * only_key,
                         5, IDX_DIM // 8)


def _in_proj(x2, pos_f, n1w, w_packed, mu_packed, qnw, knw, gmat, rope, tm):
    t = x2.shape[0]
    full = lambda shape: pl.BlockSpec(shape, lambda i: (0,) * len(shape))
    row = lambda width: pl.BlockSpec((tm, width), lambda i: (i, 0))
    out_shapes = (
        jax.ShapeDtypeStruct((t, D_GROUP), BF16),
        jax.ShapeDtypeStruct((t, D_GROUP), BF16),
        jax.ShapeDtypeStruct((t, D_GROUP), BF16),
        jax.ShapeDtypeStruct((t, D_GROUP), F32),
        jax.ShapeDtypeStruct((t, D_GROUP), F32),
        jax.ShapeDtypeStruct((t, D_GROUP), F32),
        jax.ShapeDtypeStruct((t, D_IDX), F32),
        jax.ShapeDtypeStruct((t, LANES), F32),
        jax.ShapeDtypeStruct((t, LANES), F32),
    )
    return pl.pallas_call(
        _in_proj_kernel,
        grid=(t // tm,),
        in_specs=[row(D_MODEL), row(1), full((1, D_MODEL)), full((D_MODEL, D_IN_PACKED)),
                  full((1, D_SHIFT)), full((1, D_GROUP)), full((1, D_GROUP)),
                  full((D_GROUP, D_GROUP)), full((SUBLANES, LANES))],
        out_specs=[row(D_GROUP)] * 6 + [row(D_IDX), row(LANES), row(LANES)],
        out_shape=out_shapes,
        scratch_shapes=[pltpu.VMEM((SUBLANES, D_SHIFT), F32)],
        compiler_params=pltpu.CompilerParams(dimension_semantics=("arbitrary",),
                                             vmem_limit_bytes=VMEM_LIMIT),
        name="in_proj",
    )(x2, pos_f, n1w, w_packed, mu_packed, qnw, knw, gmat, rope)


RWKV_CHUNK = 64
RWKV_UNROLL = 4
PAIR = 2 * HEAD_DIM


def _rwkv_kernel(r_ref, k_ref, v_ref, lora_ref, w0_ref, w2_ref, a0_ref, a2_ref, g2_ref,
                 kk_ref, ka_ref, rk_ref, lnw_ref, lnb_ref, g_ref, *rest, n_cast):
    cast_in, o_ref, cast_out = rest[:n_cast], rest[n_cast], rest[n_cast + 1:2 * n_cast + 1]
    s_ref, ld_s, r_s, k2_s, b_s, kk_s, y_s = rest[2 * n_cast + 1:]
    for src, dst in zip(cast_in, cast_out):
        dst[...] = src[...].astype(BF16)
    tm = r_ref.shape[0]
    n_chunks = tm // RWKV_CHUNK
    n_pairs = D_GROUP // PAIR
    c = RWKV_CHUNK

    @pl.when(pl.program_id(0) == 0)
    def _():
        s_ref[...] = jnp.zeros_like(s_ref)

    lora = lora_ref[...]
    r = r_ref[...]
    k = k_ref[...]
    v = v_ref[...]
    zarg = w0_ref[...] + _dot3(jnp.tanh(lora), w2_ref[...])
    sp = jnp.maximum(-zarg, 0.0) + jnp.log1p(jnp.exp(-jnp.abs(zarg)))
    ld_s[...] = -jnp.exp(-sp - 0.5)
    a = jax.nn.sigmoid(a0_ref[...] + _dot3(lora, a2_ref[...]))
    g = _dot3(jax.nn.sigmoid(lora), g2_ref[...])
    kk = k * kk_ref[...]
    kk = kk * lax.rsqrt(jnp.maximum(_group_sum(kk * kk, g_ref), 1e-24))
    k2 = k * (1.0 + (a - 1.0) * ka_ref[...])
    bonus = _group_sum(r * k2 * rk_ref[...], g_ref) * v
    r_s[...] = r
    k2_s[...] = k2
    kk_s[...] = kk
    b_s[...] = kk * a

    row = lax.broadcasted_iota(jnp.int32, (2 * c, 2 * c), 0)
    col = lax.broadcasted_iota(jnp.int32, (2 * c, 2 * c), 1)
    same_head = (row >= c) == (col >= c)
    strict = same_head & (col < row)
    incl = same_head & (col <= row)
    eye = (row == col).astype(F32)
    tri = (lax.broadcasted_iota(jnp.int32, (c, c), 1)
           <= lax.broadcasted_iota(jnp.int32, (c, c), 0)).astype(F32)
    lane = lax.broadcasted_iota(jnp.int32, (c, PAIR), 1)
    head0 = lane < HEAD_DIM

    def stack(z):
        return jnp.concatenate([jnp.where(head0, z, 0.0), jnp.where(head0, 0.0, z)], axis=0)

    def chunk_body(ci, carry):
        c2 = 2 * c
        units = [(cc, p) for cc in range(RWKV_UNROLL) for p in range(n_pairs)]
        uid = range(len(units))
        rows = [pl.multiple_of((ci * RWKV_UNROLL + cc) * c, c) for cc in range(RWKV_UNROLL)]
        ld_all = [ld_s[pl.ds(r0, c), :] for r0 in rows]
        cum_all = [_dotf(tri, ld) for ld in ld_all]
        rt, kt, bt, kp, vs, lhs, rhs, gam_end = [], [], [], [], [], [], [], []
        for cc, p in units:
            r0 = rows[cc]
            cols = slice(p * PAIR, (p + 1) * PAIR)
            ld = ld_all[cc][:, cols]
            cum = cum_all[cc][:, cols]
            gam = jnp.exp(cum)
            inv = jnp.exp(-cum)
            gam_prev = jnp.exp(cum - ld)
            gam_end.append(gam[c - 1:c, :])
            rt.append(stack(r_s[pl.ds(r0, c), cols] * gam))
            kt_raw = k2_s[pl.ds(r0, c), cols] * inv
            bt_raw = b_s[pl.ds(r0, c), cols] * inv
            kt.append(stack(kt_raw))
            bt.append(stack(bt_raw))
            kp.append(stack(kk_s[pl.ds(r0, c), cols] * gam_prev))
            vs.append(stack(v_ref[pl.ds(r0, c), cols]))
            lhs.append(jnp.concatenate([kp[-1], rt[-1]], axis=0))
            rhs.append(jnp.concatenate([bt_raw, bt_raw, kt_raw, kt_raw], axis=0))
        aa = [_dot_nt(lhs[u], rhs[u]) for u in uid]
        a_kk = [jnp.where(strict, aa[u][:c2, c2:], 0.0) for u in uid]
        a_rb = [jnp.where(incl, aa[u][c2:, :c2], 0.0) for u in uid]
        a_rk = [jnp.where(incl, aa[u][c2:, c2:], 0.0) for u in uid]
        n = [-jnp.where(strict, aa[u][:c2, :c2], 0.0) for u in uid]
        prod = [eye + n[u] for u in uid]
        n = [_dot(n[u], n[u]) for u in uid]
        for _ in range(int(np.log2(c)) - 2):
            both = [_dot(jnp.concatenate([n[u], prod[u]], axis=0), n[u]) for u in uid]
            prod = [prod[u] + both[u][c2:] for u in uid]
            n = [both[u][:c2] for u in uid]
        av = [_dot(jnp.concatenate([a_kk[u], a_rk[u]], axis=0), vs[u]) for u in uid]
        tinv = [prod[u] + _dot(prod[u], n[u]) for u in uid]
        wu = [_dot(tinv[u], jnp.concatenate([kp[u], av[u][:c2]], axis=1)) for u in uid]
        rb = [_dot(a_rb[u], wu[u]) for u in uid]
        tn = [_dot_tn(jnp.concatenate([wu[u], vs[u]], axis=1),
                      jnp.concatenate([bt[u], kt[u]], axis=1)) for u in uid]
        r2 = [rt[u] - rb[u][:, :PAIR] for u in uid]
        y2 = [av[u][c2:] - rb[u][:, PAIR:] for u in uid]
        s_mix = [(eye - tn[u][:PAIR, :PAIR]) * gam_end[u] for u in uid]
        s_add = [(tn[u][2 * PAIR:, PAIR:] - tn[u][PAIR:2 * PAIR, :PAIR]) * gam_end[u] for u in uid]
        state = [s_ref[p] for p in range(n_pairs)]
        for cc in range(RWKV_UNROLL):
            us = [cc * n_pairs + p for p in range(n_pairs)]
            ys = [_dot_nt(r2[u], state[p]) + y2[u] for p, u in enumerate(us)]
            state = [_dot(state[p], s_mix[u]) + s_add[u] for p, u in enumerate(us)]
            for p in range(n_pairs):
                y_s[pl.ds(rows[cc], c), p * PAIR:(p + 1) * PAIR] = ys[p][:c] + ys[p][c:]
        for p in range(n_pairs):
            s_ref[p] = state[p]
        return carry

    lax.fori_loop(0, n_chunks // RWKV_UNROLL, chunk_body, 0)

    y = y_s[...]
    mean = _group_sum(y, g_ref) * (1.0 / HEAD_DIM)
    yc = y - mean
    var = _group_sum(yc * yc, g_ref) * (1.0 / HEAD_DIM)
    yn = yc * lax.rsqrt(var + RWKV_GN_EPS) * lnw_ref[...] + lnb_ref[...]
    o_ref[...] = (yn + bonus) * g


def _rwkv(rr, rk, rv, lora, w0, w2p, a0, a2p, g2p, k_k, k_a, r_k, ln_w, ln_b, gsum, tm,
          to_cast=()):
    t = rr.shape[0]
    full = lambda shape: pl.BlockSpec(shape, lambda i: (0,) * len(shape))
    row = lambda width: pl.BlockSpec((tm, width), lambda i: (i, 0))
    vec = full((1, D_GROUP))
    big = pltpu.VMEM((tm, D_GROUP), F32)
    slabs = [w.reshape(t // tm, -1, w.shape[-1]) for w in to_cast]
    cast_specs = [pl.BlockSpec((1,) + w.shape[1:], lambda i: (i, 0, 0)) for w in slabs]
    outs = pl.pallas_call(
        functools.partial(_rwkv_kernel, n_cast=len(slabs)),
        grid=(t // tm,),
        in_specs=[row(D_GROUP), row(D_GROUP), row(D_GROUP), row(LANES),
                  vec, full((LANES, D_GROUP)), vec, full((LANES, D_GROUP)),
                  full((LANES, D_GROUP)), vec, vec, vec, vec, vec, full((D_GROUP, D_GROUP))]
        + cast_specs,
        out_specs=[row(D_GROUP)] + cast_specs,
        out_shape=[jax.ShapeDtypeStruct((t, D_GROUP), F32)]
        + [jax.ShapeDtypeStruct(w.shape, BF16) for w in slabs],
        scratch_shapes=[pltpu.VMEM((D_GROUP // PAIR, PAIR, PAIR), F32),
                        big, big, big, big, big, big],
        compiler_params=pltpu.CompilerParams(dimension_semantics=("arbitrary",),
                                             vmem_limit_bytes=VMEM_LIMIT),
        name="rwkv",
    )(rr, rk, rv, lora, w0, w2p, a0, a2p, g2p, k_k, k_a, r_k, ln_w, ln_b, gsum, *slabs)
    return outs[0], [o.reshape(w.shape) for o, w in zip(outs[1:], to_cast)]


DSA_QB = 256
DSA_KB = 512
M_INIT = -5e29
DSA_HEAD_GROUP = 4
DSA_ACC_ROWS = 32
DSA_GROUPS = 256
DSA_TOP = 10
DSA_V_ROWS = HEAD_DIM + 16


def _dsa_kernel(qi_ref, kj_ref, qt_ref, k_ref, vt_ref, iqt_ref, ik_ref, iwt_ref, o_ref,
                key_s, thr_s, need_s, over_s, m_s, acc_s, bdq_s, top_s, tops_s, *, n_sel):
    qb = qt_ref.shape[2]
    kb = k_ref.shape[0]
    step = pl.program_id(0)
    qi = qi_ref[step]
    kj = kj_ref[step]
    q0 = qi * qb
    j_last = (q0 + qb - 1) // kb
    n_kc = j_last + 1
    idx_scale = float((IDX_DIM * N_IDX_HEADS) ** -0.5)

    def causal(jblk):
        s_pos = jblk * kb + lax.broadcasted_iota(jnp.int32, (kb, qb), 0)
        t_pos = q0 + lax.broadcasted_iota(jnp.int32, (kb, qb), 1)
        return s_pos <= t_pos

    @pl.when(kj == 0)
    def _():
        iqt = iqt_ref[...].astype(BF16)
        iwt = iwt_ref[...] * idx_scale

        def score_chunk(kc, diagonal):
            ik = ik_ref[pl.ds(pl.multiple_of(kc * kb, kb), kb), :]
            score = jnp.zeros((kb, qb), F32)
            for h in range(N_IDX_HEADS):
                d = jnp.dot(ik, iqt[h * IDX_DIM:(h + 1) * IDX_DIM, :],
                            preferred_element_type=F32)
                score = score + jnp.maximum(d, 0.0) * iwt[h:h + 1, :]
            keys = to_key(score)
            if diagonal:
                ok = causal(kc)
                score = jnp.where(ok, score, -jnp.inf)
                keys = jnp.where(ok, keys, jnp.int32(INT_MIN))
            key_s[kc] = keys
            xs = [score[r * DSA_GROUPS:(r + 1) * DSA_GROUPS] for r in range(kb // DSA_GROUPS)]
            for lvl in range(DSA_TOP):
                s = tops_s[lvl]
                for r in range(len(xs)):
                    s, xs[r] = jnp.maximum(s, xs[r]), jnp.minimum(s, xs[r])
                tops_s[lvl] = s

        def to_key(score):
            bits = pltpu.bitcast(score, jnp.int32)
            return jnp.where(bits < 0, bits ^ jnp.int32(0x7FFFFFFF), bits)

        tops_s[...] = jnp.full_like(tops_s, -jnp.inf)

        def full_chunk(kc, carry):
            score_chunk(kc, False)
            return carry

        lax.fori_loop(0, j_last, full_chunk, 0)
        score_chunk(j_last, True)
        for lvl in range(DSA_TOP):
            kept = tops_s[lvl]
            top_s[lvl] = jnp.where(kept == -jnp.inf, jnp.int32(INT_MIN), to_key(kept))

        def select(count_keys):
            def bit_step(b, cur):
                bit = lax.shift_left(jnp.int32(1), jnp.int32(31) - b)
                cand = (cur | bit) ^ jnp.int32(INT_MIN)
                cnt = jnp.sum(count_keys(cand), axis=0, keepdims=True)
                return jnp.where(cnt >= float(n_sel), cur | bit, cur)

            cur = lax.fori_loop(0, 32, bit_step, jnp.zeros((1, qb), jnp.int32))
            return jnp.maximum(cur ^ jnp.int32(INT_MIN), jnp.int32(INT_MIN + 1))

        def part_count(keys, cand):
            ind = jnp.where(keys >= cand, 1.0, 0.0)
            return jnp.sum(ind.reshape(-1, DSA_ACC_ROWS, qb), axis=0)

        def count_top(cand):
            return lax.fori_loop(0, DSA_TOP, lambda lvl, a: a + part_count(top_s[lvl], cand),
                                 jnp.zeros((DSA_ACC_ROWS, qb), F32))

        def count_all(cand):
            return lax.fori_loop(0, n_kc, lambda kc, a: a + part_count(key_s[kc], cand),
                                 jnp.zeros((DSA_ACC_ROWS, qb), F32))

        def settle(count_keys):
            thr = select(count_keys)
            thr_s[...] = thr
            above = jnp.sum(count_keys(thr + 1), axis=0, keepdims=True)
            upto = jnp.sum(count_keys(thr), axis=0, keepdims=True)
            need_s[...] = float(n_sel) - above
            over_s[...] = upto - float(n_sel)
            return thr

        thr = settle(count_top)
        hidden = jnp.where(top_s[DSA_TOP - 1] >= thr, 1.0, 0.0)

        @pl.when(jnp.max(hidden) > 0.0)
        def _():
            settle(count_all)

        @pl.when(jnp.max(over_s[...]) > 0.0)
        def _():
            thr = thr_s[...]
            need = need_s[...]
            sub = LANES
            earlier = jnp.where(lax.broadcasted_iota(jnp.int32, (sub, sub), 1)
                                < lax.broadcasted_iota(jnp.int32, (sub, sub), 0), 1.0, 0.0).astype(BF16)

            def drop_surplus(kc, seen):
                keys = key_s[kc]
                tie = keys == thr
                tie_f = jnp.where(tie, 1.0, 0.0)
                ranks = []
                for r in range(kb // sub):
                    part = tie_f[r * sub:(r + 1) * sub]
                    ranks.append(seen + jnp.dot(earlier, part.astype(BF16),
                                                preferred_element_type=F32))
                    seen = seen + jnp.sum(part, axis=0, keepdims=True)
                rank = jnp.concatenate(ranks, axis=0)
                key_s[kc] = jnp.where(tie & (rank >= need), thr - 1, keys)
                return seen

            lax.fori_loop(0, n_kc, drop_surplus, jnp.zeros((1, qb), F32))

        m_s[...] = jnp.full_like(m_s, M_INIT)
        acc_s[...] = jnp.zeros_like(acc_s)
        bdq_s[...] = jnp.zeros_like(bdq_s)
        for h in range(N_HEADS):
            g, hh = divmod(h, DSA_HEAD_GROUP)
            bdq_s[g, hh * HEAD_DIM:(hh + 1) * HEAD_DIM, hh * qb:(hh + 1) * qb] = (
                qt_ref[h].astype(F32) * (HEAD_DIM ** -0.5 * LOG2_E)).astype(BF16)

    mask = key_s[kj] >= thr_s[...]
    hg = DSA_HEAD_GROUP
    st_all = [jnp.dot(k_ref[:, g * hg * HEAD_DIM:(g + 1) * hg * HEAD_DIM], bdq_s[g],
                      preferred_element_type=F32) for g in range(N_HEADS // hg)]
    for g in range(N_HEADS // hg):
        st_g = st_all[g]
        for hh in range(hg):
            h = g * hg + hh
            st = jnp.where(mask, st_g[:, hh * qb:(hh + 1) * qb], NEG_BIG)
            m_old = m_s[h]
            part = jnp.max(st.reshape(kb // DSA_ACC_ROWS, DSA_ACC_ROWS, qb), axis=0)
            m_new = jnp.maximum(m_old, jnp.max(part, axis=0, keepdims=True))
            p = jnp.exp2((st - m_new).astype(BF16))
            acc_s[h] = jnp.exp2(m_old - m_new) * acc_s[h] + jnp.dot(
                vt_ref[h], p, preferred_element_type=F32)
            m_s[h] = m_new

    @pl.when(kj == j_last)
    def _():
        for h in range(N_HEADS):
            acc = acc_s[h]
            o_ref[h] = acc[:HEAD_DIM] / acc[HEAD_DIM:HEAD_DIM + 1]


def _dsa(qt, k, vt, iqt, ik, iwt, n_sel):
    t = ik.shape[0]
    qb, kb = DSA_QB, DSA_KB
    assert n_sel <= DSA_GROUPS and kb % DSA_GROUPS == 0
    nq, nk = t // qb, t // kb
    pairs = [(i, j) for i in range(nq) for j in range((i * qb + qb - 1) // kb + 1)]
    qi = jnp.asarray(np.array([p[0] for p in pairs], np.int32))
    kj = jnp.asarray(np.array([p[1] for p in pairs], np.int32))
    grid_spec = pltpu.PrefetchScalarGridSpec(
        num_scalar_prefetch=2,
        grid=(len(pairs),),
        in_specs=[pl.BlockSpec((N_HEADS, HEAD_DIM, qb), lambda s, qi, kj: (0, 0, qi[s])),
                  pl.BlockSpec((kb, D_GROUP), lambda s, qi, kj: (kj[s], 0)),
                  pl.BlockSpec((N_HEADS, DSA_V_ROWS, kb), lambda s, qi, kj: (0, 0, kj[s])),
                  pl.BlockSpec((D_IDX, qb), lambda s, qi, kj: (0, qi[s])),
                  pl.BlockSpec((t, IDX_DIM), lambda s, qi, kj: (0, 0)),
                  pl.BlockSpec((N_IDX_HEADS, qb), lambda s, qi, kj: (0, qi[s]))],
        out_specs=pl.BlockSpec((N_HEADS, HEAD_DIM, qb), lambda s, qi, kj: (0, 0, qi[s])),
        scratch_shapes=[pltpu.VMEM((nk, kb, qb), jnp.int32),
                        pltpu.VMEM((1, qb), jnp.int32),
                        pltpu.VMEM((1, qb), F32),
                        pltpu.VMEM((1, qb), F32),
                        pltpu.VMEM((N_HEADS, 1, qb), F32),
                        pltpu.VMEM((N_HEADS, DSA_V_ROWS, qb), F32),
                        pltpu.VMEM((N_HEADS // DSA_HEAD_GROUP, DSA_HEAD_GROUP * HEAD_DIM,
                                    DSA_HEAD_GROUP * qb), BF16),
                        pltpu.VMEM((DSA_TOP, DSA_GROUPS, qb), jnp.int32),
                        pltpu.VMEM((DSA_TOP, DSA_GROUPS, qb), F32)])
    return pl.pallas_call(
        functools.partial(_dsa_kernel, n_sel=n_sel),
        grid_spec=grid_spec,
        out_shape=jax.ShapeDtypeStruct((N_HEADS, HEAD_DIM, t), F32),
        compiler_params=pltpu.CompilerParams(dimension_semantics=("arbitrary",),
                                             vmem_limit_bytes=VMEM_LIMIT),
        name="dsa",
    )(qi, kj, qt, k, vt, iqt, ik, iwt)


def _out_proj_kernel(x_ref, att_ref, rw_ref, wo_ref, n2w_ref, rwt_ref, rb_ref,
                     h_ref, xn_ref, gate_ref):
    tm = x_ref.shape[0]
    att = att_ref[...].reshape(D_GROUP, tm).T
    mix = jnp.concatenate([att, rw_ref[...]], axis=1)
    acc = x_ref[...] + _dot(mix, wo_ref[...])
    h_ref[...] = acc
    xn = acc * lax.rsqrt(jnp.mean(acc * acc, axis=-1, keepdims=True) + RMS_EPS) * n2w_ref[...]
    xn_ref[...] = xn.astype(BF16)
    logits = _dot3(xn, rwt_ref[...]) + rb_ref[...]
    lane = lax.broadcasted_iota(jnp.int32, logits.shape, 1)
    work = logits
    vals, hots = [], []
    for _ in range(TOP_K_EXPERTS):
        m = jnp.max(work, axis=1, keepdims=True)
        idx = jnp.min(jnp.where(work == m, lane, LANES), axis=1, keepdims=True)
        hot = lane == idx
        vals.append(m)
        hots.append(hot)
        work = jnp.where(hot, -jnp.inf, work)
    es = [jnp.exp(vv - vals[0]) for vv in vals]
    denom = es[0] + es[1] + es[2] + es[3]
    gates = jnp.zeros_like(logits)
    for e, hot in zip(es, hots):
        gates = gates + jnp.where(hot, e / denom, 0.0)
    gate_ref[...] = gates


def _out_proj(x2, att_t, rw, wo, n2w, rwt, rb, tm):
    t = x2.shape[0]
    full = lambda shape: pl.BlockSpec(shape, lambda i: (0,) * len(shape))
    row = lambda width: pl.BlockSpec((tm, width), lambda i: (i, 0))
    return pl.pallas_call(
        _out_proj_kernel,
        grid=(t // tm,),
        in_specs=[row(D_MODEL), pl.BlockSpec((N_HEADS, HEAD_DIM, tm), lambda i: (0, 0, i)),
                  row(D_GROUP), full((2 * D_GROUP, D_MODEL)), full((1, D_MODEL)),
                  full((D_MODEL, LANES)), full((1, LANES))],
        out_specs=[row(D_MODEL), row(D_MODEL), row(LANES)],
        out_shape=(jax.ShapeDtypeStruct((t, D_MODEL), F32),
                   jax.ShapeDtypeStruct((t, D_MODEL), BF16),
                   jax.ShapeDtypeStruct((t, LANES), F32)),
        compiler_params=pltpu.CompilerParams(dimension_semantics=("arbitrary",),
                                             vmem_limit_bytes=VMEM_LIMIT),
        name="out_proj",
    )(x2, att_t, rw, wo, n2w, rwt, rb)


MOE_TM = 1024
MOE_RB = 144
MOE_FC = 512
MOE_EXPERTS_PER_STEP = 2


def _route_kernel(gate_ref, rankt_ref, gatet_ref, cnt_ref):
    tm = gate_ref.shape[0]
    gates = gate_ref[...]
    hot = gates > 0.0
    ind = jnp.where(hot, 1.0, 0.0)
    before = (lax.broadcasted_iota(jnp.int32, (tm, tm), 1)
              < lax.broadcasted_iota(jnp.int32, (tm, tm), 0))
    rank = jnp.dot(jnp.where(before, 1.0, 0.0).astype(BF16), ind.astype(BF16),
                   preferred_element_type=F32)
    rankt_ref[...] = jnp.where(hot, rank, -1.0).T
    gatet_ref[...] = gates.T
    cnt = jnp.sum(ind, axis=0, keepdims=True)
    cnt_ref[...] = jnp.broadcast_to(cnt[None], cnt_ref.shape)


def _route(gates, tm):
    t = gates.shape[0]
    return pl.pallas_call(
        _route_kernel,
        grid=(t // tm,),
        in_specs=[pl.BlockSpec((tm, LANES), lambda i: (i, 0))],
        out_specs=[pl.BlockSpec((LANES, tm), lambda i: (0, i)),
                   pl.BlockSpec((LANES, tm), lambda i: (0, i)),
                   pl.BlockSpec((1, SUBLANES, LANES), lambda i: (i, 0, 0))],
        out_shape=(jax.ShapeDtypeStruct((LANES, t), F32),
                   jax.ShapeDtypeStruct((LANES, t), F32),
                   jax.ShapeDtypeStruct((t // tm, SUBLANES, LANES), F32)),
        compiler_params=pltpu.CompilerParams(dimension_semantics=("arbitrary",),
                                             vmem_limit_bytes=VMEM_LIMIT),
        name="route",
    )(gates)


def _moe_kernel(cnt_ref, h_ref, xn_ref, rankt_ref, gatet_ref, wgu_ref, bgu_ref,
                wdn_ref, bdn_ref, o_ref):
    i = pl.program_id(0)
    tm = h_ref.shape[0]

    @pl.when(pl.program_id(1) == 0)
    def _():
        o_ref[...] = h_ref[...]

    for sub in range(MOE_EXPERTS_PER_STEP):
        _moe_expert(pl.program_id(1) * MOE_EXPERTS_PER_STEP + sub, sub, i, tm, cnt_ref, xn_ref,
                    rankt_ref, gatet_ref, wgu_ref, bgu_ref, wdn_ref, bdn_ref, o_ref)


def _moe_expert(e, slot, i, tm, cnt_ref, xn_ref, rankt_ref, gatet_ref, wgu_ref, bgu_ref,
                wdn_ref, bdn_ref, o_ref):
    n_rows = cnt_ref[i * N_EXPERTS + e]
    n_blocks = (n_rows + MOE_RB - 1) // MOE_RB
    rank_row = rankt_ref[pl.ds(e, 1), :]
    gate_row = gatet_ref[pl.ds(e, 1), :]

    def row_block(bi, carry):
        r0 = (bi * MOE_RB).astype(F32)
        rowid = lax.broadcasted_iota(jnp.int32, (MOE_RB, tm), 0).astype(F32) + r0
        hit = rank_row == rowid
        sel = jnp.where(hit, 1.0, 0.0).astype(BF16)
        gate = jnp.sum(jnp.where(hit, gate_row, 0.0), axis=1, keepdims=True)
        xb = jnp.dot(sel, xn_ref[...], preferred_element_type=F32).astype(BF16)
        y = jnp.zeros((MOE_RB, D_MODEL), F32)
        ups = []
        for fc in range(D_FF // MOE_FC):
            c0 = fc * MOE_FC
            ups.append((jnp.dot(xb, wgu_ref[slot, :, c0:c0 + MOE_FC], preferred_element_type=F32),
                        jnp.dot(xb, wgu_ref[slot, :, D_FF + c0:D_FF + c0 + MOE_FC],
                                preferred_element_type=F32)))
        for fc in range(D_FF // MOE_FC):
            c0 = fc * MOE_FC
            hg = ups[fc][0] + bgu_ref[slot, :, c0:c0 + MOE_FC]
            hl = ups[fc][1] + bgu_ref[slot, :, D_FF + c0:D_FF + c0 + MOE_FC]
            glu = jnp.minimum(hg, SWIGLU_LIMIT)
            lin = jnp.clip(hl, -SWIGLU_LIMIT, SWIGLU_LIMIT)
            act = (lin + 1.0) * glu * jax.nn.sigmoid(SWIGLU_ALPHA * glu)
            y = y + jnp.dot(act.astype(BF16), wdn_ref[slot, c0:c0 + MOE_FC, :],
                            preferred_element_type=F32)
        y = ((y + bdn_ref[slot]) * gate).astype(BF16)
        o_ref[...] += _dot_tn(sel, y)
        return carry

    lax.fori_loop(0, n_blocks, row_block, 0)


def _moe(counts, h1, xn2, rank_t, gate_t, wgu, bgu, wdn, bdn, tm):
    t = h1.shape[0]
    eps = MOE_EXPERTS_PER_STEP
    row = lambda width: pl.BlockSpec((tm, width), lambda i, e, c: (i, 0))
    grid_spec = pltpu.PrefetchScalarGridSpec(
        num_scalar_prefetch=1,
        grid=(t // tm, N_EXPERTS // eps),
        in_specs=[row(D_MODEL), row(D_MODEL),
                  pl.BlockSpec((LANES, tm), lambda i, e, c: (0, i)),
                  pl.BlockSpec((LANES, tm), lambda i, e, c: (0, i)),
                  pl.BlockSpec((eps, D_MODEL, 2 * D_FF), lambda i, e, c: (e, 0, 0)),
                  pl.BlockSpec((eps, 1, 2 * D_FF), lambda i, e, c: (e, 0, 0)),
                  pl.BlockSpec((eps, D_FF, D_MODEL), lambda i, e, c: (e, 0, 0)),
                  pl.BlockSpec((eps, 1, D_MODEL), lambda i, e, c: (e, 0, 0))],
        out_specs=row(D_MODEL))
    return pl.pallas_call(
        _moe_kernel,
        grid_spec=grid_spec,
        out_shape=jax.ShapeDtypeStruct((t, D_MODEL), F32),
        compiler_params=pltpu.CompilerParams(dimension_semantics=("arbitrary", "arbitrary"),
                                             vmem_limit_bytes=VMEM_LIMIT),
        name="moe",
    )(counts, h1, xn2, rank_t, gate_t, wgu, bgu, wdn, bdn)


def _rope_tables():
    lane = jnp.arange(LANES)

    def rows(period, rot):
        half = rot // 2
        inv_freq = ROPE_THETA ** (-jnp.arange(half, dtype=F32) / half)
        jm = lane % period
        freq = jnp.where(jm < rot, inv_freq[jm % half], 0.0)
        sign = jnp.where(jm < half, -1.0, jnp.where(jm < rot, 1.0, 0.0))
        first = (jm < half).astype(F32)
        return [freq, sign, first]

    idx_rows = rows(IDX_DIM, IDX_DIM // 4)
    only_key = (lane < IDX_DIM).astype(F32)
    return jnp.stack(rows(HEAD_DIM, HEAD_DIM // 4) + idx_rows
                     + [only_key, jnp.zeros_like(only_key)]).astype(F32)


def _group_matrix():
    g = np.arange(D_GROUP) // HEAD_DIM
    return jnp.asarray((g[:, None] == g[None, :]).astype(np.float32), dtype=BF16)


def _pad_rows(w, r0, rows):
    return jnp.zeros((rows, w.shape[1]), w.dtype).at[r0:r0 + w.shape[0]].set(w)


def kernel(x, positions, norm1_w, w_in, q_norm_w, k_norm_w, rwkv_mu, rwkv_w0, rwkv_w2, rwkv_a0,
           rwkv_a2, rwkv_g2, rwkv_k_k, rwkv_k_a, rwkv_r_k, rwkv_ln_w, rwkv_ln_b, w_out, norm2_w,
           router_w, router_b, exp_w_gu, exp_b_gu, exp_w_down, exp_b_down):
    b, t, _ = x.shape
    assert b == 1 and w_in.shape[0] == 1, "single sequence, single layer"
    assert t % DSA_KB == 0 and t % DSA_QB == 0 and t % MOE_TM == 0
    x2 = x[0]
    pos_f = positions[0].astype(F32)[:, None]
    n_sel = min(TOPK_MAX, t // 4)

    w = w_in[0]
    a0 = 3 * D_GROUP
    att_cols = a0 + D_IDX + IDX_DIM + N_IDX_HEADS
    w_att, w_rw = w[:, :att_cols], w[:, att_cols:]
    w_sm = jnp.zeros((D_MODEL, LANES), F32).at[:, :IDX_DIM + N_IDX_HEADS].set(w_att[:, a0 + D_IDX:])
    w_packed = jnp.concatenate(
        [w_att[:, :a0], w_rw[:, :a0], w_att[:, a0:a0 + D_IDX], w_sm, w_rw[:, a0:]],
        axis=1).astype(BF16)
    mu = rwkv_mu[0][None, :]
    tile8 = lambda z: jnp.tile(z, N_HEADS)[None, :]
    gsum = _group_matrix()

    tm = 256
    q, k, v, rr, rk, rv, iq, sm, lora = _in_proj(
        x2, pos_f, norm1_w, w_packed, mu, tile8(q_norm_w[0]), tile8(k_norm_w[0]), gsum,
        _rope_tables(), tm)

    vec = lambda z: z.reshape(1, D_GROUP)
    rw, (w_gu_bf, w_dn_bf) = _rwkv(
        rr, rk, rv, lora, vec(rwkv_w0[0]), _pad_rows(rwkv_w2[0], 0, LANES), vec(rwkv_a0[0]),
        _pad_rows(rwkv_a2[0], D_DECAY_LORA, LANES),
        _pad_rows(rwkv_g2[0], D_DECAY_LORA + D_AAA_LORA, LANES),
        vec(rwkv_k_k[0]), vec(rwkv_k_a[0]), vec(rwkv_r_k[0]), vec(rwkv_ln_w[0]),
        vec(rwkv_ln_b[0]), gsum, tm, to_cast=(exp_w_gu[0], exp_w_down[0]))

    heads = lambda z: z.reshape(t, N_HEADS, HEAD_DIM)
    vt = jnp.concatenate([heads(v).transpose(1, 2, 0),
                          jnp.ones((N_HEADS, DSA_V_ROWS - HEAD_DIM, t), BF16)], axis=1)
    att_t = _dsa(heads(q).transpose(1, 2, 0), k, vt, iq.T, sm[:, :IDX_DIM].astype(BF16),
                 sm[:, IDX_DIM:IDX_DIM + N_IDX_HEADS].T, n_sel)

    rwt = jnp.zeros((D_MODEL, LANES), F32).at[:, :N_EXPERTS].set(router_w[0])
    rb = jnp.full((1, LANES), NEG_BIG, F32).at[0, :N_EXPERTS].set(router_b[0])
    h1, xn2, gates = _out_proj(x2, att_t, rw, w_out[0].astype(BF16), norm2_w, rwt, rb, tm)

    rank_t, gate_t, cnt = _route(gates, MOE_TM)
    counts = cnt[:, 0, :N_EXPERTS].astype(jnp.int32).reshape(-1)
    out = _moe(counts, h1, xn2, rank_t, gate_t, w_gu_bf, exp_b_gu[0][:, None, :], w_dn_bf,
               exp_b_down[0][:, None, :], MOE_TM)
    return out[None]
```

```python
import functools

import jax
import jax.numpy as jnp
import numpy as np
from jax import lax
from jax.experimental import pallas as pl
from jax.experimental.pallas import tpu as pltpu

F32 = jnp.float32
BF16 = jnp.bfloat16
HIGHEST = lax.Precision.HIGHEST

D_MODEL = 1024
HEAD_DIM = 64
N_HEADS = 8
D_GROUP = N_HEADS * HEAD_DIM
ROPE_THETA = 500000.0
N_IDX_HEADS = 8
IDX_DIM = 32
D_IDX = N_IDX_HEADS * IDX_DIM
TOPK_MAX = 256
D_DECAY_LORA = 32
D_AAA_LORA = 32
RWKV_GN_EPS = 64e-5
N_EXPERTS = 32
TOP_K_EXPERTS = 4
D_FF = 1024
SWIGLU_LIMIT = 7.0
SWIGLU_ALPHA = 1.702
RMS_EPS = 1e-6

LANES = 128
SUBLANES = 8
VMEM_LIMIT = 56 * 1024 * 1024

NEG_BIG = -1e30
LOG2_E = 1.4426950408889634
INT_MIN = -(2 ** 31)


def _dot(a, b):
    return jnp.dot(a.astype(BF16), b.astype(BF16), preferred_element_type=F32)


def _dotf(a, b):
    return jnp.dot(a, b, preferred_element_type=F32, precision=HIGHEST)


def _dot3(a, b):
    a_hi = a.astype(BF16)
    b_hi = b.astype(BF16)
    a_lo = (a - a_hi.astype(F32)).astype(BF16)
    b_lo = (b - b_hi.astype(F32)).astype(BF16)
    dot = functools.partial(jnp.dot, preferred_element_type=F32)
    return dot(a_hi, b_hi) + dot(a_hi, b_lo) + dot(a_lo, b_hi)


def _dot_nt(a, b):
    return lax.dot_general(a.astype(BF16), b.astype(BF16), (((1,), (1,)), ((), ())),
                           preferred_element_type=F32)


def _dot_tn(a, b):
    return lax.dot_general(a.astype(BF16), b.astype(BF16), (((0,), (0,)), ((), ())),
                           preferred_element_type=F32)


def _group_sum(z, g_ref):
    hi = z.astype(BF16)
    lo = (z - hi.astype(F32)).astype(BF16)
    g = g_ref[...]
    return (jnp.dot(hi, g, preferred_element_type=F32)
            + jnp.dot(lo, g, preferred_element_type=F32))


_C_Q, _C_K, _C_V, _C_RR, _C_RK, _C_RV = (i * D_GROUP for i in range(6))
_C_IQ = 6 * D_GROUP
_C_SM = _C_IQ + D_IDX
_C_LORA = _C_SM + LANES
D_IN_PACKED = _C_LORA + LANES
D_SHIFT = 3 * D_GROUP + LANES


def _in_proj_kernel(x_ref, pos_ref, n1w_ref, w_ref, mu_ref, qnw_ref, knw_ref, g_ref, rope_ref,
                    q_ref, k_ref, v_ref, rr_ref, rk_ref, rv_ref, iq_ref, sm_ref, lora_ref,
                    carry_ref):
    tm = x_ref.shape[0]

    @pl.when(pl.program_id(0) == 0)
    def _():
        carry_ref[...] = jnp.zeros_like(carry_ref)

    x = x_ref[...]
    xn = x * lax.rsqrt(jnp.mean(x * x, axis=-1, keepdims=True) + RMS_EPS) * n1w_ref[...]
    xb = xn.astype(BF16)
    pos = pos_ref[...]
    rope = rope_ref[...]

    def tables(frow, srow):
        ang = pos * rope[frow:frow + 1, :]
        return jnp.cos(ang), jnp.sin(ang) * rope[srow:srow + 1, :]

    def widen(z, reps):
        return jnp.concatenate([z] * reps, axis=1)

    def rotary(z, c, s, first_row, half):
        w = z.shape[1]
        first = jnp.concatenate([rope[first_row:first_row + 1, :]] * (w // LANES), axis=1) > 0.5
        partner = jnp.where(first, pltpu.roll(z, w - half, 1), pltpu.roll(z, half, 1))
        return z * c + partner * s

    def head_norm(z, w_row):
        ms = _group_sum(z * z, g_ref) * (1.0 / HEAD_DIM)
        return z * lax.rsqrt(ms + RMS_EPS) * w_row

    def proj(c0, width):
        return jnp.dot(xb, w_ref[:, c0:c0 + width], preferred_element_type=F32)

    def shift(z, c0):
        width = z.shape[1]
        row = lax.broadcasted_iota(jnp.int32, z.shape, 0)
        prev = jnp.where(row == 0, carry_ref[0:1, c0:c0 + width], pltpu.roll(z, 1, 0))
        carry_ref[0:1, c0:c0 + width] = z[tm - 1:tm, :]
        return z + (prev - z) * mu_ref[:, c0:c0 + width]

    cq, sq = (widen(z, D_GROUP // LANES) for z in tables(0, 1))
    q = rotary(head_norm(proj(_C_Q, D_GROUP), qnw_ref[...]), cq, sq, 2, HEAD_DIM // 8)
    q_ref[...] = q.astype(BF16)
    k = rotary(head_norm(proj(_C_K, D_GROUP), knw_ref[...]), cq, sq, 2, HEAD_DIM // 8)
    k_ref[...] = k.astype(BF16)
    v_ref[...] = proj(_C_V, D_GROUP).astype(BF16)

    rr_ref[...] = shift(proj(_C_RR, D_GROUP), 0)
    rk_ref[...] = shift(proj(_C_RK, D_GROUP), D_GROUP)
    rv_ref[...] = shift(proj(_C_RV, D_GROUP), 2 * D_GROUP)
    lora_ref[...] = shift(proj(_C_LORA, LANES), 3 * D_GROUP)

    ci, si = tables(3, 4)
    iq_ref[...] = rotary(proj(_C_IQ, D_IDX), widen(ci, D_IDX // LANES), widen(si, D_IDX // LANES),
                         5, IDX_DIM // 8)
    only_key = rope[6:7, :]
    sm_ref[...] = rotary(proj(_C_SM, LANES), ci * only_key + (1.0 - only_key), si * only_key,
                         5, IDX_DIM // 8)


def _in_proj(x2, pos_f, n1w, w_packed, mu_packed, qnw, knw, gmat, rope, tm):
    t = x2.shape[0]
    full = lambda shape: pl.BlockSpec(shape, lambda i: (0,) * len(shape))
    row = lambda width: pl.BlockSpec((tm, width), lambda i: (i, 0))
    out_shapes = (
        jax.ShapeDtypeStruct((t, D_GROUP), BF16),
        jax.ShapeDtypeStruct((t, D_GROUP), BF16),
        jax.ShapeDtypeStruct((t, D_GROUP), BF16),
        jax.ShapeDtypeStruct((t, D_GROUP), F32),
        jax.ShapeDtypeStruct((t, D_GROUP), F32),
        jax.ShapeDtypeStruct((t, D_GROUP), F32),
        jax.ShapeDtypeStruct((t, D_IDX), F32),
        jax.ShapeDtypeStruct((t, LANES), F32),
        jax.ShapeDtypeStruct((t, LANES), F32),
    )
    return pl.pallas_call(
        _in_proj_kernel,
        grid=(t // tm,),
        in_specs=[row(D_MODEL), row(1), full((1, D_MODEL)), full((D_MODEL, D_IN_PACKED)),
                  full((1, D_SHIFT)), full((1, D_GROUP)), full((1, D_GROUP)),
                  full((D_GROUP, D_GROUP)), full((SUBLANES, LANES))],
        out_specs=[row(D_GROUP)] * 6 + [row(D_IDX), row(LANES), row(LANES)],
        out_shape=out_shapes,
        scratch_shapes=[pltpu.VMEM((SUBLANES, D_SHIFT), F32)],
        compiler_params=pltpu.CompilerParams(dimension_semantics=("arbitrary",),
                                             vmem_limit_bytes=VMEM_LIMIT),
        name="in_proj",
    )(x2, pos_f, n1w, w_packed, mu_packed, qnw, knw, gmat, rope)


RWKV_CHUNK = 64
RWKV_UNROLL = 4
PAIR = 2 * HEAD_DIM


def _rwkv_kernel(r_ref, k_ref, v_ref, lora_ref, w0_ref, w2_ref, a0_ref, a2_ref, g2_ref,
                 kk_ref, ka_ref, rk_ref, lnw_ref, lnb_ref, g_ref, *rest, n_cast):
    cast_in, o_ref, cast_out = rest[:n_cast], rest[n_cast], rest[n_cast + 1:2 * n_cast + 1]
    s_ref, ld_s, r_s, k2_s, b_s, kk_s, y_s = rest[2 * n_cast + 1:]
    for src, dst in zip(cast_in, cast_out):
        dst[...] = src[...].astype(BF16)
    tm = r_ref.shape[0]
    n_chunks = tm // RWKV_CHUNK
    n_pairs = D_GROUP // PAIR
    c = RWKV_CHUNK

    @pl.when(pl.program_id(0) == 0)
    def _():
        s_ref[...] = jnp.zeros_like(s_ref)

    lora = lora_ref[...]
    r = r_ref[...]
    k = k_ref[...]
    v = v_ref[...]
    zarg = w0_ref[...] + _dot3(jnp.tanh(lora), w2_ref[...])
    sp = jnp.maximum(-zarg, 0.0) + jnp.log1p(jnp.exp(-jnp.abs(zarg)))
    ld_s[...] = -jnp.exp(-sp - 0.5)
    a = jax.nn.sigmoid(a0_ref[...] + _dot3(lora, a2_ref[...]))
    g = _dot3(jax.nn.sigmoid(lora), g2_ref[...])
    kk = k * kk_ref[...]
    kk = kk * lax.rsqrt(jnp.maximum(_group_sum(kk * kk, g_ref), 1e-24))
    k2 = k * (1.0 + (a - 1.0) * ka_ref[...])
    bonus = _group_sum(r * k2 * rk_ref[...], g_ref) * v
    r_s[...] = r
    k2_s[...] = k2
    kk_s[...] = kk
    b_s[...] = kk * a

    row = lax.broadcasted_iota(jnp.int32, (2 * c, 2 * c), 0)
    col = lax.broadcasted_iota(jnp.int32, (2 * c, 2 * c), 1)
    same_head = (row >= c) == (col >= c)
    strict = same_head & (col < row)
    incl = same_head & (col <= row)
    eye = (row == col).astype(F32)
    tri = (lax.broadcasted_iota(jnp.int32, (c, c), 1)
           <= lax.broadcasted_iota(jnp.int32, (c, c), 0)).astype(F32)
    lane = lax.broadcasted_iota(jnp.int32, (c, PAIR), 1)
    head0 = lane < HEAD_DIM

    def stack(z):
        return jnp.concatenate([jnp.where(head0, z, 0.0), jnp.where(head0, 0.0, z)], axis=0)

    def chunk_body(ci, carry):
        c2 = 2 * c
        units = [(cc, p) for cc in range(RWKV_UNROLL) for p in range(n_pairs)]
        uid = range(len(units))
        rows = [pl.multiple_of((ci * RWKV_UNROLL + cc) * c, c) for cc in range(RWKV_UNROLL)]
        ld_all = [ld_s[pl.ds(r0, c), :] for r0 in rows]
        cum_all = [_dotf(tri, ld) for ld in ld_all]
        rt, kt, bt, kp, vs, lhs, rhs, gam_end = [], [], [], [], [], [], [], []
        for cc, p in units:
            r0 = rows[cc]
            cols = slice(p * PAIR, (p + 1) * PAIR)
            ld = ld_all[cc][:, cols]
            cum = cum_all[cc][:, cols]
            gam = jnp.exp(cum)
            inv = jnp.exp(-cum)
            gam_prev = jnp.exp(cum - ld)
            gam_end.append(gam[c - 1:c, :])
            rt.append(stack(r_s[pl.ds(r0, c), cols] * gam))
            kt_raw = k2_s[pl.ds(r0, c), cols] * inv
            bt_raw = b_s[pl.ds(r0, c), cols] * inv
            kt.append(stack(kt_raw))
            bt.append(stack(bt_raw))
            kp.append(stack(kk_s[pl.ds(r0, c), cols] * gam_prev))
            vs.append(stack(v_ref[pl.ds(r0, c), cols]))
            lhs.append(jnp.concatenate([kp[-1], rt[-1]], axis=0))
            rhs.append(jnp.concatenate([bt_raw, bt_raw, kt_raw, kt_raw], axis=0))
        aa = [_dot_nt(lhs[u], rhs[u]) for u in uid]
        a_kk = [jnp.where(strict, aa[u][:c2, c2:], 0.0) for u in uid]
        a_rb = [jnp.where(incl, aa[u][c2:, :c2], 0.0) for u in uid]
        a_rk = [jnp.where(incl, aa[u][c2:, c2:], 0.0) for u in uid]
        n = [-jnp.where(strict, aa[u][:c2, :c2], 0.0) for u in uid]
        prod = [eye + n[u] for u in uid]
        n = [_dot(n[u], n[u]) for u in uid]
        for _ in range(int(np.log2(c)) - 2):
            both = [_dot(jnp.concatenate([n[u], prod[u]], axis=0), n[u]) for u in uid]
            prod = [prod[u] + both[u][c2:] for u in uid]
            n = [both[u][:c2] for u in uid]
        av = [_dot(jnp.concatenate([a_kk[u], a_rk[u]], axis=0), vs[u]) for u in uid]
        tinv = [prod[u] + _dot(prod[u], n[u]) for u in uid]
        wu = [_dot(tinv[u], jnp.concatenate([kp[u], av[u][:c2]], axis=1)) for u in uid]
        rb = [_dot(a_rb[u], wu[u]) for u in uid]
        tn = [_dot_tn(jnp.concatenate([wu[u], vs[u]], axis=1),
                      jnp.concatenate([bt[u], kt[u]], axis=1)) for u in uid]
        r2 = [rt[u] - rb[u][:, :PAIR] for u in uid]
        y2 = [av[u][c2:] - rb[u][:, PAIR:] for u in uid]
        s_mix = [(eye - tn[u][:PAIR, :PAIR]) * gam_end[u] for u in uid]
        s_add = [(tn[u][2 * PAIR:, PAIR:] - tn[u][PAIR:2 * PAIR, :PAIR]) * gam_end[u] for u in uid]
        state = [s_ref[p] for p in range(n_pairs)]
        for cc in range(RWKV_UNROLL):
            us = [cc * n_pairs + p for p in range(n_pairs)]
            ys = [_dot_nt(r2[u], state[p]) + y2[u] for p, u in enumerate(us)]
            state = [_dot(state[p], s_mix[u]) + s_add[u] for p, u in enumerate(us)]
            for p in range(n_pairs):
                y_s[pl.ds(rows[cc], c), p * PAIR:(p + 1) * PAIR] = ys[p][:c] + ys[p][c:]
        for p in range(n_pairs):
            s_ref[p] = state[p]
        return carry

    lax.fori_loop(0, n_chunks // RWKV_UNROLL, chunk_body, 0)

    y = y_s[...]
    mean = _group_sum(y, g_ref) * (1.0 / HEAD_DIM)
    yc = y - mean
    var = _group_sum(yc * yc, g_ref) * (1.0 / HEAD_DIM)
    yn = yc * lax.rsqrt(var + RWKV_GN_EPS) * lnw_ref[...] + lnb_ref[...]
    o_ref[...] = (yn + bonus) * g


def _rwkv(rr, rk, rv, lora, w0, w2p, a0, a2p, g2p, k_k, k_a, r_k, ln_w, ln_b, gsum, tm,
          to_cast=()):
    t = rr.shape[0]
    full = lambda shape: pl.BlockSpec(shape, lambda i: (0,) * len(shape))
    row = lambda width: pl.BlockSpec((tm, width), lambda i: (i, 0))
    vec = full((1, D_GROUP))
    big = pltpu.VMEM((tm, D_GROUP), F32)
    slabs = [w.reshape(t // tm, -1, w.shape[-1]) for w in to_cast]
    cast_specs = [pl.BlockSpec((1,) + w.shape[1:], lambda i: (i, 0, 0)) for w in slabs]
    outs = pl.pallas_call(
        functools.partial(_rwkv_kernel, n_cast=len(slabs)),
        grid=(t // tm,),
        in_specs=[row(D_GROUP), row(D_GROUP), row(D_GROUP), row(LANES),
                  vec, full((LANES, D_GROUP)), vec, full((LANES, D_GROUP)),
                  full((LANES, D_GROUP)), vec, vec, vec, vec, vec, full((D_GROUP, D_GROUP))]
        + cast_specs,
        out_specs=[row(D_GROUP)] + cast_specs,
        out_shape=[jax.ShapeDtypeStruct((t, D_GROUP), F32)]
        + [jax.ShapeDtypeStruct(w.shape, BF16) for w in slabs],
        scratch_shapes=[pltpu.VMEM((D_GROUP // PAIR, PAIR, PAIR), F32),
                        big, big, big, big, big, big],
        compiler_params=pltpu.CompilerParams(dimension_semantics=("arbitrary",),
                                             vmem_limit_bytes=VMEM_LIMIT),
        name="rwkv",
    )(rr, rk, rv, lora, w0, w2p, a0, a2p, g2p, k_k, k_a, r_k, ln_w, ln_b, gsum, *slabs)
    return outs[0], [o.reshape(w.shape) for o, w in zip(outs[1:], to_cast)]


DSA_QB = 256
DSA_KB = 512
M_INIT = -5e29
DSA_HEAD_GROUP = 4
DSA_ACC_ROWS = 32
DSA_GROUPS = 256
DSA_TOP = 10
DSA_V_ROWS = HEAD_DIM + 16


def _dsa_kernel(qi_ref, kj_ref, qt_ref, k_ref, vt_ref, iqt_ref, ik_ref, iwt_ref, o_ref,
                key_s, thr_s, need_s, over_s, m_s, acc_s, bdq_s, top_s, tops_s, *, n_sel):
    qb = qt_ref.shape[2]
    kb = k_ref.shape[0]
    step = pl.program_id(0)
    qi = qi_ref[step]
    kj = kj_ref[step]
    q0 = qi * qb
    j_last = (q0 + qb - 1) // kb
    n_kc = j_last + 1
    idx_scale = float((IDX_DIM * N_IDX_HEADS) ** -0.5)

    def causal(jblk):
        s_pos = jblk * kb + lax.broadcasted_iota(jnp.int32, (kb, qb), 0)
        t_pos = q0 + lax.broadcasted_iota(jnp.int32, (kb, qb), 1)
        return s_pos <= t_pos

    @pl.when(kj == 0)
    def _():
        iqt = iqt_ref[...].astype(BF16)
        iwt = iwt_ref[...] * idx_scale

        def score_chunk(kc, diagonal):
            ik = ik_ref[pl.ds(pl.multiple_of(kc * kb, kb), kb), :]
            score = jnp.zeros((kb, qb), F32)
            for h in range(N_IDX_HEADS):
                d = jnp.dot(ik, iqt[h * IDX_DIM:(h + 1) * IDX_DIM, :],
                            preferred_element_type=F32)
                score = score + jnp.maximum(d, 0.0) * iwt[h:h + 1, :]
            keys = to_key(score)
            if diagonal:
                ok = causal(kc)
                score = jnp.where(ok, score, -jnp.inf)
                keys = jnp.where(ok, keys, jnp.int32(INT_MIN))
            key_s[kc] = keys
            xs = [score[r * DSA_GROUPS:(r + 1) * DSA_GROUPS] for r in range(kb // DSA_GROUPS)]
            for lvl in range(DSA_TOP):
                s = tops_s[lvl]
                for r in range(len(xs)):
                    s, xs[r] = jnp.maximum(s, xs[r]), jnp.minimum(s, xs[r])
                tops_s[lvl] = s

        def to_key(score):
            bits = pltpu.bitcast(score, jnp.int32)
            return jnp.where(bits < 0, bits ^ jnp.int32(0x7FFFFFFF), bits)

        tops_s[...] = jnp.full_like(tops_s, -jnp.inf)

        def full_chunk(kc, carry):
            score_chunk(kc, False)
            return carry

        lax.fori_loop(0, j_last, full_chunk, 0)
        score_chunk(j_last, True)
        for lvl in range(DSA_TOP):
            kept = tops_s[lvl]
            top_s[lvl] = jnp.where(kept == -jnp.inf, jnp.int32(INT_MIN), to_key(kept))

        def select(count_keys):
            def bit_step(b, cur):
                bit = lax.shift_left(jnp.int32(1), jnp.int32(31) - b)
                cand = (cur | bit) ^ jnp.int32(INT_MIN)
                cnt = jnp.sum(count_keys(cand), axis=0, keepdims=True)
                return jnp.where(cnt >= float(n_sel), cur | bit, cur)

            cur = lax.fori_loop(0, 32, bit_step, jnp.zeros((1, qb), jnp.int32))
            return jnp.maximum(cur ^ jnp.int32(INT_MIN), jnp.int32(INT_MIN + 1))

        def part_count(keys, cand):
            ind = jnp.where(keys >= cand, 1.0, 0.0)
            return jnp.sum(ind.reshape(-1, DSA_ACC_ROWS, qb), axis=0)

        def count_top(cand):
            return lax.fori_loop(0, DSA_TOP, lambda lvl, a: a + part_count(top_s[lvl], cand),
                                 jnp.zeros((DSA_ACC_ROWS, qb), F32))

        def count_all(cand):
            return lax.fori_loop(0, n_kc, lambda kc, a: a + part_count(key_s[kc], cand),
                                 jnp.zeros((DSA_ACC_ROWS, qb), F32))

        def settle(count_keys):
            thr = select(count_keys)
            thr_s[...] = thr
            above = jnp.sum(count_keys(thr + 1), axis=0, keepdims=True)
            upto = jnp.sum(count_keys(thr), axis=0, keepdims=True)
            need_s[...] = float(n_sel) - above
            over_s[...] = upto - float(n_sel)
            return thr

        thr = settle(count_top)
        hidden = jnp.where(top_s[DSA_TOP - 1] >= thr, 1.0, 0.0)

        @pl.when(jnp.max(hidden) > 0.0)
        def _():
            settle(count_all)

        @pl.when(jnp.max(over_s[...]) > 0.0)
        def _():
            thr = thr_s[...]
            need = need_s[...]
            sub = LANES
            earlier = jnp.where(lax.broadcasted_iota(jnp.int32, (sub, sub), 1)
                                < lax.broadcasted_iota(jnp.int32, (sub, sub), 0), 1.0, 0.0).astype(BF16)

            def drop_surplus(kc, seen):
                keys = key_s[kc]
                tie = keys == thr
                tie_f = jnp.where(tie, 1.0, 0.0)
                ranks = []
                for r in range(kb // sub):
                    part = tie_f[r * sub:(r + 1) * sub]
                    ranks.append(seen + jnp.dot(earlier, part.astype(BF16),
                                                preferred_element_type=F32))
                    seen = seen + jnp.sum(part, axis=0, keepdims=True)
                rank = jnp.concatenate(ranks, axis=0)
                key_s[kc] = jnp.where(tie & (rank >= need), thr - 1, keys)
                return seen

            lax.fori_loop(0, n_kc, drop_surplus, jnp.zeros((1, qb), F32))

        m_s[...] = jnp.full_like(m_s, M_INIT)
        acc_s[...] = jnp.zeros_like(acc_s)
        bdq_s[...] = jnp.zeros_like(bdq_s)
        for h in range(N_HEADS):
            g, hh = divmod(h, DSA_HEAD_GROUP)
            bdq_s[g, hh * HEAD_DIM:(hh + 1) * HEAD_DIM, hh * qb:(hh + 1) * qb] = (
                qt_ref[h].astype(F32) * (HEAD_DIM ** -0.5 * LOG2_E)).astype(BF16)

    mask = key_s[kj] >= thr_s[...]
    hg = DSA_HEAD_GROUP
    st_all = [jnp.dot(k_ref[:, g * hg * HEAD_DIM:(g + 1) * hg * HEAD_DIM], bdq_s[g],
                      preferred_element_type=F32) for g in range(N_HEADS // hg)]
    for g in range(N_HEADS // hg):
        st_g = st_all[g]
        for hh in range(hg):
            h = g * hg + hh
            st = jnp.where(mask, st_g[:, hh * qb:(hh + 1) * qb], NEG_BIG)
            m_old = m_s[h]
            part = jnp.max(st.reshape(kb // DSA_ACC_ROWS, DSA_ACC_ROWS, qb), axis=0)
            m_new = jnp.maximum(m_old, jnp.max(part, axis=0, keepdims=True))
            p = jnp.exp2((st - m_new).astype(BF16))
            acc_s[h] = jnp.exp2(m_old - m_new) * acc_s[h] + jnp.dot(
                vt_ref[h], p, preferred_element_type=F32)
            m_s[h] = m_new

    @pl.when(kj == j_last)
    def _():
        for h in range(N_HEADS):
            acc = acc_s[h]
            o_ref[h] = acc[:HEAD_DIM] / acc[HEAD_DIM:HEAD_DIM + 1]


def _dsa(qt, k, vt, iqt, ik, iwt, n_sel):
    t = ik.shape[0]
    qb, kb = DSA_QB, DSA_KB
    assert n_sel <= DSA_GROUPS and kb % DSA_GROUPS == 0
    nq, nk = t // qb, t // kb
    pairs = [(i, j) for i in range(nq) for j in range((i * qb + qb - 1) // kb + 1)]
    qi = jnp.asarray(np.array([p[0] for p in pairs], np.int32))
    kj = jnp.asarray(np.array([p[1] for p in pairs], np.int32))
    grid_spec = pltpu.PrefetchScalarGridSpec(
        num_scalar_prefetch=2,
        grid=(len(pairs),),
        in_specs=[pl.BlockSpec((N_HEADS, HEAD_DIM, qb), lambda s, qi, kj: (0, 0, qi[s])),
                  pl.BlockSpec((kb, D_GROUP), lambda s, qi, kj: (kj[s], 0)),
                  pl.BlockSpec((N_HEADS, DSA_V_ROWS, kb), lambda s, qi, kj: (0, 0, kj[s])),
                  pl.BlockSpec((D_IDX, qb), lambda s, qi, kj: (0, qi[s])),
                  pl.BlockSpec((t, IDX_DIM), lambda s, qi, kj: (0, 0)),
                  pl.BlockSpec((N_IDX_HEADS, qb), lambda s, qi, kj: (0, qi[s]))],
        out_specs=pl.BlockSpec((N_HEADS, HEAD_DIM, qb), lambda s, qi, kj: (0, 0, qi[s])),
        scratch_shapes=[pltpu.VMEM((nk, kb, qb), jnp.int32),
                        pltpu.VMEM((1, qb), jnp.int32),
                        pltpu.VMEM((1, qb), F32),
                        pltpu.VMEM((1, qb), F32),
                        pltpu.VMEM((N_HEADS, 1, qb), F32),
                        pltpu.VMEM((N_HEADS, DSA_V_ROWS, qb), F32),
                        pltpu.VMEM((N_HEADS // DSA_HEAD_GROUP, DSA_HEAD_GROUP * HEAD_DIM,
                                    DSA_HEAD_GROUP * qb), BF16),
                        pltpu.VMEM((DSA_TOP, DSA_GROUPS, qb), jnp.int32),
                        pltpu.VMEM((DSA_TOP, DSA_GROUPS, qb), F32)])
    return pl.pallas_call(
        functools.partial(_dsa_kernel, n_sel=n_sel),
        grid_spec=grid_spec,
        out_shape=jax.ShapeDtypeStruct((N_HEADS, HEAD_DIM, t), F32),
        compiler_params=pltpu.CompilerParams(dimension_semantics=("arbitrary",),
                                             vmem_limit_bytes=VMEM_LIMIT),
        name="dsa",
    )(qi, kj, qt, k, vt, iqt, ik, iwt)


def _out_proj_kernel(x_ref, att_ref, rw_ref, wo_ref, n2w_ref, rwt_ref, rb_ref,
                     h_ref, xn_ref, gate_ref):
    tm = x_ref.shape[0]
    att = att_ref[...].reshape(D_GROUP, tm).T
    mix = jnp.concatenate([att, rw_ref[...]], axis=1)
    acc = x_ref[...] + _dot(mix, wo_ref[...])
    h_ref[...] = acc
    xn = acc * lax.rsqrt(jnp.mean(acc * acc, axis=-1, keepdims=True) + RMS_EPS) * n2w_ref[...]
    xn_ref[...] = xn.astype(BF16)
    logits = _dot3(xn, rwt_ref[...]) + rb_ref[...]
    lane = lax.broadcasted_iota(jnp.int32, logits.shape, 1)
    work = logits
    vals, hots = [], []
    for _ in range(TOP_K_EXPERTS):
        m = jnp.max(work, axis=1, keepdims=True)
        idx = jnp.min(jnp.where(work == m, lane, LANES), axis=1, keepdims=True)
        hot = lane == idx
        vals.append(m)
        hots.append(hot)
        work = jnp.where(hot, -jnp.inf, work)
    es = [jnp.exp(vv - vals[0]) for vv in vals]
    denom = es[0] + es[1] + es[2] + es[3]
    gates = jnp.zeros_like(logits)
    for e, hot in zip(es, hots):
        gates = gates + jnp.where(hot, e / denom, 0.0)
    gate_ref[...] = gates


def _out_proj(x2, att_t, rw, wo, n2w, rwt, rb, tm):
    t = x2.shape[0]
    full = lambda shape: pl.BlockSpec(shape, lambda i: (0,) * len(shape))
    row = lambda width: pl.BlockSpec((tm, width), lambda i: (i, 0))
    return pl.pallas_call(
        _out_proj_kernel,
        grid=(t // tm,),
        in_specs=[row(D_MODEL), pl.BlockSpec((N_HEADS, HEAD_DIM, tm), lambda i: (0, 0, i)),
                  row(D_GROUP), full((2 * D_GROUP, D_MODEL)), full((1, D_MODEL)),
                  full((D_MODEL, LANES)), full((1, LANES))],
        out_specs=[row(D_MODEL), row(D_MODEL), row(LANES)],
        out_shape=(jax.ShapeDtypeStruct((t, D_MODEL), F32),
                   jax.ShapeDtypeStruct((t, D_MODEL), BF16),
                   jax.ShapeDtypeStruct((t, LANES), F32)),
        compiler_params=pltpu.CompilerParams(dimension_semantics=("arbitrary",),
                                             vmem_limit_bytes=VMEM_LIMIT),
        name="out_proj",
    )(x2, att_t, rw, wo, n2w, rwt, rb)


MOE_TM = 1024
MOE_RB = 144
MOE_FC = 1024
MOE_EXPERTS_PER_STEP = 2


def _route_kernel(gate_ref, rankt_ref, gatet_ref, cnt_ref):
    tm = gate_ref.shape[0]
    gates = gate_ref[...]
    hot = gates > 0.0
    ind = jnp.where(hot, 1.0, 0.0)
    before = (lax.broadcasted_iota(jnp.int32, (tm, tm), 1)
              < lax.broadcasted_iota(jnp.int32, (tm, tm), 0))
    rank = jnp.dot(jnp.where(before, 1.0, 0.0).astype(BF16), ind.astype(BF16),
                   preferred_element_type=F32)
    rankt_ref[...] = jnp.where(hot, rank, -1.0).T
    gatet_ref[...] = gates.T
    cnt = jnp.sum(ind, axis=0, keepdims=True)
    cnt_ref[...] = jnp.broadcast_to(cnt[None], cnt_ref.shape)


def _route(gates, tm):
    t = gates.shape[0]
    return pl.pallas_call(
        _route_kernel,
        grid=(t // tm,),
        in_specs=[pl.BlockSpec((tm, LANES), lambda i: (i, 0))],
        out_specs=[pl.BlockSpec((LANES, tm), lambda i: (0, i)),
                   pl.BlockSpec((LANES, tm), lambda i: (0, i)),
                   pl.BlockSpec((1, SUBLANES, LANES), lambda i: (i, 0, 0))],
        out_shape=(jax.ShapeDtypeStruct((LANES, t), F32),
                   jax.ShapeDtypeStruct((LANES, t), F32),
                   jax.ShapeDtypeStruct((t // tm, SUBLANES, LANES), F32)),
        compiler_params=pltpu.CompilerParams(dimension_semantics=("arbitrary",),
                                             vmem_limit_bytes=VMEM_LIMIT),
        name="route",
    )(gates)


def _moe_kernel(cnt_ref, h_ref, xn_ref, rankt_ref, gatet_ref, wgu_ref, bgu_ref,
                wdn_ref, bdn_ref, o_ref):
    i = pl.program_id(0)
    tm = h_ref.shape[0]

    @pl.when(pl.program_id(1) == 0)
    def _():
        o_ref[...] = h_ref[...]

    for sub in range(MOE_EXPERTS_PER_STEP):
        _moe_expert(pl.program_id(1) * MOE_EXPERTS_PER_STEP + sub, sub, i, tm, cnt_ref, xn_ref,
                    rankt_ref, gatet_ref, wgu_ref, bgu_ref, wdn_ref, bdn_ref, o_ref)


def _moe_expert(e, slot, i, tm, cnt_ref, xn_ref, rankt_ref, gatet_ref, wgu_ref, bgu_ref,
                wdn_ref, bdn_ref, o_ref):
    n_rows = cnt_ref[i * N_EXPERTS + e]
    n_blocks = (n_rows + MOE_RB - 1) // MOE_RB
    rank_row = rankt_ref[pl.ds(e, 1), :]
    gate_row = gatet_ref[pl.ds(e, 1), :]

    def row_block(bi, carry):
        r0 = (bi * MOE_RB).astype(F32)
        rowid = lax.broadcasted_iota(jnp.int32, (MOE_RB, tm), 0).astype(F32) + r0
        hit = rank_row == rowid
        sel = jnp.where(hit, 1.0, 0.0).astype(BF16)
        gate = jnp.sum(jnp.where(hit, gate_row, 0.0), axis=1, keepdims=True)
        xb = jnp.dot(sel, xn_ref[...], preferred_element_type=F32).astype(BF16)
        y = jnp.zeros((MOE_RB, D_MODEL), F32)
        ups = []
        for fc in range(D_FF // MOE_FC):
            c0 = fc * MOE_FC
            ups.append((jnp.dot(xb, wgu_ref[slot, :, c0:c0 + MOE_FC], preferred_element_type=F32),
                        jnp.dot(xb, wgu_ref[slot, :, D_FF + c0:D_FF + c0 + MOE_FC],
                                preferred_element_type=F32)))
        for fc in range(D_FF // MOE_FC):
            c0 = fc * MOE_FC
            hg = ups[fc][0] + bgu_ref[slot, :, c0:c0 + MOE_FC]
            hl = ups[fc][1] + bgu_ref[slot, :, D_FF + c0:D_FF + c0 + MOE_FC]
            glu = jnp.minimum(hg, SWIGLU_LIMIT)
            lin = jnp.clip(hl, -SWIGLU_LIMIT, SWIGLU_LIMIT)
            act = (lin + 1.0) * glu * jax.nn.sigmoid(SWIGLU_ALPHA * glu)
            y = y + jnp.dot(act.astype(BF16), wdn_ref[slot, c0:c0 + MOE_FC, :],
                            preferred_element_type=F32)
        y = ((y + bdn_ref[slot]) * gate).astype(BF16)
        o_ref[...] += _dot_tn(sel, y)
        return carry

    lax.fori_loop(0, n_blocks, row_block, 0)


def _moe(counts, h1, xn2, rank_t, gate_t, wgu, bgu, wdn, bdn, tm):
    t = h1.shape[0]
    eps = MOE_EXPERTS_PER_STEP
    row = lambda width: pl.BlockSpec((tm, width), lambda i, e, c: (i, 0))
    grid_spec = pltpu.PrefetchScalarGridSpec(
        num_scalar_prefetch=1,
        grid=(t // tm, N_EXPERTS // eps),
        in_specs=[row(D_MODEL), row(D_MODEL),
                  pl.BlockSpec((LANES, tm), lambda i, e, c: (0, i)),
                  pl.BlockSpec((LANES, tm), lambda i, e, c: (0, i)),
                  pl.BlockSpec((eps, D_MODEL, 2 * D_FF), lambda i, e, c: (e, 0, 0)),
                  pl.BlockSpec((eps, 1, 2 * D_FF), lambda i, e, c: (e, 0, 0)),
                  pl.BlockSpec((eps, D_FF, D_MODEL), lambda i, e, c: (e, 0, 0)),
                  pl.BlockSpec((eps, 1, D_MODEL), lambda i, e, c: (e, 0, 0))],
        out_specs=row(D_MODEL))
    return pl.pallas_call(
        _moe_kernel,
        grid_spec=grid_spec,
        out_shape=jax.ShapeDtypeStruct((t, D_MODEL), F32),
        compiler_params=pltpu.CompilerParams(dimension_semantics=("arbitrary", "arbitrary"),
                                             vmem_limit_bytes=VMEM_LIMIT),
        name="moe",
    )(counts, h1, xn2, rank_t, gate_t, wgu, bgu, wdn, bdn)


def _rope_tables():
    lane = jnp.arange(LANES)

    def rows(period, rot):
        half = rot // 2
        inv_freq = ROPE_THETA ** (-jnp.arange(half, dtype=F32) / half)
        jm = lane % period
        freq = jnp.where(jm < rot, inv_freq[jm % half], 0.0)
        sign = jnp.where(jm < half, -1.0, jnp.where(jm < rot, 1.0, 0.0))
        first = (jm < half).astype(F32)
        return [freq, sign, first]

    idx_rows = rows(IDX_DIM, IDX_DIM // 4)
    only_key = (lane < IDX_DIM).astype(F32)
    return jnp.stack(rows(HEAD_DIM, HEAD_DIM // 4) + idx_rows
                     + [only_key, jnp.zeros_like(only_key)]).astype(F32)


def _group_matrix():
    g = np.arange(D_GROUP) // HEAD_DIM
    return jnp.asarray((g[:, None] == g[None, :]).astype(np.float32), dtype=BF16)


def _pad_rows(w, r0, rows):
    return jnp.zeros((rows, w.shape[1]), w.dtype).at[r0:r0 + w.shape[0]].set(w)


def kernel(x, positions, norm1_w, w_in, q_norm_w, k_norm_w, rwkv_mu, rwkv_w0, rwkv_w2, rwkv_a0,
           rwkv_a2, rwkv_g2, rwkv_k_k, rwkv_k_a, rwkv_r_k, rwkv_ln_w, rwkv_ln_b, w_out, norm2_w,
           router_w, router_b, exp_w_gu, exp_b_gu, exp_w_down, exp_b_down):
    b, t, _ = x.shape
    assert b == 1 and w_in.shape[0] == 1, "single sequence, single layer"
    assert t % DSA_KB == 0 and t % DSA_QB == 0 and t % MOE_TM == 0
    x2 = x[0]
    pos_f = positions[0].astype(F32)[:, None]
    n_sel = min(TOPK_MAX, t // 4)

    w = w_in[0]
    a0 = 3 * D_GROUP
    att_cols = a0 + D_IDX + IDX_DIM + N_IDX_HEADS
    w_att, w_rw = w[:, :att_cols], w[:, att_cols:]
    w_sm = jnp.zeros((D_MODEL, LANES), F32).at[:, :IDX_DIM + N_IDX_HEADS].set(w_att[:, a0 + D_IDX:])
    w_packed = jnp.concatenate(
        [w_att[:, :a0], w_rw[:, :a0], w_att[:, a0:a0 + D_IDX], w_sm, w_rw[:, a0:]],
        axis=1).astype(BF16)
    mu = rwkv_mu[0][None, :]
    tile8 = lambda z: jnp.tile(z, N_HEADS)[None, :]
    gsum = _group_matrix()

    tm = 256
    q, k, v, rr, rk, rv, iq, sm, lora = _in_proj(
        x2, pos_f, norm1_w, w_packed, mu, tile8(q_norm_w[0]), tile8(k_norm_w[0]), gsum,
        _rope_tables(), tm)

    vec = lambda z: z.reshape(1, D_GROUP)
    rw, (w_gu_bf, w_dn_bf) = _rwkv(
        rr, rk, rv, lora, vec(rwkv_w0[0]), _pad_rows(rwkv_w2[0], 0, LANES), vec(rwkv_a0[0]),
        _pad_rows(rwkv_a2[0], D_DECAY_LORA, LANES),
        _pad_rows(rwkv_g2[0], D_DECAY_LORA + D_AAA_LORA, LANES),
        vec(rwkv_k_k[0]), vec(rwkv_k_a[0]), vec(rwkv_r_k[0]), vec(rwkv_ln_w[0]),
        vec(rwkv_ln_b[0]), gsum, tm, to_cast=(exp_w_gu[0], exp_w_down[0]))

    heads = lambda z: z.reshape(t, N_HEADS, HEAD_DIM)
    vt = jnp.concatenate([heads(v).transpose(1, 2, 0),
                          jnp.ones((N_HEADS, DSA_V_ROWS - HEAD_DIM, t), BF16)], axis=1)
    att_t = _dsa(heads(q).transpose(1, 2, 0), k, vt, iq.T, sm[:, :IDX_DIM].astype(BF16),
                 sm[:, IDX_DIM:IDX_DIM + N_IDX_HEADS].T, n_sel)

    rwt = jnp.zeros((D_MODEL, LANES), F32).at[:, :N_EXPERTS].set(router_w[0])
    rb = jnp.full((1, LANES), NEG_BIG, F32).at[0, :N_EXPERTS].set(router_b[0])
    h1, xn2, gates = _out_proj(x2, att_t, rw, w_out[0].astype(BF16), norm2_w, rwt, rb, tm)

    rank_t, gate_t, cnt = _route(gates, MOE_TM)
    counts = cnt[:, 0, :N_EXPERTS].astype(jnp.int32).reshape(-1)
    out = _moe(counts, h1, xn2, rank_t, gate_t, w_gu_bf, exp_b_gu[0][:, None, :], w_dn_bf,
               exp_b_down[0][:, None, :], MOE_TM)
    return out[None]
```

```python
import functools

import jax
import jax.numpy as jnp
import numpy as np
from jax import lax
from jax.experimental import pallas as pl
from jax.experimental.pallas import tpu as pltpu

F32 = jnp.float32
BF16 = jnp.bfloat16
HIGHEST = lax.Precision.HIGHEST

D_MODEL = 1024
HEAD_DIM = 64
N_HEADS = 8
D_GROUP = N_HEADS * HEAD_DIM
ROPE_THETA = 500000.0
N_IDX_HEADS = 8
IDX_DIM = 32
D_IDX = N_IDX_HEADS * IDX_DIM
TOPK_MAX = 256
D_DECAY_LORA = 32
D_AAA_LORA = 32
RWKV_GN_EPS = 64e-5
N_EXPERTS = 32
TOP_K_EXPERTS = 4
D_FF = 1024
SWIGLU_LIMIT = 7.0
SWIGLU_ALPHA = 1.702
RMS_EPS = 1e-6

LANES = 128
SUBLANES = 8
VMEM_LIMIT = 56 * 1024 * 1024

NEG_BIG = -1e30
LOG2_E = 1.4426950408889634
INT_MIN = -(2 ** 31)


def _dot(a, b):
    return jnp.dot(a.astype(BF16), b.astype(BF16), preferred_element_type=F32)


def _dotf(a, b):
    return jnp.dot(a, b, preferred_element_type=F32, precision=HIGHEST)


def _dot3(a, b):
    a_hi = a.astype(BF16)
    b_hi = b.astype(BF16)
    a_lo = (a - a_hi.astype(F32)).astype(BF16)
    b_lo = (b - b_hi.astype(F32)).astype(BF16)
    dot = functools.partial(jnp.dot, preferred_element_type=F32)
    return dot(a_hi, b_hi) + dot(a_hi, b_lo) + dot(a_lo, b_hi)


def _dot_nt(a, b):
    return lax.dot_general(a.astype(BF16), b.astype(BF16), (((1,), (1,)), ((), ())),
                           preferred_element_type=F32)


def _dot_tn(a, b):
    return lax.dot_general(a.astype(BF16), b.astype(BF16), (((0,), (0,)), ((), ())),
                           preferred_element_type=F32)


def _group_sum(z, g_ref):
    hi = z.astype(BF16)
    lo = (z - hi.astype(F32)).astype(BF16)
    g = g_ref[...]
    return (jnp.dot(hi, g, preferred_element_type=F32)
            + jnp.dot(lo, g, preferred_element_type=F32))


_C_Q, _C_K, _C_V, _C_RR, _C_RK, _C_RV = (i * D_GROUP for i in range(6))
_C_IQ = 6 * D_GROUP
_C_SM = _C_IQ + D_IDX
_C_LORA = _C_SM + LANES
D_IN_PACKED = _C_LORA + LANES
D_SHIFT = 3 * D_GROUP + LANES


def _in_proj_kernel(x_ref, pos_ref, n1w_ref, w_ref, mu_ref, qnw_ref, knw_ref, g_ref, rope_ref,
                    q_ref, k_ref, v_ref, rr_ref, rk_ref, rv_ref, iq_ref, sm_ref, lora_ref,
                    carry_ref):
    tm = x_ref.shape[0]

    @pl.when(pl.program_id(0) == 0)
    def _():
        carry_ref[...] = jnp.zeros_like(carry_ref)

    x = x_ref[...]
    xn = x * lax.rsqrt(jnp.mean(x * x, axis=-1, keepdims=True) + RMS_EPS) * n1w_ref[...]
    xb = xn.astype(BF16)
    pos = pos_ref[...]
    rope = rope_ref[...]

    def tables(frow, srow):
        ang = pos * rope[frow:frow + 1, :]
        return jnp.cos(ang), jnp.sin(ang) * rope[srow:srow + 1, :]

    def widen(z, reps):
        return jnp.concatenate([z] * reps, axis=1)

    def rotary(z, c, s, first_row, half):
        w = z.shape[1]
        first = jnp.concatenate([rope[first_row:first_row + 1, :]] * (w // LANES), axis=1) > 0.5
        partner = jnp.where(first, pltpu.roll(z, w - half, 1), pltpu.roll(z, half, 1))
        return z * c + partner * s

    def head_norm(z, w_row):
        ms = _group_sum(z * z, g_ref) * (1.0 / HEAD_DIM)
        return z * lax.rsqrt(ms + RMS_EPS) * w_row

    def proj(c0, width):
        return jnp.dot(xb, w_ref[:, c0:c0 + width], preferred_element_type=F32)

    def shift(z, c0):
        width = z.shape[1]
        row = lax.broadcasted_iota(jnp.int32, z.shape, 0)
        prev = jnp.where(row == 0, carry_ref[0:1, c0:c0 + width], pltpu.roll(z, 1, 0))
        carry_ref[0:1, c0:c0 + width] = z[tm - 1:tm, :]
        return z + (prev - z) * mu_ref[:, c0:c0 + width]

    cq, sq = (widen(z, D_GROUP // LANES) for z in tables(0, 1))
    q = rotary(head_norm(proj(_C_Q, D_GROUP), qnw_ref[...]), cq, sq, 2, HEAD_DIM // 8)
    q_ref[...] = q.astype(BF16)
    k = rotary(head_norm(proj(_C_K, D_GROUP), knw_ref[...]), cq, sq, 2, HEAD_DIM // 8)
    k_ref[...] = k.astype(BF16)
    v_ref[...] = proj(_C_V, D_GROUP).astype(BF16)

    rr_ref[...] = shift(proj(_C_RR, D_GROUP), 0)
    rk_ref[...] = shift(proj(_C_RK, D_GROUP), D_GROUP)
    rv_ref[...] = shift(proj(_C_RV, D_GROUP), 2 * D_GROUP)
    lora_ref[...] = shift(proj(_C_LORA, LANES), 3 * D_GROUP)

    ci, si = tables(3, 4)
    iq_ref[...] = rotary(proj(_C_IQ, D_IDX), widen(ci, D_IDX // LANES), widen(si, D_IDX // LANES),
                         5, IDX_DIM // 8)
    only_key = rope[6:7, :]
    sm_ref[...] = rotary(proj(_C_SM, LANES), ci * only_key + (1.0 - only_key), si * only_key,
                         5, IDX_DIM // 8)


def _in_proj(x2, pos_f, n1w, w_packed, mu_packed, qnw, knw, gmat, rope, tm):
    t = x2.shape[0]
    full = lambda shape: pl.BlockSpec(shape, lambda i: (0,) * len(shape))
    row = lambda width: pl.BlockSpec((tm, width), lambda i: (i, 0))
    out_shapes = (
        jax.ShapeDtypeStruct((t, D_GROUP), BF16),
        jax.ShapeDtypeStruct((t, D_GROUP), BF16),
        jax.ShapeDtypeStruct((t, D_GROUP), BF16),
        jax.ShapeDtypeStruct((t, D_GROUP), F32),
        jax.ShapeDtypeStruct((t, D_GROUP), F32),
        jax.ShapeDtypeStruct((t, D_GROUP), F32),
        jax.ShapeDtypeStruct((t, D_IDX), F32),
        jax.ShapeDtypeStruct((t, LANES), F32),
        jax.ShapeDtypeStruct((t, LANES), F32),
    )
    return pl.pallas_call(
        _in_proj_kernel,
        grid=(t // tm,),
        in_specs=[row(D_MODEL), row(1), full((1, D_MODEL)), full((D_MODEL, D_IN_PACKED)),
                  full((1, D_SHIFT)), full((1, D_GROUP)), full((1, D_GROUP)),
                  full((D_GROUP, D_GROUP)), full((SUBLANES, LANES))],
        out_specs=[row(D_GROUP)] * 6 + [row(D_IDX), row(LANES), row(LANES)],
        out_shape=out_shapes,
        scratch_shapes=[pltpu.VMEM((SUBLANES, D_SHIFT), F32)],
        compiler_params=pltpu.CompilerParams(dimension_semantics=("arbitrary",),
                                             vmem_limit_bytes=VMEM_LIMIT),
        name="in_proj",
    )(x2, pos_f, n1w, w_packed, mu_packed, qnw, knw, gmat, rope)


RWKV_CHUNK = 64
RWKV_UNROLL = 4
PAIR = 2 * HEAD_DIM


def _rwkv_kernel(r_ref, k_ref, v_ref, lora_ref, w0_ref, w2_ref, a0_ref, a2_ref, g2_ref,
                 kk_ref, ka_ref, rk_ref, lnw_ref, lnb_ref, g_ref, *rest, n_cast):
    cast_in, o_ref, cast_out = rest[:n_cast], rest[n_cast], rest[n_cast + 1:2 * n_cast + 1]
    s_ref, ld_s, r_s, k2_s, b_s, kk_s, y_s = rest[2 * n_cast + 1:]
    for src, dst in zip(cast_in, cast_out):
        dst[...] = src[...].astype(BF16)
    tm = r_ref.shape[0]
    n_chunks = tm // RWKV_CHUNK
    n_pairs = D_GROUP // PAIR
    c = RWKV_CHUNK

    @pl.when(pl.program_id(0) == 0)
    def _():
        s_ref[...] = jnp.zeros_like(s_ref)

    lora = lora_ref[...]
    r = r_ref[...]
    k = k_ref[...]
    v = v_ref[...]
    zarg = w0_ref[...] + _dot3(jnp.tanh(lora), w2_ref[...])
    sp = jnp.maximum(-zarg, 0.0) + jnp.log1p(jnp.exp(-jnp.abs(zarg)))
    ld_s[...] = -jnp.exp(-sp - 0.5)
    a = jax.nn.sigmoid(a0_ref[...] + _dot3(lora, a2_ref[...]))
    g = _dot3(jax.nn.sigmoid(lora), g2_ref[...])
    kk = k * kk_ref[...]
    kk = kk * lax.rsqrt(jnp.maximum(_group_sum(kk * kk, g_ref), 1e-24))
    k2 = k * (1.0 + (a - 1.0) * ka_ref[...])
    bonus = _group_sum(r * k2 * rk_ref[...], g_ref) * v
    r_s[...] = r
    k2_s[...] = k2
    kk_s[...] = kk
    b_s[...] = kk * a

    row = lax.broadcasted_iota(jnp.int32, (2 * c, 2 * c), 0)
    col = lax.broadcasted_iota(jnp.int32, (2 * c, 2 * c), 1)
    same_head = (row >= c) == (col >= c)
    strict = same_head & (col < row)
    incl = same_head & (col <= row)
    eye = (row == col).astype(F32)
    tri = (lax.broadcasted_iota(jnp.int32, (c, c), 1)
           <= lax.broadcasted_iota(jnp.int32, (c, c), 0)).astype(F32)
    lane = lax.broadcasted_iota(jnp.int32, (c, PAIR), 1)
    head0 = lane < HEAD_DIM

    def stack(z):
        return jnp.concatenate([jnp.where(head0, z, 0.0), jnp.where(head0, 0.0, z)], axis=0)

    def chunk_body(ci, carry):
        c2 = 2 * c
        units = [(cc, p) for cc in range(RWKV_UNROLL) for p in range(n_pairs)]
        uid = range(len(units))
        rows = [pl.multiple_of((ci * RWKV_UNROLL + cc) * c, c) for cc in range(RWKV_UNROLL)]
        ld_all = [ld_s[pl.ds(r0, c), :] for r0 in rows]
        cum_all = [_dotf(tri, ld) for ld in ld_all]
        rt, kt, bt, kp, vs, lhs, rhs, gam_end = [], [], [], [], [], [], [], []
        for cc, p in units:
            r0 = rows[cc]
            cols = slice(p * PAIR, (p + 1) * PAIR)
            ld = ld_all[cc][:, cols]
            cum = cum_all[cc][:, cols]
            gam = jnp.exp(cum)
            inv = jnp.exp(-cum)
            gam_prev = jnp.exp(cum - ld)
            gam_end.append(gam[c - 1:c, :])
            rt.append(stack(r_s[pl.ds(r0, c), cols] * gam))
            kt_raw = k2_s[pl.ds(r0, c), cols] * inv
            bt_raw = b_s[pl.ds(r0, c), cols] * inv
            kt.append(stack(kt_raw))
            bt.append(stack(bt_raw))
            kp.append(stack(kk_s[pl.ds(r0, c), cols] * gam_prev))
            vs.append(stack(v_ref[pl.ds(r0, c), cols]))
            lhs.append(jnp.concatenate([kp[-1], rt[-1]], axis=0))
            rhs.append(jnp.concatenate([bt_raw, bt_raw, kt_raw, kt_raw], axis=0))
        aa = [_dot_nt(lhs[u], rhs[u]) for u in uid]
        a_kk = [jnp.where(strict, aa[u][:c2, c2:], 0.0) for u in uid]
        a_rb = [jnp.where(incl, aa[u][c2:, :c2], 0.0) for u in uid]
        a_rk = [jnp.where(incl, aa[u][c2:, c2:], 0.0) for u in uid]
        n = [-jnp.where(strict, aa[u][:c2, :c2], 0.0) for u in uid]
        prod = [eye + n[u] for u in uid]
        n = [_dot(n[u], n[u]) for u in uid]
        for _ in range(int(np.log2(c)) - 2):
            both = [_dot(jnp.concatenate([n[u], prod[u]], axis=0), n[u]) for u in uid]
            prod = [prod[u] + both[u][c2:] for u in uid]
            n = [both[u][:c2] for u in uid]
        av = [_dot(jnp.concatenate([a_kk[u], a_rk[u]], axis=0), vs[u]) for u in uid]
        tinv = [prod[u] + _dot(prod[u], n[u]) for u in uid]
        wu = [_dot(tinv[u], jnp.concatenate([kp[u], av[u][:c2]], axis=1)) for u in uid]
        rb = [_dot(a_rb[u], wu[u]) for u in uid]
        tn = [_dot_tn(jnp.concatenate([wu[u], vs[u]], axis=1),
                      jnp.concatenate([bt[u], kt[u]], axis=1)) for u in uid]
        r2 = [rt[u] - rb[u][:, :PAIR] for u in uid]
        y2 = [av[u][c2:] - rb[u][:, PAIR:] for u in uid]
        s_mix = [(eye - tn[u][:PAIR, :PAIR]) * gam_end[u] for u in uid]
        s_add = [(tn[u][2 * PAIR:, PAIR:] - tn[u][PAIR:2 * PAIR, :PAIR]) * gam_end[u] for u in uid]
        state = [s_ref[p] for p in range(n_pairs)]
        for cc in range(RWKV_UNROLL):
            us = [cc * n_pairs + p for p in range(n_pairs)]
            ys = [_dot_nt(r2[u], state[p]) + y2[u] for p, u in enumerate(us)]
            state = [_dot(state[p], s_mix[u]) + s_add[u] for p, u in enumerate(us)]
            for p in range(n_pairs):
                y_s[pl.ds(rows[cc], c), p * PAIR:(p + 1) * PAIR] = ys[p][:c] + ys[p][c:]
        for p in range(n_pairs):
            s_ref[p] = state[p]
        return carry

    lax.fori_loop(0, n_chunks // RWKV_UNROLL, chunk_body, 0)

    y = y_s[...]
    mean = _group_sum(y, g_ref) * (1.0 / HEAD_DIM)
    yc = y - mean
    var = _group_sum(yc * yc, g_ref) * (1.0 / HEAD_DIM)
    yn = yc * lax.rsqrt(var + RWKV_GN_EPS) * lnw_ref[...] + lnb_ref[...]
    o_ref[...] = (yn + bonus) * g


def _rwkv(rr, rk, rv, lora, w0, w2p, a0, a2p, g2p, k_k, k_a, r_k, ln_w, ln_b, gsum, tm,
          to_cast=()):
    t = rr.shape[0]
    full = lambda shape: pl.BlockSpec(shape, lambda i: (0,) * len(shape))
    row = lambda width: pl.BlockSpec((tm, width), lambda i: (i, 0))
    vec = full((1, D_GROUP))
    big = pltpu.VMEM((tm, D_GROUP), F32)
    slabs = [w.reshape(t // tm, -1, w.shape[-1]) for w in to_cast]
    cast_specs = [pl.BlockSpec((1,) + w.shape[1:], lambda i: (i, 0, 0)) for w in slabs]
    outs = pl.pallas_call(
        functools.partial(_rwkv_kernel, n_cast=len(slabs)),
        grid=(t // tm,),
        in_specs=[row(D_GROUP), row(D_GROUP), row(D_GROUP), row(LANES),
                  vec, full((LANES, D_GROUP)), vec, full((LANES, D_GROUP)),
                  full((LANES, D_GROUP)), vec, vec, vec, vec, vec, full((D_GROUP, D_GROUP))]
        + cast_specs,
        out_specs=[row(D_GROUP)] + cast_specs,
        out_shape=[jax.ShapeDtypeStruct((t, D_GROUP), F32)]
        + [jax.ShapeDtypeStruct(w.shape, BF16) for w in slabs],
        scratch_shapes=[pltpu.VMEM((D_GROUP // PAIR, PAIR, PAIR), F32),
                        big, big, big, big, big, big],
        compiler_params=pltpu.CompilerParams(dimension_semantics=("arbitrary",),
                                             vmem_limit_bytes=VMEM_LIMIT),
        name="rwkv",
    )(rr, rk, rv, lora, w0, w2p, a0, a2p, g2p, k_k, k_a, r_k, ln_w, ln_b, gsum, *slabs)
    return outs[0], [o.reshape(w.shape) for o, w in zip(outs[1:], to_cast)]


DSA_QB = 256
DSA_KB = 512
M_INIT = -5e29
DSA_HEAD_GROUP = 4
DSA_ACC_ROWS = 32
DSA_GROUPS = 256
DSA_TOP = 10
DSA_V_ROWS = HEAD_DIM + 16


def _dsa_kernel(qi_ref, kj_ref, qt_ref, k_ref, vt_ref, iqt_ref, ik_ref, iwt_ref, o_ref,
                key_s, thr_s, need_s, over_s, m_s, acc_s, bdq_s, top_s, tops_s, *, n_sel):
    qb = qt_ref.shape[2]
    kb = k_ref.shape[0]
    step = pl.program_id(0)
    qi = qi_ref[step]
    kj = kj_ref[step]
    q0 = qi * qb
    j_last = (q0 + qb - 1) // kb
    n_kc = j_last + 1
    idx_scale = float((IDX_DIM * N_IDX_HEADS) ** -0.5)

    def causal(jblk):
        s_pos = jblk * kb + lax.broadcasted_iota(jnp.int32, (kb, qb), 0)
        t_pos = q0 + lax.broadcasted_iota(jnp.int32, (kb, qb), 1)
        return s_pos <= t_pos

    @pl.when(kj == 0)
    def _():
        iqt = iqt_ref[...].astype(BF16)
        iwt = iwt_ref[...] * idx_scale

        def score_chunk(kc, diagonal):
            ik = ik_ref[pl.ds(pl.multiple_of(kc * kb, kb), kb), :]
            score = jnp.zeros((kb, qb), F32)
            for h in range(N_IDX_HEADS):
                d = jnp.dot(ik, iqt[h * IDX_DIM:(h + 1) * IDX_DIM, :],
                            preferred_element_type=F32)
                score = score + jnp.maximum(d, 0.0) * iwt[h:h + 1, :]
            keys = to_key(score)
            if diagonal:
                ok = causal(kc)
                score = jnp.where(ok, score, -jnp.inf)
                keys = jnp.where(ok, keys, jnp.int32(INT_MIN))
            key_s[kc] = keys
            xs = [score[r * DSA_GROUPS:(r + 1) * DSA_GROUPS] for r in range(kb // DSA_GROUPS)]
            for lvl in range(DSA_TOP):
                s = tops_s[lvl]
                for r in range(len(xs)):
                    s, xs[r] = jnp.maximum(s, xs[r]), jnp.minimum(s, xs[r])
                tops_s[lvl] = s

        def to_key(score):
            bits = pltpu.bitcast(score, jnp.int32)
            return jnp.where(bits < 0, bits ^ jnp.int32(0x7FFFFFFF), bits)

        tops_s[...] = jnp.full_like(tops_s, -jnp.inf)

        def full_chunk(kc, carry):
            score_chunk(kc, False)
            return carry

        lax.fori_loop(0, j_last, full_chunk, 0)
        score_chunk(j_last, True)
        for lvl in range(DSA_TOP):
            kept = tops_s[lvl]
            top_s[lvl] = jnp.where(kept == -jnp.inf, jnp.int32(INT_MIN), to_key(kept))

        def select(count_keys):
            def bit_step(b, cur):
                bit = lax.shift_left(jnp.int32(1), jnp.int32(31) - b)
                cand = (cur | bit) ^ jnp.int32(INT_MIN)
                cnt = jnp.sum(count_keys(cand), axis=0, keepdims=True)
                return jnp.where(cnt >= float(n_sel), cur | bit, cur)

            cur = lax.fori_loop(0, 32, bit_step, jnp.zeros((1, qb), jnp.int32))
            return jnp.maximum(cur ^ jnp.int32(INT_MIN), jnp.int32(INT_MIN + 1))

        def part_count(keys, cand):
            ind = jnp.where(keys >= cand, 1.0, 0.0)
            return jnp.sum(ind.reshape(-1, DSA_ACC_ROWS, qb), axis=0)

        def count_top(cand):
            return lax.fori_loop(0, DSA_TOP, lambda lvl, a: a + part_count(top_s[lvl], cand),
                                 jnp.zeros((DSA_ACC_ROWS, qb), F32))

        def count_all(cand):
            return lax.fori_loop(0, n_kc, lambda kc, a: a + part_count(key_s[kc], cand),
                                 jnp.zeros((DSA_ACC_ROWS, qb), F32))

        def settle(count_keys):
            thr = select(count_keys)
            thr_s[...] = thr
            above = jnp.sum(count_keys(thr + 1), axis=0, keepdims=True)
            upto = jnp.sum(count_keys(thr), axis=0, keepdims=True)
            need_s[...] = float(n_sel) - above
            over_s[...] = upto - float(n_sel)
            return thr

        thr = settle(count_top)
        hidden = jnp.where(top_s[DSA_TOP - 1] >= thr, 1.0, 0.0)

        @pl.when(jnp.max(hidden) > 0.0)
        def _():
            settle(count_all)

        @pl.when(jnp.max(over_s[...]) > 0.0)
        def _():
            thr = thr_s[...]
            need = need_s[...]
            sub = LANES
            earlier = jnp.where(lax.broadcasted_iota(jnp.int32, (sub, sub), 1)
                                < lax.broadcasted_iota(jnp.int32, (sub, sub), 0), 1.0, 0.0).astype(BF16)

            def drop_surplus(kc, seen):
                keys = key_s[kc]
                tie = keys == thr
                tie_f = jnp.where(tie, 1.0, 0.0)
                ranks = []
                for r in range(kb // sub):
                    part = tie_f[r * sub:(r + 1) * sub]
                    ranks.append(seen + jnp.dot(earlier, part.astype(BF16),
                                                preferred_element_type=F32))
                    seen = seen + jnp.sum(part, axis=0, keepdims=True)
                rank = jnp.concatenate(ranks, axis=0)
                key_s[kc] = jnp.where(tie & (rank >= need), thr - 1, keys)
                return seen

            lax.fori_loop(0, n_kc, drop_surplus, jnp.zeros((1, qb), F32))

        m_s[...] = jnp.full_like(m_s, M_INIT)
        acc_s[...] = jnp.zeros_like(acc_s)
        bdq_s[...] = jnp.zeros_like(bdq_s)
        for h in range(N_HEADS):
            g, hh = divmod(h, DSA_HEAD_GROUP)
            bdq_s[g, hh * HEAD_DIM:(hh + 1) * HEAD_DIM, hh * qb:(hh + 1) * qb] = (
                qt_ref[h].astype(F32) * (HEAD_DIM ** -0.5 * LOG2_E)).astype(BF16)

    mask = key_s[kj] >= thr_s[...]
    hg = DSA_HEAD_GROUP
    st_all = [jnp.dot(k_ref[:, g * hg * HEAD_DIM:(g + 1) * hg * HEAD_DIM], bdq_s[g],
                      preferred_element_type=F32) for g in range(N_HEADS // hg)]
    for g in range(N_HEADS // hg):
        st_g = st_all[g]
        for hh in range(hg):
            h = g * hg + hh
            st = jnp.where(mask, st_g[:, hh * qb:(hh + 1) * qb], NEG_BIG)
            m_old = m_s[h]
            part = jnp.max(st.reshape(kb // DSA_ACC_ROWS, DSA_ACC_ROWS, qb), axis=0)
            m_new = jnp.maximum(m_old, jnp.max(part, axis=0, keepdims=True))
            p = jnp.exp2((st - m_new).astype(BF16))
            acc_s[h] = jnp.exp2(m_old - m_new) * acc_s[h] + jnp.dot(
                vt_ref[h], p, preferred_element_type=F32)
            m_s[h] = m_new

    @pl.when(kj == j_last)
    def _():
        for h in range(N_HEADS):
            acc = acc_s[h]
            o_ref[h] = acc[:HEAD_DIM] / acc[HEAD_DIM:HEAD_DIM + 1]


def _dsa(qt, k, vt, iqt, ik, iwt, n_sel):
    t = ik.shape[0]
    qb, kb = DSA_QB, DSA_KB
    assert n_sel <= DSA_GROUPS and kb % DSA_GROUPS == 0
    nq, nk = t // qb, t // kb
    pairs = [(i, j) for i in range(nq) for j in range((i * qb + qb - 1) // kb + 1)]
    qi = jnp.asarray(np.array([p[0] for p in pairs], np.int32))
    kj = jnp.asarray(np.array([p[1] for p in pairs], np.int32))
    grid_spec = pltpu.PrefetchScalarGridSpec(
        num_scalar_prefetch=2,
        grid=(len(pairs),),
        in_specs=[pl.BlockSpec((N_HEADS, HEAD_DIM, qb), lambda s, qi, kj: (0, 0, qi[s])),
                  pl.BlockSpec((kb, D_GROUP), lambda s, qi, kj: (kj[s], 0)),
                  pl.BlockSpec((N_HEADS, DSA_V_ROWS, kb), lambda s, qi, kj: (0, 0, kj[s])),
                  pl.BlockSpec((D_IDX, qb), lambda s, qi, kj: (0, qi[s])),
                  pl.BlockSpec((t, IDX_DIM), lambda s, qi, kj: (0, 0)),
                  pl.BlockSpec((N_IDX_HEADS, qb), lambda s, qi, kj: (0, qi[s]))],
        out_specs=pl.BlockSpec((N_HEADS, HEAD_DIM, qb), lambda s, qi, kj: (0, 0, qi[s])),
        scratch_shapes=[pltpu.VMEM((nk, kb, qb), jnp.int32),
                        pltpu.VMEM((1, qb), jnp.int32),
                        pltpu.VMEM((1, qb), F32),
                        pltpu.VMEM((1, qb), F32),
                        pltpu.VMEM((N_HEADS, 1, qb), F32),
                        pltpu.VMEM((N_HEADS, DSA_V_ROWS, qb), F32),
                        pltpu.VMEM((N_HEADS // DSA_HEAD_GROUP, DSA_HEAD_GROUP * HEAD_DIM,
                                    DSA_HEAD_GROUP * qb), BF16),
                        pltpu.VMEM((DSA_TOP, DSA_GROUPS, qb), jnp.int32),
                        pltpu.VMEM((DSA_TOP, DSA_GROUPS, qb), F32)])
    return pl.pallas_call(
        functools.partial(_dsa_kernel, n_sel=n_sel),
        grid_spec=grid_spec,
        out_shape=jax.ShapeDtypeStruct((N_HEADS, HEAD_DIM, t), F32),
        compiler_params=pltpu.CompilerParams(dimension_semantics=("arbitrary",),
                                             vmem_limit_bytes=VMEM_LIMIT),
        name="dsa",
    )(qi, kj, qt, k, vt, iqt, ik, iwt)


def _out_proj_kernel(x_ref, att_ref, rw_ref, wo_ref, n2w_ref, rwt_ref, rb_ref,
                     h_ref, xn_ref, gate_ref):
    tm = x_ref.shape[0]
    att = att_ref[...].reshape(D_GROUP, tm).T
    mix = jnp.concatenate([att, rw_ref[...]], axis=1)
    acc = x_ref[...] + _dot(mix, wo_ref[...])
    h_ref[...] = acc
    xn = acc * lax.rsqrt(jnp.mean(acc * acc, axis=-1, keepdims=True) + RMS_EPS) * n2w_ref[...]
    xn_ref[...] = xn.astype(BF16)
    logits = _dot3(xn, rwt_ref[...]) + rb_ref[...]
    lane = lax.broadcasted_iota(jnp.int32, logits.shape, 1)
    work = logits
    vals, hots = [], []
    for _ in range(TOP_K_EXPERTS):
        m = jnp.max(work, axis=1, keepdims=True)
        idx = jnp.min(jnp.where(work == m, lane, LANES), axis=1, keepdims=True)
        hot = lane == idx
        vals.append(m)
        hots.append(hot)
        work = jnp.where(hot, -jnp.inf, work)
    es = [jnp.exp(vv - vals[0]) for vv in vals]
    denom = es[0] + es[1] + es[2] + es[3]
    gates = jnp.zeros_like(logits)
    for e, hot in zip(es, hots):
        gates = gates + jnp.where(hot, e / denom, 0.0)
    gate_ref[...] = gates


def _out_proj(x2, att_t, rw, wo, n2w, rwt, rb, tm):
    t = x2.shape[0]
    full = lambda shape: pl.BlockSpec(shape, lambda i: (0,) * len(shape))
    row = lambda width: pl.BlockSpec((tm, width), lambda i: (i, 0))
    return pl.pallas_call(
        _out_proj_kernel,
        grid=(t // tm,),
        in_specs=[row(D_MODEL), pl.BlockSpec((N_HEADS, HEAD_DIM, tm), lambda i: (0, 0, i)),
                  row(D_GROUP), full((2 * D_GROUP, D_MODEL)), full((1, D_MODEL)),
                  full((D_MODEL, LANES)), full((1, LANES))],
        out_specs=[row(D_MODEL), row(D_MODEL), row(LANES)],
        out_shape=(jax.ShapeDtypeStruct((t, D_MODEL), F32),
                   jax.ShapeDtypeStruct((t, D_MODEL), BF16),
                   jax.ShapeDtypeStruct((t, LANES), F32)),
        compiler_params=pltpu.CompilerParams(dimension_semantics=("arbitrary",),
                                             vmem_limit_bytes=VMEM_LIMIT),
        name="out_proj",
    )(x2, att_t, rw, wo, n2w, rwt, rb)


MOE_TM = 1024
MOE_RB = 144
MOE_EXPERTS_PER_STEP = 2


def _route_kernel(gate_ref, rankt_ref, gatet_ref, cnt_ref):
    tm = gate_ref.shape[0]
    gates = gate_ref[...]
    hot = gates > 0.0
    ind = jnp.where(hot, 1.0, 0.0)
    before = (lax.broadcasted_iota(jnp.int32, (tm, tm), 1)
              < lax.broadcasted_iota(jnp.int32, (tm, tm), 0))
    rank = jnp.dot(jnp.where(before, 1.0, 0.0).astype(BF16), ind.astype(BF16),
                   preferred_element_type=F32)
    rankt_ref[...] = jnp.where(hot, rank, -1.0).T
    gatet_ref[...] = gates.T
    cnt = jnp.sum(ind, axis=0, keepdims=True)
    cnt_ref[...] = jnp.broadcast_to(cnt[None], cnt_ref.shape)


def _route(gates, tm):
    t = gates.shape[0]
    return pl.pallas_call(
        _route_kernel,
        grid=(t // tm,),
        in_specs=[pl.BlockSpec((tm, LANES), lambda i: (i, 0))],
        out_specs=[pl.BlockSpec((LANES, tm), lambda i: (0, i)),
                   pl.BlockSpec((LANES, tm), lambda i: (0, i)),
                   pl.BlockSpec((1, SUBLANES, LANES), lambda i: (i, 0, 0))],
        out_shape=(jax.ShapeDtypeStruct((LANES, t), F32),
                   jax.ShapeDtypeStruct((LANES, t), F32),
                   jax.ShapeDtypeStruct((t // tm, SUBLANES, LANES), F32)),
        compiler_params=pltpu.CompilerParams(dimension_semantics=("arbitrary",),
                                             vmem_limit_bytes=VMEM_LIMIT),
        name="route",
    )(gates)


def _moe_kernel(cnt_ref, h_ref, xn_ref, rankt_ref, gatet_ref, wgu_ref, bgu_ref,
                wdn_ref, bdn_ref, o_ref):
    i = pl.program_id(0)
    tm = h_ref.shape[0]

    @pl.when(pl.program_id(1) == 0)
    def _():
        o_ref[...] = h_ref[...]

    for sub in range(MOE_EXPERTS_PER_STEP):
        _moe_expert(pl.program_id(1) * MOE_EXPERTS_PER_STEP + sub, sub, i, tm, cnt_ref, xn_ref,
                    rankt_ref, gatet_ref, wgu_ref, bgu_ref, wdn_ref, bdn_ref, o_ref)


def _moe_expert(e, slot, i, tm, cnt_ref, xn_ref, rankt_ref, gatet_ref, wgu_ref, bgu_ref,
                wdn_ref, bdn_ref, o_ref):
    n_rows = cnt_ref[i * N_EXPERTS + e]
    n_blocks = (n_rows + MOE_RB - 1) // MOE_RB
    rank_row = rankt_ref[pl.ds(e, 1), :]
    gate_row = gatet_ref[pl.ds(e, 1), :]

    def row_block(bi, carry):
        r0 = (bi * MOE_RB).astype(F32)
        rowid = lax.broadcasted_iota(jnp.int32, (MOE_RB, tm), 0).astype(F32) + r0
        hit = rank_row == rowid
        sel = jnp.where(hit, 1.0, 0.0).astype(BF16)
        gate = jnp.sum(jnp.where(hit, gate_row, 0.0), axis=1, keepdims=True)
        xb = jnp.dot(sel, xn_ref[...], preferred_element_type=F32).astype(BF16)
        up = jnp.dot(xb, wgu_ref[slot], preferred_element_type=F32) + bgu_ref[slot]
        glu = jnp.minimum(up[:, :D_FF], SWIGLU_LIMIT)
        lin = jnp.clip(up[:, D_FF:], -SWIGLU_LIMIT, SWIGLU_LIMIT)
        act = (lin + 1.0) * glu * jax.nn.sigmoid(SWIGLU_ALPHA * glu)
        y = jnp.dot(act.astype(BF16), wdn_ref[slot], preferred_element_type=F32)
        y = ((y + bdn_ref[slot]) * gate).astype(BF16)
        o_ref[...] += _dot_tn(sel, y)
        return carry

    lax.fori_loop(0, n_blocks, row_block, 0)


def _moe(counts, h1, xn2, rank_t, gate_t, wgu, bgu, wdn, bdn, tm):
    t = h1.shape[0]
    eps = MOE_EXPERTS_PER_STEP
    row = lambda width: pl.BlockSpec((tm, width), lambda i, e, c: (i, 0))
    grid_spec = pltpu.PrefetchScalarGridSpec(
        num_scalar_prefetch=1,
        grid=(t // tm, N_EXPERTS // eps),
        in_specs=[row(D_MODEL), row(D_MODEL),
                  pl.BlockSpec((LANES, tm), lambda i, e, c: (0, i)),
                  pl.BlockSpec((LANES, tm), lambda i, e, c: (0, i)),
                  pl.BlockSpec((eps, D_MODEL, 2 * D_FF), lambda i, e, c: (e, 0, 0)),
                  pl.BlockSpec((eps, 1, 2 * D_FF), lambda i, e, c: (e, 0, 0)),
                  pl.BlockSpec((eps, D_FF, D_MODEL), lambda i, e, c: (e, 0, 0)),
                  pl.BlockSpec((eps, 1, D_MODEL), lambda i, e, c: (e, 0, 0))],
        out_specs=row(D_MODEL))
    return pl.pallas_call(
        _moe_kernel,
        grid_spec=grid_spec,
        out_shape=jax.ShapeDtypeStruct((t, D_MODEL), F32),
        compiler_params=pltpu.CompilerParams(dimension_semantics=("arbitrary", "arbitrary"),
                                             vmem_limit_bytes=VMEM_LIMIT),
        name="moe",
    )(counts, h1, xn2, rank_t, gate_t, wgu, bgu, wdn, bdn)


def _rope_tables():
    lane = jnp.arange(LANES)

    def rows(period, rot):
        half = rot // 2
        inv_freq = ROPE_THETA ** (-jnp.arange(half, dtype=F32) / half)
        jm = lane % period
        freq = jnp.where(jm < rot, inv_freq[jm % half], 0.0)
        sign = jnp.where(jm < half, -1.0, jnp.where(jm < rot, 1.0, 0.0))
        first = (jm < half).astype(F32)
        return [freq, sign, first]

    idx_rows = rows(IDX_DIM, IDX_DIM // 4)
    only_key = (lane < IDX_DIM).astype(F32)
    return jnp.stack(rows(HEAD_DIM, HEAD_DIM // 4) + idx_rows
                     + [only_key, jnp.zeros_like(only_key)]).astype(F32)


def _group_matrix():
    g = np.arange(D_GROUP) // HEAD_DIM
    return jnp.asarray((g[:, None] == g[None, :]).astype(np.float32), dtype=BF16)


def _pad_rows(w, r0, rows):
    return jnp.zeros((rows, w.shape[1]), w.dtype).at[r0:r0 + w.shape[0]].set(w)


def kernel(x, positions, norm1_w, w_in, q_norm_w, k_norm_w, rwkv_mu, rwkv_w0, rwkv_w2, rwkv_a0,
           rwkv_a2, rwkv_g2, rwkv_k_k, rwkv_k_a, rwkv_r_k, rwkv_ln_w, rwkv_ln_b, w_out, norm2_w,
           router_w, router_b, exp_w_gu, exp_b_gu, exp_w_down, exp_b_down):
    b, t, _ = x.shape
    assert b == 1 and w_in.shape[0] == 1, "single sequence, single layer"
    assert t % DSA_KB == 0 and t % DSA_QB == 0 and t % MOE_TM == 0
    x2 = x[0]
    pos_f = positions[0].astype(F32)[:, None]
    n_sel = min(TOPK_MAX, t // 4)

    w = w_in[0]
    a0 = 3 * D_GROUP
    att_cols = a0 + D_IDX + IDX_DIM + N_IDX_HEADS
    w_att, w_rw = w[:, :att_cols], w[:, att_cols:]
    w_sm = jnp.zeros((D_MODEL, LANES), F32).at[:, :IDX_DIM + N_IDX_HEADS].set(w_att[:, a0 + D_IDX:])
    w_packed = jnp.concatenate(
        [w_att[:, :a0], w_rw[:, :a0], w_att[:, a0:a0 + D_IDX], w_sm, w_rw[:, a0:]],
        axis=1).astype(BF16)
    mu = rwkv_mu[0][None, :]
    tile8 = lambda z: jnp.tile(z, N_HEADS)[None, :]
    gsum = _group_matrix()

    tm = 256
    q, k, v, rr, rk, rv, iq, sm, lora = _in_proj(
        x2, pos_f, norm1_w, w_packed, mu, tile8(q_norm_w[0]), tile8(k_norm_w[0]), gsum,
        _rope_tables(), tm)

    vec = lambda z: z.reshape(1, D_GROUP)
    rw, (w_gu_bf, w_dn_bf) = _rwkv(
        rr, rk, rv, lora, vec(rwkv_w0[0]), _pad_rows(rwkv_w2[0], 0, LANES), vec(rwkv_a0[0]),
        _pad_rows(rwkv_a2[0], D_DECAY_LORA, LANES),
        _pad_rows(rwkv_g2[0], D_DECAY_LORA + D_AAA_LORA, LANES),
        vec(rwkv_k_k[0]), vec(rwkv_k_a[0]), vec(rwkv_r_k[0]), vec(rwkv_ln_w[0]),
        vec(rwkv_ln_b[0]), gsum, tm, to_cast=(exp_w_gu[0], exp_w_down[0]))

    heads = lambda z: z.reshape(t, N_HEADS, HEAD_DIM)
    vt = jnp.concatenate([heads(v).transpose(1, 2, 0),
                          jnp.ones((N_HEADS, DSA_V_ROWS - HEAD_DIM, t), BF16)], axis=1)
    att_t = _dsa(heads(q).transpose(1, 2, 0), k, vt, iq.T, sm[:, :IDX_DIM].astype(BF16),
                 sm[:, IDX_DIM:IDX_DIM + N_IDX_HEADS].T, n_sel)

    rwt = jnp.zeros((D_MODEL, LANES), F32).at[:, :N_EXPERTS].set(router_w[0])
    rb = jnp.full((1, LANES), NEG_BIG, F32).at[0, :N_EXPERTS].set(router_b[0])
    h1, xn2, gates = _out_proj(x2, att_t, rw, w_out[0].astype(BF16), norm2_w, rwt, rb, tm)

    rank_t, gate_t, cnt = _route(gates, MOE_TM)
    counts = cnt[:, 0, :N_EXPERTS].astype(jnp.int32).reshape(-1)
    out = _moe(counts, h1, xn2, rank_t, gate_t, w_gu_bf, exp_b_gu[0][:, None, :], w_dn_bf,
               exp_b_down[0][:, None, :], MOE_TM)
    return out[None]
```
